```python
import jax
import jax.numpy as jnp
from jax import lax
import numpy as np

D_MODEL = 1024
BATCH = 32
SEQ = 2048
DEPTH = 1
DEC_BATCH = 16
DEC_SEQ = 16
PAST_LEN = 2048

CHUNK = 64
Q_BLOCK = 128
NORM_EPS = 1e-6

MLA_HEADS = 8
MLA_NOPE_DIM = 64
MLA_ROPE_DIM = 32
MLA_V_DIM = 64
MLA_Q_LORA = 384
MLA_KV_LORA = 256
ROPE_THETA = 10000.0
MLA_WIDTH = MLA_HEADS * MLA_V_DIM

SB_HEADS = 8
SB_HEAD_DIM = 64
SB_WIDTH = SB_HEADS * SB_HEAD_DIM

N_EXPERTS = 256
TOP_K = 8
N_GROUPS = 8
TOPK_GROUPS = 4
EXPERT_DIM = 256
SHARED_DIM = 256
ROUTED_SCALE = 2.5
MOE_BLOCK = 256

IN_SPLITS = (MLA_Q_LORA, MLA_KV_LORA, MLA_ROPE_DIM, SB_WIDTH, SB_WIDTH, SB_WIDTH, D_MODEL, D_MODEL)
IN_WIDTH = MLA_Q_LORA + MLA_KV_LORA + MLA_ROPE_DIM + 3 * SB_WIDTH + 2 * D_MODEL

kernel_name = "hybrid_mla_stickbreak_moe_stream_step"


def rms_norm(x, g):
    xf = x.astype(jnp.float32)
    y = xf * lax.rsqrt(jnp.mean(xf * xf, axis=-1, keepdims=True) + NORM_EPS)
    return (y * g.astype(jnp.float32)).astype(x.dtype)


def apply_rope(x, pos):
    half = x.shape[-1] // 2
    inv_freq = ROPE_THETA ** (-jnp.arange(half, dtype=jnp.float32) / half)
    ang = pos.astype(jnp.float32)[:, None] * inv_freq
    ang = ang.reshape((ang.shape[0],) + (1,) * (x.ndim - 3) + (half,))
    cos, sin = jnp.cos(ang), jnp.sin(ang)
    xf = x.astype(jnp.float32)
    x1, x2 = xf[..., :half], xf[..., half:]
    return jnp.concatenate([x1 * cos - x2 * sin, x1 * sin + x2 * cos], axis=-1).astype(x.dtype)


def mla_attend(q_nope, q_rope, k_nope, k_rope, v, q_pos, k_pos):
    s = jnp.einsum("bqhd,bkhd->bhqk", q_nope, k_nope) + jnp.einsum("bqhr,bkr->bhqk", q_rope, k_rope)
    s = s.astype(jnp.float32) * ((MLA_NOPE_DIM + MLA_ROPE_DIM) ** -0.5)
    mask = (k_pos[None, :] // CHUNK) <= (q_pos[:, None] // CHUNK)
    p = jax.nn.softmax(jnp.where(mask[None, None], s, -jnp.inf), axis=-1)
    return jnp.einsum("bhqk,bkhd->bqhd", p.astype(v.dtype), v)


def sb_attend(q, k, v, q_pos, k_pos):
    z = jnp.einsum("bqhd,bkhd->bhqk", q, k).astype(jnp.float32) * (SB_HEAD_DIM ** -0.5)
    mask = (k_pos[None, :] < q_pos[:, None])[None, None]
    log_keep = jnp.where(mask, -jax.nn.softplus(z), 0.0)
    after = lax.cumsum(log_keep, axis=log_keep.ndim - 1, reverse=True) - log_keep
    a = jnp.where(mask, jnp.exp(jax.nn.log_sigmoid(z) + after), 0.0)
    return jnp.einsum("bhqk,bkhd->bqhd", a.astype(v.dtype), v)


def sweep_query_blocks(attend, qs, q_pos):
    b, s = qs[0].shape[:2]
    nb = s // Q_BLOCK
    blocks = tuple(jnp.moveaxis(q.reshape((b, nb, Q_BLOCK) + q.shape[2:]), 1, 0) for q in qs)
    out = lax.map(lambda a: attend(*a[0], a[1]), (blocks, q_pos.reshape(nb, Q_BLOCK)))
    return jnp.moveaxis(out, 0, 1).reshape((b, s) + out.shape[3:])


def token_mixers(h, pos, past, p, l, blocked):
    b, s, _ = h.shape
    proj = h @ p["w_in"][l]
    cuts = np.cumsum(IN_SPLITS)[:-1].tolist()
    q_lat, kv_lat, k_r, sb_q, sb_k, sb_v, g_mla, g_sb = jnp.split(proj, cuts, axis=-1)
    q = (rms_norm(q_lat, p["g_q_lat"][l]) @ p["w_uq"][l]).reshape(b, s, MLA_HEADS, MLA_NOPE_DIM + MLA_ROPE_DIM)
    q_nope = q[..., :MLA_NOPE_DIM]
    q_rope = apply_rope(q[..., MLA_NOPE_DIM:], pos)
    ckv = rms_norm(kv_lat, p["g_kv_lat"][l])
    krope = apply_rope(k_r, pos)
    sbq = sb_q.reshape(b, s, SB_HEADS, SB_HEAD_DIM)
    sbk = sb_k.reshape(b, s, SB_HEADS, SB_HEAD_DIM)
    sbv = sb_v.reshape(b, s, SB_HEADS, SB_HEAD_DIM)
    if past is None:
        ckv_all, krope_all, sbk_all, sbv_all = ckv, krope, sbk, sbv
    else:
        ckv_all = jnp.concatenate([past[0].astype(ckv.dtype), ckv], axis=1)
        krope_all = jnp.concatenate([past[1].astype(krope.dtype), krope], axis=1)
        sbk_all = jnp.concatenate([past[2].astype(sbk.dtype), sbk], axis=1)
        sbv_all = jnp.concatenate([past[3].astype(sbv.dtype), sbv], axis=1)
    n_keys = ckv_all.shape[1]
    k_pos = jnp.arange(n_keys)
    kv = (ckv_all @ p["w_ukv"][l]).reshape(b, n_keys, MLA_HEADS, MLA_NOPE_DIM + MLA_V_DIM)
    k_nope, v = kv[..., :MLA_NOPE_DIM], kv[..., MLA_NOPE_DIM:]
    mla_fn = lambda qn, qr, qp: mla_attend(qn, qr, k_nope, krope_all, v, qp, k_pos)
    sb_fn = lambda qq, qp: sb_attend(qq, sbk_all, sbv_all, qp, k_pos)
    if blocked:
        o_mla = sweep_query_blocks(mla_fn, (q_nope, q_rope), pos)
        o_sb = sweep_query_blocks(sb_fn, (sbq,), pos)
    else:
        o_mla = mla_fn(q_nope, q_rope, pos)
        o_sb = sb_fn(sbq, pos)
    u_mla = o_mla.reshape(b, s, MLA_WIDTH) @ p["w_mla_up"][l]
    u_sb = o_sb.reshape(b, s, SB_WIDTH) @ p["w_sb_up"][l]
    merged = jax.nn.sigmoid(g_mla) * u_mla + jax.nn.sigmoid(g_sb) * u_sb
    return merged @ p["w_out"][l], (ckv, krope, sbk, sbv)


def route(h, w_router, b_router):
    scores = jax.nn.sigmoid((h @ w_router).astype(jnp.float32))
    sel = scores + b_router.astype(jnp.float32)
    t = sel.shape[0]
    grp = sel.reshape(t, N_GROUPS, N_EXPERTS // N_GROUPS)
    grp_score = lax.top_k(grp, 2)[0].sum(-1)
    _, gidx = lax.top_k(grp_score, TOPK_GROUPS)
    gmask = jax.nn.one_hot(gidx, N_GROUPS, dtype=jnp.float32).sum(-2) > 0
    emask = jnp.repeat(gmask, N_EXPERTS // N_GROUPS, axis=-1)
    _, idx = lax.top_k(jnp.where(emask, sel, -jnp.inf), TOP_K)
    w = jnp.take_along_axis(scores, idx, axis=-1)
    w = w / (w.sum(-1, keepdims=True) + 1e-20) * ROUTED_SCALE
    return idx, w


def moe_routed(xt, idx, w, w1, w3, w2):
    t, d = xt.shape
    tk = t * TOP_K
    blk = int(min(MOE_BLOCK, max(8, tk // N_EXPERTS)))
    nb = -(-tk // blk) + N_EXPERTS
    flat_e = idx.reshape(-1).astype(jnp.int32)
    flat_t = jnp.arange(tk, dtype=jnp.int32) // TOP_K
    flat_w = w.reshape(-1)
    order = jnp.argsort(flat_e)
    se = flat_e[order]
    counts = jnp.zeros((N_EXPERTS,), jnp.int32).at[flat_e].add(1)
    padded = (counts + blk - 1) // blk * blk
    pad_end = jnp.cumsum(padded)
    pad_start = pad_end - padded
    start = jnp.cumsum(counts) - counts
    dest = pad_start[se] + (jnp.arange(tk, dtype=jnp.int32) - start[se])
    rows = nb * blk
    row_tok = jnp.full((rows,), t, jnp.int32).at[dest].set(flat_t[order])
    row_w = jnp.zeros((rows,), jnp.float32).at[dest].set(flat_w[order])
    blk_exp = jnp.minimum(jnp.searchsorted(pad_end, jnp.arange(nb, dtype=jnp.int32) * blk, side="right"), N_EXPERTS - 1)
    x_pad = jnp.concatenate([xt, jnp.zeros((1, d), xt.dtype)], axis=0)

    def step(y, blk_in):
        tok, wr, e = blk_in
        xb = x_pad[tok]
        hb = jax.nn.silu(xb @ w1[e]) * (xb @ w3[e])
        ob = (hb @ w2[e]) * wr[:, None].astype(xb.dtype)
        return y.at[tok].add(ob), None

    y0 = jnp.zeros((t + 1, d), xt.dtype)
    y, _ = lax.scan(step, y0, (row_tok.reshape(nb, blk), row_w.reshape(nb, blk), blk_exp))
    return y[:t]


def moe_ffn(h, p, l):
    b, s, d = h.shape
    ht = h.reshape(b * s, d)
    idx, w = route(ht, p["w_router"][l], p["b_router"][l])
    routed = moe_routed(ht, idx, w, p["moe_w1"][l], p["moe_w3"][l], p["moe_w2"][l])
    shared = (jax.nn.silu(ht @ p["shared_w1"][l]) * (ht @ p["shared_w3"][l])) @ p["shared_w2"][l]
    return (routed + shared).reshape(b, s, d)


def trunk(x, c, pos, past, p, blocked):
    new_rows = []
    for l in range(DEPTH):
        mod = jax.nn.silu(c) @ p["w_ada"][l] + p["b_ada"][l]
        sh1, sc1, ga1, sh2, sc2, ga2 = jnp.split(mod[:, None, :], 6, axis=-1)
        h = rms_norm(x, p["g_norm1"][l]) * (1 + sc1) + sh1
        past_l = None if past is None else (past[0][l], past[1][l], past[2][l], past[3][l])
        mix, rows = token_mixers(h, pos, past_l, p, l, blocked)
        x = x + ga1 * mix
        h2 = rms_norm(x, p["g_norm2"][l]) * (1 + sc2) + sh2
        x = x + ga2 * moe_ffn(h2, p, l)
        new_rows.append(rows)
    y = rms_norm(x, p["g_final"])
    ckv = jnp.stack([r[0] for r in new_rows])
    krope = jnp.stack([r[1] for r in new_rows])
    sbk = jnp.stack([r[2] for r in new_rows])
    sbv = jnp.stack([r[3] for r in new_rows])
    return y, ckv, krope, sbk, sbv


def setup_inputs(seed: int = 0) -> dict:
    key = jax.random.key(seed)
    ks = jax.random.split(key, 32)
    f32 = jnp.float32
    L, D = DEPTH, D_MODEL

    def nrm(k, shape, scale):
        return jax.random.normal(k, shape, f32) * scale

    def gain(k, shape):
        return 1.0 + 0.05 * jax.random.normal(k, shape, f32)

    return {
        "x_prompt": nrm(ks[0], (BATCH, SEQ, D), 1.0),
        "x_sample": nrm(ks[1], (DEC_BATCH, DEC_SEQ, D), 1.0),
        "cache_mla_ckv": nrm(ks[2], (L, DEC_BATCH, PAST_LEN, MLA_KV_LORA), 1.0),
        "cache_mla_krope": nrm(ks[3], (L, DEC_BATCH, PAST_LEN, MLA_ROPE_DIM), 1.0),
        "cache_sb_k": nrm(ks[4], (L, DEC_BATCH, PAST_LEN, SB_HEADS, SB_HEAD_DIM), 1.0),
        "cache_sb_v": nrm(ks[5], (L, DEC_BATCH, PAST_LEN, SB_HEADS, SB_HEAD_DIM), 1.0),
        "c_prompt": nrm(ks[6], (BATCH, D), 1.0),
        "c_sample": nrm(ks[7], (DEC_BATCH, D), 1.0),
        "w_ada": nrm(ks[8], (L, D, 6 * D), 0.3 * D ** -0.5),
        "b_ada": nrm(ks[9], (L, 6 * D), 0.02),
        "g_norm1": gain(ks[10], (L, D)),
        "w_in": nrm(ks[11], (L, D, IN_WIDTH), D ** -0.5),
        "g_q_lat": gain(ks[12], (L, MLA_Q_LORA)),
        "w_uq": nrm(ks[13], (L, MLA_Q_LORA, MLA_HEADS * (MLA_NOPE_DIM + MLA_ROPE_DIM)), MLA_Q_LORA ** -0.5),
        "g_kv_lat": gain(ks[14], (L, MLA_KV_LORA)),
        "w_ukv": nrm(ks[15], (L, MLA_KV_LORA, MLA_HEADS * (MLA_NOPE_DIM + MLA_V_DIM)), MLA_KV_LORA ** -0.5),
        "w_mla_up": nrm(ks[16], (L, MLA_WIDTH, D), MLA_WIDTH ** -0.5),
        "w_sb_up": nrm(ks[17], (L, SB_WIDTH, D), SB_WIDTH ** -0.5),
        "w_out": nrm(ks[18], (L, D, D), D ** -0.5),
        "g_norm2": gain(ks[19], (L, D)),
        "w_router": nrm(ks[20], (L, D, N_EXPERTS), D ** -0.5),
        "b_router": nrm(ks[21], (L, N_EXPERTS), 0.01),
        "moe_w1": nrm(ks[22], (L, N_EXPERTS, D, EXPERT_DIM), D ** -0.5),
        "moe_w3": nrm(ks[23], (L, N_EXPERTS, D, EXPERT_DIM), D ** -0.5),
        "moe_w2": nrm(ks[24], (L, N_EXPERTS, EXPERT_DIM, D), EXPERT_DIM ** -0.5),
        "shared_w1": nrm(ks[25], (L, D, SHARED_DIM), D ** -0.5),
        "shared_w3": nrm(ks[26], (L, D, SHARED_DIM), D ** -0.5),
        "shared_w2": nrm(ks[27], (L, SHARED_DIM, D), SHARED_DIM ** -0.5),
        "g_final": gain(ks[28], (D,)),
    }


def reference(x_prompt, x_sample, cache_mla_ckv, cache_mla_krope, cache_sb_k, cache_sb_v,
              c_prompt, c_sample, w_ada, b_ada, g_norm1, w_in, g_q_lat, w_uq, g_kv_lat, w_ukv,
              w_mla_up, w_sb_up, w_out, g_norm2, w_router, b_router, moe_w1, moe_w3, moe_w2,
              shared_w1, shared_w3, shared_w2, g_final):
    p = dict(w_ada=w_ada, b_ada=b_ada, g_norm1=g_norm1, w_in=w_in, g_q_lat=g_q_lat, w_uq=w_uq,
             g_kv_lat=g_kv_lat, w_ukv=w_ukv, w_mla_up=w_mla_up, w_sb_up=w_sb_up, w_out=w_out,
             g_norm2=g_norm2, w_router=w_router, b_router=b_router, moe_w1=moe_w1, moe_w3=moe_w3,
             moe_w2=moe_w2, shared_w1=shared_w1, shared_w3=shared_w3, shared_w2=shared_w2,
             g_final=g_final)
    pos_p = jnp.arange(x_prompt.shape[1])
    y_prompt, ckv_p, krope_p, sbk_p, sbv_p = trunk(x_prompt, c_prompt, pos_p, None, p, True)
    past_len = cache_mla_ckv.shape[2]
    pos_s = past_len + jnp.arange(x_sample.shape[1])
    past = (cache_mla_ckv, cache_mla_krope, cache_sb_k, cache_sb_v)
    y_sample, ckv_s, krope_s, sbk_s, sbv_s = trunk(x_sample, c_sample, pos_s, past, p, False)
    return (y_prompt, y_sample, ckv_p, krope_p, sbk_p, sbv_p, ckv_s, krope_s, sbk_s, sbv_s)
```

```python
import functools

import jax
import jax.numpy as jnp
from jax import lax
from jax.experimental import pallas as pl
from jax.experimental.pallas import tpu as pltpu
from jax.experimental.pallas import tpu_sc as plsc

F32 = jnp.float32
BF16 = jnp.bfloat16

D_MODEL = 1024
NORM_EPS = 1e-6
CHUNK = 64
MLA_HEADS = 8
MLA_NOPE = 64
MLA_ROPE = 32
MLA_V = 64
MLA_Q_LORA = 384
MLA_KV_LORA = 256
ROPE_THETA = 10000.0
SB_HEADS = 8
SB_DIM = 64
SB_WIDTH = SB_HEADS * SB_DIM
N_EXPERTS = 256
TOP_K = 8
N_GROUPS = 8
TOPK_GROUPS = 4
GROUP_SIZE = N_EXPERTS // N_GROUPS
EXPERT_DIM = 256
ROUTED_SCALE = 2.5

LANES = 128
SC_CORES = 2
SC_SUBCORES = 16
SC_WORKERS = SC_CORES * SC_SUBCORES
VMEM_LIMIT = 56 * 1024 * 1024

HEAD_PAD = LANES
QP_WIDTH = MLA_HEADS * HEAD_PAD
HALF = D_MODEL // 2

C_QLAT = 0
C_KV = C_QLAT + MLA_Q_LORA
C_SBQ = C_KV + MLA_KV_LORA
C_SBK = C_SBQ + SB_WIDTH
C_SBV = C_SBK + SB_WIDTH
C_GATE = C_SBV + SB_WIDTH
C_KR = C_GATE + 2 * D_MODEL
C_END = C_KR + 2 * LANES

MOE_BLOCK = 256
ATTN_BLOCK = 256
SC_CHUNK = 64


def _rms(x):
    return x * lax.rsqrt(jnp.mean(x * x, axis=-1, keepdims=True) + NORM_EPS)


def _silu(x):
    return x * jax.nn.sigmoid(x)


def _pack_halves(lo, hi):
    lo_bits = lax.bitcast_convert_type(lo.astype(BF16).astype(F32), jnp.uint32) >> 16
    hi_bits = lax.bitcast_convert_type(hi.astype(BF16).astype(F32), jnp.uint32) & jnp.uint32(0xFFFF0000)
    return lax.bitcast_convert_type(lo_bits | hi_bits, jnp.int32)


def _unpack_halves(p):
    u = lax.bitcast_convert_type(p, jnp.uint32)
    lo = lax.bitcast_convert_type(u << 16, F32)
    hi = lax.bitcast_convert_type(u & jnp.uint32(0xFFFF0000), F32)
    return lo, hi


def _dot(a, b):
    return jnp.dot(a, b, preferred_element_type=F32)


def _dot_nt(a, b):
    return lax.dot_general(a, b, (((1,), (1,)), ((), ())), preferred_element_type=F32)


def _ada_kernel(c_ref, w_ref, b_ref, o_ref):
    c = c_ref[...]
    o_ref[...] = _dot(_silu(c).astype(BF16), w_ref[...].astype(BF16)) + b_ref[...]


def _ada(c, w_ada, b_ada):
    n = c.shape[0]
    width = w_ada.shape[1]
    return pl.pallas_call(
        _ada_kernel,
        grid=(width // D_MODEL,),
        in_specs=[pl.BlockSpec((n, D_MODEL), lambda j: (0, 0)),
                  pl.BlockSpec((D_MODEL, D_MODEL), lambda j: (0, j)),
                  pl.BlockSpec((1, D_MODEL), lambda j: (0, j))],
        out_specs=pl.BlockSpec((n, D_MODEL), lambda j: (0, j)),
        out_shape=jax.ShapeDtypeStruct((n, width), F32),
        name="ada",
    )(c, w_ada, b_ada.reshape(1, width))


def _in_kernel(x_ref, mod_ref, g1_ref, win_ref, gq_ref, wuq_ref, gkv_ref, wukv_ref, cos_ref, sin_ref,
               qp_ref, kmla_ref, vmla_ref, sbq_ref, sbk16_ref, sbv16_ref, gates_ref,
               ckv_ref, krope_ref, sbk_ref, sbv_ref):
    x = x_ref[0]
    mod = mod_ref[0]
    h = _rms(x) * g1_ref[...] * (1.0 + mod[1:2]) + mod[0:1]
    hb = h.astype(BF16)

    def seg(a, b):
        return _dot(hb, win_ref[:, a:b])

    cos = cos_ref[...]
    sin = sin_ref[...]
    cos8 = jnp.tile(cos, (1, MLA_HEADS))
    sin8 = jnp.tile(sin, (1, MLA_HEADS))

    qn = (_rms(seg(C_QLAT, C_KV)) * gq_ref[...]).astype(BF16)
    q2 = _dot(qn, wuq_ref[...])
    qp_ref[0] = (q2[:, :QP_WIDTH] * cos8 + q2[:, QP_WIDTH:] * sin8).astype(BF16)

    ckv = _rms(seg(C_KV, C_SBQ)) * gkv_ref[...]
    ckv_ref[0] = ckv
    kv = _dot(ckv.astype(BF16), wukv_ref[...])
    kr2 = seg(C_KR, C_END)
    krp = kr2[:, :LANES] * cos + kr2[:, LANES:] * sin
    krope_ref[0] = krp[:, MLA_NOPE:MLA_NOPE + MLA_ROPE]
    kmla_ref[0] = (kv[:, :QP_WIDTH] + jnp.tile(krp, (1, MLA_HEADS))).astype(BF16)
    vmla_ref[0] = kv[:, QP_WIDTH:].astype(BF16)

    sbq_ref[0] = seg(C_SBQ, C_SBK).astype(BF16)
    sbk = seg(C_SBK, C_SBV)
    sbk_ref[0] = sbk
    sbk16_ref[0] = sbk.astype(BF16)
    sbv = seg(C_SBV, C_GATE)
    sbv_ref[0] = sbv
    sbv16_ref[0] = sbv.astype(BF16)
    gates_ref[0] = jax.nn.sigmoid(seg(C_GATE, C_KR)).astype(BF16)


def _in_proj(x, mod, wts, cos_t, sin_t, tm):
    b, s, _ = x.shape
    ns = s // tm
    tok = lambda w: pl.BlockSpec((1, tm, w), lambda i, j: (i, j, 0))
    full = lambda a: pl.BlockSpec(a.shape, lambda i, j: (0,) * a.ndim)
    out_widths = [(QP_WIDTH, BF16), (QP_WIDTH, BF16), (SB_WIDTH, BF16), (SB_WIDTH, BF16), (SB_WIDTH, BF16),
                  (SB_WIDTH, BF16), (2 * D_MODEL, BF16), (MLA_KV_LORA, F32), (MLA_ROPE, F32),
                  (SB_WIDTH, F32), (SB_WIDTH, F32)]
    return pl.pallas_call(
        _in_kernel,
        grid=(b, ns),
        in_specs=[tok(D_MODEL),
                  pl.BlockSpec((1, 6, D_MODEL), lambda i, j: (i, 0, 0)),
                  full(wts["g1"]), full(wts["w_in"]), full(wts["gq"]), full(wts["w_uq"]),
                  full(wts["gkv"]), full(wts["w_ukv"]),
                  pl.BlockSpec((tm, LANES), lambda i, j: (j, 0)),
                  pl.BlockSpec((tm, LANES), lambda i, j: (j, 0))],
        out_specs=[tok(w) for w, _ in out_widths],
        out_shape=[jax.ShapeDtypeStruct((b, s, w), dt) for w, dt in out_widths],
        compiler_params=pltpu.CompilerParams(dimension_semantics=("parallel", "parallel"),
                                             vmem_limit_bytes=VMEM_LIMIT),
        name="in_proj",
    )(x, mod, wts["g1"], wts["w_in"], wts["gq"], wts["w_uq"], wts["gkv"], wts["w_ukv"], cos_t, sin_t)


def _kvup_kernel(ckv_ref, kr_ref, wukv_ref, place_ref, kmla_ref, vmla_ref):
    kv = _dot(ckv_ref[0].astype(BF16), wukv_ref[...])
    kr = _dot(kr_ref[0].astype(BF16), place_ref[...])
    kmla_ref[0] = (kv[:, :QP_WIDTH] + kr).astype(BF16)
    vmla_ref[0] = kv[:, QP_WIDTH:].astype(BF16)


def _kv_up(ckv, krope, w_ukv_r, place, tm):
    b, p, _ = ckv.shape
    return pl.pallas_call(
        _kvup_kernel,
        grid=(b, p // tm),
        in_specs=[pl.BlockSpec((1, tm, MLA_KV_LORA), lambda i, j: (i, j, 0)),
                  pl.BlockSpec((1, tm, MLA_ROPE), lambda i, j: (i, j, 0)),
                  pl.BlockSpec(w_ukv_r.shape, lambda i, j: (0, 0)),
                  pl.BlockSpec(place.shape, lambda i, j: (0, 0))],
        out_specs=[pl.BlockSpec((1, tm, QP_WIDTH), lambda i, j: (i, j, 0)),
                   pl.BlockSpec((1, tm, SB_WIDTH), lambda i, j: (i, j, 0))],
        out_shape=[jax.ShapeDtypeStruct((b, p, QP_WIDTH), BF16), jax.ShapeDtypeStruct((b, p, SB_WIDTH), BF16)],
        compiler_params=pltpu.CompilerParams(dimension_semantics=("parallel", "parallel")),
        name="kv_up",
    )(ckv, krope, w_ukv_r, place)


def _attn_kernel(*refs, tq, past_len, past_blk):
    has_past = past_len > 0
    if has_past:
        (qp_ref, kmla_ref, vmla_ref, sbq_ref, sbk_ref, sbv_ref, pkmla_ref, pvmla_ref, psbk_ref, psbv_ref,
         gates_ref, x_ref, mod_ref, wmu_ref, wsu_ref, wo_ref, g2_ref, x1_ref, h2_ref) = refs
    else:
        (qp_ref, kmla_ref, vmla_ref, sbq_ref, sbk_ref, sbv_ref,
         gates_ref, x_ref, mod_ref, wmu_ref, wsu_ref, wo_ref, g2_ref, x1_ref, h2_ref) = refs
    i = pl.program_id(1)
    n_past = past_len // past_blk

    lane = lax.broadcasted_iota(jnp.int32, (1, LANES), 1)
    half_masks = (lane < MLA_V, lane >= MLA_V)
    row = lax.broadcasted_iota(jnp.int32, (tq, tq), 0)
    col = lax.broadcasted_iota(jnp.int32, (tq, tq), 1)
    base = past_len + i * tq
    chunk_mask = ((base + col) // CHUNK) <= ((base + row) // CHUNK)
    causal_mask = col < row

    def tri(n):
        r = lax.broadcasted_iota(jnp.int32, (n, n), 0)
        c = lax.broadcasted_iota(jnp.int32, (n, n), 1)
        return jnp.where(r > c, 1.0, 0.0).astype(BF16)

    tri_new = tri(tq)
    tri_past = tri(past_blk) if has_past else None

    def mla_block(q_h, k, v, carry, mask):
        m, l, acc = carry
        s = _dot_nt(q_h, k)
        if mask is not None:
            s = jnp.where(mask, s, -jnp.inf)
        m_new = jnp.maximum(m, jnp.max(s, axis=-1, keepdims=True))
        p = jnp.exp(s - m_new)
        alpha = jnp.exp(m - m_new)
        l = alpha * l + jnp.sum(p, axis=-1, keepdims=True)
        acc = alpha * acc + _dot(p.astype(BF16), v)
        return m_new, l, acc

    def sb_block(q_h, k, v, carry, mask, tri_m):
        c, acc = carry
        z = _dot_nt(q_h, k)
        soft = jnp.log(1.0 + jnp.exp(-jnp.abs(z)))
        log_sig = jnp.minimum(z, 0.0) - soft
        log_keep = log_sig - z
        if mask is not None:
            log_keep = jnp.where(mask, log_keep, 0.0)
        hi = log_keep.astype(BF16)
        lo = (log_keep - hi.astype(F32)).astype(BF16)
        after = _dot(hi, tri_m) + _dot(lo, tri_m) + c
        a = jnp.exp(log_sig + after)
        if mask is not None:
            a = jnp.where(mask, a, 0.0)
        acc = acc + _dot(a.astype(BF16), v)
        c = c + jnp.sum(log_keep, axis=-1, keepdims=True)
        return c, acc

    def load(ref, start, size, c0):
        return ref[0, pl.ds(start, size), c0:c0 + LANES].astype(BF16)

    o_mla = []
    o_sb = []
    for pair in range(MLA_HEADS // 2):
        vcol = pair * LANES
        mla_pair = jnp.zeros((tq, LANES), F32)
        sb_pair = jnp.zeros((tq, LANES), F32)
        for sub in range(2):
            head = 2 * pair + sub
            keep = half_masks[sub]
            kcol = head * HEAD_PAD

            q_h = qp_ref[0, :, kcol:kcol + HEAD_PAD]
            carry = (jnp.full((tq, 1), -jnp.inf, F32), jnp.zeros((tq, 1), F32), jnp.zeros((tq, LANES), F32))
            if has_past:
                def past_step(j, cr, q_h=q_h, kcol=kcol, keep=keep, vcol=vcol):
                    st = pl.multiple_of(j * past_blk, past_blk)
                    v = jnp.where(keep, load(pvmla_ref, st, past_blk, vcol), 0)
                    return mla_block(q_h, load(pkmla_ref, st, past_blk, kcol), v, cr, None)
                carry = lax.fori_loop(0, n_past, past_step, carry)

            def new_step(j, cr, q_h=q_h, kcol=kcol, keep=keep, vcol=vcol):
                st = pl.multiple_of(j * tq, tq)
                v = jnp.where(keep, load(vmla_ref, st, tq, vcol), 0)
                return mla_block(q_h, load(kmla_ref, st, tq, kcol), v, cr, None)
            carry = lax.fori_loop(0, i, new_step, carry)
            st = pl.multiple_of(i * tq, tq)
            v = jnp.where(keep, load(vmla_ref, st, tq, vcol), 0)
            _, l, acc = mla_block(q_h, load(kmla_ref, st, tq, kcol), v, carry, chunk_mask)
            mla_pair = mla_pair + acc / l

            q_s = jnp.where(keep, sbq_ref[0, :, vcol:vcol + LANES], 0)
            v = jnp.where(keep, load(sbv_ref, st, tq, vcol), 0)
            carry = (jnp.zeros((tq, 1), F32), jnp.zeros((tq, LANES), F32))
            carry = sb_block(q_s, load(sbk_ref, st, tq, vcol), v, carry, causal_mask, tri_new)

            def new_sb_step(t, cr, q_s=q_s, keep=keep, vcol=vcol):
                st = pl.multiple_of((i - 1 - t) * tq, tq)
                v = jnp.where(keep, load(sbv_ref, st, tq, vcol), 0)
                return sb_block(q_s, load(sbk_ref, st, tq, vcol), v, cr, None, tri_new)
            carry = lax.fori_loop(0, i, new_sb_step, carry)
            if has_past:
                def past_sb_step(t, cr, q_s=q_s, keep=keep, vcol=vcol):
                    st = pl.multiple_of((n_past - 1 - t) * past_blk, past_blk)
                    v = jnp.where(keep, load(psbv_ref, st, past_blk, vcol), 0)
                    return sb_block(q_s, load(psbk_ref, st, past_blk, vcol), v, cr, None, tri_past)
                carry = lax.fori_loop(0, n_past, past_sb_step, carry)
            sb_pair = sb_pair + carry[1]
        o_mla.append(mla_pair)
        o_sb.append(sb_pair)

    u_mla = _dot(jnp.concatenate(o_mla, axis=1).astype(BF16), wmu_ref[...])
    u_sb = _dot(jnp.concatenate(o_sb, axis=1).astype(BF16), wsu_ref[...])
    gates = gates_ref[0]
    merged = gates[:, :D_MODEL].astype(F32) * u_mla + gates[:, D_MODEL:].astype(F32) * u_sb
    mix = _dot(merged.astype(BF16), wo_ref[...])
    mod = mod_ref[0]
    x1 = x_ref[0] + mod[2:3] * mix
    x1_ref[0] = x1
    h2 = _rms(x1) * g2_ref[...] * (1.0 + mod[4:5]) + mod[3:4]
    h2_ref[0] = _pack_halves(h2[:, :HALF], h2[:, HALF:])


def _attention(qp, kmla, vmla, sbq, sbk16, sbv16, past, gates, x, mod, wts, tq, past_blk):
    b, s, _ = x.shape
    past_len = 0 if past is None else past[0].shape[1]
    tok = lambda w: pl.BlockSpec((1, tq, w), lambda i, j: (i, j, 0))
    seq = lambda a: pl.BlockSpec((1,) + a.shape[1:], lambda i, j: (i, 0, 0))
    full = lambda a: pl.BlockSpec(a.shape, lambda i, j: (0,) * a.ndim)
    kv_args = [kmla, vmla, sbk16, sbv16]
    args = [qp, kmla, vmla, sbq, sbk16, sbv16]
    specs = [tok(QP_WIDTH), seq(kmla), seq(vmla), tok(SB_WIDTH), seq(sbk16), seq(sbv16)]
    del kv_args
    if past is not None:
        args += list(past)
        specs += [seq(a) for a in past]
    args += [gates, x, mod, wts["w_mla_up"], wts["w_sb_up"], wts["w_out"], wts["g2"]]
    specs += [tok(2 * D_MODEL), tok(D_MODEL), pl.BlockSpec((1, 6, D_MODEL), lambda i, j: (i, 0, 0)),
              full(wts["w_mla_up"]), full(wts["w_sb_up"]), full(wts["w_out"]), full(wts["g2"])]
    return pl.pallas_call(
        functools.partial(_attn_kernel, tq=tq, past_len=past_len, past_blk=past_blk),
        grid=(b, s // tq),
        in_specs=specs,
        out_specs=[tok(D_MODEL), tok(HALF)],
        out_shape=[jax.ShapeDtypeStruct((b, s, D_MODEL), F32), jax.ShapeDtypeStruct((b, s, HALF), jnp.int32)],
        compiler_params=pltpu.CompilerParams(dimension_semantics=("parallel", "arbitrary"),
                                             vmem_limit_bytes=VMEM_LIMIT),
        name="attention",
    )(*args)


def _route_kernel(h2_ref, wr_ref, br_ref, idx_ref, wt_ref):
    lo, hi = _unpack_halves(h2_ref[...])
    tm = lo.shape[0]
    logits = _dot_nt(wr_ref[:, :HALF], lo.astype(BF16)) + _dot_nt(wr_ref[:, HALF:], hi.astype(BF16))
    scores = jax.nn.sigmoid(logits)
    sel = scores + br_ref[...]
    neg = -jnp.inf

    grp = sel.reshape(N_GROUPS, GROUP_SIZE, tm)
    within = lax.broadcasted_iota(jnp.int32, grp.shape, 1)
    top1 = jnp.max(grp, axis=1, keepdims=True)
    first = jnp.min(jnp.where(grp == top1, within, GROUP_SIZE), axis=1, keepdims=True)
    top2 = jnp.max(jnp.where(within == first, neg, grp), axis=1, keepdims=True)
    gscore = (top1 + top2).reshape(N_GROUPS, tm)

    gid = lax.broadcasted_iota(jnp.int32, gscore.shape, 0)
    chosen = jnp.zeros(gscore.shape, jnp.bool_)
    for _ in range(TOPK_GROUPS):
        best = jnp.max(gscore, axis=0, keepdims=True)
        pick = jnp.min(jnp.where(gscore == best, gid, N_GROUPS), axis=0, keepdims=True)
        hit = gid == pick
        chosen = jnp.logical_or(chosen, hit)
        gscore = jnp.where(hit, neg, gscore)
    chosen3 = jnp.broadcast_to(chosen.reshape(N_GROUPS, 1, tm), grp.shape)
    cand = jnp.where(chosen3, grp, neg).reshape(N_EXPERTS, tm)

    eid = lax.broadcasted_iota(jnp.int32, cand.shape, 0)
    picks = []
    weights = []
    for _ in range(TOP_K):
        best = jnp.max(cand, axis=0, keepdims=True)
        pick = jnp.min(jnp.where(cand == best, eid, N_EXPERTS), axis=0, keepdims=True)
        hit = eid == pick
        weights.append(jnp.sum(jnp.where(hit, scores, 0.0), axis=0, keepdims=True))
        picks.append(pick)
        cand = jnp.where(hit, neg, cand)
    w = jnp.concatenate(weights, axis=0)
    idx_ref[...] = jnp.concatenate(picks, axis=0)
    wt_ref[...] = w / (jnp.sum(w, axis=0, keepdims=True) + 1e-20) * ROUTED_SCALE


def _route(h2p, wr_t, br, tm):
    t = h2p.shape[0]
    return pl.pallas_call(
        _route_kernel,
        grid=(t // tm,),
        in_specs=[pl.BlockSpec((tm, HALF), lambda i: (i, 0)),
                  pl.BlockSpec(wr_t.shape, lambda i: (0, 0)),
                  pl.BlockSpec(br.shape, lambda i: (0, 0))],
        out_specs=[pl.BlockSpec((TOP_K, tm), lambda i: (0, i)), pl.BlockSpec((TOP_K, tm), lambda i: (0, i))],
        out_shape=[jax.ShapeDtypeStruct((TOP_K, t), jnp.int32), jax.ShapeDtypeStruct((TOP_K, t), F32)],
        compiler_params=pltpu.CompilerParams(dimension_semantics=("parallel",)),
        name="route",
    )(h2p, wr_t, br)


def _sc_gather_rows(table, idx, chunk):
    n_rows = idx.shape[0]
    width = table.shape[1]
    per_worker = n_rows // SC_WORKERS
    n_chunks = per_worker // chunk
    assert per_worker * SC_WORKERS == n_rows and n_chunks * chunk == per_worker
    mesh = plsc.VectorSubcoreMesh(core_axis_name="c", subcore_axis_name="s",
                                  num_cores=SC_CORES, num_subcores=SC_SUBCORES)

    @functools.partial(
        pl.kernel, mesh=mesh,
        out_type=jax.ShapeDtypeStruct((n_rows, width), table.dtype),
        scratch_types=[pltpu.VMEM((chunk,), jnp.int32), pltpu.VMEM((chunk, width), table.dtype),
                       pltpu.SemaphoreType.DMA],
    )
    def gather(table_hbm, idx_hbm, out_hbm, idx_v, rows_v, sem):
        wid = lax.axis_index("s") * SC_CORES + lax.axis_index("c")
        base = wid * per_worker

        @pl.loop(0, n_chunks)
        def _(c):
            off = pl.multiple_of(base + c * chunk, 8)
            pltpu.sync_copy(idx_hbm.at[pl.ds(off, chunk)], idx_v)
            pltpu.async_copy(table_hbm.at[idx_v], rows_v, sem).wait()
            pltpu.sync_copy(rows_v, out_hbm.at[pl.ds(off, chunk)])

    return gather(table, idx)


def _expert_kernel(be_ref, xs_ref, w1_ref, w3_ref, w2_ref, os_ref, wb1, wb3, wb2):
    i = pl.program_id(0)
    changed = jnp.logical_or(i == 0, be_ref[i] != be_ref[jnp.maximum(i - 1, 0)])

    @pl.when(changed)
    def _():
        wb1[...] = w1_ref[0].astype(BF16)
        wb3[...] = w3_ref[0].astype(BF16)
        wb2[...] = w2_ref[0].astype(BF16)

    lo, hi = _unpack_halves(xs_ref[...])
    lo = lo.astype(BF16)
    hi = hi.astype(BF16)
    a = _dot(lo, wb1[:HALF, :]) + _dot(hi, wb1[HALF:, :])
    b = _dot(lo, wb3[:HALF, :]) + _dot(hi, wb3[HALF:, :])
    o = _dot((_silu(a) * b).astype(BF16), wb2[...])
    os_ref[...] = _pack_halves(o[:, :HALF], o[:, HALF:])


def _experts(xs, blk_exp, w1, w3, w2, blk):
    rows = xs.shape[0]
    nb = rows // blk
    grid_spec = pltpu.PrefetchScalarGridSpec(
        num_scalar_prefetch=1,
        grid=(nb,),
        in_specs=[pl.BlockSpec((blk, HALF), lambda i, be: (i, 0)),
                  pl.BlockSpec((1, D_MODEL, EXPERT_DIM), lambda i, be: (be[i], 0, 0)),
                  pl.BlockSpec((1, D_MODEL, EXPERT_DIM), lambda i, be: (be[i], 0, 0)),
                  pl.BlockSpec((1, EXPERT_DIM, D_MODEL), lambda i, be: (be[i], 0, 0))],
        out_specs=pl.BlockSpec((blk, HALF), lambda i, be: (i, 0)),
        scratch_shapes=[pltpu.VMEM((D_MODEL, EXPERT_DIM), BF16), pltpu.VMEM((D_MODEL, EXPERT_DIM), BF16),
                        pltpu.VMEM((EXPERT_DIM, D_MODEL), BF16)],
    )
    return pl.pallas_call(
        _expert_kernel,
        grid_spec=grid_spec,
        out_shape=jax.ShapeDtypeStruct((rows, HALF), jnp.int32),
        compiler_params=pltpu.CompilerParams(dimension_semantics=("arbitrary",)),
        name="experts",
    )(blk_exp, xs, w1, w3, w2)


def _combine_kernel(g_ref, wt_ref, h2_ref, x1_ref, mod_ref, sw1_ref, sw3_ref, sw2_ref, gf_ref, y_ref):
    wt = wt_ref[...]
    lo_acc = None
    for k in range(TOP_K):
        lo, hi = _unpack_halves(g_ref[:, k * HALF:(k + 1) * HALF])
        wk = wt[:, k:k + 1]
        lo_acc = wk * lo if lo_acc is None else lo_acc + wk * lo
        hi_acc = wk * hi if k == 0 else hi_acc + wk * hi
    routed = jnp.concatenate([lo_acc, hi_acc], axis=1)
    lo, hi = _unpack_halves(h2_ref[...])
    lo = lo.astype(BF16)
    hi = hi.astype(BF16)
    a = _dot(lo, sw1_ref[:HALF, :]) + _dot(hi, sw1_ref[HALF:, :])
    b = _dot(lo, sw3_ref[:HALF, :]) + _dot(hi, sw3_ref[HALF:, :])
    shared = _dot((_silu(a) * b).astype(BF16), sw2_ref[...])
    mod = mod_ref[0]
    x2 = x1_ref[...] + mod[5:6] * (routed + shared)
    y_ref[...] = _rms(x2) * gf_ref[...]


def _combine(gathered, wt, h2p, x1, mod, wts, tm, tile0, tiles_per_batch):
    t = x1.shape[0]
    full = lambda a: pl.BlockSpec(a.shape, lambda i: (0,) * a.ndim)
    shifted = lambda w: pl.BlockSpec((tm, w), lambda i: (i + tile0, 0))
    return pl.pallas_call(
        _combine_kernel,
        grid=(t // tm,),
        in_specs=[shifted(TOP_K * HALF), shifted(TOP_K), shifted(HALF),
                  pl.BlockSpec((tm, D_MODEL), lambda i: (i, 0)),
                  pl.BlockSpec((1, 6, D_MODEL), lambda i: (i // tiles_per_batch, 0, 0)),
                  full(wts["sw1"]), full(wts["sw3"]), full(wts["sw2"]), full(wts["gf"])],
        out_specs=pl.BlockSpec((tm, D_MODEL), lambda i: (i, 0)),
        out_shape=jax.ShapeDtypeStruct((t, D_MODEL), F32),
        compiler_params=pltpu.CompilerParams(dimension_semantics=("parallel",), vmem_limit_bytes=VMEM_LIMIT),
        name="combine",
    )(gathered, wt, h2p, x1, mod, wts["sw1"], wts["sw3"], wts["sw2"], wts["gf"])


def _rope_tables(pos):
    half = MLA_ROPE // 2
    inv_freq = ROPE_THETA ** (-jnp.arange(half, dtype=F32) / half)
    ang = pos.astype(F32)[:, None] * inv_freq
    cos, sin = jnp.cos(ang), jnp.sin(ang)
    n = pos.shape[0]
    ones = jnp.ones((n, MLA_NOPE), F32)
    z_nope = jnp.zeros((n, MLA_NOPE), F32)
    z_pad = jnp.zeros((n, HEAD_PAD - MLA_NOPE - MLA_ROPE), F32)
    return (jnp.concatenate([ones, cos, cos, z_pad], axis=1),
            jnp.concatenate([z_nope, -sin, sin, z_pad], axis=1))


def _prep_weights(g_norm1, w_in, g_q_lat, w_uq, g_kv_lat, w_ukv, w_mla_up, w_sb_up, w_out, g_norm2,
                  w_router, b_router, shared_w1, shared_w3, shared_w2, g_final):
    half = MLA_ROPE // 2
    w = w_in[0]
    o = 0
    parts = {}
    for name, width in (("qlat", MLA_Q_LORA), ("kv", MLA_KV_LORA), ("kr", MLA_ROPE), ("sbq", SB_WIDTH),
                        ("sbk", SB_WIDTH), ("sbv", SB_WIDTH), ("gm", D_MODEL), ("gs", D_MODEL)):
        parts[name] = w[:, o:o + width]
        o += width
    kr = parts["kr"]
    z_nope = jnp.zeros((D_MODEL, MLA_NOPE), F32)
    z_pad = jnp.zeros((D_MODEL, HEAD_PAD - MLA_NOPE - MLA_ROPE), F32)
    kr_seg = jnp.concatenate([z_nope, kr, z_pad, z_nope, kr[:, half:], kr[:, :half], z_pad], axis=1)
    w_in_r = jnp.concatenate([parts["qlat"], parts["kv"], parts["sbq"] * (SB_DIM ** -0.5), parts["sbk"],
                              parts["sbv"], parts["gm"], parts["gs"], kr_seg], axis=1).astype(BF16)

    scale = (MLA_NOPE + MLA_ROPE) ** -0.5
    uq = w_uq[0].reshape(MLA_Q_LORA, MLA_HEADS, MLA_NOPE + MLA_ROPE) * scale
    nope, rope = uq[..., :MLA_NOPE], uq[..., MLA_NOPE:]
    zq_nope = jnp.zeros_like(nope)
    zq_pad = jnp.zeros((MLA_Q_LORA, MLA_HEADS, HEAD_PAD - MLA_NOPE - MLA_ROPE), F32)
    main = jnp.concatenate([nope, rope, zq_pad], axis=-1).reshape(MLA_Q_LORA, QP_WIDTH)
    swapped = jnp.concatenate([zq_nope, rope[..., half:], rope[..., :half], zq_pad], axis=-1)
    w_uq_r = jnp.concatenate([main, swapped.reshape(MLA_Q_LORA, QP_WIDTH)], axis=1).astype(BF16)

    ukv = w_ukv[0].reshape(MLA_KV_LORA, MLA_HEADS, MLA_NOPE + MLA_V)
    k_nope, v = ukv[..., :MLA_NOPE], ukv[..., MLA_NOPE:]
    k_pad = jnp.concatenate([k_nope, jnp.zeros((MLA_KV_LORA, MLA_HEADS, HEAD_PAD - MLA_NOPE), F32)], axis=-1)
    w_ukv_r = jnp.concatenate([k_pad.reshape(MLA_KV_LORA, QP_WIDTH),
                               v.reshape(MLA_KV_LORA, MLA_HEADS * MLA_V)], axis=1).astype(BF16)

    eye = jnp.eye(MLA_ROPE, dtype=F32)
    place_head = jnp.concatenate([jnp.zeros((MLA_ROPE, MLA_NOPE), F32), eye,
                                  jnp.zeros((MLA_ROPE, HEAD_PAD - MLA_NOPE - MLA_ROPE), F32)], axis=1)
    place = jnp.tile(place_head, (1, MLA_HEADS)).astype(BF16)

    return dict(
        g1=g_norm1[0].reshape(1, D_MODEL), w_in=w_in_r, gq=g_q_lat[0].reshape(1, MLA_Q_LORA), w_uq=w_uq_r,
        gkv=g_kv_lat[0].reshape(1, MLA_KV_LORA), w_ukv=w_ukv_r, place=place,
        w_mla_up=w_mla_up[0].astype(BF16), w_sb_up=w_sb_up[0].astype(BF16), w_out=w_out[0].astype(BF16),
        g2=g_norm2[0].reshape(1, D_MODEL), wr_t=w_router[0].T.astype(BF16),
        br=b_router[0].reshape(N_EXPERTS, 1),
        sw1=shared_w1[0].astype(BF16), sw3=shared_w3[0].astype(BF16), sw2=shared_w2[0].astype(BF16),
        gf=g_final.reshape(1, D_MODEL))


def _dispatch_plan(idx_kt, blk):
    t = idx_kt.shape[1]
    n_assign = t * TOP_K
    nb = -(-n_assign // blk) + N_EXPERTS
    flat_e = idx_kt.T.reshape(-1)
    order = jnp.argsort(flat_e, stable=True).astype(jnp.int32)
    se = flat_e[order]
    bounds = jnp.searchsorted(se, jnp.arange(N_EXPERTS + 1, dtype=jnp.int32), side="left").astype(jnp.int32)
    start = bounds[:-1]
    counts = bounds[1:] - start
    padded = (counts + blk - 1) // blk * blk
    pad_end = jnp.cumsum(padded)
    pad_start = pad_end - padded
    blk_exp = jnp.minimum(jnp.searchsorted(pad_end, jnp.arange(nb, dtype=jnp.int32) * blk, side="right"),
                          N_EXPERTS - 1).astype(jnp.int32)
    r = jnp.arange(nb * blk, dtype=jnp.int32)
    e_r = blk_exp[r // blk]
    j = r - pad_start[e_r]
    valid = j < counts[e_r]
    src = jnp.where(valid, start[e_r] + j, 0)
    row_tok = jnp.where(valid, order[src] // TOP_K, 0).astype(jnp.int32)
    inv = jnp.argsort(order).astype(jnp.int32)
    pos = (pad_start[flat_e] + inv - start[flat_e]).astype(jnp.int32)
    return row_tok, blk_exp, pos


def _moe_rows(h2p, idx_kt, w1, w3, w2, gather_rows):
    row_tok, blk_exp, pos = _dispatch_plan(idx_kt, MOE_BLOCK)
    xs = gather_rows(h2p, row_tok)
    os_ = _experts(xs, blk_exp, w1, w3, w2, MOE_BLOCK)
    return gather_rows(os_, pos).reshape(h2p.shape[0], TOP_K * HALF)


def _forward(x_prompt, x_sample, cache_mla_ckv, cache_mla_krope, cache_sb_k, cache_sb_v, c_prompt, c_sample,
             w_ada, b_ada, moe_w1, moe_w3, moe_w2, wts, gather_rows, attn_block, route_block):
    bp, sp, _ = x_prompt.shape
    bs, ss, _ = x_sample.shape
    past_len = cache_mla_ckv.shape[2]

    mod = _ada(jnp.concatenate([c_prompt, c_sample], axis=0), w_ada[0], b_ada[0]).reshape(bp + bs, 6, D_MODEL)
    mod_p, mod_s = mod[:bp], mod[bp:]

    cos_p, sin_p = _rope_tables(jnp.arange(sp))
    (qp, kmla, vmla, sbq, sbk16, sbv16, gates, ckv_p, krope_p, sbk_p, sbv_p) = _in_proj(
        x_prompt, mod_p, wts, cos_p, sin_p, attn_block)
    x1_p, h2_p = _attention(qp, kmla, vmla, sbq, sbk16, sbv16, None, gates, x_prompt, mod_p, wts,
                            attn_block, attn_block)

    cos_s, sin_s = _rope_tables(past_len + jnp.arange(ss))
    (qs, kmla_s, vmla_s, sbq_s, sbk16_s, sbv16_s, gates_s, ckv_s, krope_s, sbk_s, sbv_s) = _in_proj(
        x_sample, mod_s, wts, cos_s, sin_s, ss)
    pkmla, pvmla = _kv_up(cache_mla_ckv[0], cache_mla_krope[0], wts["w_ukv"], wts["place"], attn_block)
    past = (pkmla, pvmla, cache_sb_k[0].reshape(bs, past_len, SB_WIDTH), cache_sb_v[0].reshape(bs, past_len, SB_WIDTH))
    x1_s, h2_s = _attention(qs, kmla_s, vmla_s, sbq_s, sbk16_s, sbv16_s, past, gates_s, x_sample, mod_s, wts,
                            ss, attn_block)

    tp, ts = bp * sp, bs * ss
    h2_all = jnp.concatenate([h2_p.reshape(tp, HALF), h2_s.reshape(ts, HALF)], axis=0)
    idx_kt, wt_kt = _route(h2_all, wts["wr_t"], wts["br"], route_block)
    gathered = _moe_rows(h2_all, idx_kt, moe_w1[0], moe_w3[0], moe_w2[0], gather_rows)
    wt = wt_kt.T
    y_p = _combine(gathered, wt, h2_all, x1_p.reshape(tp, D_MODEL), mod_p, wts, attn_block, 0, sp // attn_block)
    y_s = _combine(gathered, wt, h2_all, x1_s.reshape(ts, D_MODEL), mod_s, wts, ss, tp // ss, 1)

    heads = lambda a, b_, s_: a.reshape(1, b_, s_, SB_HEADS, SB_DIM)
    return (y_p.reshape(bp, sp, D_MODEL), y_s.reshape(bs, ss, D_MODEL),
            ckv_p[None], krope_p[None], heads(sbk_p, bp, sp), heads(sbv_p, bp, sp),
            ckv_s[None], krope_s[None], heads(sbk_s, bs, ss), heads(sbv_s, bs, ss))


def kernel(x_prompt, x_sample, cache_mla_ckv, cache_mla_krope, cache_sb_k, cache_sb_v, c_prompt, c_sample, w_ada, b_ada, g_norm1, w_in, g_q_lat, w_uq, g_kv_lat, w_ukv, w_mla_up, w_sb_up, w_out, g_norm2, w_router, b_router, moe_w1, moe_w3, moe_w2, shared_w1, shared_w3, shared_w2, g_final):
    wts = _prep_weights(g_norm1, w_in, g_q_lat, w_uq, g_kv_lat, w_ukv, w_mla_up, w_sb_up, w_out, g_norm2,
                        w_router, b_router, shared_w1, shared_w3, shared_w2, g_final)
    gather_rows = functools.partial(_sc_gather_rows, chunk=SC_CHUNK)
    return _forward(x_prompt, x_sample, cache_mla_ckv, cache_mla_krope, cache_sb_k, cache_sb_v, c_prompt, c_sample,
                    w_ada, b_ada, moe_w1, moe_w3, moe_w2, wts, gather_rows, ATTN_BLOCK, ATTN_BLOCK)
```

```python
import functools

import jax
import jax.numpy as jnp
from jax import lax
from jax.experimental import pallas as pl
from jax.experimental.pallas import tpu as pltpu
from jax.experimental.pallas import tpu_sc as plsc

F32 = jnp.float32
BF16 = jnp.bfloat16

D_MODEL = 1024
NORM_EPS = 1e-6
CHUNK = 64
MLA_HEADS = 8
MLA_NOPE = 64
MLA_ROPE = 32
MLA_V = 64
MLA_Q_LORA = 384
MLA_KV_LORA = 256
ROPE_THETA = 10000.0
SB_HEADS = 8
SB_DIM = 64
SB_WIDTH = SB_HEADS * SB_DIM
N_EXPERTS = 256
TOP_K = 8
N_GROUPS = 8
TOPK_GROUPS = 4
GROUP_SIZE = N_EXPERTS // N_GROUPS
EXPERT_DIM = 256
ROUTED_SCALE = 2.5

LANES = 128
SC_CORES = 2
SC_SUBCORES = 16
SC_WORKERS = SC_CORES * SC_SUBCORES
VMEM_LIMIT = 56 * 1024 * 1024

HEAD_PAD = LANES
QP_WIDTH = MLA_HEADS * HEAD_PAD
HALF = D_MODEL // 2

C_QLAT = 0
C_KV = C_QLAT + MLA_Q_LORA
C_SBQ = C_KV + MLA_KV_LORA
C_SBK = C_SBQ + SB_WIDTH
C_SBV = C_SBK + SB_WIDTH
C_GATE = C_SBV + SB_WIDTH
C_KR = C_GATE + 2 * D_MODEL
C_END = C_KR + 2 * LANES

MOE_BLOCK = 512
ATTN_BLOCK = 256
SC_CHUNK = 64


def _rms(x):
    return x * lax.rsqrt(jnp.mean(x * x, axis=-1, keepdims=True) + NORM_EPS)


def _silu(x):
    return x * jax.nn.sigmoid(x)


def _pack_halves(lo, hi):
    lo_bits = lax.bitcast_convert_type(lo.astype(BF16).astype(F32), jnp.uint32) >> 16
    hi_bits = lax.bitcast_convert_type(hi.astype(BF16).astype(F32), jnp.uint32) & jnp.uint32(0xFFFF0000)
    return lax.bitcast_convert_type(lo_bits | hi_bits, jnp.int32)


def _unpack_halves(p):
    u = lax.bitcast_convert_type(p, jnp.uint32)
    lo = lax.bitcast_convert_type(u << 16, F32)
    hi = lax.bitcast_convert_type(u & jnp.uint32(0xFFFF0000), F32)
    return lo, hi


def _dot(a, b):
    return jnp.dot(a, b, preferred_element_type=F32)


def _dot_nt(a, b):
    return lax.dot_general(a, b, (((1,), (1,)), ((), ())), preferred_element_type=F32)


def _ada_kernel(c_ref, w_ref, b_ref, o_ref):
    c = c_ref[...]
    o_ref[...] = _dot(_silu(c).astype(BF16), w_ref[...].astype(BF16)) + b_ref[...]


def _ada(c, w_ada, b_ada):
    n = c.shape[0]
    width = w_ada.shape[1]
    return pl.pallas_call(
        _ada_kernel,
        grid=(width // D_MODEL,),
        in_specs=[pl.BlockSpec((n, D_MODEL), lambda j: (0, 0)),
                  pl.BlockSpec((D_MODEL, D_MODEL), lambda j: (0, j)),
                  pl.BlockSpec((1, D_MODEL), lambda j: (0, j))],
        out_specs=pl.BlockSpec((n, D_MODEL), lambda j: (0, j)),
        out_shape=jax.ShapeDtypeStruct((n, width), F32),
        name="ada",
    )(c, w_ada, b_ada.reshape(1, width))


def _in_kernel(x_ref, mod_ref, g1_ref, win_ref, gq_ref, wuq_ref, gkv_ref, wukv_ref, cos_ref, sin_ref,
               qp_ref, kmla_ref, vmla_ref, sbq_ref, sbk16_ref, sbv16_ref, gates_ref,
               ckv_ref, krope_ref, sbk_ref, sbv_ref):
    x = x_ref[0]
    mod = mod_ref[0]
    h = _rms(x) * g1_ref[...] * (1.0 + mod[1:2]) + mod[0:1]
    hb = h.astype(BF16)

    def seg(a, b):
        return _dot(hb, win_ref[:, a:b])

    cos = cos_ref[...]
    sin = sin_ref[...]
    cos8 = jnp.tile(cos, (1, MLA_HEADS))
    sin8 = jnp.tile(sin, (1, MLA_HEADS))

    qn = (_rms(seg(C_QLAT, C_KV)) * gq_ref[...]).astype(BF16)
    q2 = _dot(qn, wuq_ref[...])
    qp_ref[0] = (q2[:, :QP_WIDTH] * cos8 + q2[:, QP_WIDTH:] * sin8).astype(BF16)

    ckv = _rms(seg(C_KV, C_SBQ)) * gkv_ref[...]
    ckv_ref[0] = ckv
    kv = _dot(ckv.astype(BF16), wukv_ref[...])
    kr2 = seg(C_KR, C_END)
    krp = kr2[:, :LANES] * cos + kr2[:, LANES:] * sin
    krope_ref[0] = krp[:, MLA_NOPE:MLA_NOPE + MLA_ROPE]
    kmla_ref[0] = (kv[:, :QP_WIDTH] + jnp.tile(krp, (1, MLA_HEADS))).astype(BF16)
    vmla_ref[0] = kv[:, QP_WIDTH:].astype(BF16)

    sbq_ref[0] = seg(C_SBQ, C_SBK).astype(BF16)
    sbk = seg(C_SBK, C_SBV)
    sbk_ref[0] = sbk
    sbk16_ref[0] = sbk.astype(BF16)
    sbv = seg(C_SBV, C_GATE)
    sbv_ref[0] = sbv
    sbv16_ref[0] = sbv.astype(BF16)
    gates_ref[0] = jax.nn.sigmoid(seg(C_GATE, C_KR)).astype(BF16)


def _in_proj(x, mod, wts, cos_t, sin_t, tm):
    b, s, _ = x.shape
    ns = s // tm
    tok = lambda w: pl.BlockSpec((1, tm, w), lambda i, j: (i, j, 0))
    full = lambda a: pl.BlockSpec(a.shape, lambda i, j: (0,) * a.ndim)
    out_widths = [(QP_WIDTH, BF16), (QP_WIDTH, BF16), (SB_WIDTH, BF16), (SB_WIDTH, BF16), (SB_WIDTH, BF16),
                  (SB_WIDTH, BF16), (2 * D_MODEL, BF16), (MLA_KV_LORA, F32), (MLA_ROPE, F32),
                  (SB_WIDTH, F32), (SB_WIDTH, F32)]
    return pl.pallas_call(
        _in_kernel,
        grid=(b, ns),
        in_specs=[tok(D_MODEL),
                  pl.BlockSpec((1, 6, D_MODEL), lambda i, j: (i, 0, 0)),
                  full(wts["g1"]), full(wts["w_in"]), full(wts["gq"]), full(wts["w_uq"]),
                  full(wts["gkv"]), full(wts["w_ukv"]),
                  pl.BlockSpec((tm, LANES), lambda i, j: (j, 0)),
                  pl.BlockSpec((tm, LANES), lambda i, j: (j, 0))],
        out_specs=[tok(w) for w, _ in out_widths],
        out_shape=[jax.ShapeDtypeStruct((b, s, w), dt) for w, dt in out_widths],
        compiler_params=pltpu.CompilerParams(dimension_semantics=("parallel", "parallel"),
                                             vmem_limit_bytes=VMEM_LIMIT),
        name="in_proj",
    )(x, mod, wts["g1"], wts["w_in"], wts["gq"], wts["w_uq"], wts["gkv"], wts["w_ukv"], cos_t, sin_t)


def _kvup_kernel(ckv_ref, kr_ref, wukv_ref, place_ref, kmla_ref, vmla_ref):
    kv = _dot(ckv_ref[0].astype(BF16), wukv_ref[...])
    kr = _dot(kr_ref[0].astype(BF16), place_ref[...])
    kmla_ref[0] = (kv[:, :QP_WIDTH] + kr).astype(BF16)
    vmla_ref[0] = kv[:, QP_WIDTH:].astype(BF16)


def _kv_up(ckv, krope, w_ukv_r, place, tm):
    b, p, _ = ckv.shape
    return pl.pallas_call(
        _kvup_kernel,
        grid=(b, p // tm),
        in_specs=[pl.BlockSpec((1, tm, MLA_KV_LORA), lambda i, j: (i, j, 0)),
                  pl.BlockSpec((1, tm, MLA_ROPE), lambda i, j: (i, j, 0)),
                  pl.BlockSpec(w_ukv_r.shape, lambda i, j: (0, 0)),
                  pl.BlockSpec(place.shape, lambda i, j: (0, 0))],
        out_specs=[pl.BlockSpec((1, tm, QP_WIDTH), lambda i, j: (i, j, 0)),
                   pl.BlockSpec((1, tm, SB_WIDTH), lambda i, j: (i, j, 0))],
        out_shape=[jax.ShapeDtypeStruct((b, p, QP_WIDTH), BF16), jax.ShapeDtypeStruct((b, p, SB_WIDTH), BF16)],
        compiler_params=pltpu.CompilerParams(dimension_semantics=("parallel", "parallel")),
        name="kv_up",
    )(ckv, krope, w_ukv_r, place)


def _attn_kernel(*refs, tq, past_len, past_blk):
    has_past = past_len > 0
    if has_past:
        (qp_ref, kmla_ref, vmla_ref, sbq_ref, sbk_ref, sbv_ref, pkmla_ref, pvmla_ref, psbk_ref, psbv_ref,
         gates_ref, x_ref, mod_ref, wmu_ref, wsu_ref, wo_ref, g2_ref, x1_ref, h2_ref) = refs
    else:
        (qp_ref, kmla_ref, vmla_ref, sbq_ref, sbk_ref, sbv_ref,
         gates_ref, x_ref, mod_ref, wmu_ref, wsu_ref, wo_ref, g2_ref, x1_ref, h2_ref) = refs
    i = pl.program_id(1)
    n_past = past_len // past_blk

    lane = lax.broadcasted_iota(jnp.int32, (1, LANES), 1)
    half_masks = (lane < MLA_V, lane >= MLA_V)
    row = lax.broadcasted_iota(jnp.int32, (tq, tq), 0)
    col = lax.broadcasted_iota(jnp.int32, (tq, tq), 1)
    base = past_len + i * tq
    chunk_mask = ((base + col) // CHUNK) <= ((base + row) // CHUNK)
    causal_mask = col < row

    def tri(n):
        r = lax.broadcasted_iota(jnp.int32, (n, n), 0)
        c = lax.broadcasted_iota(jnp.int32, (n, n), 1)
        return jnp.where(r > c, 1.0, 0.0).astype(BF16)

    tri_new = tri(tq)
    tri_past = tri(past_blk) if has_past else None

    def mla_block(q_h, k, v, carry, mask):
        m, l, acc = carry
        s = _dot_nt(q_h, k)
        if mask is not None:
            s = jnp.where(mask, s, -jnp.inf)
        m_new = jnp.maximum(m, jnp.max(s, axis=-1, keepdims=True))
        p = jnp.exp(s - m_new)
        alpha = jnp.exp(m - m_new)
        l = alpha * l + jnp.sum(p, axis=-1, keepdims=True)
        acc = alpha * acc + _dot(p.astype(BF16), v)
        return m_new, l, acc

    def sb_block(q_h, k, v, carry, mask, tri_m):
        c, acc = carry
        z = _dot_nt(q_h, k)
        soft = jnp.log(1.0 + jnp.exp(-jnp.abs(z)))
        log_sig = jnp.minimum(z, 0.0) - soft
        log_keep = log_sig - z
        if mask is not None:
            log_keep = jnp.where(mask, log_keep, 0.0)
        hi = log_keep.astype(BF16)
        lo = (log_keep - hi.astype(F32)).astype(BF16)
        after = _dot(hi, tri_m) + _dot(lo, tri_m) + c
        a = jnp.exp(log_sig + after)
        if mask is not None:
            a = jnp.where(mask, a, 0.0)
        acc = acc + _dot(a.astype(BF16), v)
        c = c + jnp.sum(log_keep, axis=-1, keepdims=True)
        return c, acc

    def load(ref, start, size, c0):
        return ref[0, pl.ds(start, size), c0:c0 + LANES].astype(BF16)

    st_diag = pl.multiple_of(i * tq, tq)
    o_mla = []
    o_sb = []
    for pair in range(MLA_HEADS // 2):
        vcol = pair * LANES
        kcols = [(2 * pair + sub) * HEAD_PAD for sub in range(2)]
        q_m = [qp_ref[0, :, kc:kc + HEAD_PAD] for kc in kcols]
        q_s = [jnp.where(half_masks[sub], sbq_ref[0, :, vcol:vcol + LANES], 0) for sub in range(2)]

        def pair_blocks(cr, mla_src, mla_start, sb_src, sb_start, size, masks, tri_m,
                        vcol=vcol, kcols=kcols, q_m=q_m, q_s=q_s):
            k_ref, v_ref = mla_src
            sk_ref, sv_ref = sb_src
            v_m = load(v_ref, mla_start, size, vcol)
            k_s = load(sk_ref, sb_start, size, vcol)
            v_s = load(sv_ref, sb_start, size, vcol)
            out = []
            for sub in range(2):
                keep = half_masks[sub]
                out.append(mla_block(q_m[sub], load(k_ref, mla_start, size, kcols[sub]),
                                     jnp.where(keep, v_m, 0), cr[2 * sub], masks[0]))
                out.append(sb_block(q_s[sub], k_s, jnp.where(keep, v_s, 0), cr[2 * sub + 1], masks[1], tri_m))
            return tuple(out)

        mla0 = (jnp.full((tq, 1), -jnp.inf, F32), jnp.zeros((tq, 1), F32), jnp.zeros((tq, LANES), F32))
        sb0 = (jnp.zeros((tq, 1), F32), jnp.zeros((tq, LANES), F32))
        new_src = ((kmla_ref, vmla_ref), (sbk_ref, sbv_ref))
        carry = pair_blocks((mla0, sb0, mla0, sb0), new_src[0], st_diag, new_src[1], st_diag, tq,
                            (chunk_mask, causal_mask), tri_new)

        def new_step(t, cr, pair_blocks=pair_blocks):
            fwd = pl.multiple_of(t * tq, tq)
            rev = pl.multiple_of((i - 1 - t) * tq, tq)
            return pair_blocks(cr, new_src[0], fwd, new_src[1], rev, tq, (None, None), tri_new)
        carry = lax.fori_loop(0, i, new_step, carry)
        if has_past:
            past_src = ((pkmla_ref, pvmla_ref), (psbk_ref, psbv_ref))

            def past_step(t, cr, pair_blocks=pair_blocks):
                fwd = pl.multiple_of(t * past_blk, past_blk)
                rev = pl.multiple_of((n_past - 1 - t) * past_blk, past_blk)
                return pair_blocks(cr, past_src[0], fwd, past_src[1], rev, past_blk, (None, None), tri_past)
            carry = lax.fori_loop(0, n_past, past_step, carry)
        (_, l0, a0), (_, s0), (_, l1, a1), (_, s1) = carry
        o_mla.append(a0 / l0 + a1 / l1)
        o_sb.append(s0 + s1)

    u_mla = _dot(jnp.concatenate(o_mla, axis=1).astype(BF16), wmu_ref[...])
    u_sb = _dot(jnp.concatenate(o_sb, axis=1).astype(BF16), wsu_ref[...])
    gates = gates_ref[0]
    merged = gates[:, :D_MODEL].astype(F32) * u_mla + gates[:, D_MODEL:].astype(F32) * u_sb
    mix = _dot(merged.astype(BF16), wo_ref[...])
    mod = mod_ref[0]
    x1 = x_ref[0] + mod[2:3] * mix
    x1_ref[0] = x1
    h2 = _rms(x1) * g2_ref[...] * (1.0 + mod[4:5]) + mod[3:4]
    h2_ref[0] = _pack_halves(h2[:, :HALF], h2[:, HALF:])


def _attention(qp, kmla, vmla, sbq, sbk16, sbv16, past, gates, x, mod, wts, tq, past_blk):
    b, s, _ = x.shape
    past_len = 0 if past is None else past[0].shape[1]
    tok = lambda w: pl.BlockSpec((1, tq, w), lambda i, j: (i, j, 0))
    seq = lambda a: pl.BlockSpec((1,) + a.shape[1:], lambda i, j: (i, 0, 0))
    full = lambda a: pl.BlockSpec(a.shape, lambda i, j: (0,) * a.ndim)
    args = [qp, kmla, vmla, sbq, sbk16, sbv16]
    specs = [tok(QP_WIDTH), seq(kmla), seq(vmla), tok(SB_WIDTH), seq(sbk16), seq(sbv16)]
    if past is not None:
        args += list(past)
        specs += [seq(a) for a in past]
    args += [gates, x, mod, wts["w_mla_up"], wts["w_sb_up"], wts["w_out"], wts["g2"]]
    specs += [tok(2 * D_MODEL), tok(D_MODEL), pl.BlockSpec((1, 6, D_MODEL), lambda i, j: (i, 0, 0)),
              full(wts["w_mla_up"]), full(wts["w_sb_up"]), full(wts["w_out"]), full(wts["g2"])]
    return pl.pallas_call(
        functools.partial(_attn_kernel, tq=tq, past_len=past_len, past_blk=past_blk),
        grid=(b, s // tq),
        in_specs=specs,
        out_specs=[tok(D_MODEL), tok(HALF)],
        out_shape=[jax.ShapeDtypeStruct((b, s, D_MODEL), F32), jax.ShapeDtypeStruct((b, s, HALF), jnp.int32)],
        compiler_params=pltpu.CompilerParams(dimension_semantics=("parallel", "arbitrary"),
                                             vmem_limit_bytes=VMEM_LIMIT),
        name="attention",
    )(*args)


def _route_kernel(h2_ref, wr_ref, br_ref, idx_ref, wt_ref, rank_ref, cnt_ref, seen_ref):
    lo, hi = _unpack_halves(h2_ref[...])
    tm = lo.shape[0]
    logits = _dot_nt(wr_ref[:, :HALF], lo.astype(BF16)) + _dot_nt(wr_ref[:, HALF:], hi.astype(BF16))
    scores = jax.nn.sigmoid(logits)
    sel = scores + br_ref[...]
    neg = -jnp.inf

    grp = sel.reshape(N_GROUPS, GROUP_SIZE, tm)
    within = lax.broadcasted_iota(jnp.int32, grp.shape, 1)
    top1 = jnp.max(grp, axis=1, keepdims=True)
    first = jnp.min(jnp.where(grp == top1, within, GROUP_SIZE), axis=1, keepdims=True)
    top2 = jnp.max(jnp.where(within == first, neg, grp), axis=1, keepdims=True)
    gscore = (top1 + top2).reshape(N_GROUPS, tm)

    gid = lax.broadcasted_iota(jnp.int32, gscore.shape, 0)
    chosen = jnp.zeros(gscore.shape, jnp.bool_)
    for _ in range(TOPK_GROUPS):
        best = jnp.max(gscore, axis=0, keepdims=True)
        pick = jnp.min(jnp.where(gscore == best, gid, N_GROUPS), axis=0, keepdims=True)
        hit = gid == pick
        chosen = jnp.logical_or(chosen, hit)
        gscore = jnp.where(hit, neg, gscore)
    chosen3 = jnp.broadcast_to(chosen.reshape(N_GROUPS, 1, tm), grp.shape)
    cand = jnp.where(chosen3, grp, neg).reshape(N_EXPERTS, tm)

    eid = lax.broadcasted_iota(jnp.int32, cand.shape, 0)
    picks = []
    weights = []
    for _ in range(TOP_K):
        best = jnp.max(cand, axis=0, keepdims=True)
        pick = jnp.min(jnp.where(cand == best, eid, N_EXPERTS), axis=0, keepdims=True)
        hit = eid == pick
        weights.append(jnp.sum(jnp.where(hit, scores, 0.0), axis=0, keepdims=True))
        picks.append(pick)
        cand = jnp.where(hit, neg, cand)
    w = jnp.concatenate(weights, axis=0)
    idx_ref[...] = jnp.concatenate(picks, axis=0)
    wt_ref[...] = w / (jnp.sum(w, axis=0, keepdims=True) + 1e-20) * ROUTED_SCALE

    @pl.when(pl.program_id(0) == 0)
    def _():
        seen_ref[...] = jnp.zeros_like(seen_ref)

    onehot = jnp.zeros(cand.shape, F32)
    for pick in picks:
        onehot = onehot + jnp.where(eid == pick, 1.0, 0.0)
    src = lax.broadcasted_iota(jnp.int32, (tm, tm), 0)
    dst = lax.broadcasted_iota(jnp.int32, (tm, tm), 1)
    earlier = jnp.where(src < dst, 1.0, 0.0).astype(BF16)
    before = _dot(onehot.astype(BF16), earlier) + seen_ref[...]
    rank_ref[...] = jnp.concatenate(
        [jnp.sum(jnp.where(eid == pick, before, 0.0), axis=0, keepdims=True) for pick in picks],
        axis=0).astype(jnp.int32)
    seen = seen_ref[...] + jnp.sum(onehot, axis=1, keepdims=True)
    seen_ref[...] = seen
    cnt_ref[...] = seen.astype(jnp.int32)


def _route(h2p, wr_t, br, tm):
    t = h2p.shape[0]
    kt = lambda: pl.BlockSpec((TOP_K, tm), lambda i: (0, i))
    return pl.pallas_call(
        _route_kernel,
        grid=(t // tm,),
        in_specs=[pl.BlockSpec((tm, HALF), lambda i: (i, 0)),
                  pl.BlockSpec(wr_t.shape, lambda i: (0, 0)),
                  pl.BlockSpec(br.shape, lambda i: (0, 0))],
        out_specs=[kt(), kt(), kt(), pl.BlockSpec((N_EXPERTS, 1), lambda i: (0, 0))],
        out_shape=[jax.ShapeDtypeStruct((TOP_K, t), jnp.int32), jax.ShapeDtypeStruct((TOP_K, t), F32),
                   jax.ShapeDtypeStruct((TOP_K, t), jnp.int32), jax.ShapeDtypeStruct((N_EXPERTS, 1), jnp.int32)],
        scratch_shapes=[pltpu.VMEM((N_EXPERTS, 1), F32)],
        compiler_params=pltpu.CompilerParams(dimension_semantics=("arbitrary",)),
        name="route",
    )(h2p, wr_t, br)


def _position_kernel(idx_ref, rank_ref, start_ref, pos_ref):
    idx = idx_ref[...]
    eid = lax.broadcasted_iota(jnp.int32, (N_EXPERTS, idx.shape[1]), 0)
    start = start_ref[...]
    base = jnp.concatenate(
        [jnp.sum(jnp.where(eid == idx[k:k + 1, :], start, 0.0), axis=0, keepdims=True) for k in range(TOP_K)],
        axis=0)
    pos_ref[...] = base.astype(jnp.int32) + rank_ref[...]


def _positions(idx_kt, rank_kt, pad_start, tm):
    t = idx_kt.shape[1]
    kt = lambda: pl.BlockSpec((TOP_K, tm), lambda i: (0, i))
    return pl.pallas_call(
        _position_kernel,
        grid=(t // tm,),
        in_specs=[kt(), kt(), pl.BlockSpec((N_EXPERTS, 1), lambda i: (0, 0))],
        out_specs=kt(),
        out_shape=jax.ShapeDtypeStruct((TOP_K, t), jnp.int32),
        compiler_params=pltpu.CompilerParams(dimension_semantics=("parallel",)),
        name="positions",
    )(idx_kt, rank_kt, pad_start.astype(F32).reshape(N_EXPERTS, 1))


def _sc_mesh():
    return plsc.VectorSubcoreMesh(core_axis_name="c", subcore_axis_name="s",
                                  num_cores=SC_CORES, num_subcores=SC_SUBCORES)


def _sc_scatter_rows(rows, pos, n_out, chunk):
    t, width = rows.shape
    copies = pos.shape[0] // t
    spans = SC_WORKERS // copies
    per_worker = t // spans
    n_chunks = per_worker // chunk
    assert copies * spans == SC_WORKERS and per_worker * spans == t and n_chunks * chunk == per_worker

    @functools.partial(
        pl.kernel, mesh=_sc_mesh(),
        out_type=jax.ShapeDtypeStruct((n_out, width), rows.dtype),
        scratch_types=[pltpu.VMEM((chunk,), jnp.int32), pltpu.VMEM((chunk, width), rows.dtype)],
    )
    def scatter(rows_hbm, pos_hbm, out_hbm, idx_v, rows_v):
        wid = lax.axis_index("s") * SC_CORES + lax.axis_index("c")
        copy = wid % copies
        base = (wid // copies) * per_worker

        @pl.loop(0, n_chunks)
        def _(c):
            off = pl.multiple_of(base + c * chunk, 8)
            pltpu.sync_copy(pos_hbm.at[pl.ds(pl.multiple_of(copy * t + off, 8), chunk)], idx_v)
            pltpu.sync_copy(rows_hbm.at[pl.ds(off, chunk)], rows_v)
            pltpu.sync_copy(rows_v, out_hbm.at[idx_v])

    return scatter(rows, pos)


def _sc_gather_rows(table, idx, chunk):
    n_rows = idx.shape[0]
    width = table.shape[1]
    per_worker = n_rows // SC_WORKERS
    n_chunks = per_worker // chunk
    assert per_worker * SC_WORKERS == n_rows and n_chunks * chunk == per_worker
    mesh = _sc_mesh()

    @functools.partial(
        pl.kernel, mesh=mesh,
        out_type=jax.ShapeDtypeStruct((n_rows, width), table.dtype),
        scratch_types=[pltpu.VMEM((chunk,), jnp.int32), pltpu.VMEM((chunk, width), table.dtype),
                       pltpu.SemaphoreType.DMA],
    )
    def gather(table_hbm, idx_hbm, out_hbm, idx_v, rows_v, sem):
        wid = lax.axis_index("s") * SC_CORES + lax.axis_index("c")
        base = wid * per_worker

        @pl.loop(0, n_chunks)
        def _(c):
            off = pl.multiple_of(base + c * chunk, 8)
            pltpu.sync_copy(idx_hbm.at[pl.ds(off, chunk)], idx_v)
            pltpu.async_copy(table_hbm.at[idx_v], rows_v, sem).wait()
            pltpu.sync_copy(rows_v, out_hbm.at[pl.ds(off, chunk)])

    return gather(table, idx)


def _expert_kernel(be_ref, nv_ref, nu_ref, xs_ref, w1_ref, w3_ref, w2_ref, os_ref, wb1, wb3, wb2):
    i = pl.program_id(0)

    @pl.when(i < nu_ref[0])
    def _():
        @pl.when(jnp.logical_or(i == 0, be_ref[i] != be_ref[jnp.maximum(i - 1, 0)]))
        def _():
            wb1[...] = w1_ref[0].astype(BF16)
            wb3[...] = w3_ref[0].astype(BF16)
            wb2[...] = w2_ref[0].astype(BF16)

        packed = xs_ref[...]
        live = lax.broadcasted_iota(jnp.int32, packed.shape, 0) < nv_ref[i]
        lo, hi = _unpack_halves(jnp.where(live, packed, 0))
        lo = lo.astype(BF16)
        hi = hi.astype(BF16)
        a = _dot(lo, wb1[:HALF, :]) + _dot(hi, wb1[HALF:, :])
        b = _dot(lo, wb3[:HALF, :]) + _dot(hi, wb3[HALF:, :])
        o = _dot((_silu(a) * b).astype(BF16), wb2[...])
        os_ref[...] = _pack_halves(o[:, :HALF], o[:, HALF:])


def _experts(xs, blk_exp, blk_valid, n_used, w1, w3, w2, blk):
    rows = xs.shape[0]
    nb = rows // blk
    last = lambda i, nu: jnp.minimum(i, nu[0] - 1)
    grid_spec = pltpu.PrefetchScalarGridSpec(
        num_scalar_prefetch=3,
        grid=(nb,),
        in_specs=[pl.BlockSpec((blk, HALF), lambda i, be, nv, nu: (last(i, nu), 0)),
                  pl.BlockSpec((1, D_MODEL, EXPERT_DIM), lambda i, be, nv, nu: (be[last(i, nu)], 0, 0)),
                  pl.BlockSpec((1, D_MODEL, EXPERT_DIM), lambda i, be, nv, nu: (be[last(i, nu)], 0, 0)),
                  pl.BlockSpec((1, EXPERT_DIM, D_MODEL), lambda i, be, nv, nu: (be[last(i, nu)], 0, 0))],
        out_specs=pl.BlockSpec((blk, HALF), lambda i, be, nv, nu: (last(i, nu), 0)),
        scratch_shapes=[pltpu.VMEM((D_MODEL, EXPERT_DIM), BF16), pltpu.VMEM((D_MODEL, EXPERT_DIM), BF16),
                        pltpu.VMEM((EXPERT_DIM, D_MODEL), BF16)],
    )
    return pl.pallas_call(
        _expert_kernel,
        grid_spec=grid_spec,
        out_shape=jax.ShapeDtypeStruct((rows, HALF), jnp.int32),
        compiler_params=pltpu.CompilerParams(dimension_semantics=("arbitrary",), vmem_limit_bytes=VMEM_LIMIT),
        name="experts",
    )(blk_exp, blk_valid, n_used, xs, w1, w3, w2)


def _combine_kernel(g_ref, wt_ref, h2_ref, x1_ref, mod_ref, sw1_ref, sw3_ref, sw2_ref, gf_ref, y_ref):
    wt = wt_ref[...]
    lo_acc = None
    for k in range(TOP_K):
        lo, hi = _unpack_halves(g_ref[k])
        wk = wt[:, k:k + 1]
        lo_acc = wk * lo if lo_acc is None else lo_acc + wk * lo
        hi_acc = wk * hi if k == 0 else hi_acc + wk * hi
    routed = jnp.concatenate([lo_acc, hi_acc], axis=1)
    lo, hi = _unpack_halves(h2_ref[...])
    lo = lo.astype(BF16)
    hi = hi.astype(BF16)
    a = _dot(lo, sw1_ref[:HALF, :]) + _dot(hi, sw1_ref[HALF:, :])
    b = _dot(lo, sw3_ref[:HALF, :]) + _dot(hi, sw3_ref[HALF:, :])
    shared = _dot((_silu(a) * b).astype(BF16), sw2_ref[...])
    mod = mod_ref[0]
    x2 = x1_ref[...] + mod[5:6] * (routed + shared)
    y_ref[...] = _rms(x2) * gf_ref[...]


def _combine(gathered, wt, h2p, x1, mod, wts, tm, tile0, tiles_per_batch):
    t = x1.shape[0]
    full = lambda a: pl.BlockSpec(a.shape, lambda i: (0,) * a.ndim)
    shifted = lambda w: pl.BlockSpec((tm, w), lambda i: (i + tile0, 0))
    return pl.pallas_call(
        _combine_kernel,
        grid=(t // tm,),
        in_specs=[pl.BlockSpec((TOP_K, tm, HALF), lambda i: (0, i + tile0, 0)), shifted(TOP_K), shifted(HALF),
                  pl.BlockSpec((tm, D_MODEL), lambda i: (i, 0)),
                  pl.BlockSpec((1, 6, D_MODEL), lambda i: (i // tiles_per_batch, 0, 0)),
                  full(wts["sw1"]), full(wts["sw3"]), full(wts["sw2"]), full(wts["gf"])],
        out_specs=pl.BlockSpec((tm, D_MODEL), lambda i: (i, 0)),
        out_shape=jax.ShapeDtypeStruct((t, D_MODEL), F32),
        compiler_params=pltpu.CompilerParams(dimension_semantics=("parallel",), vmem_limit_bytes=VMEM_LIMIT),
        name="combine",
    )(gathered, wt, h2p, x1, mod, wts["sw1"], wts["sw3"], wts["sw2"], wts["gf"])


def _rope_tables(pos):
    half = MLA_ROPE // 2
    inv_freq = ROPE_THETA ** (-jnp.arange(half, dtype=F32) / half)
    ang = pos.astype(F32)[:, None] * inv_freq
    cos, sin = jnp.cos(ang), jnp.sin(ang)
    n = pos.shape[0]
    ones = jnp.ones((n, MLA_NOPE), F32)
    z_nope = jnp.zeros((n, MLA_NOPE), F32)
    z_pad = jnp.zeros((n, HEAD_PAD - MLA_NOPE - MLA_ROPE), F32)
    return (jnp.concatenate([ones, cos, cos, z_pad], axis=1),
            jnp.concatenate([z_nope, -sin, sin, z_pad], axis=1))


def _prep_weights(g_norm1, w_in, g_q_lat, w_uq, g_kv_lat, w_ukv, w_mla_up, w_sb_up, w_out, g_norm2,
                  w_router, b_router, shared_w1, shared_w3, shared_w2, g_final):
    half = MLA_ROPE // 2
    w = w_in[0]
    o = 0
    parts = {}
    for name, width in (("qlat", MLA_Q_LORA), ("kv", MLA_KV_LORA), ("kr", MLA_ROPE), ("sbq", SB_WIDTH),
                        ("sbk", SB_WIDTH), ("sbv", SB_WIDTH), ("gm", D_MODEL), ("gs", D_MODEL)):
        parts[name] = w[:, o:o + width]
        o += width
    kr = parts["kr"]
    z_nope = jnp.zeros((D_MODEL, MLA_NOPE), F32)
    z_pad = jnp.zeros((D_MODEL, HEAD_PAD - MLA_NOPE - MLA_ROPE), F32)
    kr_seg = jnp.concatenate([z_nope, kr, z_pad, z_nope, kr[:, half:], kr[:, :half], z_pad], axis=1)
    w_in_r = jnp.concatenate([parts["qlat"], parts["kv"], parts["sbq"] * (SB_DIM ** -0.5), parts["sbk"],
                              parts["sbv"], parts["gm"], parts["gs"], kr_seg], axis=1).astype(BF16)

    scale = (MLA_NOPE + MLA_ROPE) ** -0.5
    uq = w_uq[0].reshape(MLA_Q_LORA, MLA_HEADS, MLA_NOPE + MLA_ROPE) * scale
    nope, rope = uq[..., :MLA_NOPE], uq[..., MLA_NOPE:]
    zq_nope = jnp.zeros_like(nope)
    zq_pad = jnp.zeros((MLA_Q_LORA, MLA_HEADS, HEAD_PAD - MLA_NOPE - MLA_ROPE), F32)
    main = jnp.concatenate([nope, rope, zq_pad], axis=-1).reshape(MLA_Q_LORA, QP_WIDTH)
    swapped = jnp.concatenate([zq_nope, rope[..., half:], rope[..., :half], zq_pad], axis=-1)
    w_uq_r = jnp.concatenate([main, swapped.reshape(MLA_Q_LORA, QP_WIDTH)], axis=1).astype(BF16)

    ukv = w_ukv[0].reshape(MLA_KV_LORA, MLA_HEADS, MLA_NOPE + MLA_V)
    k_nope, v = ukv[..., :MLA_NOPE], ukv[..., MLA_NOPE:]
    k_pad = jnp.concatenate([k_nope, jnp.zeros((MLA_KV_LORA, MLA_HEADS, HEAD_PAD - MLA_NOPE), F32)], axis=-1)
    w_ukv_r = jnp.concatenate([k_pad.reshape(MLA_KV_LORA, QP_WIDTH),
                               v.reshape(MLA_KV_LORA, MLA_HEADS * MLA_V)], axis=1).astype(BF16)

    eye = jnp.eye(MLA_ROPE, dtype=F32)
    place_head = jnp.concatenate([jnp.zeros((MLA_ROPE, MLA_NOPE), F32), eye,
                                  jnp.zeros((MLA_ROPE, HEAD_PAD - MLA_NOPE - MLA_ROPE), F32)], axis=1)
    place = jnp.tile(place_head, (1, MLA_HEADS)).astype(BF16)

    return dict(
        g1=g_norm1[0].reshape(1, D_MODEL), w_in=w_in_r, gq=g_q_lat[0].reshape(1, MLA_Q_LORA), w_uq=w_uq_r,
        gkv=g_kv_lat[0].reshape(1, MLA_KV_LORA), w_ukv=w_ukv_r, place=place,
        w_mla_up=w_mla_up[0].astype(BF16), w_sb_up=w_sb_up[0].astype(BF16), w_out=w_out[0].astype(BF16),
        g2=g_norm2[0].reshape(1, D_MODEL), wr_t=w_router[0].T.astype(BF16),
        br=b_router[0].reshape(N_EXPERTS, 1),
        sw1=shared_w1[0].astype(BF16), sw3=shared_w3[0].astype(BF16), sw2=shared_w2[0].astype(BF16),
        gf=g_final.reshape(1, D_MODEL))


def _block_plan(counts, nb, blk):
    padded = (counts + blk - 1) // blk * blk
    pad_end = jnp.cumsum(padded)
    pad_start = pad_end - padded
    first_row = jnp.arange(nb, dtype=jnp.int32) * blk
    blk_exp = jnp.minimum(jnp.sum((pad_end[None, :] <= first_row[:, None]).astype(jnp.int32), axis=1),
                          N_EXPERTS - 1)
    own = blk_exp[:, None] == jnp.arange(N_EXPERTS, dtype=jnp.int32)[None, :]
    seg_end = jnp.sum(jnp.where(own, (pad_start + counts)[None, :], 0), axis=1)
    blk_valid = jnp.clip(seg_end - first_row, 0, blk).astype(jnp.int32)
    n_used = (pad_end[-1:] // blk).astype(jnp.int32)
    return pad_start, blk_exp.astype(jnp.int32), blk_valid, n_used


def _moe_rows(h2p, idx_kt, rank_kt, counts, w1, w3, w2, scatter_rows, gather_rows, blk, tm):
    t = h2p.shape[0]
    nb = -(-t * TOP_K // blk) + N_EXPERTS
    pad_start, blk_exp, blk_valid, n_used = _block_plan(counts.reshape(N_EXPERTS), nb, blk)
    pos = _positions(idx_kt, rank_kt, pad_start, tm).reshape(TOP_K * t)
    xs = scatter_rows(h2p, pos, nb * blk)
    os_ = _experts(xs, blk_exp, blk_valid, n_used, w1, w3, w2, blk)
    return gather_rows(os_, pos).reshape(TOP_K, t, HALF)


def _forward(x_prompt, x_sample, cache_mla_ckv, cache_mla_krope, cache_sb_k, cache_sb_v, c_prompt, c_sample,
             w_ada, b_ada, moe_w1, moe_w3, moe_w2, wts, scatter_rows, gather_rows, attn_block, route_block,
             moe_block):
    bp, sp, _ = x_prompt.shape
    bs, ss, _ = x_sample.shape
    past_len = cache_mla_ckv.shape[2]

    mod = _ada(jnp.concatenate([c_prompt, c_sample], axis=0), w_ada[0], b_ada[0]).reshape(bp + bs, 6, D_MODEL)
    mod_p, mod_s = mod[:bp], mod[bp:]

    cos_p, sin_p = _rope_tables(jnp.arange(sp))
    (qp, kmla, vmla, sbq, sbk16, sbv16, gates, ckv_p, krope_p, sbk_p, sbv_p) = _in_proj(
        x_prompt, mod_p, wts, cos_p, sin_p, attn_block)
    x1_p, h2_p = _attention(qp, kmla, vmla, sbq, sbk16, sbv16, None, gates, x_prompt, mod_p, wts,
                            attn_block, attn_block)

    cos_s, sin_s = _rope_tables(past_len + jnp.arange(ss))
    (qs, kmla_s, vmla_s, sbq_s, sbk16_s, sbv16_s, gates_s, ckv_s, krope_s, sbk_s, sbv_s) = _in_proj(
        x_sample, mod_s, wts, cos_s, sin_s, ss)
    pkmla, pvmla = _kv_up(cache_mla_ckv[0], cache_mla_krope[0], wts["w_ukv"], wts["place"], attn_block)
    past = (pkmla, pvmla, cache_sb_k[0].reshape(bs, past_len, SB_WIDTH), cache_sb_v[0].reshape(bs, past_len, SB_WIDTH))
    x1_s, h2_s = _attention(qs, kmla_s, vmla_s, sbq_s, sbk16_s, sbv16_s, past, gates_s, x_sample, mod_s, wts,
                            ss, attn_block)

    tp, ts = bp * sp, bs * ss
    h2_all = jnp.concatenate([h2_p.reshape(tp, HALF), h2_s.reshape(ts, HALF)], axis=0)
    idx_kt, wt_kt, rank_kt, counts = _route(h2_all, wts["wr_t"], wts["br"], route_block)
    gathered = _moe_rows(h2_all, idx_kt, rank_kt, counts, moe_w1[0], moe_w3[0], moe_w2[0],
                         scatter_rows, gather_rows, moe_block, route_block)
    wt = wt_kt.T
    y_p = _combine(gathered, wt, h2_all, x1_p.reshape(tp, D_MODEL), mod_p, wts, attn_block, 0, sp // attn_block)
    y_s = _combine(gathered, wt, h2_all, x1_s.reshape(ts, D_MODEL), mod_s, wts, ss, tp // ss, 1)

    heads = lambda a, b_, s_: a.reshape(1, b_, s_, SB_HEADS, SB_DIM)
    return (y_p.reshape(bp, sp, D_MODEL), y_s.reshape(bs, ss, D_MODEL),
            ckv_p[None], krope_p[None], heads(sbk_p, bp, sp), heads(sbv_p, bp, sp),
            ckv_s[None], krope_s[None], heads(sbk_s, bs, ss), heads(sbv_s, bs, ss))


def kernel(x_prompt, x_sample, cache_mla_ckv, cache_mla_krope, cache_sb_k, cache_sb_v, c_prompt, c_sample, w_ada, b_ada, g_norm1, w_in, g_q_lat, w_uq, g_kv_lat, w_ukv, w_mla_up, w_sb_up, w_out, g_norm2, w_router, b_router, moe_w1, moe_w3, moe_w2, shared_w1, shared_w3, shared_w2, g_final):
    wts = _prep_weights(g_norm1, w_in, g_q_lat, w_uq, g_kv_lat, w_ukv, w_mla_up, w_sb_up, w_out, g_norm2,
                        w_router, b_router, shared_w1, shared_w3, shared_w2, g_final)
    scatter_rows = functools.partial(_sc_scatter_rows, chunk=SC_CHUNK)
    gather_rows = functools.partial(_sc_gather_rows, chunk=SC_CHUNK)
    return _forward(x_prompt, x_sample, cache_mla_ckv, cache_mla_krope, cache_sb_k, cache_sb_v, c_prompt, c_sample,
                    w_ada, b_ada, moe_w1, moe_w3, moe_w2, wts, scatter_rows, gather_rows, ATTN_BLOCK, ATTN_BLOCK,
                    MOE_BLOCK)
```

```python
import functools

import jax
import jax.numpy as jnp
from jax import lax
from jax.experimental import pallas as pl
from jax.experimental.pallas import tpu as pltpu
from jax.experimental.pallas import tpu_sc as plsc

F32 = jnp.float32
BF16 = jnp.bfloat16

D_MODEL = 1024
NORM_EPS = 1e-6
CHUNK = 64
MLA_HEADS = 8
MLA_NOPE = 64
MLA_ROPE = 32
MLA_V = 64
MLA_Q_LORA = 384
MLA_KV_LORA = 256
ROPE_THETA = 10000.0
SB_HEADS = 8
SB_DIM = 64
SB_WIDTH = SB_HEADS * SB_DIM
N_EXPERTS = 256
TOP_K = 8
N_GROUPS = 8
TOPK_GROUPS = 4
GROUP_SIZE = N_EXPERTS // N_GROUPS
EXPERT_DIM = 256
ROUTED_SCALE = 2.5

LANES = 128
SC_CORES = 2
SC_SUBCORES = 16
SC_WORKERS = SC_CORES * SC_SUBCORES
VMEM_LIMIT = 56 * 1024 * 1024

HEAD_PAD = LANES
QP_WIDTH = MLA_HEADS * HEAD_PAD
HALF = D_MODEL // 2

C_QLAT = 0
C_KV = C_QLAT + MLA_Q_LORA
C_SBQ = C_KV + MLA_KV_LORA
C_SBK = C_SBQ + SB_WIDTH
C_SBV = C_SBK + SB_WIDTH
C_GATE = C_SBV + SB_WIDTH
C_KR = C_GATE + 2 * D_MODEL
C_END = C_KR + 2 * LANES

MOE_BLOCK = 512
TOKEN_BLOCK = 256
ATTN_BLOCK = 512
SC_CHUNK = 64


def _rms(x):
    return x * lax.rsqrt(jnp.mean(x * x, axis=-1, keepdims=True) + NORM_EPS)


def _silu(x):
    return x * jax.nn.sigmoid(x)


def _pack_halves(lo, hi):
    lo_bits = lax.bitcast_convert_type(lo.astype(BF16).astype(F32), jnp.uint32) >> 16
    hi_bits = lax.bitcast_convert_type(hi.astype(BF16).astype(F32), jnp.uint32) & jnp.uint32(0xFFFF0000)
    return lax.bitcast_convert_type(lo_bits | hi_bits, jnp.int32)


def _unpack_halves(p):
    u = lax.bitcast_convert_type(p, jnp.uint32)
    lo = lax.bitcast_convert_type(u << 16, F32)
    hi = lax.bitcast_convert_type(u & jnp.uint32(0xFFFF0000), F32)
    return lo, hi


def _dot(a, b):
    return jnp.dot(a, b, preferred_element_type=F32)


def _dot_nt(a, b):
    return lax.dot_general(a, b, (((1,), (1,)), ((), ())), preferred_element_type=F32)


def _ada_kernel(c_ref, w_ref, b_ref, o_ref):
    c = c_ref[...]
    o_ref[...] = _dot(_silu(c).astype(BF16), w_ref[...].astype(BF16)) + b_ref[...]


def _ada(c, w_ada, b_ada):
    n = c.shape[0]
    width = w_ada.shape[1]
    return pl.pallas_call(
        _ada_kernel,
        grid=(width // D_MODEL,),
        in_specs=[pl.BlockSpec((n, D_MODEL), lambda j: (0, 0)),
                  pl.BlockSpec((D_MODEL, D_MODEL), lambda j: (0, j)),
                  pl.BlockSpec((1, D_MODEL), lambda j: (0, j))],
        out_specs=pl.BlockSpec((n, D_MODEL), lambda j: (0, j)),
        out_shape=jax.ShapeDtypeStruct((n, width), F32),
        name="ada",
    )(c, w_ada, b_ada.reshape(1, width))


def _in_kernel(x_ref, mod_ref, g1_ref, win_ref, gq_ref, wuq_ref, gkv_ref, wukv_ref, cos_ref, sin_ref,
               qp_ref, kmla_ref, vmla_ref, sbq_ref, sbk16_ref, sbv16_ref, gates_ref,
               ckv_ref, krope_ref, sbk_ref, sbv_ref):
    x = x_ref[0]
    mod = mod_ref[0]
    h = _rms(x) * g1_ref[...] * (1.0 + mod[1:2]) + mod[0:1]
    hb = h.astype(BF16)

    def seg(a, b):
        return _dot(hb, win_ref[:, a:b])

    cos = cos_ref[...]
    sin = sin_ref[...]
    cos8 = jnp.tile(cos, (1, MLA_HEADS))
    sin8 = jnp.tile(sin, (1, MLA_HEADS))

    qn = (_rms(seg(C_QLAT, C_KV)) * gq_ref[...]).astype(BF16)
    q2 = _dot(qn, wuq_ref[...])
    qp_ref[0] = (q2[:, :QP_WIDTH] * cos8 + q2[:, QP_WIDTH:] * sin8).astype(BF16)

    ckv = _rms(seg(C_KV, C_SBQ)) * gkv_ref[...]
    ckv_ref[0] = ckv
    kv = _dot(ckv.astype(BF16), wukv_ref[...])
    kr2 = seg(C_KR, C_END)
    krp = kr2[:, :LANES] * cos + kr2[:, LANES:] * sin
    krope_ref[0] = krp[:, MLA_NOPE:MLA_NOPE + MLA_ROPE]
    kmla_ref[0] = (kv[:, :QP_WIDTH] + jnp.tile(krp, (1, MLA_HEADS))).astype(BF16)
    vmla_ref[0] = kv[:, QP_WIDTH:].astype(BF16)

    sbq_ref[0] = seg(C_SBQ, C_SBK).astype(BF16)
    sbk = seg(C_SBK, C_SBV)
    sbk_ref[0] = sbk
    sbk16_ref[0] = sbk.astype(BF16)
    sbv = seg(C_SBV, C_GATE)
    sbv_ref[0] = sbv
    sbv16_ref[0] = sbv.astype(BF16)
    gates_ref[0] = jax.nn.sigmoid(seg(C_GATE, C_KR)).astype(BF16)


def _in_proj(x, mod, wts, cos_t, sin_t, tm):
    b, s, _ = x.shape
    ns = s // tm
    tok = lambda w: pl.BlockSpec((1, tm, w), lambda i, j: (i, j, 0))
    full = lambda a: pl.BlockSpec(a.shape, lambda i, j: (0,) * a.ndim)
    out_widths = [(QP_WIDTH, BF16), (QP_WIDTH, BF16), (SB_WIDTH, BF16), (SB_WIDTH, BF16), (SB_WIDTH, BF16),
                  (SB_WIDTH, BF16), (2 * D_MODEL, BF16), (MLA_KV_LORA, F32), (MLA_ROPE, F32),
                  (SB_WIDTH, F32), (SB_WIDTH, F32)]
    return pl.pallas_call(
        _in_kernel,
        grid=(b, ns),
        in_specs=[tok(D_MODEL),
                  pl.BlockSpec((1, 6, D_MODEL), lambda i, j: (i, 0, 0)),
                  full(wts["g1"]), full(wts["w_in"]), full(wts["gq"]), full(wts["w_uq"]),
                  full(wts["gkv"]), full(wts["w_ukv"]),
                  pl.BlockSpec((tm, LANES), lambda i, j: (j, 0)),
                  pl.BlockSpec((tm, LANES), lambda i, j: (j, 0))],
        out_specs=[tok(w) for w, _ in out_widths],
        out_shape=[jax.ShapeDtypeStruct((b, s, w), dt) for w, dt in out_widths],
        compiler_params=pltpu.CompilerParams(dimension_semantics=("parallel", "parallel"),
                                             vmem_limit_bytes=VMEM_LIMIT),
        name="in_proj",
    )(x, mod, wts["g1"], wts["w_in"], wts["gq"], wts["w_uq"], wts["gkv"], wts["w_ukv"], cos_t, sin_t)


def _kvup_kernel(ckv_ref, kr_ref, wukv_ref, place_ref, kmla_ref, vmla_ref):
    kv = _dot(ckv_ref[0].astype(BF16), wukv_ref[...])
    kr = _dot(kr_ref[0].astype(BF16), place_ref[...])
    kmla_ref[0] = (kv[:, :QP_WIDTH] + kr).astype(BF16)
    vmla_ref[0] = kv[:, QP_WIDTH:].astype(BF16)


def _kv_up(ckv, krope, w_ukv_r, place, tm):
    b, p, _ = ckv.shape
    return pl.pallas_call(
        _kvup_kernel,
        grid=(b, p // tm),
        in_specs=[pl.BlockSpec((1, tm, MLA_KV_LORA), lambda i, j: (i, j, 0)),
                  pl.BlockSpec((1, tm, MLA_ROPE), lambda i, j: (i, j, 0)),
                  pl.BlockSpec(w_ukv_r.shape, lambda i, j: (0, 0)),
                  pl.BlockSpec(place.shape, lambda i, j: (0, 0))],
        out_specs=[pl.BlockSpec((1, tm, QP_WIDTH), lambda i, j: (i, j, 0)),
                   pl.BlockSpec((1, tm, SB_WIDTH), lambda i, j: (i, j, 0))],
        out_shape=[jax.ShapeDtypeStruct((b, p, QP_WIDTH), BF16), jax.ShapeDtypeStruct((b, p, SB_WIDTH), BF16)],
        compiler_params=pltpu.CompilerParams(dimension_semantics=("parallel", "parallel")),
        name="kv_up",
    )(ckv, krope, w_ukv_r, place)


def _tri(n):
    r = lax.broadcasted_iota(jnp.int32, (n, n), 0)
    c = lax.broadcasted_iota(jnp.int32, (n, n), 1)
    return jnp.where(r > c, 1.0, 0.0).astype(BF16)


def _stick_terms(z):
    log_sig = jnp.minimum(z, 0.0) - jnp.log(1.0 + jnp.exp(-jnp.abs(z)))
    return log_sig, log_sig - z


def _split_bf16(x):
    hi = x.astype(BF16)
    return hi, (x - hi.astype(F32)).astype(BF16)


def _finish_mixer(o_mla, o_sb, gates_ref, x_ref, mod_ref, wmu_ref, wsu_ref, wo_ref, g2_ref, x1_ref, h2_ref):
    u_mla = _dot(o_mla.astype(BF16), wmu_ref[...])
    u_sb = _dot(o_sb.astype(BF16), wsu_ref[...])
    gates = gates_ref[0]
    merged = gates[:, :D_MODEL].astype(F32) * u_mla + gates[:, D_MODEL:].astype(F32) * u_sb
    mix = _dot(merged.astype(BF16), wo_ref[...])
    mod = mod_ref[0]
    x1 = x_ref[0] + mod[2:3] * mix
    x1_ref[0] = x1
    h2 = _rms(x1) * g2_ref[...] * (1.0 + mod[4:5]) + mod[3:4]
    h2_ref[0] = _pack_halves(h2[:, :HALF], h2[:, HALF:])


def _prompt_attn_kernel(qp_ref, kmla_ref, vmla_ref, sbq_ref, sbk_ref, sbv_ref,
                        gates_ref, x_ref, mod_ref, wmu_ref, wsu_ref, wo_ref, g2_ref, x1_ref, h2_ref,
                        m_ref, l_ref, acc_ref, c_ref, sacc_ref, *, tq):
    i = pl.program_id(1)
    lane = lax.broadcasted_iota(jnp.int32, (1, LANES), 1)
    half_masks = (lane < MLA_V, lane >= MLA_V)
    row = lax.broadcasted_iota(jnp.int32, (tq, tq), 0)
    col = lax.broadcasted_iota(jnp.int32, (tq, tq), 1)
    chunk_mask = (col // CHUNK) <= (row // CHUNK)
    causal_mask = col < row
    tri_m = _tri(tq)
    st_diag = pl.multiple_of(i * tq, tq)

    def load(ref, start, c0):
        return ref[0, pl.ds(start, tq), c0:c0 + LANES]

    def mla_block(sub, q_h, k, v, mask):
        s = _dot_nt(q_h, k)
        if mask is not None:
            s = jnp.where(mask, s, -jnp.inf)
            m_new = jnp.max(s, axis=-1, keepdims=True)
            p = jnp.exp(s - m_new)
            l_ref[sub] = jnp.sum(p, axis=-1, keepdims=True)
            acc_ref[sub] = _dot(p.astype(BF16), v)
        else:
            m_old = m_ref[sub]
            m_new = jnp.maximum(m_old, jnp.max(s, axis=-1, keepdims=True))
            p = jnp.exp(s - m_new)
            alpha = jnp.exp(m_old - m_new)
            l_ref[sub] = alpha * l_ref[sub] + jnp.sum(p, axis=-1, keepdims=True)
            acc_ref[sub] = alpha * acc_ref[sub] + _dot(p.astype(BF16), v)
        m_ref[sub] = m_new

    def sb_block(sub, q_h, k, v, mask):
        log_sig, log_keep = _stick_terms(_dot_nt(q_h, k))
        if mask is not None:
            log_keep = jnp.where(mask, log_keep, 0.0)
        hi, lo = _split_bf16(log_keep)
        after = _dot(hi, tri_m) + _dot(lo, tri_m)
        total = jnp.sum(log_keep, axis=-1, keepdims=True)
        if mask is not None:
            a = jnp.where(mask, jnp.exp(log_sig + after), 0.0)
            sacc_ref[sub] = _dot(a.astype(BF16), v)
            c_ref[sub] = total
        else:
            c_old = c_ref[sub]
            a = jnp.exp(log_sig + after + c_old)
            sacc_ref[sub] = sacc_ref[sub] + _dot(a.astype(BF16), v)
            c_ref[sub] = c_old + total

    o_mla = []
    o_sb = []
    for pair in range(MLA_HEADS // 2):
        vcol = pair * LANES
        kcols = [(2 * pair + sub) * HEAD_PAD for sub in range(2)]
        q_m = [qp_ref[0, :, kc:kc + HEAD_PAD] for kc in kcols]
        q_s = [jnp.where(half_masks[sub], sbq_ref[0, :, vcol:vcol + LANES], 0) for sub in range(2)]

        def pair_blocks(mla_start, sb_start, masks, vcol=vcol, kcols=kcols, q_m=q_m, q_s=q_s):
            v_m = load(vmla_ref, mla_start, vcol)
            k_s = load(sbk_ref, sb_start, vcol)
            v_s = load(sbv_ref, sb_start, vcol)
            for sub in range(2):
                keep = half_masks[sub]
                mla_block(sub, q_m[sub], load(kmla_ref, mla_start, kcols[sub]), jnp.where(keep, v_m, 0), masks[0])
                sb_block(sub, q_s[sub], k_s, jnp.where(keep, v_s, 0), masks[1])

        pair_blocks(st_diag, st_diag, (chunk_mask, causal_mask))

        def step(t, _, pair_blocks=pair_blocks):
            pair_blocks(pl.multiple_of(t * tq, tq), pl.multiple_of((i - 1 - t) * tq, tq), (None, None))
            return 0
        lax.fori_loop(0, i, step, 0)
        o_mla.append(acc_ref[0] / l_ref[0] + acc_ref[1] / l_ref[1])
        o_sb.append(sacc_ref[0] + sacc_ref[1])

    _finish_mixer(jnp.concatenate(o_mla, axis=1), jnp.concatenate(o_sb, axis=1),
                  gates_ref, x_ref, mod_ref, wmu_ref, wsu_ref, wo_ref, g2_ref, x1_ref, h2_ref)


def _decode_attn_kernel(qp_ref, kmla_ref, vmla_ref, sbq_ref, sbk_ref, sbv_ref, pkmla_ref, pvmla_ref, psbk_ref,
                        psbv_ref, gates_ref, x_ref, mod_ref, wmu_ref, wsu_ref, wo_ref, g2_ref, x1_ref, h2_ref,
                        *, tq, past_len, past_blk):
    n_past = past_len // past_blk
    lane = lax.broadcasted_iota(jnp.int32, (1, LANES), 1)
    half_masks = (lane < MLA_V, lane >= MLA_V)
    row = lax.broadcasted_iota(jnp.int32, (tq, tq), 0)
    col = lax.broadcasted_iota(jnp.int32, (tq, tq), 1)
    chunk_mask = ((past_len + col) // CHUNK) <= ((past_len + row) // CHUNK)
    causal_mask = col < row
    tri_new = _tri(tq)
    tri_past = _tri(past_blk)

    o_mla = []
    o_sb = []
    for pair in range(MLA_HEADS // 2):
        vcol = pair * LANES
        v_new = vmla_ref[0, :, vcol:vcol + LANES]
        v_past = pvmla_ref[0, :, vcol:vcol + LANES]
        sk_new = sbk_ref[0, :, vcol:vcol + LANES]
        sv_new = sbv_ref[0, :, vcol:vcol + LANES]
        sk_past = psbk_ref[0, :, vcol:vcol + LANES].astype(BF16)
        sv_past = psbv_ref[0, :, vcol:vcol + LANES].astype(BF16)
        mla_pair = None
        sb_pair = None
        for sub in range(2):
            keep = half_masks[sub]
            kcol = (2 * pair + sub) * HEAD_PAD

            q_h = qp_ref[0, :, kcol:kcol + HEAD_PAD]
            s_past = _dot_nt(q_h, pkmla_ref[0, :, kcol:kcol + HEAD_PAD])
            s_new = jnp.where(chunk_mask, _dot_nt(q_h, kmla_ref[0, :, kcol:kcol + HEAD_PAD]), -jnp.inf)
            m = jnp.maximum(jnp.max(s_past, axis=-1, keepdims=True), jnp.max(s_new, axis=-1, keepdims=True))
            p_past = jnp.exp(s_past - m)
            p_new = jnp.exp(s_new - m)
            denom = jnp.sum(p_past, axis=-1, keepdims=True) + jnp.sum(p_new, axis=-1, keepdims=True)
            o = (_dot(p_past.astype(BF16), jnp.where(keep, v_past, 0))
                 + _dot(p_new.astype(BF16), jnp.where(keep, v_new, 0))) / denom
            mla_pair = o if mla_pair is None else mla_pair + o

            q_s = jnp.where(keep, sbq_ref[0, :, vcol:vcol + LANES], 0)
            ls_new, lk_new = _stick_terms(_dot_nt(q_s, sk_new))
            lk_new = jnp.where(causal_mask, lk_new, 0.0)
            hi, lo = _split_bf16(lk_new)
            a_new = jnp.where(causal_mask, jnp.exp(ls_new + _dot(hi, tri_new) + _dot(lo, tri_new)), 0.0)
            acc = _dot(a_new.astype(BF16), jnp.where(keep, sv_new, 0))
            later = jnp.sum(lk_new, axis=-1, keepdims=True)

            ls_past, lk_past = _stick_terms(_dot_nt(q_s, sk_past))
            hi, lo = _split_bf16(lk_past)
            blocks = lambda a: [a[:, b * past_blk:(b + 1) * past_blk] for b in range(n_past)]
            stacked = jnp.concatenate(blocks(hi) + blocks(lo), axis=0)
            within = _dot(stacked, tri_past)
            after = []
            for b in reversed(range(n_past)):
                after.append(within[b * tq:(b + 1) * tq] + within[(n_past + b) * tq:(n_past + b + 1) * tq] + later)
                later = later + jnp.sum(lk_past[:, b * past_blk:(b + 1) * past_blk], axis=-1, keepdims=True)
            a_past = jnp.exp(ls_past + jnp.concatenate(after[::-1], axis=1))
            acc = acc + _dot(a_past.astype(BF16), jnp.where(keep, sv_past, 0))
            sb_pair = acc if sb_pair is None else sb_pair + acc
        o_mla.append(mla_pair)
        o_sb.append(sb_pair)

    _finish_mixer(jnp.concatenate(o_mla, axis=1), jnp.concatenate(o_sb, axis=1),
                  gates_ref, x_ref, mod_ref, wmu_ref, wsu_ref, wo_ref, g2_ref, x1_ref, h2_ref)


def _mixer_call(kernel_fn, name, tq, args_kv, gates, x, mod, wts, scratch):
    b, s, _ = x.shape
    tok = lambda w: pl.BlockSpec((1, tq, w), lambda i, j: (i, j, 0))
    seq = lambda a: pl.BlockSpec((1,) + a.shape[1:], lambda i, j: (i, 0, 0))
    full = lambda a: pl.BlockSpec(a.shape, lambda i, j: (0,) * a.ndim)
    args = list(args_kv) + [gates, x, mod, wts["w_mla_up"], wts["w_sb_up"], wts["w_out"], wts["g2"]]
    specs = [seq(a) if whole else tok(a.shape[-1]) for a, whole in
             zip(args_kv, (False, True, True, False) + (True,) * (len(args_kv) - 4))]
    specs += [tok(2 * D_MODEL), tok(D_MODEL), pl.BlockSpec((1, 6, D_MODEL), lambda i, j: (i, 0, 0)),
              full(wts["w_mla_up"]), full(wts["w_sb_up"]), full(wts["w_out"]), full(wts["g2"])]
    return pl.pallas_call(
        kernel_fn,
        grid=(b, s // tq),
        in_specs=specs,
        out_specs=[tok(D_MODEL), tok(HALF)],
        out_shape=[jax.ShapeDtypeStruct((b, s, D_MODEL), F32), jax.ShapeDtypeStruct((b, s, HALF), jnp.int32)],
        scratch_shapes=scratch,
        compiler_params=pltpu.CompilerParams(dimension_semantics=("parallel", "arbitrary"),
                                             vmem_limit_bytes=VMEM_LIMIT),
        name=name,
    )(*args)


def _prompt_attention(qp, kmla, vmla, sbq, sbk16, sbv16, gates, x, mod, wts, tq):
    col = lambda: pltpu.VMEM((2, tq, 1), F32)
    wide = lambda: pltpu.VMEM((2, tq, LANES), F32)
    return _mixer_call(functools.partial(_prompt_attn_kernel, tq=tq), "attention", tq,
                       (qp, kmla, vmla, sbq, sbk16, sbv16), gates, x, mod, wts,
                       [col(), col(), wide(), col(), wide()])


def _decode_attention(qp, kmla, vmla, sbq, sbk16, sbv16, past, gates, x, mod, wts, past_blk):
    tq = x.shape[1]
    past_len = past[0].shape[1]
    kern = functools.partial(_decode_attn_kernel, tq=tq, past_len=past_len, past_blk=past_blk)
    return _mixer_call(kern, "decode_attention", tq, (qp, kmla, vmla, sbq, sbk16, sbv16) + tuple(past),
                       gates, x, mod, wts, [])


def _route_kernel(h2_ref, wr_ref, br_ref, idx_ref, wt_ref, rank_ref, cnt_ref, seen_ref):
    lo, hi = _unpack_halves(h2_ref[...])
    tm = lo.shape[0]
    logits = _dot_nt(wr_ref[:, :HALF], lo.astype(BF16)) + _dot_nt(wr_ref[:, HALF:], hi.astype(BF16))
    scores = jax.nn.sigmoid(logits)
    sel = scores + br_ref[...]
    neg = -jnp.inf

    grp = sel.reshape(N_GROUPS, GROUP_SIZE, tm)
    within = lax.broadcasted_iota(jnp.int32, grp.shape, 1)
    top1 = jnp.max(grp, axis=1, keepdims=True)
    first = jnp.min(jnp.where(grp == top1, within, GROUP_SIZE), axis=1, keepdims=True)
    top2 = jnp.max(jnp.where(within == first, neg, grp), axis=1, keepdims=True)
    gscore = (top1 + top2).reshape(N_GROUPS, tm)

    gid = lax.broadcasted_iota(jnp.int32, gscore.shape, 0)
    chosen = jnp.zeros(gscore.shape, jnp.bool_)
    for _ in range(TOPK_GROUPS):
        best = jnp.max(gscore, axis=0, keepdims=True)
        pick = jnp.min(jnp.where(gscore == best, gid, N_GROUPS), axis=0, keepdims=True)
        hit = gid == pick
        chosen = jnp.logical_or(chosen, hit)
        gscore = jnp.where(hit, neg, gscore)
    chosen3 = jnp.broadcast_to(chosen.reshape(N_GROUPS, 1, tm), grp.shape)
    cand = jnp.where(chosen3, grp, neg).reshape(N_EXPERTS, tm)

    eid = lax.broadcasted_iota(jnp.int32, cand.shape, 0)
    picks = []
    weights = []
    for _ in range(TOP_K):
        best = jnp.max(cand, axis=0, keepdims=True)
        pick = jnp.min(jnp.where(cand == best, eid, N_EXPERTS), axis=0, keepdims=True)
        hit = eid == pick
        weights.append(jnp.sum(jnp.where(hit, scores, 0.0), axis=0, keepdims=True))
        picks.append(pick)
        cand = jnp.where(hit, neg, cand)
    w = jnp.concatenate(weights, axis=0)
    idx_ref[...] = jnp.concatenate(picks, axis=0)
    wt_ref[...] = w / (jnp.sum(w, axis=0, keepdims=True) + 1e-20) * ROUTED_SCALE

    @pl.when(pl.program_id(0) == 0)
    def _():
        seen_ref[...] = jnp.zeros_like(seen_ref)

    onehot = jnp.zeros(cand.shape, F32)
    for pick in picks:
        onehot = onehot + jnp.where(eid == pick, 1.0, 0.0)
    src = lax.broadcasted_iota(jnp.int32, (tm, tm), 0)
    dst = lax.broadcasted_iota(jnp.int32, (tm, tm), 1)
    earlier = jnp.where(src < dst, 1.0, 0.0).astype(BF16)
    before = _dot(onehot.astype(BF16), earlier) + seen_ref[...]
    rank_ref[...] = jnp.concatenate(
        [jnp.sum(jnp.where(eid == pick, before, 0.0), axis=0, keepdims=True) for pick in picks],
        axis=0).astype(jnp.int32)
    seen = seen_ref[...] + jnp.sum(onehot, axis=1, keepdims=True)
    seen_ref[...] = seen
    cnt_ref[...] = seen.astype(jnp.int32)


def _route(h2p, wr_t, br, tm):
    t = h2p.shape[0]
    kt = lambda: pl.BlockSpec((TOP_K, tm), lambda i: (0, i))
    return pl.pallas_call(
        _route_kernel,
        grid=(t // tm,),
        in_specs=[pl.BlockSpec((tm, HALF), lambda i: (i, 0)),
                  pl.BlockSpec(wr_t.shape, lambda i: (0, 0)),
                  pl.BlockSpec(br.shape, lambda i: (0, 0))],
        out_specs=[kt(), kt(), kt(), pl.BlockSpec((N_EXPERTS, 1), lambda i: (0, 0))],
        out_shape=[jax.ShapeDtypeStruct((TOP_K, t), jnp.int32), jax.ShapeDtypeStruct((TOP_K, t), F32),
                   jax.ShapeDtypeStruct((TOP_K, t), jnp.int32), jax.ShapeDtypeStruct((N_EXPERTS, 1), jnp.int32)],
        scratch_shapes=[pltpu.VMEM((N_EXPERTS, 1), F32)],
        compiler_params=pltpu.CompilerParams(dimension_semantics=("arbitrary",)),
        name="route",
    )(h2p, wr_t, br)


def _position_kernel(idx_ref, rank_ref, start_ref, pos_ref):
    idx = idx_ref[...]
    eid = lax.broadcasted_iota(jnp.int32, (N_EXPERTS, idx.shape[1]), 0)
    start = start_ref[...]
    base = jnp.concatenate(
        [jnp.sum(jnp.where(eid == idx[k:k + 1, :], start, 0.0), axis=0, keepdims=True) for k in range(TOP_K)],
        axis=0)
    pos_ref[...] = base.astype(jnp.int32) + rank_ref[...]


def _positions(idx_kt, rank_kt, pad_start, tm):
    t = idx_kt.shape[1]
    kt = lambda: pl.BlockSpec((TOP_K, tm), lambda i: (0, i))
    return pl.pallas_call(
        _position_kernel,
        grid=(t // tm,),
        in_specs=[kt(), kt(), pl.BlockSpec((N_EXPERTS, 1), lambda i: (0, 0))],
        out_specs=kt(),
        out_shape=jax.ShapeDtypeStruct((TOP_K, t), jnp.int32),
        compiler_params=pltpu.CompilerParams(dimension_semantics=("parallel",)),
        name="positions",
    )(idx_kt, rank_kt, pad_start.astype(F32).reshape(N_EXPERTS, 1))


def _sc_mesh():
    return plsc.VectorSubcoreMesh(core_axis_name="c", subcore_axis_name="s",
                                  num_cores=SC_CORES, num_subcores=SC_SUBCORES)


def _sc_scatter_rows(rows, pos, n_out, chunk):
    t, width = rows.shape
    copies = pos.shape[0] // t
    spans = SC_WORKERS // copies
    per_worker = t // spans
    n_chunks = per_worker // chunk
    assert copies * spans == SC_WORKERS and per_worker * spans == t and n_chunks * chunk == per_worker

    @functools.partial(
        pl.kernel, mesh=_sc_mesh(),
        out_type=jax.ShapeDtypeStruct((n_out, width), rows.dtype),
        scratch_types=[pltpu.VMEM((chunk,), jnp.int32), pltpu.VMEM((chunk, width), rows.dtype)],
    )
    def scatter(rows_hbm, pos_hbm, out_hbm, idx_v, rows_v):
        wid = lax.axis_index("s") * SC_CORES + lax.axis_index("c")
        copy = wid % copies
        base = (wid // copies) * per_worker

        @pl.loop(0, n_chunks)
        def _(c):
            off = pl.multiple_of(base + c * chunk, 8)
            pltpu.sync_copy(pos_hbm.at[pl.ds(pl.multiple_of(copy * t + off, 8), chunk)], idx_v)
            pltpu.sync_copy(rows_hbm.at[pl.ds(off, chunk)], rows_v)
            pltpu.sync_copy(rows_v, out_hbm.at[idx_v])

    return scatter(rows, pos)


def _sc_gather_rows(table, idx, chunk):
    n_rows = idx.shape[0]
    width = table.shape[1]
    per_worker = n_rows // SC_WORKERS
    n_chunks = per_worker // chunk
    assert per_worker * SC_WORKERS == n_rows and n_chunks * chunk == per_worker
    mesh = _sc_mesh()

    @functools.partial(
        pl.kernel, mesh=mesh,
        out_type=jax.ShapeDtypeStruct((n_rows, width), table.dtype),
        scratch_types=[pltpu.VMEM((chunk,), jnp.int32), pltpu.VMEM((chunk, width), table.dtype),
                       pltpu.SemaphoreType.DMA],
    )
    def gather(table_hbm, idx_hbm, out_hbm, idx_v, rows_v, sem):
        wid = lax.axis_index("s") * SC_CORES + lax.axis_index("c")
        base = wid * per_worker

        @pl.loop(0, n_chunks)
        def _(c):
            off = pl.multiple_of(base + c * chunk, 8)
            pltpu.sync_copy(idx_hbm.at[pl.ds(off, chunk)], idx_v)
            pltpu.async_copy(table_hbm.at[idx_v], rows_v, sem).wait()
            pltpu.sync_copy(rows_v, out_hbm.at[pl.ds(off, chunk)])

    return gather(table, idx)


def _expert_kernel(be_ref, nv_ref, nu_ref, xs_ref, w1_ref, w3_ref, w2_ref, os_ref, wb1, wb3, wb2):
    i = pl.program_id(0)

    @pl.when(i < nu_ref[0])
    def _():
        @pl.when(jnp.logical_or(i == 0, be_ref[i] != be_ref[jnp.maximum(i - 1, 0)]))
        def _():
            wb1[...] = w1_ref[0].astype(BF16)
            wb3[...] = w3_ref[0].astype(BF16)
            wb2[...] = w2_ref[0].astype(BF16)

        packed = xs_ref[...]
        live = lax.broadcasted_iota(jnp.int32, packed.shape, 0) < nv_ref[i]
        lo, hi = _unpack_halves(jnp.where(live, packed, 0))
        lo = lo.astype(BF16)
        hi = hi.astype(BF16)
        a = _dot(lo, wb1[:HALF, :]) + _dot(hi, wb1[HALF:, :])
        b = _dot(lo, wb3[:HALF, :]) + _dot(hi, wb3[HALF:, :])
        o = _dot((_silu(a) * b).astype(BF16), wb2[...])
        os_ref[...] = _pack_halves(o[:, :HALF], o[:, HALF:])


def _experts(xs, blk_exp, blk_valid, n_used, w1, w3, w2, blk):
    rows = xs.shape[0]
    nb = rows // blk
    last = lambda i, nu: jnp.minimum(i, nu[0] - 1)
    grid_spec = pltpu.PrefetchScalarGridSpec(
        num_scalar_prefetch=3,
        grid=(nb,),
        in_specs=[pl.BlockSpec((blk, HALF), lambda i, be, nv, nu: (last(i, nu), 0)),
                  pl.BlockSpec((1, D_MODEL, EXPERT_DIM), lambda i, be, nv, nu: (be[last(i, nu)], 0, 0)),
                  pl.BlockSpec((1, D_MODEL, EXPERT_DIM), lambda i, be, nv, nu: (be[last(i, nu)], 0, 0)),
                  pl.BlockSpec((1, EXPERT_DIM, D_MODEL), lambda i, be, nv, nu: (be[last(i, nu)], 0, 0))],
        out_specs=pl.BlockSpec((blk, HALF), lambda i, be, nv, nu: (last(i, nu), 0)),
        scratch_shapes=[pltpu.VMEM((D_MODEL, EXPERT_DIM), BF16), pltpu.VMEM((D_MODEL, EXPERT_DIM), BF16),
                        pltpu.VMEM((EXPERT_DIM, D_MODEL), BF16)],
    )
    return pl.pallas_call(
        _expert_kernel,
        grid_spec=grid_spec,
        out_shape=jax.ShapeDtypeStruct((rows, HALF), jnp.int32),
        compiler_params=pltpu.CompilerParams(dimension_semantics=("arbitrary",), vmem_limit_bytes=VMEM_LIMIT),
        name="experts",
    )(blk_exp, blk_valid, n_used, xs, w1, w3, w2)


def _combine_kernel(g_ref, wt_ref, h2_ref, x1_ref, mod_ref, sw1_ref, sw3_ref, sw2_ref, gf_ref, y_ref):
    wt = wt_ref[...]
    lo_acc = None
    for k in range(TOP_K):
        lo, hi = _unpack_halves(g_ref[k])
        wk = wt[:, k:k + 1]
        lo_acc = wk * lo if lo_acc is None else lo_acc + wk * lo
        hi_acc = wk * hi if k == 0 else hi_acc + wk * hi
    routed = jnp.concatenate([lo_acc, hi_acc], axis=1)
    lo, hi = _unpack_halves(h2_ref[...])
    lo = lo.astype(BF16)
    hi = hi.astype(BF16)
    a = _dot(lo, sw1_ref[:HALF, :]) + _dot(hi, sw1_ref[HALF:, :])
    b = _dot(lo, sw3_ref[:HALF, :]) + _dot(hi, sw3_ref[HALF:, :])
    shared = _dot((_silu(a) * b).astype(BF16), sw2_ref[...])
    mod = mod_ref[0]
    x2 = x1_ref[...] + mod[5:6] * (routed + shared)
    y_ref[...] = _rms(x2) * gf_ref[...]


def _combine(gathered, wt, h2p, x1, mod, wts, tm, tile0, tiles_per_batch):
    t = x1.shape[0]
    full = lambda a: pl.BlockSpec(a.shape, lambda i: (0,) * a.ndim)
    shifted = lambda w: pl.BlockSpec((tm, w), lambda i: (i + tile0, 0))
    return pl.pallas_call(
        _combine_kernel,
        grid=(t // tm,),
        in_specs=[pl.BlockSpec((TOP_K, tm, HALF), lambda i: (0, i + tile0, 0)), shifted(TOP_K), shifted(HALF),
                  pl.BlockSpec((tm, D_MODEL), lambda i: (i, 0)),
                  pl.BlockSpec((1, 6, D_MODEL), lambda i: (i // tiles_per_batch, 0, 0)),
                  full(wts["sw1"]), full(wts["sw3"]), full(wts["sw2"]), full(wts["gf"])],
        out_specs=pl.BlockSpec((tm, D_MODEL), lambda i: (i, 0)),
        out_shape=jax.ShapeDtypeStruct((t, D_MODEL), F32),
        compiler_params=pltpu.CompilerParams(dimension_semantics=("parallel",), vmem_limit_bytes=VMEM_LIMIT),
        name="combine",
    )(gathered, wt, h2p, x1, mod, wts["sw1"], wts["sw3"], wts["sw2"], wts["gf"])


def _rope_tables(pos):
    half = MLA_ROPE // 2
    inv_freq = ROPE_THETA ** (-jnp.arange(half, dtype=F32) / half)
    ang = pos.astype(F32)[:, None] * inv_freq
    cos, sin = jnp.cos(ang), jnp.sin(ang)
    n = pos.shape[0]
    ones = jnp.ones((n, MLA_NOPE), F32)
    z_nope = jnp.zeros((n, MLA_NOPE), F32)
    z_pad = jnp.zeros((n, HEAD_PAD - MLA_NOPE - MLA_ROPE), F32)
    return (jnp.concatenate([ones, cos, cos, z_pad], axis=1),
            jnp.concatenate([z_nope, -sin, sin, z_pad], axis=1))


def _prep_weights(g_norm1, w_in, g_q_lat, w_uq, g_kv_lat, w_ukv, w_mla_up, w_sb_up, w_out, g_norm2,
                  w_router, b_router, shared_w1, shared_w3, shared_w2, g_final):
    half = MLA_ROPE // 2
    w = w_in[0]
    o = 0
    parts = {}
    for name, width in (("qlat", MLA_Q_LORA), ("kv", MLA_KV_LORA), ("kr", MLA_ROPE), ("sbq", SB_WIDTH),
                        ("sbk", SB_WIDTH), ("sbv", SB_WIDTH), ("gm", D_MODEL), ("gs", D_MODEL)):
        parts[name] = w[:, o:o + width]
        o += width
    kr = parts["kr"]
    z_nope = jnp.zeros((D_MODEL, MLA_NOPE), F32)
    z_pad = jnp.zeros((D_MODEL, HEAD_PAD - MLA_NOPE - MLA_ROPE), F32)
    kr_seg = jnp.concatenate([z_nope, kr, z_pad, z_nope, kr[:, half:], kr[:, :half], z_pad], axis=1)
    w_in_r = jnp.concatenate([parts["qlat"], parts["kv"], parts["sbq"] * (SB_DIM ** -0.5), parts["sbk"],
                              parts["sbv"], parts["gm"], parts["gs"], kr_seg], axis=1).astype(BF16)

    scale = (MLA_NOPE + MLA_ROPE) ** -0.5
    uq = w_uq[0].reshape(MLA_Q_LORA, MLA_HEADS, MLA_NOPE + MLA_ROPE) * scale
    nope, rope = uq[..., :MLA_NOPE], uq[..., MLA_NOPE:]
    zq_nope = jnp.zeros_like(nope)
    zq_pad = jnp.zeros((MLA_Q_LORA, MLA_HEADS, HEAD_PAD - MLA_NOPE - MLA_ROPE), F32)
    main = jnp.concatenate([nope, rope, zq_pad], axis=-1).reshape(MLA_Q_LORA, QP_WIDTH)
    swapped = jnp.concatenate([zq_nope, rope[..., half:], rope[..., :half], zq_pad], axis=-1)
    w_uq_r = jnp.concatenate([main, swapped.reshape(MLA_Q_LORA, QP_WIDTH)], axis=1).astype(BF16)

    ukv = w_ukv[0].reshape(MLA_KV_LORA, MLA_HEADS, MLA_NOPE + MLA_V)
    k_nope, v = ukv[..., :MLA_NOPE], ukv[..., MLA_NOPE:]
    k_pad = jnp.concatenate([k_nope, jnp.zeros((MLA_KV_LORA, MLA_HEADS, HEAD_PAD - MLA_NOPE), F32)], axis=-1)
    w_ukv_r = jnp.concatenate([k_pad.reshape(MLA_KV_LORA, QP_WIDTH),
                               v.reshape(MLA_KV_LORA, MLA_HEADS * MLA_V)], axis=1).astype(BF16)

    eye = jnp.eye(MLA_ROPE, dtype=F32)
    place_head = jnp.concatenate([jnp.zeros((MLA_ROPE, MLA_NOPE), F32), eye,
                                  jnp.zeros((MLA_ROPE, HEAD_PAD - MLA_NOPE - MLA_ROPE), F32)], axis=1)
    place = jnp.tile(place_head, (1, MLA_HEADS)).astype(BF16)

    return dict(
        g1=g_norm1[0].reshape(1, D_MODEL), w_in=w_in_r, gq=g_q_lat[0].reshape(1, MLA_Q_LORA), w_uq=w_uq_r,
        gkv=g_kv_lat[0].reshape(1, MLA_KV_LORA), w_ukv=w_ukv_r, place=place,
        w_mla_up=w_mla_up[0].astype(BF16), w_sb_up=w_sb_up[0].astype(BF16), w_out=w_out[0].astype(BF16),
        g2=g_norm2[0].reshape(1, D_MODEL), wr_t=w_router[0].T.astype(BF16),
        br=b_router[0].reshape(N_EXPERTS, 1),
        sw1=shared_w1[0].astype(BF16), sw3=shared_w3[0].astype(BF16), sw2=shared_w2[0].astype(BF16),
        gf=g_final.reshape(1, D_MODEL))


def _block_plan(counts, nb, blk):
    padded = (counts + blk - 1) // blk * blk
    pad_end = jnp.cumsum(padded)
    pad_start = pad_end - padded
    first_row = jnp.arange(nb, dtype=jnp.int32) * blk
    blk_exp = jnp.minimum(jnp.sum((pad_end[None, :] <= first_row[:, None]).astype(jnp.int32), axis=1),
                          N_EXPERTS - 1)
    own = blk_exp[:, None] == jnp.arange(N_EXPERTS, dtype=jnp.int32)[None, :]
    seg_end = jnp.sum(jnp.where(own, (pad_start + counts)[None, :], 0), axis=1)
    blk_valid = jnp.clip(seg_end - first_row, 0, blk).astype(jnp.int32)
    n_used = (pad_end[-1:] // blk).astype(jnp.int32)
    return pad_start, blk_exp.astype(jnp.int32), blk_valid, n_used


def _moe_rows(h2p, idx_kt, rank_kt, counts, w1, w3, w2, scatter_rows, gather_rows, blk, tm):
    t = h2p.shape[0]
    nb = -(-t * TOP_K // blk) + N_EXPERTS
    pad_start, blk_exp, blk_valid, n_used = _block_plan(counts.reshape(N_EXPERTS), nb, blk)
    pos = _positions(idx_kt, rank_kt, pad_start, tm).reshape(TOP_K * t)
    xs = scatter_rows(h2p, pos, nb * blk)
    os_ = _experts(xs, blk_exp, blk_valid, n_used, w1, w3, w2, blk)
    return gather_rows(os_, pos).reshape(TOP_K, t, HALF)


def _forward(x_prompt, x_sample, cache_mla_ckv, cache_mla_krope, cache_sb_k, cache_sb_v, c_prompt, c_sample,
             w_ada, b_ada, moe_w1, moe_w3, moe_w2, wts, scatter_rows, gather_rows, token_block, attn_block,
             route_block, moe_block):
    bp, sp, _ = x_prompt.shape
    bs, ss, _ = x_sample.shape
    past_len = cache_mla_ckv.shape[2]

    mod = _ada(jnp.concatenate([c_prompt, c_sample], axis=0), w_ada[0], b_ada[0]).reshape(bp + bs, 6, D_MODEL)
    mod_p, mod_s = mod[:bp], mod[bp:]

    cos_p, sin_p = _rope_tables(jnp.arange(sp))
    (qp, kmla, vmla, sbq, sbk16, sbv16, gates, ckv_p, krope_p, sbk_p, sbv_p) = _in_proj(
        x_prompt, mod_p, wts, cos_p, sin_p, token_block)
    x1_p, h2_p = _prompt_attention(qp, kmla, vmla, sbq, sbk16, sbv16, gates, x_prompt, mod_p, wts, attn_block)

    cos_s, sin_s = _rope_tables(past_len + jnp.arange(ss))
    (qs, kmla_s, vmla_s, sbq_s, sbk16_s, sbv16_s, gates_s, ckv_s, krope_s, sbk_s, sbv_s) = _in_proj(
        x_sample, mod_s, wts, cos_s, sin_s, ss)
    pkmla, pvmla = _kv_up(cache_mla_ckv[0], cache_mla_krope[0], wts["w_ukv"], wts["place"], token_block)
    past = (pkmla, pvmla, cache_sb_k[0].reshape(bs, past_len, SB_WIDTH), cache_sb_v[0].reshape(bs, past_len, SB_WIDTH))
    x1_s, h2_s = _decode_attention(qs, kmla_s, vmla_s, sbq_s, sbk16_s, sbv16_s, past, gates_s, x_sample, mod_s,
                                   wts, token_block)

    tp, ts = bp * sp, bs * ss
    h2_all = jnp.concatenate([h2_p.reshape(tp, HALF), h2_s.reshape(ts, HALF)], axis=0)
    idx_kt, wt_kt, rank_kt, counts = _route(h2_all, wts["wr_t"], wts["br"], route_block)
    gathered = _moe_rows(h2_all, idx_kt, rank_kt, counts, moe_w1[0], moe_w3[0], moe_w2[0],
                         scatter_rows, gather_rows, moe_block, route_block)
    wt = wt_kt.T
    y_p = _combine(gathered, wt, h2_all, x1_p.reshape(tp, D_MODEL), mod_p, wts, token_block, 0, sp // token_block)
    y_s = _combine(gathered, wt, h2_all, x1_s.reshape(ts, D_MODEL), mod_s, wts, ss, tp // ss, 1)

    heads = lambda a, b_, s_: a.reshape(1, b_, s_, SB_HEADS, SB_DIM)
    return (y_p.reshape(bp, sp, D_MODEL), y_s.reshape(bs, ss, D_MODEL),
            ckv_p[None], krope_p[None], heads(sbk_p, bp, sp), heads(sbv_p, bp, sp),
            ckv_s[None], krope_s[None], heads(sbk_s, bs, ss), heads(sbv_s, bs, ss))


def kernel(x_prompt, x_sample, cache_mla_ckv, cache_mla_krope, cache_sb_k, cache_sb_v, c_prompt, c_sample, w_ada, b_ada, g_norm1, w_in, g_q_lat, w_uq, g_kv_lat, w_ukv, w_mla_up, w_sb_up, w_out, g_norm2, w_router, b_router, moe_w1, moe_w3, moe_w2, shared_w1, shared_w3, shared_w2, g_final):
    wts = _prep_weights(g_norm1, w_in, g_q_lat, w_uq, g_kv_lat, w_ukv, w_mla_up, w_sb_up, w_out, g_norm2,
                        w_router, b_router, shared_w1, shared_w3, shared_w2, g_final)
    scatter_rows = functools.partial(_sc_scatter_rows, chunk=SC_CHUNK)
    gather_rows = functools.partial(_sc_gather_rows, chunk=SC_CHUNK)
    return _forward(x_prompt, x_sample, cache_mla_ckv, cache_mla_krope, cache_sb_k, cache_sb_v, c_prompt, c_sample,
                    w_ada, b_ada, moe_w1, moe_w3, moe_w2, wts, scatter_rows, gather_rows, TOKEN_BLOCK, ATTN_BLOCK,
                    TOKEN_BLOCK, MOE_BLOCK)
```

```python
import functools

import jax
import jax.numpy as jnp
from jax import lax
from jax.experimental import pallas as pl
from jax.experimental.pallas import tpu as pltpu
from jax.experimental.pallas import tpu_sc as plsc

F32 = jnp.float32
BF16 = jnp.bfloat16

D_MODEL = 1024
NORM_EPS = 1e-6
CHUNK = 64
MLA_HEADS = 8
MLA_NOPE = 64
MLA_ROPE = 32
MLA_V = 64
MLA_Q_LORA = 384
MLA_KV_LORA = 256
ROPE_THETA = 10000.0
SB_HEADS = 8
SB_DIM = 64
SB_WIDTH = SB_HEADS * SB_DIM
N_EXPERTS = 256
TOP_K = 8
N_GROUPS = 8
TOPK_GROUPS = 4
GROUP_SIZE = N_EXPERTS // N_GROUPS
EXPERT_DIM = 256
ROUTED_SCALE = 2.5

LANES = 128
SC_CORES = 2
SC_SUBCORES = 16
SC_WORKERS = SC_CORES * SC_SUBCORES
VMEM_LIMIT = 56 * 1024 * 1024

HEAD_PAD = LANES
QP_WIDTH = MLA_HEADS * HEAD_PAD
HALF = D_MODEL // 2

C_QLAT = 0
C_KV = C_QLAT + MLA_Q_LORA
C_SBQ = C_KV + MLA_KV_LORA
C_SBK = C_SBQ + SB_WIDTH
C_SBV = C_SBK + SB_WIDTH
C_GATE = C_SBV + SB_WIDTH
C_KR = C_GATE + 2 * D_MODEL
C_END = C_KR + 2 * LANES

MOE_BLOCK = 256
TOKEN_BLOCK = 256
ATTN_BLOCK = 512
SC_CHUNK = 64


def _rms(x):
    return x * lax.rsqrt(jnp.mean(x * x, axis=-1, keepdims=True) + NORM_EPS)


def _silu(x):
    return x * jax.nn.sigmoid(x)


def _pack_halves(lo, hi):
    lo_bits = lax.bitcast_convert_type(lo.astype(BF16).astype(F32), jnp.uint32) >> 16
    hi_bits = lax.bitcast_convert_type(hi.astype(BF16).astype(F32), jnp.uint32) & jnp.uint32(0xFFFF0000)
    return lax.bitcast_convert_type(lo_bits | hi_bits, jnp.int32)


def _unpack_halves(p):
    u = lax.bitcast_convert_type(p, jnp.uint32)
    lo = lax.bitcast_convert_type(u << 16, F32)
    hi = lax.bitcast_convert_type(u & jnp.uint32(0xFFFF0000), F32)
    return lo, hi


def _dot(a, b):
    return jnp.dot(a, b, preferred_element_type=F32)


def _dot_nt(a, b):
    return lax.dot_general(a, b, (((1,), (1,)), ((), ())), preferred_element_type=F32)


def _ada_kernel(c_ref, w_ref, b_ref, o_ref):
    c = c_ref[...]
    o_ref[...] = _dot(_silu(c).astype(BF16), w_ref[...].astype(BF16)) + b_ref[...]


def _ada(c, w_ada, b_ada):
    n = c.shape[0]
    width = w_ada.shape[1]
    return pl.pallas_call(
        _ada_kernel,
        grid=(width // D_MODEL,),
        in_specs=[pl.BlockSpec((n, D_MODEL), lambda j: (0, 0)),
                  pl.BlockSpec((D_MODEL, D_MODEL), lambda j: (0, j)),
                  pl.BlockSpec((1, D_MODEL), lambda j: (0, j))],
        out_specs=pl.BlockSpec((n, D_MODEL), lambda j: (0, j)),
        out_shape=jax.ShapeDtypeStruct((n, width), F32),
        name="ada",
    )(c, w_ada, b_ada.reshape(1, width))


def _in_kernel(x_ref, mod_ref, g1_ref, win_ref, gq_ref, wuq_ref, gkv_ref, wukv_ref, cos_ref, sin_ref,
               qp_ref, kmla_ref, vmla_ref, sbq_ref, sbk16_ref, sbv16_ref, gates_ref,
               ckv_ref, krope_ref, sbk_ref, sbv_ref):
    x = x_ref[0]
    mod = mod_ref[0]
    h = _rms(x) * g1_ref[...] * (1.0 + mod[1:2]) + mod[0:1]
    hb = h.astype(BF16)

    def seg(a, b):
        return _dot(hb, win_ref[:, a:b])

    cos = cos_ref[...]
    sin = sin_ref[...]
    cos8 = jnp.tile(cos, (1, MLA_HEADS))
    sin8 = jnp.tile(sin, (1, MLA_HEADS))

    qn = (_rms(seg(C_QLAT, C_KV)) * gq_ref[...]).astype(BF16)
    q2 = _dot(qn, wuq_ref[...])
    qp_ref[0] = (q2[:, :QP_WIDTH] * cos8 + q2[:, QP_WIDTH:] * sin8).astype(BF16)

    ckv = _rms(seg(C_KV, C_SBQ)) * gkv_ref[...]
    ckv_ref[0] = ckv
    kv = _dot(ckv.astype(BF16), wukv_ref[...])
    kr2 = seg(C_KR, C_END)
    krp = kr2[:, :LANES] * cos + kr2[:, LANES:] * sin
    krope_ref[0] = krp[:, MLA_NOPE:MLA_NOPE + MLA_ROPE]
    kmla_ref[0] = (kv[:, :QP_WIDTH] + jnp.tile(krp, (1, MLA_HEADS))).astype(BF16)
    vmla_ref[0] = kv[:, QP_WIDTH:].astype(BF16)

    sbq_ref[0] = seg(C_SBQ, C_SBK).astype(BF16)
    sbk = seg(C_SBK, C_SBV)
    sbk_ref[0] = sbk
    sbk16_ref[0] = sbk.astype(BF16)
    sbv = seg(C_SBV, C_GATE)
    sbv_ref[0] = sbv
    sbv16_ref[0] = sbv.astype(BF16)
    gates_ref[0] = jax.nn.sigmoid(seg(C_GATE, C_KR)).astype(BF16)


def _in_proj(x, mod, wts, cos_t, sin_t, tm):
    b, s, _ = x.shape
    ns = s // tm
    tok = lambda w: pl.BlockSpec((1, tm, w), lambda i, j: (i, j, 0))
    full = lambda a: pl.BlockSpec(a.shape, lambda i, j: (0,) * a.ndim)
    out_widths = [(QP_WIDTH, BF16), (QP_WIDTH, BF16), (SB_WIDTH, BF16), (SB_WIDTH, BF16), (SB_WIDTH, BF16),
                  (SB_WIDTH, BF16), (2 * D_MODEL, BF16), (MLA_KV_LORA, F32), (MLA_ROPE, F32),
                  (SB_WIDTH, F32), (SB_WIDTH, F32)]
    return pl.pallas_call(
        _in_kernel,
        grid=(b, ns),
        in_specs=[tok(D_MODEL),
                  pl.BlockSpec((1, 6, D_MODEL), lambda i, j: (i, 0, 0)),
                  full(wts["g1"]), full(wts["w_in"]), full(wts["gq"]), full(wts["w_uq"]),
                  full(wts["gkv"]), full(wts["w_ukv"]),
                  pl.BlockSpec((tm, LANES), lambda i, j: (j, 0)),
                  pl.BlockSpec((tm, LANES), lambda i, j: (j, 0))],
        out_specs=[tok(w) for w, _ in out_widths],
        out_shape=[jax.ShapeDtypeStruct((b, s, w), dt) for w, dt in out_widths],
        compiler_params=pltpu.CompilerParams(dimension_semantics=("parallel", "parallel"),
                                             vmem_limit_bytes=VMEM_LIMIT),
        name="in_proj",
    )(x, mod, wts["g1"], wts["w_in"], wts["gq"], wts["w_uq"], wts["gkv"], wts["w_ukv"], cos_t, sin_t)


def _kvup_kernel(ckv_ref, kr_ref, wukv_ref, place_ref, kmla_ref, vmla_ref):
    kv = _dot(ckv_ref[0].astype(BF16), wukv_ref[...])
    kr = _dot(kr_ref[0].astype(BF16), place_ref[...])
    kmla_ref[0] = (kv[:, :QP_WIDTH] + kr).astype(BF16)
    vmla_ref[0] = kv[:, QP_WIDTH:].astype(BF16)


def _kv_up(ckv, krope, w_ukv_r, place, tm):
    b, p, _ = ckv.shape
    return pl.pallas_call(
        _kvup_kernel,
        grid=(b, p // tm),
        in_specs=[pl.BlockSpec((1, tm, MLA_KV_LORA), lambda i, j: (i, j, 0)),
                  pl.BlockSpec((1, tm, MLA_ROPE), lambda i, j: (i, j, 0)),
                  pl.BlockSpec(w_ukv_r.shape, lambda i, j: (0, 0)),
                  pl.BlockSpec(place.shape, lambda i, j: (0, 0))],
        out_specs=[pl.BlockSpec((1, tm, QP_WIDTH), lambda i, j: (i, j, 0)),
                   pl.BlockSpec((1, tm, SB_WIDTH), lambda i, j: (i, j, 0))],
        out_shape=[jax.ShapeDtypeStruct((b, p, QP_WIDTH), BF16), jax.ShapeDtypeStruct((b, p, SB_WIDTH), BF16)],
        compiler_params=pltpu.CompilerParams(dimension_semantics=("parallel", "parallel")),
        name="kv_up",
    )(ckv, krope, w_ukv_r, place)


def _tri(n):
    r = lax.broadcasted_iota(jnp.int32, (n, n), 0)
    c = lax.broadcasted_iota(jnp.int32, (n, n), 1)
    return jnp.where(r > c, 1.0, 0.0).astype(BF16)


def _stick_terms(z):
    log_sig = jnp.minimum(z, 0.0) - jnp.log(1.0 + jnp.exp(-jnp.abs(z)))
    return log_sig, log_sig - z


def _split_bf16(x):
    hi = x.astype(BF16)
    return hi, (x - hi.astype(F32)).astype(BF16)


def _finish_mixer(o_mla, o_sb, gates_ref, x_ref, mod_ref, wmu_ref, wsu_ref, wo_ref, g2_ref, x1_ref, h2_ref):
    u_mla = _dot(o_mla.astype(BF16), wmu_ref[...])
    u_sb = _dot(o_sb.astype(BF16), wsu_ref[...])
    gates = gates_ref[0]
    merged = gates[:, :D_MODEL].astype(F32) * u_mla + gates[:, D_MODEL:].astype(F32) * u_sb
    mix = _dot(merged.astype(BF16), wo_ref[...])
    mod = mod_ref[0]
    x1 = x_ref[0] + mod[2:3] * mix
    x1_ref[0] = x1
    h2 = _rms(x1) * g2_ref[...] * (1.0 + mod[4:5]) + mod[3:4]
    h2_ref[0] = _pack_halves(h2[:, :HALF], h2[:, HALF:])


def _prompt_attn_kernel(qp_ref, kmla_ref, vmla_ref, sbq_ref, sbk_ref, sbv_ref,
                        gates_ref, x_ref, mod_ref, wmu_ref, wsu_ref, wo_ref, g2_ref, x1_ref, h2_ref,
                        m_ref, l_ref, acc_ref, c_ref, sacc_ref, *, tq):
    i = pl.program_id(1)
    lane = lax.broadcasted_iota(jnp.int32, (1, LANES), 1)
    half_masks = (lane < MLA_V, lane >= MLA_V)
    row = lax.broadcasted_iota(jnp.int32, (tq, tq), 0)
    col = lax.broadcasted_iota(jnp.int32, (tq, tq), 1)
    chunk_mask = (col // CHUNK) <= (row // CHUNK)
    causal_mask = col < row
    tri_m = _tri(tq)
    st_diag = pl.multiple_of(i * tq, tq)

    def load(ref, start, c0):
        return ref[0, pl.ds(start, tq), c0:c0 + LANES]

    def mla_block(sub, q_h, k, v, mask):
        s = _dot_nt(q_h, k)
        if mask is not None:
            s = jnp.where(mask, s, -jnp.inf)
            m_new = jnp.max(s, axis=-1, keepdims=True)
            p = jnp.exp(s - m_new)
            l_ref[sub] = jnp.sum(p, axis=-1, keepdims=True)
            acc_ref[sub] = _dot(p.astype(BF16), v)
        else:
            m_old = m_ref[sub]
            m_new = jnp.maximum(m_old, jnp.max(s, axis=-1, keepdims=True))
            p = jnp.exp(s - m_new)
            alpha = jnp.exp(m_old - m_new)
            l_ref[sub] = alpha * l_ref[sub] + jnp.sum(p, axis=-1, keepdims=True)
            acc_ref[sub] = alpha * acc_ref[sub] + _dot(p.astype(BF16), v)
        m_ref[sub] = m_new

    def sb_block(sub, q_h, k, v, mask):
        log_sig, log_keep = _stick_terms(_dot_nt(q_h, k))
        if mask is not None:
            log_keep = jnp.where(mask, log_keep, 0.0)
        hi, lo = _split_bf16(log_keep)
        after = _dot(hi, tri_m) + _dot(lo, tri_m)
        total = jnp.sum(log_keep, axis=-1, keepdims=True)
        if mask is not None:
            a = jnp.where(mask, jnp.exp(log_sig + after), 0.0)
            sacc_ref[sub] = _dot(a.astype(BF16), v)
            c_ref[sub] = total
        else:
            c_old = c_ref[sub]
            a = jnp.exp(log_sig + after + c_old)
            sacc_ref[sub] = sacc_ref[sub] + _dot(a.astype(BF16), v)
            c_ref[sub] = c_old + total

    o_mla = []
    o_sb = []
    for pair in range(MLA_HEADS // 2):
        vcol = pair * LANES
        kcols = [(2 * pair + sub) * HEAD_PAD for sub in range(2)]
        q_m = [qp_ref[0, :, kc:kc + HEAD_PAD] for kc in kcols]
        q_s = [jnp.where(half_masks[sub], sbq_ref[0, :, vcol:vcol + LANES], 0) for sub in range(2)]

        def pair_blocks(mla_start, sb_start, masks, vcol=vcol, kcols=kcols, q_m=q_m, q_s=q_s):
            v_m = load(vmla_ref, mla_start, vcol)
            k_s = load(sbk_ref, sb_start, vcol)
            v_s = load(sbv_ref, sb_start, vcol)
            for sub in range(2):
                keep = half_masks[sub]
                mla_block(sub, q_m[sub], load(kmla_ref, mla_start, kcols[sub]), jnp.where(keep, v_m, 0), masks[0])
                sb_block(sub, q_s[sub], k_s, jnp.where(keep, v_s, 0), masks[1])

        pair_blocks(st_diag, st_diag, (chunk_mask, causal_mask))

        def step(t, _, pair_blocks=pair_blocks):
            pair_blocks(pl.multiple_of(t * tq, tq), pl.multiple_of((i - 1 - t) * tq, tq), (None, None))
            return 0
        lax.fori_loop(0, i, step, 0)
        o_mla.append(acc_ref[0] / l_ref[0] + acc_ref[1] / l_ref[1])
        o_sb.append(sacc_ref[0] + sacc_ref[1])

    _finish_mixer(jnp.concatenate(o_mla, axis=1), jnp.concatenate(o_sb, axis=1),
                  gates_ref, x_ref, mod_ref, wmu_ref, wsu_ref, wo_ref, g2_ref, x1_ref, h2_ref)


def _decode_attn_kernel(qp_ref, kmla_ref, vmla_ref, sbq_ref, sbk_ref, sbv_ref, pkmla_ref, pvmla_ref, psbk_ref,
                        psbv_ref, gates_ref, x_ref, mod_ref, wmu_ref, wsu_ref, wo_ref, g2_ref, x1_ref, h2_ref,
                        *, tq, past_len, past_blk):
    n_past = past_len // past_blk
    lane = lax.broadcasted_iota(jnp.int32, (1, LANES), 1)
    half_masks = (lane < MLA_V, lane >= MLA_V)
    row = lax.broadcasted_iota(jnp.int32, (tq, tq), 0)
    col = lax.broadcasted_iota(jnp.int32, (tq, tq), 1)
    chunk_mask = ((past_len + col) // CHUNK) <= ((past_len + row) // CHUNK)
    causal_mask = col < row
    tri_new = _tri(tq)
    tri_past = _tri(past_blk)

    o_mla = []
    o_sb = []
    for pair in range(MLA_HEADS // 2):
        vcol = pair * LANES
        v_new = vmla_ref[0, :, vcol:vcol + LANES]
        v_past = pvmla_ref[0, :, vcol:vcol + LANES]
        sk_new = sbk_ref[0, :, vcol:vcol + LANES]
        sv_new = sbv_ref[0, :, vcol:vcol + LANES]
        sk_past = psbk_ref[0, :, vcol:vcol + LANES].astype(BF16)
        sv_past = psbv_ref[0, :, vcol:vcol + LANES].astype(BF16)
        mla_pair = None
        sb_pair = None
        for sub in range(2):
            keep = half_masks[sub]
            kcol = (2 * pair + sub) * HEAD_PAD

            q_h = qp_ref[0, :, kcol:kcol + HEAD_PAD]
            s_past = _dot_nt(q_h, pkmla_ref[0, :, kcol:kcol + HEAD_PAD])
            s_new = jnp.where(chunk_mask, _dot_nt(q_h, kmla_ref[0, :, kcol:kcol + HEAD_PAD]), -jnp.inf)
            m = jnp.maximum(jnp.max(s_past, axis=-1, keepdims=True), jnp.max(s_new, axis=-1, keepdims=True))
            p_past = jnp.exp(s_past - m)
            p_new = jnp.exp(s_new - m)
            denom = jnp.sum(p_past, axis=-1, keepdims=True) + jnp.sum(p_new, axis=-1, keepdims=True)
            o = (_dot(p_past.astype(BF16), jnp.where(keep, v_past, 0))
                 + _dot(p_new.astype(BF16), jnp.where(keep, v_new, 0))) / denom
            mla_pair = o if mla_pair is None else mla_pair + o

            q_s = jnp.where(keep, sbq_ref[0, :, vcol:vcol + LANES], 0)
            ls_new, lk_new = _stick_terms(_dot_nt(q_s, sk_new))
            lk_new = jnp.where(causal_mask, lk_new, 0.0)
            hi, lo = _split_bf16(lk_new)
            a_new = jnp.where(causal_mask, jnp.exp(ls_new + _dot(hi, tri_new) + _dot(lo, tri_new)), 0.0)
            acc = _dot(a_new.astype(BF16), jnp.where(keep, sv_new, 0))
            later = jnp.sum(lk_new, axis=-1, keepdims=True)

            ls_past, lk_past = _stick_terms(_dot_nt(q_s, sk_past))
            hi, lo = _split_bf16(lk_past)
            blocks = lambda a: [a[:, b * past_blk:(b + 1) * past_blk] for b in range(n_past)]
            stacked = jnp.concatenate(blocks(hi) + blocks(lo), axis=0)
            within = _dot(stacked, tri_past)
            after = []
            for b in reversed(range(n_past)):
                after.append(within[b * tq:(b + 1) * tq] + within[(n_past + b) * tq:(n_past + b + 1) * tq] + later)
                later = later + jnp.sum(lk_past[:, b * past_blk:(b + 1) * past_blk], axis=-1, keepdims=True)
            a_past = jnp.exp(ls_past + jnp.concatenate(after[::-1], axis=1))
            acc = acc + _dot(a_past.astype(BF16), jnp.where(keep, sv_past, 0))
            sb_pair = acc if sb_pair is None else sb_pair + acc
        o_mla.append(mla_pair)
        o_sb.append(sb_pair)

    _finish_mixer(jnp.concatenate(o_mla, axis=1), jnp.concatenate(o_sb, axis=1),
                  gates_ref, x_ref, mod_ref, wmu_ref, wsu_ref, wo_ref, g2_ref, x1_ref, h2_ref)


def _mixer_call(kernel_fn, name, tq, args_kv, gates, x, mod, wts, scratch, batch0=0, n_batch=None):
    s = x.shape[1]
    b = x.shape[0] if n_batch is None else n_batch
    tok = lambda w: pl.BlockSpec((1, tq, w), lambda i, j: (i + batch0, j, 0))
    seq = lambda a: pl.BlockSpec((1,) + a.shape[1:], lambda i, j: (i + batch0, 0, 0))
    full = lambda a: pl.BlockSpec(a.shape, lambda i, j: (0,) * a.ndim)
    out = lambda w: pl.BlockSpec((1, tq, w), lambda i, j: (i, j, 0))
    args = list(args_kv) + [gates, x, mod, wts["w_mla_up"], wts["w_sb_up"], wts["w_out"], wts["g2"]]
    specs = [seq(a) if whole else tok(a.shape[-1]) for a, whole in
             zip(args_kv, (False, True, True, False) + (True,) * (len(args_kv) - 4))]
    specs += [tok(2 * D_MODEL), tok(D_MODEL), pl.BlockSpec((1, 6, D_MODEL), lambda i, j: (i + batch0, 0, 0)),
              full(wts["w_mla_up"]), full(wts["w_sb_up"]), full(wts["w_out"]), full(wts["g2"])]
    return pl.pallas_call(
        kernel_fn,
        grid=(b, s // tq),
        in_specs=specs,
        out_specs=[out(D_MODEL), out(HALF)],
        out_shape=[jax.ShapeDtypeStruct((b, s, D_MODEL), F32), jax.ShapeDtypeStruct((b, s, HALF), jnp.int32)],
        scratch_shapes=scratch,
        compiler_params=pltpu.CompilerParams(dimension_semantics=("parallel", "arbitrary"),
                                             vmem_limit_bytes=VMEM_LIMIT),
        name=name,
    )(*args)


def _prompt_attention(qp, kmla, vmla, sbq, sbk16, sbv16, gates, x, mod, wts, tq, batch0, n_batch):
    col = lambda: pltpu.VMEM((2, tq, 1), F32)
    wide = lambda: pltpu.VMEM((2, tq, LANES), F32)
    return _mixer_call(functools.partial(_prompt_attn_kernel, tq=tq), "attention", tq,
                       (qp, kmla, vmla, sbq, sbk16, sbv16), gates, x, mod, wts,
                       [col(), col(), wide(), col(), wide()], batch0, n_batch)


def _decode_attention(qp, kmla, vmla, sbq, sbk16, sbv16, past, gates, x, mod, wts, past_blk):
    tq = x.shape[1]
    past_len = past[0].shape[1]
    kern = functools.partial(_decode_attn_kernel, tq=tq, past_len=past_len, past_blk=past_blk)
    return _mixer_call(kern, "decode_attention", tq, (qp, kmla, vmla, sbq, sbk16, sbv16) + tuple(past),
                       gates, x, mod, wts, [])


def _route_kernel(h2_ref, wr_ref, br_ref, idx_ref, wt_ref, rank_ref, cnt_ref, seen_ref):
    lo, hi = _unpack_halves(h2_ref[...])
    tm = lo.shape[0]
    logits = _dot_nt(wr_ref[:, :HALF], lo.astype(BF16)) + _dot_nt(wr_ref[:, HALF:], hi.astype(BF16))
    scores = jax.nn.sigmoid(logits)
    sel = scores + br_ref[...]
    neg = -jnp.inf

    grp = sel.reshape(N_GROUPS, GROUP_SIZE, tm)
    within = lax.broadcasted_iota(jnp.int32, grp.shape, 1)
    top1 = jnp.max(grp, axis=1, keepdims=True)
    first = jnp.min(jnp.where(grp == top1, within, GROUP_SIZE), axis=1, keepdims=True)
    top2 = jnp.max(jnp.where(within == first, neg, grp), axis=1, keepdims=True)
    gscore = (top1 + top2).reshape(N_GROUPS, tm)

    gid = lax.broadcasted_iota(jnp.int32, gscore.shape, 0)
    chosen = jnp.zeros(gscore.shape, jnp.bool_)
    for _ in range(TOPK_GROUPS):
        best = jnp.max(gscore, axis=0, keepdims=True)
        pick = jnp.min(jnp.where(gscore == best, gid, N_GROUPS), axis=0, keepdims=True)
        hit = gid == pick
        chosen = jnp.logical_or(chosen, hit)
        gscore = jnp.where(hit, neg, gscore)
    chosen3 = jnp.broadcast_to(chosen.reshape(N_GROUPS, 1, tm), grp.shape)
    cand = jnp.where(chosen3, grp, neg).reshape(N_EXPERTS, tm)

    eid = lax.broadcasted_iota(jnp.int32, cand.shape, 0)
    picks = []
    weights = []
    for _ in range(TOP_K):
        best = jnp.max(cand, axis=0, keepdims=True)
        pick = jnp.min(jnp.where(cand == best, eid, N_EXPERTS), axis=0, keepdims=True)
        hit = eid == pick
        weights.append(jnp.sum(jnp.where(hit, scores, 0.0), axis=0, keepdims=True))
        picks.append(pick)
        cand = jnp.where(hit, neg, cand)
    w = jnp.concatenate(weights, axis=0)
    idx_ref[...] = jnp.concatenate(picks, axis=0)
    wt_ref[...] = w / (jnp.sum(w, axis=0, keepdims=True) + 1e-20) * ROUTED_SCALE

    @pl.when(pl.program_id(0) == 0)
    def _():
        seen_ref[...] = jnp.zeros_like(seen_ref)

    onehot = jnp.zeros(cand.shape, F32)
    for pick in picks:
        onehot = onehot + jnp.where(eid == pick, 1.0, 0.0)
    src = lax.broadcasted_iota(jnp.int32, (tm, tm), 0)
    dst = lax.broadcasted_iota(jnp.int32, (tm, tm), 1)
    earlier = jnp.where(src < dst, 1.0, 0.0).astype(BF16)
    before = _dot(onehot.astype(BF16), earlier) + seen_ref[...]
    rank_ref[...] = jnp.concatenate(
        [jnp.sum(jnp.where(eid == pick, before, 0.0), axis=0, keepdims=True) for pick in picks],
        axis=0).astype(jnp.int32)
    seen = seen_ref[...] + jnp.sum(onehot, axis=1, keepdims=True)
    seen_ref[...] = seen
    cnt_ref[...] = seen.astype(jnp.int32)


def _route(h2p, wr_t, br, tm):
    t = h2p.shape[0]
    kt = lambda: pl.BlockSpec((TOP_K, tm), lambda i: (0, i))
    return pl.pallas_call(
        _route_kernel,
        grid=(t // tm,),
        in_specs=[pl.BlockSpec((tm, HALF), lambda i: (i, 0)),
                  pl.BlockSpec(wr_t.shape, lambda i: (0, 0)),
                  pl.BlockSpec(br.shape, lambda i: (0, 0))],
        out_specs=[kt(), kt(), kt(), pl.BlockSpec((N_EXPERTS, 1), lambda i: (0, 0))],
        out_shape=[jax.ShapeDtypeStruct((TOP_K, t), jnp.int32), jax.ShapeDtypeStruct((TOP_K, t), F32),
                   jax.ShapeDtypeStruct((TOP_K, t), jnp.int32), jax.ShapeDtypeStruct((N_EXPERTS, 1), jnp.int32)],
        scratch_shapes=[pltpu.VMEM((N_EXPERTS, 1), F32)],
        compiler_params=pltpu.CompilerParams(dimension_semantics=("arbitrary",)),
        name="route",
    )(h2p, wr_t, br)


def _position_kernel(idx_ref, rank_ref, start_ref, pos_ref):
    idx = idx_ref[...]
    eid = lax.broadcasted_iota(jnp.int32, (N_EXPERTS, idx.shape[1]), 0)
    start = start_ref[...]
    base = jnp.concatenate(
        [jnp.sum(jnp.where(eid == idx[k:k + 1, :], start, 0.0), axis=0, keepdims=True) for k in range(TOP_K)],
        axis=0)
    pos_ref[...] = base.astype(jnp.int32) + rank_ref[...]


def _positions(idx_kt, rank_kt, pad_start, tm):
    t = idx_kt.shape[1]
    kt = lambda: pl.BlockSpec((TOP_K, tm), lambda i: (0, i))
    return pl.pallas_call(
        _position_kernel,
        grid=(t // tm,),
        in_specs=[kt(), kt(), pl.BlockSpec((N_EXPERTS, 1), lambda i: (0, 0))],
        out_specs=kt(),
        out_shape=jax.ShapeDtypeStruct((TOP_K, t), jnp.int32),
        compiler_params=pltpu.CompilerParams(dimension_semantics=("parallel",)),
        name="positions",
    )(idx_kt, rank_kt, pad_start.astype(F32).reshape(N_EXPERTS, 1))


def _sc_mesh():
    return plsc.VectorSubcoreMesh(core_axis_name="c", subcore_axis_name="s",
                                  num_cores=SC_CORES, num_subcores=SC_SUBCORES)


def _sc_scatter_rows(rows, pos, n_out, chunk):
    t, width = rows.shape
    copies = pos.shape[0] // t
    per_worker = t // SC_WORKERS
    n_chunks = per_worker // chunk
    tail = per_worker - n_chunks * chunk
    assert per_worker * SC_WORKERS == t and tail % 8 == 0
    tail_rows = max(tail, 8)

    @functools.partial(
        pl.kernel, mesh=_sc_mesh(),
        out_type=jax.ShapeDtypeStruct((n_out, width), rows.dtype),
        scratch_types=[pltpu.VMEM((copies, chunk), jnp.int32), pltpu.VMEM((chunk, width), rows.dtype),
                       pltpu.VMEM((copies, tail_rows), jnp.int32), pltpu.VMEM((tail_rows, width), rows.dtype),
                       pltpu.SemaphoreType.DMA, pltpu.SemaphoreType.DMA],
    )
    def scatter(rows_hbm, pos_hbm, out_hbm, idx_v, rows_v, idx_t, rows_t, load_sem, store_sem):
        wid = lax.axis_index("s") * SC_CORES + lax.axis_index("c")
        base = wid * per_worker

        def move(off, n, idx_buf, row_buf):
            loads = [pltpu.async_copy(rows_hbm.at[pl.ds(off, n)], row_buf, load_sem)]
            for k in range(copies):
                src = pos_hbm.at[pl.ds(pl.multiple_of(k * t + off, 8), n)]
                loads.append(pltpu.async_copy(src, idx_buf.at[k], load_sem))
            for cp in loads:
                cp.wait()
            stores = [pltpu.async_copy(row_buf, out_hbm.at[idx_buf.at[k]], store_sem) for k in range(copies)]
            for cp in stores:
                cp.wait()

        @pl.loop(0, n_chunks)
        def _(c):
            move(pl.multiple_of(base + c * chunk, 8), chunk, idx_v, rows_v)

        if tail:
            move(pl.multiple_of(base + n_chunks * chunk, 8), tail, idx_t, rows_t)

    return scatter(rows, pos)


def _sc_gather_rows(table, idx, chunk):
    n_rows = idx.shape[0]
    width = table.shape[1]
    per_worker = n_rows // SC_WORKERS
    n_chunks = per_worker // chunk
    assert per_worker * SC_WORKERS == n_rows and n_chunks * chunk == per_worker
    mesh = _sc_mesh()

    @functools.partial(
        pl.kernel, mesh=mesh,
        out_type=jax.ShapeDtypeStruct((n_rows, width), table.dtype),
        scratch_types=[pltpu.VMEM((chunk,), jnp.int32), pltpu.VMEM((chunk, width), table.dtype),
                       pltpu.SemaphoreType.DMA],
    )
    def gather(table_hbm, idx_hbm, out_hbm, idx_v, rows_v, sem):
        wid = lax.axis_index("s") * SC_CORES + lax.axis_index("c")
        base = wid * per_worker

        @pl.loop(0, n_chunks)
        def _(c):
            off = pl.multiple_of(base + c * chunk, 8)
            pltpu.sync_copy(idx_hbm.at[pl.ds(off, chunk)], idx_v)
            pltpu.async_copy(table_hbm.at[idx_v], rows_v, sem).wait()
            pltpu.sync_copy(rows_v, out_hbm.at[pl.ds(off, chunk)])

    return gather(table, idx)


def _expert_kernel(be_ref, nv_ref, nu_ref, xs_ref, w1_ref, w3_ref, w2_ref, os_ref, wb1, wb3, wb2):
    i = pl.program_id(0)

    @pl.when(i < nu_ref[0])
    def _():
        @pl.when(jnp.logical_or(i == 0, be_ref[i] != be_ref[jnp.maximum(i - 1, 0)]))
        def _():
            wb1[...] = w1_ref[0].astype(BF16)
            wb3[...] = w3_ref[0].astype(BF16)
            wb2[...] = w2_ref[0].astype(BF16)

        packed = xs_ref[...]
        live = lax.broadcasted_iota(jnp.int32, packed.shape, 0) < nv_ref[i]
        lo, hi = _unpack_halves(jnp.where(live, packed, 0))
        lo = lo.astype(BF16)
        hi = hi.astype(BF16)
        a = _dot(lo, wb1[:HALF, :]) + _dot(hi, wb1[HALF:, :])
        b = _dot(lo, wb3[:HALF, :]) + _dot(hi, wb3[HALF:, :])
        o = _dot((_silu(a) * b).astype(BF16), wb2[...])
        os_ref[...] = _pack_halves(o[:, :HALF], o[:, HALF:])


def _experts(xs, blk_exp, blk_valid, n_used, w1, w3, w2, blk):
    rows = xs.shape[0]
    nb = rows // blk
    last = lambda i, nu: jnp.minimum(i, nu[0] - 1)
    grid_spec = pltpu.PrefetchScalarGridSpec(
        num_scalar_prefetch=3,
        grid=(nb,),
        in_specs=[pl.BlockSpec((blk, HALF), lambda i, be, nv, nu: (last(i, nu), 0)),
                  pl.BlockSpec((1, D_MODEL, EXPERT_DIM), lambda i, be, nv, nu: (be[last(i, nu)], 0, 0)),
                  pl.BlockSpec((1, D_MODEL, EXPERT_DIM), lambda i, be, nv, nu: (be[last(i, nu)], 0, 0)),
                  pl.BlockSpec((1, EXPERT_DIM, D_MODEL), lambda i, be, nv, nu: (be[last(i, nu)], 0, 0))],
        out_specs=pl.BlockSpec((blk, HALF), lambda i, be, nv, nu: (last(i, nu), 0)),
        scratch_shapes=[pltpu.VMEM((D_MODEL, EXPERT_DIM), BF16), pltpu.VMEM((D_MODEL, EXPERT_DIM), BF16),
                        pltpu.VMEM((EXPERT_DIM, D_MODEL), BF16)],
    )
    return pl.pallas_call(
        _expert_kernel,
        grid_spec=grid_spec,
        out_shape=jax.ShapeDtypeStruct((rows, HALF), jnp.int32),
        compiler_params=pltpu.CompilerParams(dimension_semantics=("arbitrary",), vmem_limit_bytes=VMEM_LIMIT),
        name="experts",
    )(blk_exp, blk_valid, n_used, xs, w1, w3, w2)


def _combine_kernel(g_ref, wt_ref, h2_ref, x1_ref, mod_ref, sw1_ref, sw3_ref, sw2_ref, gf_ref, y_ref):
    wt = wt_ref[...]
    lo_acc = None
    for k in range(TOP_K):
        lo, hi = _unpack_halves(g_ref[k])
        wk = wt[:, k:k + 1]
        lo_acc = wk * lo if lo_acc is None else lo_acc + wk * lo
        hi_acc = wk * hi if k == 0 else hi_acc + wk * hi
    routed = jnp.concatenate([lo_acc, hi_acc], axis=1)
    lo, hi = _unpack_halves(h2_ref[...])
    lo = lo.astype(BF16)
    hi = hi.astype(BF16)
    a = _dot(lo, sw1_ref[:HALF, :]) + _dot(hi, sw1_ref[HALF:, :])
    b = _dot(lo, sw3_ref[:HALF, :]) + _dot(hi, sw3_ref[HALF:, :])
    shared = _dot((_silu(a) * b).astype(BF16), sw2_ref[...])
    mod = mod_ref[0]
    x2 = x1_ref[...] + mod[5:6] * (routed + shared)
    y_ref[...] = _rms(x2) * gf_ref[...]


def _combine_into_kernel(g_ref, wt_ref, h2_ref, x1_ref, mod_ref, sw1_ref, sw3_ref, sw2_ref, gf_ref, prev_ref, y_ref):
    del prev_ref
    _combine_kernel(g_ref, wt_ref, h2_ref, x1_ref, mod_ref, sw1_ref, sw3_ref, sw2_ref, gf_ref, y_ref)


def _combine(gathered, wt, h2p, x1, mod, wts, tm, src_tile0, mod_batch0, tiles_per_batch, out_rows, out_tile0,
             y_prev=None):
    t = x1.shape[0]
    full = lambda a: pl.BlockSpec(a.shape, lambda i: (0,) * a.ndim)
    shifted = lambda w: pl.BlockSpec((tm, w), lambda i: (i + src_tile0, 0))
    args = [gathered, wt, h2p, x1, mod, wts["sw1"], wts["sw3"], wts["sw2"], wts["gf"]]
    specs = [pl.BlockSpec((TOP_K, tm, HALF), lambda i: (0, i + src_tile0, 0)), shifted(TOP_K), shifted(HALF),
             pl.BlockSpec((tm, D_MODEL), lambda i: (i, 0)),
             pl.BlockSpec((1, 6, D_MODEL), lambda i: (i // tiles_per_batch + mod_batch0, 0, 0)),
             full(wts["sw1"]), full(wts["sw3"]), full(wts["sw2"]), full(wts["gf"])]
    aliases = {}
    body = _combine_kernel
    if y_prev is not None:
        args.append(y_prev)
        specs.append(pl.BlockSpec(memory_space=pl.ANY))
        aliases = {len(args) - 1: 0}
        body = _combine_into_kernel
    return pl.pallas_call(
        body,
        grid=(t // tm,),
        in_specs=specs,
        out_specs=pl.BlockSpec((tm, D_MODEL), lambda i: (i + out_tile0, 0)),
        out_shape=jax.ShapeDtypeStruct((out_rows, D_MODEL), F32),
        input_output_aliases=aliases,
        compiler_params=pltpu.CompilerParams(dimension_semantics=("parallel",), vmem_limit_bytes=VMEM_LIMIT),
        name="combine",
    )(*args)


def _rope_tables(pos):
    half = MLA_ROPE // 2
    inv_freq = ROPE_THETA ** (-jnp.arange(half, dtype=F32) / half)
    ang = pos.astype(F32)[:, None] * inv_freq
    cos, sin = jnp.cos(ang), jnp.sin(ang)
    n = pos.shape[0]
    ones = jnp.ones((n, MLA_NOPE), F32)
    z_nope = jnp.zeros((n, MLA_NOPE), F32)
    z_pad = jnp.zeros((n, HEAD_PAD - MLA_NOPE - MLA_ROPE), F32)
    return (jnp.concatenate([ones, cos, cos, z_pad], axis=1),
            jnp.concatenate([z_nope, -sin, sin, z_pad], axis=1))


def _prep_weights(g_norm1, w_in, g_q_lat, w_uq, g_kv_lat, w_ukv, w_mla_up, w_sb_up, w_out, g_norm2,
                  w_router, b_router, shared_w1, shared_w3, shared_w2, g_final):
    half = MLA_ROPE // 2
    w = w_in[0]
    o = 0
    parts = {}
    for name, width in (("qlat", MLA_Q_LORA), ("kv", MLA_KV_LORA), ("kr", MLA_ROPE), ("sbq", SB_WIDTH),
                        ("sbk", SB_WIDTH), ("sbv", SB_WIDTH), ("gm", D_MODEL), ("gs", D_MODEL)):
        parts[name] = w[:, o:o + width]
        o += width
    kr = parts["kr"]
    z_nope = jnp.zeros((D_MODEL, MLA_NOPE), F32)
    z_pad = jnp.zeros((D_MODEL, HEAD_PAD - MLA_NOPE - MLA_ROPE), F32)
    kr_seg = jnp.concatenate([z_nope, kr, z_pad, z_nope, kr[:, half:], kr[:, :half], z_pad], axis=1)
    w_in_r = jnp.concatenate([parts["qlat"], parts["kv"], parts["sbq"] * (SB_DIM ** -0.5), parts["sbk"],
                              parts["sbv"], parts["gm"], parts["gs"], kr_seg], axis=1).astype(BF16)

    scale = (MLA_NOPE + MLA_ROPE) ** -0.5
    uq = w_uq[0].reshape(MLA_Q_LORA, MLA_HEADS, MLA_NOPE + MLA_ROPE) * scale
    nope, rope = uq[..., :MLA_NOPE], uq[..., MLA_NOPE:]
    zq_nope = jnp.zeros_like(nope)
    zq_pad = jnp.zeros((MLA_Q_LORA, MLA_HEADS, HEAD_PAD - MLA_NOPE - MLA_ROPE), F32)
    main = jnp.concatenate([nope, rope, zq_pad], axis=-1).reshape(MLA_Q_LORA, QP_WIDTH)
    swapped = jnp.concatenate([zq_nope, rope[..., half:], rope[..., :half], zq_pad], axis=-1)
    w_uq_r = jnp.concatenate([main, swapped.reshape(MLA_Q_LORA, QP_WIDTH)], axis=1).astype(BF16)

    ukv = w_ukv[0].reshape(MLA_KV_LORA, MLA_HEADS, MLA_NOPE + MLA_V)
    k_nope, v = ukv[..., :MLA_NOPE], ukv[..., MLA_NOPE:]
    k_pad = jnp.concatenate([k_nope, jnp.zeros((MLA_KV_LORA, MLA_HEADS, HEAD_PAD - MLA_NOPE), F32)], axis=-1)
    w_ukv_r = jnp.concatenate([k_pad.reshape(MLA_KV_LORA, QP_WIDTH),
                               v.reshape(MLA_KV_LORA, MLA_HEADS * MLA_V)], axis=1).astype(BF16)

    eye = jnp.eye(MLA_ROPE, dtype=F32)
    place_head = jnp.concatenate([jnp.zeros((MLA_ROPE, MLA_NOPE), F32), eye,
                                  jnp.zeros((MLA_ROPE, HEAD_PAD - MLA_NOPE - MLA_ROPE), F32)], axis=1)
    place = jnp.tile(place_head, (1, MLA_HEADS)).astype(BF16)

    return dict(
        g1=g_norm1[0].reshape(1, D_MODEL), w_in=w_in_r, gq=g_q_lat[0].reshape(1, MLA_Q_LORA), w_uq=w_uq_r,
        gkv=g_kv_lat[0].reshape(1, MLA_KV_LORA), w_ukv=w_ukv_r, place=place,
        w_mla_up=w_mla_up[0].astype(BF16), w_sb_up=w_sb_up[0].astype(BF16), w_out=w_out[0].astype(BF16),
        g2=g_norm2[0].reshape(1, D_MODEL), wr_t=w_router[0].T.astype(BF16),
        br=b_router[0].reshape(N_EXPERTS, 1),
        sw1=shared_w1[0].astype(BF16), sw3=shared_w3[0].astype(BF16), sw2=shared_w2[0].astype(BF16),
        gf=g_final.reshape(1, D_MODEL))


def _block_plan(counts, nb, blk):
    padded = (counts + blk - 1) // blk * blk
    pad_end = jnp.cumsum(padded)
    pad_start = pad_end - padded
    first_row = jnp.arange(nb, dtype=jnp.int32) * blk
    blk_exp = jnp.minimum(jnp.sum((pad_end[None, :] <= first_row[:, None]).astype(jnp.int32), axis=1),
                          N_EXPERTS - 1)
    own = blk_exp[:, None] == jnp.arange(N_EXPERTS, dtype=jnp.int32)[None, :]
    seg_end = jnp.sum(jnp.where(own, (pad_start + counts)[None, :], 0), axis=1)
    blk_valid = jnp.clip(seg_end - first_row, 0, blk).astype(jnp.int32)
    n_used = (pad_end[-1:] // blk).astype(jnp.int32)
    return pad_start, blk_exp.astype(jnp.int32), blk_valid, n_used


def _moe_rows(h2p, idx_kt, rank_kt, counts, w1, w3, w2, scatter_rows, gather_rows, blk, tm):
    t = h2p.shape[0]
    nb = -(-t * TOP_K // blk) + N_EXPERTS
    pad_start, blk_exp, blk_valid, n_used = _block_plan(counts.reshape(N_EXPERTS), nb, blk)
    pos = _positions(idx_kt, rank_kt, pad_start, tm).reshape(TOP_K * t)
    xs = scatter_rows(h2p, pos, nb * blk)
    os_ = _experts(xs, blk_exp, blk_valid, n_used, w1, w3, w2, blk)
    return gather_rows(os_, pos).reshape(TOP_K, t, HALF)


def _forward(x_prompt, x_sample, cache_mla_ckv, cache_mla_krope, cache_sb_k, cache_sb_v, c_prompt, c_sample,
             w_ada, b_ada, moe_w1, moe_w3, moe_w2, wts, scatter_rows, gather_rows, token_block, attn_block,
             route_block, moe_block):
    bp, sp, _ = x_prompt.shape
    bs, ss, _ = x_sample.shape
    past_len = cache_mla_ckv.shape[2]

    mod = _ada(jnp.concatenate([c_prompt, c_sample], axis=0), w_ada[0], b_ada[0]).reshape(bp + bs, 6, D_MODEL)
    mod_p, mod_s = mod[:bp], mod[bp:]

    cos_p, sin_p = _rope_tables(jnp.arange(sp))
    (qp, kmla, vmla, sbq, sbk16, sbv16, gates, ckv_p, krope_p, sbk_p, sbv_p) = _in_proj(
        x_prompt, mod_p, wts, cos_p, sin_p, token_block)
    half_b = bp // 2
    mixed = [_prompt_attention(qp, kmla, vmla, sbq, sbk16, sbv16, gates, x_prompt, mod_p, wts, attn_block,
                               b0, half_b) for b0 in (0, half_b)]

    cos_s, sin_s = _rope_tables(past_len + jnp.arange(ss))
    (qs, kmla_s, vmla_s, sbq_s, sbk16_s, sbv16_s, gates_s, ckv_s, krope_s, sbk_s, sbv_s) = _in_proj(
        x_sample, mod_s, wts, cos_s, sin_s, ss)
    pkmla, pvmla = _kv_up(cache_mla_ckv[0], cache_mla_krope[0], wts["w_ukv"], wts["place"], token_block)
    past = (pkmla, pvmla, cache_sb_k[0].reshape(bs, past_len, SB_WIDTH), cache_sb_v[0].reshape(bs, past_len, SB_WIDTH))
    x1_s, h2_s = _decode_attention(qs, kmla_s, vmla_s, sbq_s, sbk16_s, sbv16_s, past, gates_s, x_sample, mod_s,
                                   wts, token_block)

    tp, ts = bp * sp, bs * ss
    th = half_b * sp

    def moe(h2_rows):
        idx_kt, wt_kt, rank_kt, counts = _route(h2_rows, wts["wr_t"], wts["br"], route_block)
        gathered = _moe_rows(h2_rows, idx_kt, rank_kt, counts, moe_w1[0], moe_w3[0], moe_w2[0],
                             scatter_rows, gather_rows, moe_block, route_block)
        return gathered, wt_kt.T

    tiles_per_batch = sp // token_block
    (x1_a, h2_a), (x1_b, h2_b) = mixed
    h2_a = h2_a.reshape(th, HALF)
    g_a, wt_a = moe(h2_a)
    y_p = _combine(g_a, wt_a, h2_a, x1_a.reshape(th, D_MODEL), mod_p, wts, token_block,
                   0, 0, tiles_per_batch, tp, 0)
    h2_b = jnp.concatenate([h2_b.reshape(th, HALF), h2_s.reshape(ts, HALF)], axis=0)
    g_b, wt_b = moe(h2_b)
    y_p = _combine(g_b, wt_b, h2_b, x1_b.reshape(th, D_MODEL), mod_p, wts, token_block,
                   0, half_b, tiles_per_batch, tp, th // token_block, y_prev=y_p)
    y_s = _combine(g_b, wt_b, h2_b, x1_s.reshape(ts, D_MODEL), mod_s, wts, ss, th // ss, 0, 1, ts, 0)

    heads = lambda a, b_, s_: a.reshape(1, b_, s_, SB_HEADS, SB_DIM)
    return (y_p.reshape(bp, sp, D_MODEL), y_s.reshape(bs, ss, D_MODEL),
            ckv_p[None], krope_p[None], heads(sbk_p, bp, sp), heads(sbv_p, bp, sp),
            ckv_s[None], krope_s[None], heads(sbk_s, bs, ss), heads(sbv_s, bs, ss))


def kernel(x_prompt, x_sample, cache_mla_ckv, cache_mla_krope, cache_sb_k, cache_sb_v, c_prompt, c_sample, w_ada, b_ada, g_norm1, w_in, g_q_lat, w_uq, g_kv_lat, w_ukv, w_mla_up, w_sb_up, w_out, g_norm2, w_router, b_router, moe_w1, moe_w3, moe_w2, shared_w1, shared_w3, shared_w2, g_final):
    wts = _prep_weights(g_norm1, w_in, g_q_lat, w_uq, g_kv_lat, w_ukv, w_mla_up, w_sb_up, w_out, g_norm2,
                        w_router, b_router, shared_w1, shared_w3, shared_w2, g_final)
    scatter_rows = functools.partial(_sc_scatter_rows, chunk=SC_CHUNK)
    gather_rows = functools.partial(_sc_gather_rows, chunk=SC_CHUNK)
    return _forward(x_prompt, x_sample, cache_mla_ckv, cache_mla_krope, cache_sb_k, cache_sb_v, c_prompt, c_sample,
                    w_ada, b_ada, moe_w1, moe_w3, moe_w2, wts, scatter_rows, gather_rows, TOKEN_BLOCK, ATTN_BLOCK,
                    TOKEN_BLOCK, MOE_BLOCK)
```

```python
import functools

import jax
import jax.numpy as jnp
from jax import lax
from jax.experimental import pallas as pl
from jax.experimental.pallas import tpu as pltpu
from jax.experimental.pallas import tpu_sc as plsc

F32 = jnp.float32
BF16 = jnp.bfloat16

D_MODEL = 1024
NORM_EPS = 1e-6
CHUNK = 64
MLA_HEADS = 8
MLA_NOPE = 64
MLA_ROPE = 32
MLA_V = 64
MLA_Q_LORA = 384
MLA_KV_LORA = 256
ROPE_THETA = 10000.0
SB_HEADS = 8
SB_DIM = 64
SB_WIDTH = SB_HEADS * SB_DIM
N_EXPERTS = 256
TOP_K = 8
N_GROUPS = 8
TOPK_GROUPS = 4
GROUP_SIZE = N_EXPERTS // N_GROUPS
EXPERT_DIM = 256
ROUTED_SCALE = 2.5

LANES = 128
MXU_TILE = 256
SC_CORES = 2
SC_SUBCORES = 16
SC_WORKERS = SC_CORES * SC_SUBCORES
VMEM_LIMIT = 60 * 1024 * 1024

HEAD_PAD = LANES
QP_WIDTH = MLA_HEADS * HEAD_PAD
HALF = D_MODEL // 2

C_QLAT = 0
C_KV = C_QLAT + MLA_Q_LORA
C_SBQ = C_KV + MLA_KV_LORA
C_SBK = C_SBQ + SB_WIDTH
C_SBV = C_SBK + SB_WIDTH
C_GATE = C_SBV + SB_WIDTH
C_KR = C_GATE + 2 * D_MODEL
C_END = C_KR + 2 * LANES

MOE_BLOCK = 512
TOKEN_BLOCK = 256
ATTN_BLOCK = 512
SC_CHUNK = 64


def _rms(x):
    return x * lax.rsqrt(jnp.mean(x * x, axis=-1, keepdims=True) + NORM_EPS)


def _silu(x):
    return x * jax.nn.sigmoid(x)


def _pack_halves(lo, hi):
    lo_bits = lax.bitcast_convert_type(lo.astype(BF16).astype(F32), jnp.uint32) >> 16
    hi_bits = lax.bitcast_convert_type(hi.astype(BF16).astype(F32), jnp.uint32) & jnp.uint32(0xFFFF0000)
    return lax.bitcast_convert_type(lo_bits | hi_bits, jnp.int32)


def _unpack_halves(p):
    u = lax.bitcast_convert_type(p, jnp.uint32)
    lo = lax.bitcast_convert_type(u << 16, F32)
    hi = lax.bitcast_convert_type(u & jnp.uint32(0xFFFF0000), F32)
    return lo, hi


def _dot(a, b):
    return jnp.dot(a, b, preferred_element_type=F32)


def _dot_nt(a, b):
    return lax.dot_general(a, b, (((1,), (1,)), ((), ())), preferred_element_type=F32)


def _ada_kernel(c_ref, w_ref, b_ref, o_ref):
    c = c_ref[...]
    o_ref[...] = _dot(_silu(c).astype(BF16), w_ref[...].astype(BF16)) + b_ref[...]


def _ada(c, w_ada, b_ada):
    n = c.shape[0]
    width = w_ada.shape[1]
    return pl.pallas_call(
        _ada_kernel,
        grid=(width // D_MODEL,),
        in_specs=[pl.BlockSpec((n, D_MODEL), lambda j: (0, 0)),
                  pl.BlockSpec((D_MODEL, D_MODEL), lambda j: (0, j)),
                  pl.BlockSpec((1, D_MODEL), lambda j: (0, j))],
        out_specs=pl.BlockSpec((n, D_MODEL), lambda j: (0, j)),
        out_shape=jax.ShapeDtypeStruct((n, width), F32),
        name="ada",
    )(c, w_ada, b_ada.reshape(1, width))


def _in_kernel(x_ref, mod_ref, g1_ref, win_ref, gq_ref, wuq_ref, gkv_ref, wukv_ref, cos_ref, sin_ref,
               qp_ref, kmla_ref, vmla_ref, sbq_ref, sbk16_ref, sbv16_ref, gates_ref,
               ckv_ref, krope_ref, sbk_ref, sbv_ref):
    x = x_ref[0]
    mod = mod_ref[0]
    h = _rms(x) * g1_ref[...] * (1.0 + mod[1:2]) + mod[0:1]
    hb = h.astype(BF16)

    def seg(a, b):
        return _dot(hb, win_ref[:, a:b])

    cos = cos_ref[...]
    sin = sin_ref[...]
    cos8 = jnp.tile(cos, (1, MLA_HEADS))
    sin8 = jnp.tile(sin, (1, MLA_HEADS))

    qn = (_rms(seg(C_QLAT, C_KV)) * gq_ref[...]).astype(BF16)
    q2 = _dot(qn, wuq_ref[...])
    qp_ref[0] = (q2[:, :QP_WIDTH] * cos8 + q2[:, QP_WIDTH:] * sin8).astype(BF16)

    ckv = _rms(seg(C_KV, C_SBQ)) * gkv_ref[...]
    ckv_ref[0] = ckv
    kv = _dot(ckv.astype(BF16), wukv_ref[...])
    kr2 = seg(C_KR, C_END)
    krp = kr2[:, :LANES] * cos + kr2[:, LANES:] * sin
    krope_ref[0] = krp[:, MLA_NOPE:MLA_NOPE + MLA_ROPE]
    kmla_ref[0] = (kv[:, :QP_WIDTH] + jnp.tile(krp, (1, MLA_HEADS))).astype(BF16)
    vmla_ref[0] = kv[:, QP_WIDTH:].astype(BF16)

    sbq_ref[0] = seg(C_SBQ, C_SBK).astype(BF16)
    sbk = seg(C_SBK, C_SBV)
    sbk_ref[0] = sbk
    sbk16_ref[0] = sbk.astype(BF16)
    sbv = seg(C_SBV, C_GATE)
    sbv_ref[0] = sbv
    sbv16_ref[0] = sbv.astype(BF16)
    gates_ref[0] = jax.nn.sigmoid(seg(C_GATE, C_KR)).astype(BF16)


def _in_proj(x, mod, wts, cos_t, sin_t, tm):
    b, s, _ = x.shape
    ns = s // tm
    tok = lambda w: pl.BlockSpec((1, tm, w), lambda i, j: (i, j, 0))
    full = lambda a: pl.BlockSpec(a.shape, lambda i, j: (0,) * a.ndim)
    out_widths = [(QP_WIDTH, BF16), (QP_WIDTH, BF16), (SB_WIDTH, BF16), (SB_WIDTH, BF16), (SB_WIDTH, BF16),
                  (SB_WIDTH, BF16), (2 * D_MODEL, BF16), (MLA_KV_LORA, F32), (MLA_ROPE, F32),
                  (SB_WIDTH, F32), (SB_WIDTH, F32)]
    return pl.pallas_call(
        _in_kernel,
        grid=(b, ns),
        in_specs=[tok(D_MODEL),
                  pl.BlockSpec((1, 6, D_MODEL), lambda i, j: (i, 0, 0)),
                  full(wts["g1"]), full(wts["w_in"]), full(wts["gq"]), full(wts["w_uq"]),
                  full(wts["gkv"]), full(wts["w_ukv"]),
                  pl.BlockSpec((tm, LANES), lambda i, j: (j, 0)),
                  pl.BlockSpec((tm, LANES), lambda i, j: (j, 0))],
        out_specs=[tok(w) for w, _ in out_widths],
        out_shape=[jax.ShapeDtypeStruct((b, s, w), dt) for w, dt in out_widths],
        compiler_params=pltpu.CompilerParams(dimension_semantics=("parallel", "parallel"),
                                             vmem_limit_bytes=VMEM_LIMIT),
        name="in_proj",
    )(x, mod, wts["g1"], wts["w_in"], wts["gq"], wts["w_uq"], wts["gkv"], wts["w_ukv"], cos_t, sin_t)


def _kvup_kernel(ckv_ref, kr_ref, wukv_ref, place_ref, kmla_ref, vmla_ref):
    kv = _dot(ckv_ref[0].astype(BF16), wukv_ref[...])
    kr = _dot(kr_ref[0].astype(BF16), place_ref[...])
    kmla_ref[0] = (kv[:, :QP_WIDTH] + kr).astype(BF16)
    vmla_ref[0] = kv[:, QP_WIDTH:].astype(BF16)


def _kv_up(ckv, krope, w_ukv_r, place, tm):
    b, p, _ = ckv.shape
    return pl.pallas_call(
        _kvup_kernel,
        grid=(b, p // tm),
        in_specs=[pl.BlockSpec((1, tm, MLA_KV_LORA), lambda i, j: (i, j, 0)),
                  pl.BlockSpec((1, tm, MLA_ROPE), lambda i, j: (i, j, 0)),
                  pl.BlockSpec(w_ukv_r.shape, lambda i, j: (0, 0)),
                  pl.BlockSpec(place.shape, lambda i, j: (0, 0))],
        out_specs=[pl.BlockSpec((1, tm, QP_WIDTH), lambda i, j: (i, j, 0)),
                   pl.BlockSpec((1, tm, SB_WIDTH), lambda i, j: (i, j, 0))],
        out_shape=[jax.ShapeDtypeStruct((b, p, QP_WIDTH), BF16), jax.ShapeDtypeStruct((b, p, SB_WIDTH), BF16)],
        compiler_params=pltpu.CompilerParams(dimension_semantics=("parallel", "parallel")),
        name="kv_up",
    )(ckv, krope, w_ukv_r, place)


def _tri(n):
    r = lax.broadcasted_iota(jnp.int32, (n, n), 0)
    c = lax.broadcasted_iota(jnp.int32, (n, n), 1)
    return jnp.where(r > c, 1.0, 0.0).astype(BF16)


def _stick_terms(z):
    log_sig = jnp.minimum(z, 0.0) - jnp.log(1.0 + jnp.exp(-jnp.abs(z)))
    return log_sig, log_sig - z


def _split_bf16(x):
    hi = x.astype(BF16)
    return hi, (x - hi.astype(F32)).astype(BF16)


def _finish_mixer(o_mla, o_sb, gates_ref, x_ref, mod_ref, wmu_ref, wsu_ref, wo_ref, g2_ref, x1_ref, h2_ref):
    u_mla = _dot(o_mla.astype(BF16), wmu_ref[...])
    u_sb = _dot(o_sb.astype(BF16), wsu_ref[...])
    gates = gates_ref[0]
    merged = gates[:, :D_MODEL].astype(F32) * u_mla + gates[:, D_MODEL:].astype(F32) * u_sb
    mix = _dot(merged.astype(BF16), wo_ref[...])
    mod = mod_ref[0]
    x1 = x_ref[0] + mod[2:3] * mix
    x1_ref[0] = x1
    h2 = _rms(x1) * g2_ref[...] * (1.0 + mod[4:5]) + mod[3:4]
    h2_ref[0] = _pack_halves(h2[:, :HALF], h2[:, HALF:])


def _prompt_attn_kernel(qp_ref, kmla_ref, vmla_ref, sbq_ref, sbk_ref, sbv_ref,
                        gates_ref, x_ref, mod_ref, wmu_ref, wsu_ref, wo_ref, g2_ref, x1_ref, h2_ref,
                        m_ref, l_ref, acc_ref, c_ref, sacc_ref, *, tq):
    i = pl.program_id(1)
    lane = lax.broadcasted_iota(jnp.int32, (1, LANES), 1)
    half_masks = (lane < MLA_V, lane >= MLA_V)
    row = lax.broadcasted_iota(jnp.int32, (tq, tq), 0)
    col = lax.broadcasted_iota(jnp.int32, (tq, tq), 1)
    chunk_mask = (col // CHUNK) <= (row // CHUNK)
    causal_mask = col < row
    piece = min(tq, MXU_TILE)
    tri_m = _tri(piece)
    st_diag = pl.multiple_of(i * tq, tq)

    def load(ref, start, c0):
        return ref[0, pl.ds(start, tq), c0:c0 + LANES]

    def mla_block(sub, q_h, k, v, mask):
        s = _dot_nt(q_h, k)
        if mask is not None:
            s = jnp.where(mask, s, -jnp.inf)
            m_new = jnp.max(s, axis=-1, keepdims=True)
            p = jnp.exp(s - m_new)
            l_ref[sub] = jnp.sum(p, axis=-1, keepdims=True)
            acc_ref[sub] = _dot(p.astype(BF16), v)
        else:
            m_old = m_ref[sub]
            m_new = jnp.maximum(m_old, jnp.max(s, axis=-1, keepdims=True))
            p = jnp.exp(s - m_new)
            alpha = jnp.exp(m_old - m_new)
            l_ref[sub] = alpha * l_ref[sub] + jnp.sum(p, axis=-1, keepdims=True)
            acc_ref[sub] = alpha * acc_ref[sub] + _dot(p.astype(BF16), v)
        m_ref[sub] = m_new

    def sb_block(sub, q_h, k, v, mask):
        log_sig, log_keep = _stick_terms(_dot_nt(q_h, k))
        if mask is not None:
            log_keep = jnp.where(mask, log_keep, 0.0)
        hi, lo = _split_bf16(log_keep)
        pieces = []
        total = None
        for b in reversed(range(tq // piece)):
            sl = slice(b * piece, (b + 1) * piece)
            inner = _dot(hi[:, sl], tri_m) + _dot(lo[:, sl], tri_m)
            piece_total = jnp.sum(log_keep[:, sl], axis=-1, keepdims=True)
            pieces.append(inner if total is None else inner + total)
            total = piece_total if total is None else total + piece_total
        after = jnp.concatenate(pieces[::-1], axis=1)
        if mask is not None:
            a = jnp.where(mask, jnp.exp(log_sig + after), 0.0)
            sacc_ref[sub] = _dot(a.astype(BF16), v)
            c_ref[sub] = total
        else:
            c_old = c_ref[sub]
            a = jnp.exp(log_sig + after + c_old)
            sacc_ref[sub] = sacc_ref[sub] + _dot(a.astype(BF16), v)
            c_ref[sub] = c_old + total

    o_mla = []
    o_sb = []
    for pair in range(MLA_HEADS // 2):
        vcol = pair * LANES
        kcols = [(2 * pair + sub) * HEAD_PAD for sub in range(2)]
        q_m = [qp_ref[0, :, kc:kc + HEAD_PAD] for kc in kcols]
        q_s = [jnp.where(half_masks[sub], sbq_ref[0, :, vcol:vcol + LANES], 0) for sub in range(2)]

        def pair_blocks(mla_start, sb_start, masks, vcol=vcol, kcols=kcols, q_m=q_m, q_s=q_s):
            v_m = load(vmla_ref, mla_start, vcol)
            k_s = load(sbk_ref, sb_start, vcol)
            v_s = load(sbv_ref, sb_start, vcol)
            for sub in range(2):
                keep = half_masks[sub]
                mla_block(sub, q_m[sub], load(kmla_ref, mla_start, kcols[sub]), jnp.where(keep, v_m, 0), masks[0])
                sb_block(sub, q_s[sub], k_s, jnp.where(keep, v_s, 0), masks[1])

        pair_blocks(st_diag, st_diag, (chunk_mask, causal_mask))

        def step(t, _, pair_blocks=pair_blocks):
            pair_blocks(pl.multiple_of(t * tq, tq), pl.multiple_of((i - 1 - t) * tq, tq), (None, None))
            return 0
        lax.fori_loop(0, i, step, 0)
        o_mla.append(acc_ref[0] / l_ref[0] + acc_ref[1] / l_ref[1])
        o_sb.append(sacc_ref[0] + sacc_ref[1])

    _finish_mixer(jnp.concatenate(o_mla, axis=1), jnp.concatenate(o_sb, axis=1),
                  gates_ref, x_ref, mod_ref, wmu_ref, wsu_ref, wo_ref, g2_ref, x1_ref, h2_ref)


def _decode_attn_kernel(qp_ref, kmla_ref, vmla_ref, sbq_ref, sbk_ref, sbv_ref, pkmla_ref, pvmla_ref, psbk_ref,
                        psbv_ref, gates_ref, x_ref, mod_ref, wmu_ref, wsu_ref, wo_ref, g2_ref, x1_ref, h2_ref,
                        *, tq, past_len, past_blk):
    n_past = past_len // past_blk
    lane = lax.broadcasted_iota(jnp.int32, (1, LANES), 1)
    half_masks = (lane < MLA_V, lane >= MLA_V)
    row = lax.broadcasted_iota(jnp.int32, (tq, tq), 0)
    col = lax.broadcasted_iota(jnp.int32, (tq, tq), 1)
    chunk_mask = ((past_len + col) // CHUNK) <= ((past_len + row) // CHUNK)
    causal_mask = col < row
    tri_new = _tri(tq)
    tri_past = _tri(past_blk)

    o_mla = []
    o_sb = []
    for pair in range(MLA_HEADS // 2):
        vcol = pair * LANES
        v_new = vmla_ref[0, :, vcol:vcol + LANES]
        v_past = pvmla_ref[0, :, vcol:vcol + LANES]
        sk_new = sbk_ref[0, :, vcol:vcol + LANES]
        sv_new = sbv_ref[0, :, vcol:vcol + LANES]
        sk_past = psbk_ref[0, :, vcol:vcol + LANES].astype(BF16)
        sv_past = psbv_ref[0, :, vcol:vcol + LANES].astype(BF16)
        mla_pair = None
        sb_pair = None
        for sub in range(2):
            keep = half_masks[sub]
            kcol = (2 * pair + sub) * HEAD_PAD

            q_h = qp_ref[0, :, kcol:kcol + HEAD_PAD]
            s_past = _dot_nt(q_h, pkmla_ref[0, :, kcol:kcol + HEAD_PAD])
            s_new = jnp.where(chunk_mask, _dot_nt(q_h, kmla_ref[0, :, kcol:kcol + HEAD_PAD]), -jnp.inf)
            m = jnp.maximum(jnp.max(s_past, axis=-1, keepdims=True), jnp.max(s_new, axis=-1, keepdims=True))
            p_past = jnp.exp(s_past - m)
            p_new = jnp.exp(s_new - m)
            denom = jnp.sum(p_past, axis=-1, keepdims=True) + jnp.sum(p_new, axis=-1, keepdims=True)
            o = (_dot(p_past.astype(BF16), jnp.where(keep, v_past, 0))
                 + _dot(p_new.astype(BF16), jnp.where(keep, v_new, 0))) / denom
            mla_pair = o if mla_pair is None else mla_pair + o

            q_s = jnp.where(keep, sbq_ref[0, :, vcol:vcol + LANES], 0)
            ls_new, lk_new = _stick_terms(_dot_nt(q_s, sk_new))
            lk_new = jnp.where(causal_mask, lk_new, 0.0)
            hi, lo = _split_bf16(lk_new)
            a_new = jnp.where(causal_mask, jnp.exp(ls_new + _dot(hi, tri_new) + _dot(lo, tri_new)), 0.0)
            acc = _dot(a_new.astype(BF16), jnp.where(keep, sv_new, 0))
            later = jnp.sum(lk_new, axis=-1, keepdims=True)

            ls_past, lk_past = _stick_terms(_dot_nt(q_s, sk_past))
            hi, lo = _split_bf16(lk_past)
            blocks = lambda a: [a[:, b * past_blk:(b + 1) * past_blk] for b in range(n_past)]
            stacked = jnp.concatenate(blocks(hi) + blocks(lo), axis=0)
            within = _dot(stacked, tri_past)
            after = []
            for b in reversed(range(n_past)):
                after.append(within[b * tq:(b + 1) * tq] + within[(n_past + b) * tq:(n_past + b + 1) * tq] + later)
                later = later + jnp.sum(lk_past[:, b * past_blk:(b + 1) * past_blk], axis=-1, keepdims=True)
            a_past = jnp.exp(ls_past + jnp.concatenate(after[::-1], axis=1))
            acc = acc + _dot(a_past.astype(BF16), jnp.where(keep, sv_past, 0))
            sb_pair = acc if sb_pair is None else sb_pair + acc
        o_mla.append(mla_pair)
        o_sb.append(sb_pair)

    _finish_mixer(jnp.concatenate(o_mla, axis=1), jnp.concatenate(o_sb, axis=1),
                  gates_ref, x_ref, mod_ref, wmu_ref, wsu_ref, wo_ref, g2_ref, x1_ref, h2_ref)


def _mixer_call(kernel_fn, name, tq, args_kv, gates, x, mod, wts, scratch, batch0=0, n_batch=None):
    s = x.shape[1]
    b = x.shape[0] if n_batch is None else n_batch
    tok = lambda w: pl.BlockSpec((1, tq, w), lambda i, j: (i + batch0, j, 0))
    seq = lambda a: pl.BlockSpec((1,) + a.shape[1:], lambda i, j: (i + batch0, 0, 0))
    full = lambda a: pl.BlockSpec(a.shape, lambda i, j: (0,) * a.ndim)
    out = lambda w: pl.BlockSpec((1, tq, w), lambda i, j: (i, j, 0))
    args = list(args_kv) + [gates, x, mod, wts["w_mla_up"], wts["w_sb_up"], wts["w_out"], wts["g2"]]
    specs = [seq(a) if whole else tok(a.shape[-1]) for a, whole in
             zip(args_kv, (False, True, True, False) + (True,) * (len(args_kv) - 4))]
    specs += [tok(2 * D_MODEL), tok(D_MODEL), pl.BlockSpec((1, 6, D_MODEL), lambda i, j: (i + batch0, 0, 0)),
              full(wts["w_mla_up"]), full(wts["w_sb_up"]), full(wts["w_out"]), full(wts["g2"])]
    return pl.pallas_call(
        kernel_fn,
        grid=(b, s // tq),
        in_specs=specs,
        out_specs=[out(D_MODEL), out(HALF)],
        out_shape=[jax.ShapeDtypeStruct((b, s, D_MODEL), F32), jax.ShapeDtypeStruct((b, s, HALF), jnp.int32)],
        scratch_shapes=scratch,
        compiler_params=pltpu.CompilerParams(dimension_semantics=("parallel", "arbitrary"),
                                             vmem_limit_bytes=VMEM_LIMIT),
        name=name,
    )(*args)


def _prompt_attention(qp, kmla, vmla, sbq, sbk16, sbv16, gates, x, mod, wts, tq, batch0, n_batch):
    col = lambda: pltpu.VMEM((2, tq, 1), F32)
    wide = lambda: pltpu.VMEM((2, tq, LANES), F32)
    return _mixer_call(functools.partial(_prompt_attn_kernel, tq=tq), "attention", tq,
                       (qp, kmla, vmla, sbq, sbk16, sbv16), gates, x, mod, wts,
                       [col(), col(), wide(), col(), wide()], batch0, n_batch)


def _decode_attention(qp, kmla, vmla, sbq, sbk16, sbv16, past, gates, x, mod, wts, past_blk):
    tq = x.shape[1]
    past_len = past[0].shape[1]
    kern = functools.partial(_decode_attn_kernel, tq=tq, past_len=past_len, past_blk=past_blk)
    return _mixer_call(kern, "decode_attention", tq, (qp, kmla, vmla, sbq, sbk16, sbv16) + tuple(past),
                       gates, x, mod, wts, [])


def _route_kernel(h2_ref, wr_ref, br_ref, idx_ref, wt_ref, rank_ref, cnt_ref, seen_ref):
    lo, hi = _unpack_halves(h2_ref[...])
    tm = lo.shape[0]
    logits = _dot_nt(wr_ref[:, :HALF], lo.astype(BF16)) + _dot_nt(wr_ref[:, HALF:], hi.astype(BF16))
    scores = jax.nn.sigmoid(logits)
    sel = scores + br_ref[...]
    neg = -jnp.inf

    grp = sel.reshape(N_GROUPS, GROUP_SIZE, tm)
    within = lax.broadcasted_iota(jnp.int32, grp.shape, 1)
    top1 = jnp.max(grp, axis=1, keepdims=True)
    first = jnp.min(jnp.where(grp == top1, within, GROUP_SIZE), axis=1, keepdims=True)
    top2 = jnp.max(jnp.where(within == first, neg, grp), axis=1, keepdims=True)
    gscore = (top1 + top2).reshape(N_GROUPS, tm)

    gid = lax.broadcasted_iota(jnp.int32, gscore.shape, 0)
    chosen = jnp.zeros(gscore.shape, jnp.bool_)
    for _ in range(TOPK_GROUPS):
        best = jnp.max(gscore, axis=0, keepdims=True)
        pick = jnp.min(jnp.where(gscore == best, gid, N_GROUPS), axis=0, keepdims=True)
        hit = gid == pick
        chosen = jnp.logical_or(chosen, hit)
        gscore = jnp.where(hit, neg, gscore)
    chosen3 = jnp.broadcast_to(chosen.reshape(N_GROUPS, 1, tm), grp.shape)
    cand = jnp.where(chosen3, grp, neg).reshape(N_EXPERTS, tm)

    eid = lax.broadcasted_iota(jnp.int32, cand.shape, 0)
    picks = []
    weights = []
    for _ in range(TOP_K):
        best = jnp.max(cand, axis=0, keepdims=True)
        pick = jnp.min(jnp.where(cand == best, eid, N_EXPERTS), axis=0, keepdims=True)
        hit = eid == pick
        weights.append(jnp.sum(jnp.where(hit, scores, 0.0), axis=0, keepdims=True))
        picks.append(pick)
        cand = jnp.where(hit, neg, cand)
    w = jnp.concatenate(weights, axis=0)
    idx_ref[...] = jnp.concatenate(picks, axis=0)
    wt_ref[...] = w / (jnp.sum(w, axis=0, keepdims=True) + 1e-20) * ROUTED_SCALE

    @pl.when(pl.program_id(0) == 0)
    def _():
        seen_ref[...] = jnp.zeros_like(seen_ref)

    onehot = jnp.zeros(cand.shape, F32)
    for pick in picks:
        onehot = onehot + jnp.where(eid == pick, 1.0, 0.0)
    src = lax.broadcasted_iota(jnp.int32, (tm, tm), 0)
    dst = lax.broadcasted_iota(jnp.int32, (tm, tm), 1)
    earlier = jnp.where(src < dst, 1.0, 0.0).astype(BF16)
    before = _dot(onehot.astype(BF16), earlier) + seen_ref[...]
    rank_ref[...] = jnp.concatenate(
        [jnp.sum(jnp.where(eid == pick, before, 0.0), axis=0, keepdims=True) for pick in picks],
        axis=0).astype(jnp.int32)
    seen = seen_ref[...] + jnp.sum(onehot, axis=1, keepdims=True)
    seen_ref[...] = seen
    cnt_ref[...] = seen.astype(jnp.int32)


def _route(h2p, wr_t, br, tm):
    t = h2p.shape[0]
    kt = lambda: pl.BlockSpec((TOP_K, tm), lambda i: (0, i))
    return pl.pallas_call(
        _route_kernel,
        grid=(t // tm,),
        in_specs=[pl.BlockSpec((tm, HALF), lambda i: (i, 0)),
                  pl.BlockSpec(wr_t.shape, lambda i: (0, 0)),
                  pl.BlockSpec(br.shape, lambda i: (0, 0))],
        out_specs=[kt(), kt(), kt(), pl.BlockSpec((N_EXPERTS, 1), lambda i: (0, 0))],
        out_shape=[jax.ShapeDtypeStruct((TOP_K, t), jnp.int32), jax.ShapeDtypeStruct((TOP_K, t), F32),
                   jax.ShapeDtypeStruct((TOP_K, t), jnp.int32), jax.ShapeDtypeStruct((N_EXPERTS, 1), jnp.int32)],
        scratch_shapes=[pltpu.VMEM((N_EXPERTS, 1), F32)],
        compiler_params=pltpu.CompilerParams(dimension_semantics=("arbitrary",)),
        name="route",
    )(h2p, wr_t, br)


def _position_kernel(idx_ref, rank_ref, start_ref, pos_ref):
    idx = idx_ref[...]
    eid = lax.broadcasted_iota(jnp.int32, (N_EXPERTS, idx.shape[1]), 0)
    start = start_ref[...]
    base = jnp.concatenate(
        [jnp.sum(jnp.where(eid == idx[k:k + 1, :], start, 0.0), axis=0, keepdims=True) for k in range(TOP_K)],
        axis=0)
    pos_ref[...] = base.astype(jnp.int32) + rank_ref[...]


def _positions(idx_kt, rank_kt, pad_start, tm):
    t = idx_kt.shape[1]
    kt = lambda: pl.BlockSpec((TOP_K, tm), lambda i: (0, i))
    return pl.pallas_call(
        _position_kernel,
        grid=(t // tm,),
        in_specs=[kt(), kt(), pl.BlockSpec((N_EXPERTS, 1), lambda i: (0, 0))],
        out_specs=kt(),
        out_shape=jax.ShapeDtypeStruct((TOP_K, t), jnp.int32),
        compiler_params=pltpu.CompilerParams(dimension_semantics=("parallel",)),
        name="positions",
    )(idx_kt, rank_kt, pad_start.astype(F32).reshape(N_EXPERTS, 1))


def _sc_mesh():
    return plsc.VectorSubcoreMesh(core_axis_name="c", subcore_axis_name="s",
                                  num_cores=SC_CORES, num_subcores=SC_SUBCORES)


def _sc_scatter_rows(rows, pos, n_out, chunk):
    t, width = rows.shape
    copies = pos.shape[0] // t
    per_worker = t // SC_WORKERS
    n_chunks = per_worker // chunk
    tail = per_worker - n_chunks * chunk
    assert per_worker * SC_WORKERS == t and tail % 8 == 0
    tail_rows = max(tail, 8)

    @functools.partial(
        pl.kernel, mesh=_sc_mesh(),
        out_type=jax.ShapeDtypeStruct((n_out, width), rows.dtype),
        scratch_types=[pltpu.VMEM((copies, chunk), jnp.int32), pltpu.VMEM((chunk, width), rows.dtype),
                       pltpu.VMEM((copies, tail_rows), jnp.int32), pltpu.VMEM((tail_rows, width), rows.dtype),
                       pltpu.SemaphoreType.DMA, pltpu.SemaphoreType.DMA],
    )
    def scatter(rows_hbm, pos_hbm, out_hbm, idx_v, rows_v, idx_t, rows_t, load_sem, store_sem):
        wid = lax.axis_index("s") * SC_CORES + lax.axis_index("c")
        base = wid * per_worker

        def move(off, n, idx_buf, row_buf):
            loads = [pltpu.async_copy(rows_hbm.at[pl.ds(off, n)], row_buf, load_sem)]
            for k in range(copies):
                src = pos_hbm.at[pl.ds(pl.multiple_of(k * t + off, 8), n)]
                loads.append(pltpu.async_copy(src, idx_buf.at[k], load_sem))
            for cp in loads:
                cp.wait()
            stores = [pltpu.async_copy(row_buf, out_hbm.at[idx_buf.at[k]], store_sem) for k in range(copies)]
            for cp in stores:
                cp.wait()

        @pl.loop(0, n_chunks)
        def _(c):
            move(pl.multiple_of(base + c * chunk, 8), chunk, idx_v, rows_v)

        if tail:
            move(pl.multiple_of(base + n_chunks * chunk, 8), tail, idx_t, rows_t)

    return scatter(rows, pos)


def _sc_gather_rows(table, idx, chunk):
    n_rows = idx.shape[0]
    width = table.shape[1]
    per_worker = n_rows // SC_WORKERS
    n_chunks = per_worker // chunk
    assert per_worker * SC_WORKERS == n_rows and n_chunks * chunk == per_worker
    mesh = _sc_mesh()

    @functools.partial(
        pl.kernel, mesh=mesh,
        out_type=jax.ShapeDtypeStruct((n_rows, width), table.dtype),
        scratch_types=[pltpu.VMEM((chunk,), jnp.int32), pltpu.VMEM((chunk, width), table.dtype),
                       pltpu.SemaphoreType.DMA],
    )
    def gather(table_hbm, idx_hbm, out_hbm, idx_v, rows_v, sem):
        wid = lax.axis_index("s") * SC_CORES + lax.axis_index("c")
        base = wid * per_worker

        @pl.loop(0, n_chunks)
        def _(c):
            off = pl.multiple_of(base + c * chunk, 8)
            pltpu.sync_copy(idx_hbm.at[pl.ds(off, chunk)], idx_v)
            pltpu.async_copy(table_hbm.at[idx_v], rows_v, sem).wait()
            pltpu.sync_copy(rows_v, out_hbm.at[pl.ds(off, chunk)])

    return gather(table, idx)


def _expert_kernel(be_ref, nv_ref, nu_ref, xs_ref, w1_ref, w3_ref, w2_ref, os_ref, wb1, wb3, wb2):
    i = pl.program_id(0)

    @pl.when(i < nu_ref[0])
    def _():
        @pl.when(jnp.logical_or(i == 0, be_ref[i] != be_ref[jnp.maximum(i - 1, 0)]))
        def _():
            wb1[...] = w1_ref[0].astype(BF16)
            wb3[...] = w3_ref[0].astype(BF16)
            wb2[...] = w2_ref[0].astype(BF16)

        packed = xs_ref[...]
        live = lax.broadcasted_iota(jnp.int32, packed.shape, 0) < nv_ref[i]
        lo, hi = _unpack_halves(jnp.where(live, packed, 0))
        lo = lo.astype(BF16)
        hi = hi.astype(BF16)
        a = _dot(lo, wb1[:HALF, :]) + _dot(hi, wb1[HALF:, :])
        b = _dot(lo, wb3[:HALF, :]) + _dot(hi, wb3[HALF:, :])
        o = _dot((_silu(a) * b).astype(BF16), wb2[...])
        os_ref[...] = _pack_halves(o[:, :HALF], o[:, HALF:])


def _experts(xs, blk_exp, blk_valid, n_used, w1, w3, w2, blk):
    rows = xs.shape[0]
    nb = rows // blk
    last = lambda i, nu: jnp.minimum(i, nu[0] - 1)
    grid_spec = pltpu.PrefetchScalarGridSpec(
        num_scalar_prefetch=3,
        grid=(nb,),
        in_specs=[pl.BlockSpec((blk, HALF), lambda i, be, nv, nu: (last(i, nu), 0)),
                  pl.BlockSpec((1, D_MODEL, EXPERT_DIM), lambda i, be, nv, nu: (be[last(i, nu)], 0, 0)),
                  pl.BlockSpec((1, D_MODEL, EXPERT_DIM), lambda i, be, nv, nu: (be[last(i, nu)], 0, 0)),
                  pl.BlockSpec((1, EXPERT_DIM, D_MODEL), lambda i, be, nv, nu: (be[last(i, nu)], 0, 0))],
        out_specs=pl.BlockSpec((blk, HALF), lambda i, be, nv, nu: (last(i, nu), 0)),
        scratch_shapes=[pltpu.VMEM((D_MODEL, EXPERT_DIM), BF16), pltpu.VMEM((D_MODEL, EXPERT_DIM), BF16),
                        pltpu.VMEM((EXPERT_DIM, D_MODEL), BF16)],
    )
    return pl.pallas_call(
        _expert_kernel,
        grid_spec=grid_spec,
        out_shape=jax.ShapeDtypeStruct((rows, HALF), jnp.int32),
        compiler_params=pltpu.CompilerParams(dimension_semantics=("arbitrary",), vmem_limit_bytes=VMEM_LIMIT),
        name="experts",
    )(blk_exp, blk_valid, n_used, xs, w1, w3, w2)


def _combine_kernel(g_ref, wt_ref, h2_ref, x1_ref, mod_ref, sw1_ref, sw3_ref, sw2_ref, gf_ref, y_ref):
    wt = wt_ref[...]
    lo_acc = None
    for k in range(TOP_K):
        lo, hi = _unpack_halves(g_ref[k])
        wk = wt[:, k:k + 1]
        lo_acc = wk * lo if lo_acc is None else lo_acc + wk * lo
        hi_acc = wk * hi if k == 0 else hi_acc + wk * hi
    routed = jnp.concatenate([lo_acc, hi_acc], axis=1)
    lo, hi = _unpack_halves(h2_ref[...])
    lo = lo.astype(BF16)
    hi = hi.astype(BF16)
    a = _dot(lo, sw1_ref[:HALF, :]) + _dot(hi, sw1_ref[HALF:, :])
    b = _dot(lo, sw3_ref[:HALF, :]) + _dot(hi, sw3_ref[HALF:, :])
    shared = _dot((_silu(a) * b).astype(BF16), sw2_ref[...])
    mod = mod_ref[0]
    x2 = x1_ref[...] + mod[5:6] * (routed + shared)
    y_ref[...] = _rms(x2) * gf_ref[...]


def _combine_into_kernel(g_ref, wt_ref, h2_ref, x1_ref, mod_ref, sw1_ref, sw3_ref, sw2_ref, gf_ref, prev_ref, y_ref):
    del prev_ref
    _combine_kernel(g_ref, wt_ref, h2_ref, x1_ref, mod_ref, sw1_ref, sw3_ref, sw2_ref, gf_ref, y_ref)


def _combine(gathered, wt, h2p, x1, mod, wts, tm, src_tile0, mod_batch0, tiles_per_batch, out_rows, out_tile0,
             y_prev=None):
    t = x1.shape[0]
    full = lambda a: pl.BlockSpec(a.shape, lambda i: (0,) * a.ndim)
    shifted = lambda w: pl.BlockSpec((tm, w), lambda i: (i + src_tile0, 0))
    args = [gathered, wt, h2p, x1, mod, wts["sw1"], wts["sw3"], wts["sw2"], wts["gf"]]
    specs = [pl.BlockSpec((TOP_K, tm, HALF), lambda i: (0, i + src_tile0, 0)), shifted(TOP_K), shifted(HALF),
             pl.BlockSpec((tm, D_MODEL), lambda i: (i, 0)),
             pl.BlockSpec((1, 6, D_MODEL), lambda i: (i // tiles_per_batch + mod_batch0, 0, 0)),
             full(wts["sw1"]), full(wts["sw3"]), full(wts["sw2"]), full(wts["gf"])]
    aliases = {}
    body = _combine_kernel
    if y_prev is not None:
        args.append(y_prev)
        specs.append(pl.BlockSpec(memory_space=pl.ANY))
        aliases = {len(args) - 1: 0}
        body = _combine_into_kernel
    return pl.pallas_call(
        body,
        grid=(t // tm,),
        in_specs=specs,
        out_specs=pl.BlockSpec((tm, D_MODEL), lambda i: (i + out_tile0, 0)),
        out_shape=jax.ShapeDtypeStruct((out_rows, D_MODEL), F32),
        input_output_aliases=aliases,
        compiler_params=pltpu.CompilerParams(dimension_semantics=("parallel",), vmem_limit_bytes=VMEM_LIMIT),
        name="combine",
    )(*args)


def _rope_tables(pos):
    half = MLA_ROPE // 2
    inv_freq = ROPE_THETA ** (-jnp.arange(half, dtype=F32) / half)
    ang = pos.astype(F32)[:, None] * inv_freq
    cos, sin = jnp.cos(ang), jnp.sin(ang)
    n = pos.shape[0]
    ones = jnp.ones((n, MLA_NOPE), F32)
    z_nope = jnp.zeros((n, MLA_NOPE), F32)
    z_pad = jnp.zeros((n, HEAD_PAD - MLA_NOPE - MLA_ROPE), F32)
    return (jnp.concatenate([ones, cos, cos, z_pad], axis=1),
            jnp.concatenate([z_nope, -sin, sin, z_pad], axis=1))


def _prep_weights(g_norm1, w_in, g_q_lat, w_uq, g_kv_lat, w_ukv, w_mla_up, w_sb_up, w_out, g_norm2,
                  w_router, b_router, shared_w1, shared_w3, shared_w2, g_final):
    half = MLA_ROPE // 2
    w = w_in[0]
    o = 0
    parts = {}
    for name, width in (("qlat", MLA_Q_LORA), ("kv", MLA_KV_LORA), ("kr", MLA_ROPE), ("sbq", SB_WIDTH),
                        ("sbk", SB_WIDTH), ("sbv", SB_WIDTH), ("gm", D_MODEL), ("gs", D_MODEL)):
        parts[name] = w[:, o:o + width]
        o += width
    kr = parts["kr"]
    z_nope = jnp.zeros((D_MODEL, MLA_NOPE), F32)
    z_pad = jnp.zeros((D_MODEL, HEAD_PAD - MLA_NOPE - MLA_ROPE), F32)
    kr_seg = jnp.concatenate([z_nope, kr, z_pad, z_nope, kr[:, half:], kr[:, :half], z_pad], axis=1)
    w_in_r = jnp.concatenate([parts["qlat"], parts["kv"], parts["sbq"] * (SB_DIM ** -0.5), parts["sbk"],
                              parts["sbv"], parts["gm"], parts["gs"], kr_seg], axis=1).astype(BF16)

    scale = (MLA_NOPE + MLA_ROPE) ** -0.5
    uq = w_uq[0].reshape(MLA_Q_LORA, MLA_HEADS, MLA_NOPE + MLA_ROPE) * scale
    nope, rope = uq[..., :MLA_NOPE], uq[..., MLA_NOPE:]
    zq_nope = jnp.zeros_like(nope)
    zq_pad = jnp.zeros((MLA_Q_LORA, MLA_HEADS, HEAD_PAD - MLA_NOPE - MLA_ROPE), F32)
    main = jnp.concatenate([nope, rope, zq_pad], axis=-1).reshape(MLA_Q_LORA, QP_WIDTH)
    swapped = jnp.concatenate([zq_nope, rope[..., half:], rope[..., :half], zq_pad], axis=-1)
    w_uq_r = jnp.concatenate([main, swapped.reshape(MLA_Q_LORA, QP_WIDTH)], axis=1).astype(BF16)

    ukv = w_ukv[0].reshape(MLA_KV_LORA, MLA_HEADS, MLA_NOPE + MLA_V)
    k_nope, v = ukv[..., :MLA_NOPE], ukv[..., MLA_NOPE:]
    k_pad = jnp.concatenate([k_nope, jnp.zeros((MLA_KV_LORA, MLA_HEADS, HEAD_PAD - MLA_NOPE), F32)], axis=-1)
    w_ukv_r = jnp.concatenate([k_pad.reshape(MLA_KV_LORA, QP_WIDTH),
                               v.reshape(MLA_KV_LORA, MLA_HEADS * MLA_V)], axis=1).astype(BF16)

    eye = jnp.eye(MLA_ROPE, dtype=F32)
    place_head = jnp.concatenate([jnp.zeros((MLA_ROPE, MLA_NOPE), F32), eye,
                                  jnp.zeros((MLA_ROPE, HEAD_PAD - MLA_NOPE - MLA_ROPE), F32)], axis=1)
    place = jnp.tile(place_head, (1, MLA_HEADS)).astype(BF16)

    return dict(
        g1=g_norm1[0].reshape(1, D_MODEL), w_in=w_in_r, gq=g_q_lat[0].reshape(1, MLA_Q_LORA), w_uq=w_uq_r,
        gkv=g_kv_lat[0].reshape(1, MLA_KV_LORA), w_ukv=w_ukv_r, place=place,
        w_mla_up=w_mla_up[0].astype(BF16), w_sb_up=w_sb_up[0].astype(BF16), w_out=w_out[0].astype(BF16),
        g2=g_norm2[0].reshape(1, D_MODEL), wr_t=w_router[0].T.astype(BF16),
        br=b_router[0].reshape(N_EXPERTS, 1),
        sw1=shared_w1[0].astype(BF16), sw3=shared_w3[0].astype(BF16), sw2=shared_w2[0].astype(BF16),
        gf=g_final.reshape(1, D_MODEL))


def _block_plan(counts, nb, blk):
    padded = (counts + blk - 1) // blk * blk
    pad_end = jnp.cumsum(padded)
    pad_start = pad_end - padded
    first_row = jnp.arange(nb, dtype=jnp.int32) * blk
    blk_exp = jnp.minimum(jnp.sum((pad_end[None, :] <= first_row[:, None]).astype(jnp.int32), axis=1),
                          N_EXPERTS - 1)
    own = blk_exp[:, None] == jnp.arange(N_EXPERTS, dtype=jnp.int32)[None, :]
    seg_end = jnp.sum(jnp.where(own, (pad_start + counts)[None, :], 0), axis=1)
    blk_valid = jnp.clip(seg_end - first_row, 0, blk).astype(jnp.int32)
    n_used = (pad_end[-1:] // blk).astype(jnp.int32)
    return pad_start, blk_exp.astype(jnp.int32), blk_valid, n_used


def _moe_rows(h2p, idx_kt, rank_kt, counts, w1, w3, w2, scatter_rows, gather_rows, blk, tm):
    t = h2p.shape[0]
    nb = -(-t * TOP_K // blk) + N_EXPERTS
    pad_start, blk_exp, blk_valid, n_used = _block_plan(counts.reshape(N_EXPERTS), nb, blk)
    pos = _positions(idx_kt, rank_kt, pad_start, tm).reshape(TOP_K * t)
    xs = scatter_rows(h2p, pos, nb * blk)
    os_ = _experts(xs, blk_exp, blk_valid, n_used, w1, w3, w2, blk)
    return gather_rows(os_, pos).reshape(TOP_K, t, HALF)


def _forward(x_prompt, x_sample, cache_mla_ckv, cache_mla_krope, cache_sb_k, cache_sb_v, c_prompt, c_sample,
             w_ada, b_ada, moe_w1, moe_w3, moe_w2, wts, scatter_rows, gather_rows, token_block, attn_block,
             route_block, moe_block):
    bp, sp, _ = x_prompt.shape
    bs, ss, _ = x_sample.shape
    past_len = cache_mla_ckv.shape[2]

    mod = _ada(jnp.concatenate([c_prompt, c_sample], axis=0), w_ada[0], b_ada[0]).reshape(bp + bs, 6, D_MODEL)
    mod_p, mod_s = mod[:bp], mod[bp:]

    cos_p, sin_p = _rope_tables(jnp.arange(sp))
    (qp, kmla, vmla, sbq, sbk16, sbv16, gates, ckv_p, krope_p, sbk_p, sbv_p) = _in_proj(
        x_prompt, mod_p, wts, cos_p, sin_p, token_block)
    half_b = bp // 2
    mixed = [_prompt_attention(qp, kmla, vmla, sbq, sbk16, sbv16, gates, x_prompt, mod_p, wts, attn_block,
                               b0, half_b) for b0 in (0, half_b)]

    cos_s, sin_s = _rope_tables(past_len + jnp.arange(ss))
    (qs, kmla_s, vmla_s, sbq_s, sbk16_s, sbv16_s, gates_s, ckv_s, krope_s, sbk_s, sbv_s) = _in_proj(
        x_sample, mod_s, wts, cos_s, sin_s, ss)
    pkmla, pvmla = _kv_up(cache_mla_ckv[0], cache_mla_krope[0], wts["w_ukv"], wts["place"], token_block)
    past = (pkmla, pvmla, cache_sb_k[0].reshape(bs, past_len, SB_WIDTH), cache_sb_v[0].reshape(bs, past_len, SB_WIDTH))
    x1_s, h2_s = _decode_attention(qs, kmla_s, vmla_s, sbq_s, sbk16_s, sbv16_s, past, gates_s, x_sample, mod_s,
                                   wts, token_block)

    tp, ts = bp * sp, bs * ss
    th = half_b * sp

    def moe(h2_rows):
        idx_kt, wt_kt, rank_kt, counts = _route(h2_rows, wts["wr_t"], wts["br"], route_block)
        gathered = _moe_rows(h2_rows, idx_kt, rank_kt, counts, moe_w1[0], moe_w3[0], moe_w2[0],
                             scatter_rows, gather_rows, moe_block, route_block)
        return gathered, wt_kt.T

    tiles_per_batch = sp // token_block
    (x1_a, h2_a), (x1_b, h2_b) = mixed
    h2_a = h2_a.reshape(th, HALF)
    g_a, wt_a = moe(h2_a)
    y_p = _combine(g_a, wt_a, h2_a, x1_a.reshape(th, D_MODEL), mod_p, wts, token_block,
                   0, 0, tiles_per_batch, tp, 0)
    h2_b = jnp.concatenate([h2_b.reshape(th, HALF), h2_s.reshape(ts, HALF)], axis=0)
    g_b, wt_b = moe(h2_b)
    y_p = _combine(g_b, wt_b, h2_b, x1_b.reshape(th, D_MODEL), mod_p, wts, token_block,
                   0, half_b, tiles_per_batch, tp, th // token_block, y_prev=y_p)
    y_s = _combine(g_b, wt_b, h2_b, x1_s.reshape(ts, D_MODEL), mod_s, wts, ss, th // ss, 0, 1, ts, 0)

    heads = lambda a, b_, s_: a.reshape(1, b_, s_, SB_HEADS, SB_DIM)
    return (y_p.reshape(bp, sp, D_MODEL), y_s.reshape(bs, ss, D_MODEL),
            ckv_p[None], krope_p[None], heads(sbk_p, bp, sp), heads(sbv_p, bp, sp),
            ckv_s[None], krope_s[None], heads(sbk_s, bs, ss), heads(sbv_s, bs, ss))


def kernel(x_prompt, x_sample, cache_mla_ckv, cache_mla_krope, cache_sb_k, cache_sb_v, c_prompt, c_sample, w_ada, b_ada, g_norm1, w_in, g_q_lat, w_uq, g_kv_lat, w_ukv, w_mla_up, w_sb_up, w_out, g_norm2, w_router, b_router, moe_w1, moe_w3, moe_w2, shared_w1, shared_w3, shared_w2, g_final):
    wts = _prep_weights(g_norm1, w_in, g_q_lat, w_uq, g_kv_lat, w_ukv, w_mla_up, w_sb_up, w_out, g_norm2,
                        w_router, b_router, shared_w1, shared_w3, shared_w2, g_final)
    scatter_rows = functools.partial(_sc_scatter_rows, chunk=SC_CHUNK)
    gather_rows = functools.partial(_sc_gather_rows, chunk=SC_CHUNK)
    return _forward(x_prompt, x_sample, cache_mla_ckv, cache_mla_krope, cache_sb_k, cache_sb_v, c_prompt, c_sample,
                    w_ada, b_ada, moe_w1, moe_w3, moe_w2, wts, scatter_rows, gather_rows, TOKEN_BLOCK, ATTN_BLOCK,
                    TOKEN_BLOCK, MOE_BLOCK)
```

```python
import functools

import jax
import jax.numpy as jnp
from jax import lax
from jax.experimental import pallas as pl
from jax.experimental.pallas import tpu as pltpu
from jax.experimental.pallas import tpu_sc as plsc

F32 = jnp.float32
BF16 = jnp.bfloat16

D_MODEL = 1024
NORM_EPS = 1e-6
CHUNK = 64
MLA_HEADS = 8
MLA_NOPE = 64
MLA_ROPE = 32
MLA_V = 64
MLA_Q_LORA = 384
MLA_KV_LORA = 256
ROPE_THETA = 10000.0
SB_HEADS = 8
SB_DIM = 64
SB_WIDTH = SB_HEADS * SB_DIM
N_EXPERTS = 256
TOP_K = 8
N_GROUPS = 8
TOPK_GROUPS = 4
GROUP_SIZE = N_EXPERTS // N_GROUPS
EXPERT_DIM = 256
ROUTED_SCALE = 2.5
LOG2_E = 1.4426950408889634

LANES = 128
MXU_TILE = 256
SC_CORES = 2
SC_SUBCORES = 16
SC_WORKERS = SC_CORES * SC_SUBCORES
VMEM_LIMIT = 60 * 1024 * 1024

HEAD_PAD = LANES
QP_WIDTH = MLA_HEADS * HEAD_PAD
HALF = D_MODEL // 2

C_QLAT = 0
C_KV = C_QLAT + MLA_Q_LORA
C_SBQ = C_KV + MLA_KV_LORA
C_SBK = C_SBQ + SB_WIDTH
C_SBV = C_SBK + SB_WIDTH
C_GATE = C_SBV + SB_WIDTH
C_KR = C_GATE + 2 * D_MODEL
C_END = C_KR + LANES

MOE_BLOCK = 512
TOKEN_BLOCK = 256
ATTN_BLOCK = 512
SC_CHUNK = 64


def _rms(x):
    return x * lax.rsqrt(jnp.mean(x * x, axis=-1, keepdims=True) + NORM_EPS)


def _silu(x):
    return x * jax.nn.sigmoid(x)


def _pack_halves(lo, hi):
    lo_bits = lax.bitcast_convert_type(lo.astype(BF16).astype(F32), jnp.uint32) >> 16
    hi_bits = lax.bitcast_convert_type(hi.astype(BF16).astype(F32), jnp.uint32) & jnp.uint32(0xFFFF0000)
    return lax.bitcast_convert_type(lo_bits | hi_bits, jnp.int32)


def _unpack_halves(p):
    u = lax.bitcast_convert_type(p, jnp.uint32)
    lo = lax.bitcast_convert_type(u << 16, F32)
    hi = lax.bitcast_convert_type(u & jnp.uint32(0xFFFF0000), F32)
    return lo, hi


def _dot(a, b):
    return jnp.dot(a, b, preferred_element_type=F32)


def _dot_nt(a, b):
    return lax.dot_general(a, b, (((1,), (1,)), ((), ())), preferred_element_type=F32)


def _ada_kernel(c_ref, w_ref, b_ref, o_ref):
    c = c_ref[...]
    o_ref[...] = _dot(_silu(c).astype(BF16), w_ref[...].astype(BF16)) + b_ref[...]


def _ada(c, w_ada, b_ada):
    n = c.shape[0]
    width = w_ada.shape[1]
    return pl.pallas_call(
        _ada_kernel,
        grid=(width // D_MODEL,),
        in_specs=[pl.BlockSpec((n, D_MODEL), lambda j: (0, 0)),
                  pl.BlockSpec((D_MODEL, D_MODEL), lambda j: (0, j)),
                  pl.BlockSpec((1, D_MODEL), lambda j: (0, j))],
        out_specs=pl.BlockSpec((n, D_MODEL), lambda j: (0, j)),
        out_shape=jax.ShapeDtypeStruct((n, width), F32),
        name="ada",
    )(c, w_ada, b_ada.reshape(1, width))


def _in_kernel(x_ref, mod_ref, g1_ref, win_ref, gq_ref, wuq_ref, gkv_ref, wukv_ref, cos_ref, sin_ref,
               qp_ref, kmla_ref, vmla_ref, sbq_ref, sbk16_ref, sbv16_ref, gates_ref,
               ckv_ref, krope_ref, sbk_ref, sbv_ref):
    x = x_ref[0]
    mod = mod_ref[0]
    h = _rms(x) * g1_ref[...] * (1.0 + mod[1:2]) + mod[0:1]
    hb = h.astype(BF16)

    def seg(a, b):
        return _dot(hb, win_ref[:, a:b])

    cos = cos_ref[...]
    sin = sin_ref[...]
    lane = lax.broadcasted_iota(jnp.int32, (1, LANES), 1)
    half = MLA_ROPE // 2

    def rotate(blk):
        other = jnp.where(lane < MLA_NOPE + half, pltpu.roll(blk, LANES - half, 1), pltpu.roll(blk, half, 1))
        return blk * cos + other * sin

    qn = (_rms(seg(C_QLAT, C_KV)) * gq_ref[...]).astype(BF16)
    q = _dot(qn, wuq_ref[...])
    qp_ref[0] = jnp.concatenate([rotate(q[:, h * HEAD_PAD:(h + 1) * HEAD_PAD]) for h in range(MLA_HEADS)],
                                axis=1).astype(BF16)

    ckv = _rms(seg(C_KV, C_SBQ)) * gkv_ref[...]
    ckv_ref[0] = ckv
    kv = _dot(ckv.astype(BF16), wukv_ref[...])
    krp = rotate(seg(C_KR, C_END))
    krope_ref[0] = krp[:, MLA_NOPE:MLA_NOPE + MLA_ROPE]
    kmla_ref[0] = (kv[:, :QP_WIDTH] + jnp.tile(krp, (1, MLA_HEADS))).astype(BF16)
    vmla_ref[0] = kv[:, QP_WIDTH:].astype(BF16)

    sbq_ref[0] = seg(C_SBQ, C_SBK).astype(BF16)
    sbk = seg(C_SBK, C_SBV)
    sbk_ref[0] = sbk
    sbk16_ref[0] = sbk.astype(BF16)
    sbv = seg(C_SBV, C_GATE)
    sbv_ref[0] = sbv
    sbv16_ref[0] = sbv.astype(BF16)
    gates_ref[0] = jax.nn.sigmoid(seg(C_GATE, C_KR)).astype(BF16)


def _in_proj(x, mod, wts, cos_t, sin_t, tm):
    b, s, _ = x.shape
    ns = s // tm
    tok = lambda w: pl.BlockSpec((1, tm, w), lambda i, j: (i, j, 0))
    full = lambda a: pl.BlockSpec(a.shape, lambda i, j: (0,) * a.ndim)
    out_widths = [(QP_WIDTH, BF16), (QP_WIDTH, BF16), (SB_WIDTH, BF16), (SB_WIDTH, BF16), (SB_WIDTH, BF16),
                  (SB_WIDTH, BF16), (2 * D_MODEL, BF16), (MLA_KV_LORA, F32), (MLA_ROPE, F32),
                  (SB_WIDTH, F32), (SB_WIDTH, F32)]
    return pl.pallas_call(
        _in_kernel,
        grid=(b, ns),
        in_specs=[tok(D_MODEL),
                  pl.BlockSpec((1, 6, D_MODEL), lambda i, j: (i, 0, 0)),
                  full(wts["g1"]), full(wts["w_in"]), full(wts["gq"]), full(wts["w_uq"]),
                  full(wts["gkv"]), full(wts["w_ukv"]),
                  pl.BlockSpec((tm, LANES), lambda i, j: (j, 0)),
                  pl.BlockSpec((tm, LANES), lambda i, j: (j, 0))],
        out_specs=[tok(w) for w, _ in out_widths],
        out_shape=[jax.ShapeDtypeStruct((b, s, w), dt) for w, dt in out_widths],
        compiler_params=pltpu.CompilerParams(dimension_semantics=("parallel", "parallel"),
                                             vmem_limit_bytes=VMEM_LIMIT),
        name="in_proj",
    )(x, mod, wts["g1"], wts["w_in"], wts["gq"], wts["w_uq"], wts["gkv"], wts["w_ukv"], cos_t, sin_t)


def _kvup_kernel(ckv_ref, kr_ref, wukv_ref, place_ref, kmla_ref, vmla_ref):
    kv = _dot(ckv_ref[0].astype(BF16), wukv_ref[...])
    kr = _dot(kr_ref[0].astype(BF16), place_ref[...])
    kmla_ref[0] = (kv[:, :QP_WIDTH] + kr).astype(BF16)
    vmla_ref[0] = kv[:, QP_WIDTH:].astype(BF16)


def _kv_up(ckv, krope, w_ukv_r, place, tm):
    b, p, _ = ckv.shape
    return pl.pallas_call(
        _kvup_kernel,
        grid=(b, p // tm),
        in_specs=[pl.BlockSpec((1, tm, MLA_KV_LORA), lambda i, j: (i, j, 0)),
                  pl.BlockSpec((1, tm, MLA_ROPE), lambda i, j: (i, j, 0)),
                  pl.BlockSpec(w_ukv_r.shape, lambda i, j: (0, 0)),
                  pl.BlockSpec(place.shape, lambda i, j: (0, 0))],
        out_specs=[pl.BlockSpec((1, tm, QP_WIDTH), lambda i, j: (i, j, 0)),
                   pl.BlockSpec((1, tm, SB_WIDTH), lambda i, j: (i, j, 0))],
        out_shape=[jax.ShapeDtypeStruct((b, p, QP_WIDTH), BF16), jax.ShapeDtypeStruct((b, p, SB_WIDTH), BF16)],
        compiler_params=pltpu.CompilerParams(dimension_semantics=("parallel", "parallel")),
        name="kv_up",
    )(ckv, krope, w_ukv_r, place)


def _tri(n):
    r = lax.broadcasted_iota(jnp.int32, (n, n), 0)
    c = lax.broadcasted_iota(jnp.int32, (n, n), 1)
    return jnp.where(r > c, 1.0, 0.0).astype(BF16)


def _stick_terms(z):
    log_sig = jnp.minimum(z, 0.0) - jnp.log2(1.0 + jnp.exp2(-jnp.abs(z)))
    return log_sig, log_sig - z


def _split_bf16(x):
    hi = x.astype(BF16)
    return hi, (x - hi.astype(F32)).astype(BF16)


def _finish_mixer(o_mla, o_sb, gates_ref, x_ref, mod_ref, wmu_ref, wsu_ref, wo_ref, g2_ref, x1_ref, h2_ref):
    u_mla = _dot(o_mla.astype(BF16), wmu_ref[...])
    u_sb = _dot(o_sb.astype(BF16), wsu_ref[...])
    gates = gates_ref[0]
    merged = gates[:, :D_MODEL].astype(F32) * u_mla + gates[:, D_MODEL:].astype(F32) * u_sb
    mix = _dot(merged.astype(BF16), wo_ref[...])
    mod = mod_ref[0]
    x1 = x_ref[0] + mod[2:3] * mix
    x1_ref[0] = x1
    h2 = _rms(x1) * g2_ref[...] * (1.0 + mod[4:5]) + mod[3:4]
    h2_ref[0] = _pack_halves(h2[:, :HALF], h2[:, HALF:])


def _prompt_attn_kernel(qp_ref, kmla_ref, vmla_ref, sbq_ref, sbk_ref, sbv_ref,
                        gates_ref, x_ref, mod_ref, wmu_ref, wsu_ref, wo_ref, g2_ref, x1_ref, h2_ref,
                        m_ref, l_ref, acc_ref, c_ref, sacc_ref, *, tq):
    i = pl.program_id(1)
    lane = lax.broadcasted_iota(jnp.int32, (1, LANES), 1)
    half_masks = (lane < MLA_V, lane >= MLA_V)
    row = lax.broadcasted_iota(jnp.int32, (tq, tq), 0)
    col = lax.broadcasted_iota(jnp.int32, (tq, tq), 1)
    chunk_mask = (col // CHUNK) <= (row // CHUNK)
    causal_mask = col < row
    piece = min(tq, MXU_TILE)
    tri_m = _tri(piece)
    st_diag = pl.multiple_of(i * tq, tq)

    def load(ref, start, c0):
        return ref[0, pl.ds(start, tq), c0:c0 + LANES]

    def mla_block(sub, q_h, k, v, mask):
        s = _dot_nt(q_h, k)
        if mask is not None:
            s = jnp.where(mask, s, -jnp.inf)
            m_new = jnp.max(s, axis=-1, keepdims=True)
            p = jnp.exp2(s - m_new)
            l_ref[sub] = jnp.sum(p, axis=-1, keepdims=True)
            acc_ref[sub] = _dot(p.astype(BF16), v)
        else:
            m_old = m_ref[sub]
            m_new = jnp.maximum(m_old, jnp.max(s, axis=-1, keepdims=True))
            p = jnp.exp2(s - m_new)
            alpha = jnp.exp2(m_old - m_new)
            l_ref[sub] = alpha * l_ref[sub] + jnp.sum(p, axis=-1, keepdims=True)
            acc_ref[sub] = alpha * acc_ref[sub] + _dot(p.astype(BF16), v)
        m_ref[sub] = m_new

    def sb_block(sub, q_h, k, v, mask):
        log_sig, log_keep = _stick_terms(_dot_nt(q_h, k))
        if mask is not None:
            log_keep = jnp.where(mask, log_keep, 0.0)
        summand = log_keep.astype(BF16)
        pieces = []
        total = None
        for b in reversed(range(tq // piece)):
            sl = slice(b * piece, (b + 1) * piece)
            inner = _dot(summand[:, sl], tri_m)
            piece_total = jnp.sum(log_keep[:, sl], axis=-1, keepdims=True)
            pieces.append(inner if total is None else inner + total)
            total = piece_total if total is None else total + piece_total
        after = jnp.concatenate(pieces[::-1], axis=1)
        if mask is not None:
            a = jnp.where(mask, jnp.exp2(log_sig + after), 0.0)
            sacc_ref[sub] = _dot(a.astype(BF16), v)
            c_ref[sub] = total
        else:
            c_old = c_ref[sub]
            a = jnp.exp2(log_sig + after + c_old)
            sacc_ref[sub] = sacc_ref[sub] + _dot(a.astype(BF16), v)
            c_ref[sub] = c_old + total

    o_mla = []
    o_sb = []
    for pair in range(MLA_HEADS // 2):
        vcol = pair * LANES
        kcols = [(2 * pair + sub) * HEAD_PAD for sub in range(2)]
        q_m = [qp_ref[0, :, kc:kc + HEAD_PAD] for kc in kcols]
        q_s = [jnp.where(half_masks[sub], sbq_ref[0, :, vcol:vcol + LANES], 0) for sub in range(2)]

        def pair_blocks(mla_start, sb_start, masks, vcol=vcol, kcols=kcols, q_m=q_m, q_s=q_s):
            v_m = load(vmla_ref, mla_start, vcol)
            k_s = load(sbk_ref, sb_start, vcol)
            v_s = load(sbv_ref, sb_start, vcol)
            for sub in range(2):
                keep = half_masks[sub]
                mla_block(sub, q_m[sub], load(kmla_ref, mla_start, kcols[sub]), jnp.where(keep, v_m, 0), masks[0])
                sb_block(sub, q_s[sub], k_s, jnp.where(keep, v_s, 0), masks[1])

        pair_blocks(st_diag, st_diag, (chunk_mask, causal_mask))

        def step(t, _, pair_blocks=pair_blocks):
            pair_blocks(pl.multiple_of(t * tq, tq), pl.multiple_of((i - 1 - t) * tq, tq), (None, None))
            return 0
        lax.fori_loop(0, i, step, 0)
        o_mla.append(acc_ref[0] / l_ref[0] + acc_ref[1] / l_ref[1])
        o_sb.append(sacc_ref[0] + sacc_ref[1])

    _finish_mixer(jnp.concatenate(o_mla, axis=1), jnp.concatenate(o_sb, axis=1),
                  gates_ref, x_ref, mod_ref, wmu_ref, wsu_ref, wo_ref, g2_ref, x1_ref, h2_ref)


def _decode_attn_kernel(qp_ref, kmla_ref, vmla_ref, sbq_ref, sbk_ref, sbv_ref, pkmla_ref, pvmla_ref, psbk_ref,
                        psbv_ref, gates_ref, x_ref, mod_ref, wmu_ref, wsu_ref, wo_ref, g2_ref, x1_ref, h2_ref,
                        *, tq, past_len, past_blk):
    n_past = past_len // past_blk
    lane = lax.broadcasted_iota(jnp.int32, (1, LANES), 1)
    half_masks = (lane < MLA_V, lane >= MLA_V)
    row = lax.broadcasted_iota(jnp.int32, (tq, tq), 0)
    col = lax.broadcasted_iota(jnp.int32, (tq, tq), 1)
    chunk_mask = ((past_len + col) // CHUNK) <= ((past_len + row) // CHUNK)
    causal_mask = col < row
    tri_new = _tri(tq)
    tri_past = _tri(past_blk)

    o_mla = []
    o_sb = []
    for pair in range(MLA_HEADS // 2):
        vcol = pair * LANES
        v_new = vmla_ref[0, :, vcol:vcol + LANES]
        v_past = pvmla_ref[0, :, vcol:vcol + LANES]
        sk_new = sbk_ref[0, :, vcol:vcol + LANES]
        sv_new = sbv_ref[0, :, vcol:vcol + LANES]
        sk_past = psbk_ref[0, :, vcol:vcol + LANES].astype(BF16)
        sv_past = psbv_ref[0, :, vcol:vcol + LANES].astype(BF16)
        mla_pair = None
        sb_pair = None
        for sub in range(2):
            keep = half_masks[sub]
            kcol = (2 * pair + sub) * HEAD_PAD

            q_h = qp_ref[0, :, kcol:kcol + HEAD_PAD]
            s_past = _dot_nt(q_h, pkmla_ref[0, :, kcol:kcol + HEAD_PAD])
            s_new = jnp.where(chunk_mask, _dot_nt(q_h, kmla_ref[0, :, kcol:kcol + HEAD_PAD]), -jnp.inf)
            m = jnp.maximum(jnp.max(s_past, axis=-1, keepdims=True), jnp.max(s_new, axis=-1, keepdims=True))
            p_past = jnp.exp2(s_past - m)
            p_new = jnp.exp2(s_new - m)
            denom = jnp.sum(p_past, axis=-1, keepdims=True) + jnp.sum(p_new, axis=-1, keepdims=True)
            o = (_dot(p_past.astype(BF16), jnp.where(keep, v_past, 0))
                 + _dot(p_new.astype(BF16), jnp.where(keep, v_new, 0))) / denom
            mla_pair = o if mla_pair is None else mla_pair + o

            q_s = jnp.where(keep, sbq_ref[0, :, vcol:vcol + LANES], 0)
            ls_new, lk_new = _stick_terms(_dot_nt(q_s, sk_new))
            lk_new = jnp.where(causal_mask, lk_new, 0.0)
            hi, lo = _split_bf16(lk_new)
            a_new = jnp.where(causal_mask, jnp.exp2(ls_new + _dot(hi, tri_new) + _dot(lo, tri_new)), 0.0)
            acc = _dot(a_new.astype(BF16), jnp.where(keep, sv_new, 0))
            later = jnp.sum(lk_new, axis=-1, keepdims=True)

            ls_past, lk_past = _stick_terms(_dot_nt(q_s, sk_past))
            hi, lo = _split_bf16(lk_past)
            blocks = lambda a: [a[:, b * past_blk:(b + 1) * past_blk] for b in range(n_past)]
            stacked = jnp.concatenate(blocks(hi) + blocks(lo), axis=0)
            within = _dot(stacked, tri_past)
            after = []
            for b in reversed(range(n_past)):
                after.append(within[b * tq:(b + 1) * tq] + within[(n_past + b) * tq:(n_past + b + 1) * tq] + later)
                later = later + jnp.sum(lk_past[:, b * past_blk:(b + 1) * past_blk], axis=-1, keepdims=True)
            a_past = jnp.exp2(ls_past + jnp.concatenate(after[::-1], axis=1))
            acc = acc + _dot(a_past.astype(BF16), jnp.where(keep, sv_past, 0))
            sb_pair = acc if sb_pair is None else sb_pair + acc
        o_mla.append(mla_pair)
        o_sb.append(sb_pair)

    _finish_mixer(jnp.concatenate(o_mla, axis=1), jnp.concatenate(o_sb, axis=1),
                  gates_ref, x_ref, mod_ref, wmu_ref, wsu_ref, wo_ref, g2_ref, x1_ref, h2_ref)


def _mixer_call(kernel_fn, name, tq, args_kv, gates, x, mod, wts, scratch, batch0=0, n_batch=None):
    s = x.shape[1]
    b = x.shape[0] if n_batch is None else n_batch
    tok = lambda w: pl.BlockSpec((1, tq, w), lambda i, j: (i + batch0, j, 0))
    seq = lambda a: pl.BlockSpec((1,) + a.shape[1:], lambda i, j: (i + batch0, 0, 0))
    full = lambda a: pl.BlockSpec(a.shape, lambda i, j: (0,) * a.ndim)
    out = lambda w: pl.BlockSpec((1, tq, w), lambda i, j: (i, j, 0))
    args = list(args_kv) + [gates, x, mod, wts["w_mla_up"], wts["w_sb_up"], wts["w_out"], wts["g2"]]
    specs = [seq(a) if whole else tok(a.shape[-1]) for a, whole in
             zip(args_kv, (False, True, True, False) + (True,) * (len(args_kv) - 4))]
    specs += [tok(2 * D_MODEL), tok(D_MODEL), pl.BlockSpec((1, 6, D_MODEL), lambda i, j: (i + batch0, 0, 0)),
              full(wts["w_mla_up"]), full(wts["w_sb_up"]), full(wts["w_out"]), full(wts["g2"])]
    return pl.pallas_call(
        kernel_fn,
        grid=(b, s // tq),
        in_specs=specs,
        out_specs=[out(D_MODEL), out(HALF)],
        out_shape=[jax.ShapeDtypeStruct((b, s, D_MODEL), F32), jax.ShapeDtypeStruct((b, s, HALF), jnp.int32)],
        scratch_shapes=scratch,
        compiler_params=pltpu.CompilerParams(dimension_semantics=("parallel", "arbitrary"),
                                             vmem_limit_bytes=VMEM_LIMIT),
        name=name,
    )(*args)


def _prompt_attention(qp, kmla, vmla, sbq, sbk16, sbv16, gates, x, mod, wts, tq, batch0, n_batch):
    col = lambda: pltpu.VMEM((2, tq, 1), F32)
    wide = lambda: pltpu.VMEM((2, tq, LANES), F32)
    return _mixer_call(functools.partial(_prompt_attn_kernel, tq=tq), "attention", tq,
                       (qp, kmla, vmla, sbq, sbk16, sbv16), gates, x, mod, wts,
                       [col(), col(), wide(), col(), wide()], batch0, n_batch)


def _decode_attention(qp, kmla, vmla, sbq, sbk16, sbv16, past, gates, x, mod, wts, past_blk):
    tq = x.shape[1]
    past_len = past[0].shape[1]
    kern = functools.partial(_decode_attn_kernel, tq=tq, past_len=past_len, past_blk=past_blk)
    return _mixer_call(kern, "decode_attention", tq, (qp, kmla, vmla, sbq, sbk16, sbv16) + tuple(past),
                       gates, x, mod, wts, [])


def _route_kernel(h2_ref, wr_ref, br_ref, idx_ref, wt_ref, rank_ref, cnt_ref, seen_ref):
    lo, hi = _unpack_halves(h2_ref[...])
    tm = lo.shape[0]
    logits = _dot_nt(wr_ref[:, :HALF], lo.astype(BF16)) + _dot_nt(wr_ref[:, HALF:], hi.astype(BF16))
    scores = jax.nn.sigmoid(logits)
    sel = scores + br_ref[...]
    neg = -jnp.inf

    grp = sel.reshape(N_GROUPS, GROUP_SIZE, tm)
    within = lax.broadcasted_iota(jnp.int32, grp.shape, 1)
    top1 = jnp.max(grp, axis=1, keepdims=True)
    first = jnp.min(jnp.where(grp == top1, within, GROUP_SIZE), axis=1, keepdims=True)
    top2 = jnp.max(jnp.where(within == first, neg, grp), axis=1, keepdims=True)
    gscore = (top1 + top2).reshape(N_GROUPS, tm)

    gid = lax.broadcasted_iota(jnp.int32, gscore.shape, 0)
    chosen = jnp.zeros(gscore.shape, jnp.bool_)
    for _ in range(TOPK_GROUPS):
        best = jnp.max(gscore, axis=0, keepdims=True)
        pick = jnp.min(jnp.where(gscore == best, gid, N_GROUPS), axis=0, keepdims=True)
        hit = gid == pick
        chosen = jnp.logical_or(chosen, hit)
        gscore = jnp.where(hit, neg, gscore)
    chosen3 = jnp.broadcast_to(chosen.reshape(N_GROUPS, 1, tm), grp.shape)
    cand = jnp.where(chosen3, grp, neg).reshape(N_EXPERTS, tm)

    eid = lax.broadcasted_iota(jnp.int32, cand.shape, 0)
    picks = []
    weights = []
    for _ in range(TOP_K):
        best = jnp.max(cand, axis=0, keepdims=True)
        pick = jnp.min(jnp.where(cand == best, eid, N_EXPERTS), axis=0, keepdims=True)
        hit = eid == pick
        weights.append(jnp.sum(jnp.where(hit, scores, 0.0), axis=0, keepdims=True))
        picks.append(pick)
        cand = jnp.where(hit, neg, cand)
    w = jnp.concatenate(weights, axis=0)
    idx_ref[...] = jnp.concatenate(picks, axis=0)
    wt_ref[...] = w / (jnp.sum(w, axis=0, keepdims=True) + 1e-20) * ROUTED_SCALE

    @pl.when(pl.program_id(0) == 0)
    def _():
        seen_ref[...] = jnp.zeros_like(seen_ref)

    onehot = jnp.zeros(cand.shape, F32)
    for pick in picks:
        onehot = onehot + jnp.where(eid == pick, 1.0, 0.0)
    src = lax.broadcasted_iota(jnp.int32, (tm, tm), 0)
    dst = lax.broadcasted_iota(jnp.int32, (tm, tm), 1)
    earlier = jnp.where(src < dst, 1.0, 0.0).astype(BF16)
    before = _dot(onehot.astype(BF16), earlier) + seen_ref[...]
    rank_ref[...] = jnp.concatenate(
        [jnp.sum(jnp.where(eid == pick, before, 0.0), axis=0, keepdims=True) for pick in picks],
        axis=0).astype(jnp.int32)
    seen = seen_ref[...] + jnp.sum(onehot, axis=1, keepdims=True)
    seen_ref[...] = seen
    cnt_ref[...] = seen.astype(jnp.int32)


def _route(h2p, wr_t, br, tm):
    t = h2p.shape[0]
    kt = lambda: pl.BlockSpec((TOP_K, tm), lambda i: (0, i))
    return pl.pallas_call(
        _route_kernel,
        grid=(t // tm,),
        in_specs=[pl.BlockSpec((tm, HALF), lambda i: (i, 0)),
                  pl.BlockSpec(wr_t.shape, lambda i: (0, 0)),
                  pl.BlockSpec(br.shape, lambda i: (0, 0))],
        out_specs=[kt(), kt(), kt(), pl.BlockSpec((N_EXPERTS, 1), lambda i: (0, 0))],
        out_shape=[jax.ShapeDtypeStruct((TOP_K, t), jnp.int32), jax.ShapeDtypeStruct((TOP_K, t), F32),
                   jax.ShapeDtypeStruct((TOP_K, t), jnp.int32), jax.ShapeDtypeStruct((N_EXPERTS, 1), jnp.int32)],
        scratch_shapes=[pltpu.VMEM((N_EXPERTS, 1), F32)],
        compiler_params=pltpu.CompilerParams(dimension_semantics=("arbitrary",)),
        name="route",
    )(h2p, wr_t, br)


def _position_kernel(idx_ref, rank_ref, start_ref, pos_ref):
    idx = idx_ref[...]
    eid = lax.broadcasted_iota(jnp.int32, (N_EXPERTS, idx.shape[1]), 0)
    start = start_ref[...]
    base = jnp.concatenate(
        [jnp.sum(jnp.where(eid == idx[k:k + 1, :], start, 0.0), axis=0, keepdims=True) for k in range(TOP_K)],
        axis=0)
    pos_ref[...] = base.astype(jnp.int32) + rank_ref[...]


def _positions(idx_kt, rank_kt, pad_start, tm):
    t = idx_kt.shape[1]
    kt = lambda: pl.BlockSpec((TOP_K, tm), lambda i: (0, i))
    return pl.pallas_call(
        _position_kernel,
        grid=(t // tm,),
        in_specs=[kt(), kt(), pl.BlockSpec((N_EXPERTS, 1), lambda i: (0, 0))],
        out_specs=kt(),
        out_shape=jax.ShapeDtypeStruct((TOP_K, t), jnp.int32),
        compiler_params=pltpu.CompilerParams(dimension_semantics=("parallel",)),
        name="positions",
    )(idx_kt, rank_kt, pad_start.astype(F32).reshape(N_EXPERTS, 1))


def _sc_mesh():
    return plsc.VectorSubcoreMesh(core_axis_name="c", subcore_axis_name="s",
                                  num_cores=SC_CORES, num_subcores=SC_SUBCORES)


def _sc_scatter_rows(rows, pos, n_out, chunk):
    t, width = rows.shape
    copies = pos.shape[0] // t
    per_worker = t // SC_WORKERS
    n_chunks = per_worker // chunk
    tail = per_worker - n_chunks * chunk
    assert per_worker * SC_WORKERS == t and tail % 8 == 0
    tail_rows = max(tail, 8)

    @functools.partial(
        pl.kernel, mesh=_sc_mesh(),
        out_type=jax.ShapeDtypeStruct((n_out, width), rows.dtype),
        scratch_types=[pltpu.VMEM((copies, chunk), jnp.int32), pltpu.VMEM((chunk, width), rows.dtype),
                       pltpu.VMEM((copies, tail_rows), jnp.int32), pltpu.VMEM((tail_rows, width), rows.dtype),
                       pltpu.SemaphoreType.DMA, pltpu.SemaphoreType.DMA],
    )
    def scatter(rows_hbm, pos_hbm, out_hbm, idx_v, rows_v, idx_t, rows_t, load_sem, store_sem):
        wid = lax.axis_index("s") * SC_CORES + lax.axis_index("c")
        base = wid * per_worker

        def move(off, n, idx_buf, row_buf):
            loads = [pltpu.async_copy(rows_hbm.at[pl.ds(off, n)], row_buf, load_sem)]
            for k in range(copies):
                src = pos_hbm.at[pl.ds(pl.multiple_of(k * t + off, 8), n)]
                loads.append(pltpu.async_copy(src, idx_buf.at[k], load_sem))
            for cp in loads:
                cp.wait()
            stores = [pltpu.async_copy(row_buf, out_hbm.at[idx_buf.at[k]], store_sem) for k in range(copies)]
            for cp in stores:
                cp.wait()

        @pl.loop(0, n_chunks)
        def _(c):
            move(pl.multiple_of(base + c * chunk, 8), chunk, idx_v, rows_v)

        if tail:
            move(pl.multiple_of(base + n_chunks * chunk, 8), tail, idx_t, rows_t)

    return scatter(rows, pos)


def _sc_gather_rows(table, idx, chunk):
    n_rows = idx.shape[0]
    width = table.shape[1]
    per_worker = n_rows // SC_WORKERS
    n_chunks = per_worker // chunk
    assert per_worker * SC_WORKERS == n_rows and n_chunks * chunk == per_worker
    mesh = _sc_mesh()

    @functools.partial(
        pl.kernel, mesh=mesh,
        out_type=jax.ShapeDtypeStruct((n_rows, width), table.dtype),
        scratch_types=[pltpu.VMEM((chunk,), jnp.int32), pltpu.VMEM((chunk, width), table.dtype),
                       pltpu.SemaphoreType.DMA],
    )
    def gather(table_hbm, idx_hbm, out_hbm, idx_v, rows_v, sem):
        wid = lax.axis_index("s") * SC_CORES + lax.axis_index("c")
        base = wid * per_worker

        @pl.loop(0, n_chunks)
        def _(c):
            off = pl.multiple_of(base + c * chunk, 8)
            pltpu.sync_copy(idx_hbm.at[pl.ds(off, chunk)], idx_v)
            pltpu.async_copy(table_hbm.at[idx_v], rows_v, sem).wait()
            pltpu.sync_copy(rows_v, out_hbm.at[pl.ds(off, chunk)])

    return gather(table, idx)


def _expert_kernel(be_ref, nv_ref, nu_ref, xs_ref, w1_ref, w3_ref, w2_ref, os_ref, wb1, wb3, wb2):
    i = pl.program_id(0)

    @pl.when(i < nu_ref[0])
    def _():
        @pl.when(jnp.logical_or(i == 0, be_ref[i] != be_ref[jnp.maximum(i - 1, 0)]))
        def _():
            wb1[...] = w1_ref[0].astype(BF16)
            wb3[...] = w3_ref[0].astype(BF16)
            wb2[...] = w2_ref[0].astype(BF16)

        packed = xs_ref[...]
        live = lax.broadcasted_iota(jnp.int32, packed.shape, 0) < nv_ref[i]
        lo, hi = _unpack_halves(jnp.where(live, packed, 0))
        lo = lo.astype(BF16)
        hi = hi.astype(BF16)
        a = _dot(lo, wb1[:HALF, :]) + _dot(hi, wb1[HALF:, :])
        b = _dot(lo, wb3[:HALF, :]) + _dot(hi, wb3[HALF:, :])
        o = _dot((_silu(a) * b).astype(BF16), wb2[...])
        os_ref[...] = _pack_halves(o[:, :HALF], o[:, HALF:])


def _experts(xs, blk_exp, blk_valid, n_used, w1, w3, w2, blk):
    rows = xs.shape[0]
    nb = rows // blk
    last = lambda i, nu: jnp.minimum(i, nu[0] - 1)
    grid_spec = pltpu.PrefetchScalarGridSpec(
        num_scalar_prefetch=3,
        grid=(nb,),
        in_specs=[pl.BlockSpec((blk, HALF), lambda i, be, nv, nu: (last(i, nu), 0)),
                  pl.BlockSpec((1, D_MODEL, EXPERT_DIM), lambda i, be, nv, nu: (be[last(i, nu)], 0, 0)),
                  pl.BlockSpec((1, D_MODEL, EXPERT_DIM), lambda i, be, nv, nu: (be[last(i, nu)], 0, 0)),
                  pl.BlockSpec((1, EXPERT_DIM, D_MODEL), lambda i, be, nv, nu: (be[last(i, nu)], 0, 0))],
        out_specs=pl.BlockSpec((blk, HALF), lambda i, be, nv, nu: (last(i, nu), 0)),
        scratch_shapes=[pltpu.VMEM((D_MODEL, EXPERT_DIM), BF16), pltpu.VMEM((D_MODEL, EXPERT_DIM), BF16),
                        pltpu.VMEM((EXPERT_DIM, D_MODEL), BF16)],
    )
    return pl.pallas_call(
        _expert_kernel,
        grid_spec=grid_spec,
        out_shape=jax.ShapeDtypeStruct((rows, HALF), jnp.int32),
        compiler_params=pltpu.CompilerParams(dimension_semantics=("arbitrary",), vmem_limit_bytes=VMEM_LIMIT),
        name="experts",
    )(blk_exp, blk_valid, n_used, xs, w1, w3, w2)


def _combine_kernel(g_ref, wt_ref, h2_ref, x1_ref, mod_ref, sw1_ref, sw3_ref, sw2_ref, gf_ref, y_ref):
    wt = wt_ref[...]
    lo_acc = None
    for k in range(TOP_K):
        lo, hi = _unpack_halves(g_ref[k])
        wk = wt[:, k:k + 1]
        lo_acc = wk * lo if lo_acc is None else lo_acc + wk * lo
        hi_acc = wk * hi if k == 0 else hi_acc + wk * hi
    routed = jnp.concatenate([lo_acc, hi_acc], axis=1)
    lo, hi = _unpack_halves(h2_ref[...])
    lo = lo.astype(BF16)
    hi = hi.astype(BF16)
    a = _dot(lo, sw1_ref[:HALF, :]) + _dot(hi, sw1_ref[HALF:, :])
    b = _dot(lo, sw3_ref[:HALF, :]) + _dot(hi, sw3_ref[HALF:, :])
    shared = _dot((_silu(a) * b).astype(BF16), sw2_ref[...])
    mod = mod_ref[0]
    x2 = x1_ref[...] + mod[5:6] * (routed + shared)
    y_ref[...] = _rms(x2) * gf_ref[...]


def _combine_into_kernel(g_ref, wt_ref, h2_ref, x1_ref, mod_ref, sw1_ref, sw3_ref, sw2_ref, gf_ref, prev_ref, y_ref):
    del prev_ref
    _combine_kernel(g_ref, wt_ref, h2_ref, x1_ref, mod_ref, sw1_ref, sw3_ref, sw2_ref, gf_ref, y_ref)


def _combine(gathered, wt, h2p, x1, mod, wts, tm, src_tile0, mod_batch0, tiles_per_batch, out_rows, out_tile0,
             y_prev=None):
    t = x1.shape[0]
    full = lambda a: pl.BlockSpec(a.shape, lambda i: (0,) * a.ndim)
    shifted = lambda w: pl.BlockSpec((tm, w), lambda i: (i + src_tile0, 0))
    args = [gathered, wt, h2p, x1, mod, wts["sw1"], wts["sw3"], wts["sw2"], wts["gf"]]
    specs = [pl.BlockSpec((TOP_K, tm, HALF), lambda i: (0, i + src_tile0, 0)), shifted(TOP_K), shifted(HALF),
             pl.BlockSpec((tm, D_MODEL), lambda i: (i, 0)),
             pl.BlockSpec((1, 6, D_MODEL), lambda i: (i // tiles_per_batch + mod_batch0, 0, 0)),
             full(wts["sw1"]), full(wts["sw3"]), full(wts["sw2"]), full(wts["gf"])]
    aliases = {}
    body = _combine_kernel
    if y_prev is not None:
        args.append(y_prev)
        specs.append(pl.BlockSpec(memory_space=pl.ANY))
        aliases = {len(args) - 1: 0}
        body = _combine_into_kernel
    return pl.pallas_call(
        body,
        grid=(t // tm,),
        in_specs=specs,
        out_specs=pl.BlockSpec((tm, D_MODEL), lambda i: (i + out_tile0, 0)),
        out_shape=jax.ShapeDtypeStruct((out_rows, D_MODEL), F32),
        input_output_aliases=aliases,
        compiler_params=pltpu.CompilerParams(dimension_semantics=("parallel",), vmem_limit_bytes=VMEM_LIMIT),
        name="combine",
    )(*args)


def _rope_tables(pos):
    half = MLA_ROPE // 2
    inv_freq = ROPE_THETA ** (-jnp.arange(half, dtype=F32) / half)
    ang = pos.astype(F32)[:, None] * inv_freq
    cos, sin = jnp.cos(ang), jnp.sin(ang)
    n = pos.shape[0]
    ones = jnp.ones((n, MLA_NOPE), F32)
    z_nope = jnp.zeros((n, MLA_NOPE), F32)
    z_pad = jnp.zeros((n, HEAD_PAD - MLA_NOPE - MLA_ROPE), F32)
    return (jnp.concatenate([ones, cos, cos, z_pad], axis=1),
            jnp.concatenate([z_nope, -sin, sin, z_pad], axis=1))


def _prep_weights(g_norm1, w_in, g_q_lat, w_uq, g_kv_lat, w_ukv, w_mla_up, w_sb_up, w_out, g_norm2,
                  w_router, b_router, shared_w1, shared_w3, shared_w2, g_final):
    w = w_in[0]
    o = 0
    parts = {}
    for name, width in (("qlat", MLA_Q_LORA), ("kv", MLA_KV_LORA), ("kr", MLA_ROPE), ("sbq", SB_WIDTH),
                        ("sbk", SB_WIDTH), ("sbv", SB_WIDTH), ("gm", D_MODEL), ("gs", D_MODEL)):
        parts[name] = w[:, o:o + width]
        o += width
    kr = parts["kr"]
    z_nope = jnp.zeros((D_MODEL, MLA_NOPE), F32)
    z_pad = jnp.zeros((D_MODEL, HEAD_PAD - MLA_NOPE - MLA_ROPE), F32)
    kr_seg = jnp.concatenate([z_nope, kr, z_pad], axis=1)
    w_in_r = jnp.concatenate([parts["qlat"], parts["kv"], parts["sbq"] * (SB_DIM ** -0.5 * LOG2_E), parts["sbk"],
                              parts["sbv"], parts["gm"], parts["gs"], kr_seg], axis=1).astype(BF16)

    scale = (MLA_NOPE + MLA_ROPE) ** -0.5 * LOG2_E
    uq = w_uq[0].reshape(MLA_Q_LORA, MLA_HEADS, MLA_NOPE + MLA_ROPE) * scale
    nope, rope = uq[..., :MLA_NOPE], uq[..., MLA_NOPE:]
    zq_pad = jnp.zeros((MLA_Q_LORA, MLA_HEADS, HEAD_PAD - MLA_NOPE - MLA_ROPE), F32)
    w_uq_r = jnp.concatenate([nope, rope, zq_pad], axis=-1).reshape(MLA_Q_LORA, QP_WIDTH).astype(BF16)

    ukv = w_ukv[0].reshape(MLA_KV_LORA, MLA_HEADS, MLA_NOPE + MLA_V)
    k_nope, v = ukv[..., :MLA_NOPE], ukv[..., MLA_NOPE:]
    k_pad = jnp.concatenate([k_nope, jnp.zeros((MLA_KV_LORA, MLA_HEADS, HEAD_PAD - MLA_NOPE), F32)], axis=-1)
    w_ukv_r = jnp.concatenate([k_pad.reshape(MLA_KV_LORA, QP_WIDTH),
                               v.reshape(MLA_KV_LORA, MLA_HEADS * MLA_V)], axis=1).astype(BF16)

    eye = jnp.eye(MLA_ROPE, dtype=F32)
    place_head = jnp.concatenate([jnp.zeros((MLA_ROPE, MLA_NOPE), F32), eye,
                                  jnp.zeros((MLA_ROPE, HEAD_PAD - MLA_NOPE - MLA_ROPE), F32)], axis=1)
    place = jnp.tile(place_head, (1, MLA_HEADS)).astype(BF16)

    return dict(
        g1=g_norm1[0].reshape(1, D_MODEL), w_in=w_in_r, gq=g_q_lat[0].reshape(1, MLA_Q_LORA), w_uq=w_uq_r,
        gkv=g_kv_lat[0].reshape(1, MLA_KV_LORA), w_ukv=w_ukv_r, place=place,
        w_mla_up=w_mla_up[0].astype(BF16), w_sb_up=w_sb_up[0].astype(BF16), w_out=w_out[0].astype(BF16),
        g2=g_norm2[0].reshape(1, D_MODEL), wr_t=w_router[0].T.astype(BF16),
        br=b_router[0].reshape(N_EXPERTS, 1),
        sw1=shared_w1[0].astype(BF16), sw3=shared_w3[0].astype(BF16), sw2=shared_w2[0].astype(BF16),
        gf=g_final.reshape(1, D_MODEL))


def _block_plan(counts, nb, blk):
    padded = (counts + blk - 1) // blk * blk
    pad_end = jnp.cumsum(padded)
    pad_start = pad_end - padded
    first_row = jnp.arange(nb, dtype=jnp.int32) * blk
    blk_exp = jnp.minimum(jnp.sum((pad_end[None, :] <= first_row[:, None]).astype(jnp.int32), axis=1),
                          N_EXPERTS - 1)
    own = blk_exp[:, None] == jnp.arange(N_EXPERTS, dtype=jnp.int32)[None, :]
    seg_end = jnp.sum(jnp.where(own, (pad_start + counts)[None, :], 0), axis=1)
    blk_valid = jnp.clip(seg_end - first_row, 0, blk).astype(jnp.int32)
    n_used = (pad_end[-1:] // blk).astype(jnp.int32)
    return pad_start, blk_exp.astype(jnp.int32), blk_valid, n_used


def _moe_rows(h2p, idx_kt, rank_kt, counts, w1, w3, w2, scatter_rows, gather_rows, blk, tm):
    t = h2p.shape[0]
    nb = -(-t * TOP_K // blk) + N_EXPERTS
    pad_start, blk_exp, blk_valid, n_used = _block_plan(counts.reshape(N_EXPERTS), nb, blk)
    pos = _positions(idx_kt, rank_kt, pad_start, tm).reshape(TOP_K * t)
    xs = scatter_rows(h2p, pos, nb * blk)
    os_ = _experts(xs, blk_exp, blk_valid, n_used, w1, w3, w2, blk)
    return gather_rows(os_, pos).reshape(TOP_K, t, HALF)


def _forward(x_prompt, x_sample, cache_mla_ckv, cache_mla_krope, cache_sb_k, cache_sb_v, c_prompt, c_sample,
             w_ada, b_ada, moe_w1, moe_w3, moe_w2, wts, scatter_rows, gather_rows, token_block, attn_block,
             route_block, moe_block):
    bp, sp, _ = x_prompt.shape
    bs, ss, _ = x_sample.shape
    past_len = cache_mla_ckv.shape[2]

    mod = _ada(jnp.concatenate([c_prompt, c_sample], axis=0), w_ada[0], b_ada[0]).reshape(bp + bs, 6, D_MODEL)
    mod_p, mod_s = mod[:bp], mod[bp:]

    cos_p, sin_p = _rope_tables(jnp.arange(sp))
    (qp, kmla, vmla, sbq, sbk16, sbv16, gates, ckv_p, krope_p, sbk_p, sbv_p) = _in_proj(
        x_prompt, mod_p, wts, cos_p, sin_p, token_block)
    half_b = bp // 2
    mixed = [_prompt_attention(qp, kmla, vmla, sbq, sbk16, sbv16, gates, x_prompt, mod_p, wts, attn_block,
                               b0, half_b) for b0 in (0, half_b)]

    cos_s, sin_s = _rope_tables(past_len + jnp.arange(ss))
    (qs, kmla_s, vmla_s, sbq_s, sbk16_s, sbv16_s, gates_s, ckv_s, krope_s, sbk_s, sbv_s) = _in_proj(
        x_sample, mod_s, wts, cos_s, sin_s, ss)
    pkmla, pvmla = _kv_up(cache_mla_ckv[0], cache_mla_krope[0], wts["w_ukv"], wts["place"], token_block)
    past = (pkmla, pvmla, cache_sb_k[0].reshape(bs, past_len, SB_WIDTH), cache_sb_v[0].reshape(bs, past_len, SB_WIDTH))
    x1_s, h2_s = _decode_attention(qs, kmla_s, vmla_s, sbq_s, sbk16_s, sbv16_s, past, gates_s, x_sample, mod_s,
                                   wts, token_block)

    tp, ts = bp * sp, bs * ss
    th = half_b * sp

    def moe(h2_rows):
        idx_kt, wt_kt, rank_kt, counts = _route(h2_rows, wts["wr_t"], wts["br"], route_block)
        gathered = _moe_rows(h2_rows, idx_kt, rank_kt, counts, moe_w1[0], moe_w3[0], moe_w2[0],
                             scatter_rows, gather_rows, moe_block, route_block)
        return gathered, wt_kt.T

    tiles_per_batch = sp // token_block
    (x1_a, h2_a), (x1_b, h2_b) = mixed
    h2_a = h2_a.reshape(th, HALF)
    g_a, wt_a = moe(h2_a)
    y_p = _combine(g_a, wt_a, h2_a, x1_a.reshape(th, D_MODEL), mod_p, wts, token_block,
                   0, 0, tiles_per_batch, tp, 0)
    h2_b = jnp.concatenate([h2_b.reshape(th, HALF), h2_s.reshape(ts, HALF)], axis=0)
    g_b, wt_b = moe(h2_b)
    y_p = _combine(g_b, wt_b, h2_b, x1_b.reshape(th, D_MODEL), mod_p, wts, token_block,
                   0, half_b, tiles_per_batch, tp, th // token_block, y_prev=y_p)
    y_s = _combine(g_b, wt_b, h2_b, x1_s.reshape(ts, D_MODEL), mod_s, wts, ss, th // ss, 0, 1, ts, 0)

    heads = lambda a, b_, s_: a.reshape(1, b_, s_, SB_HEADS, SB_DIM)
    return (y_p.reshape(bp, sp, D_MODEL), y_s.reshape(bs, ss, D_MODEL),
            ckv_p[None], krope_p[None], heads(sbk_p, bp, sp), heads(sbv_p, bp, sp),
            ckv_s[None], krope_s[None], heads(sbk_s, bs, ss), heads(sbv_s, bs, ss))


def kernel(x_prompt, x_sample, cache_mla_ckv, cache_mla_krope, cache_sb_k, cache_sb_v, c_prompt, c_sample, w_ada, b_ada, g_norm1, w_in, g_q_lat, w_uq, g_kv_lat, w_ukv, w_mla_up, w_sb_up, w_out, g_norm2, w_router, b_router, moe_w1, moe_w3, moe_w2, shared_w1, shared_w3, shared_w2, g_final):
    wts = _prep_weights(g_norm1, w_in, g_q_lat, w_uq, g_kv_lat, w_ukv, w_mla_up, w_sb_up, w_out, g_norm2,
                        w_router, b_router, shared_w1, shared_w3, shared_w2, g_final)
    scatter_rows = functools.partial(_sc_scatter_rows, chunk=SC_CHUNK)
    gather_rows = functools.partial(_sc_gather_rows, chunk=SC_CHUNK)
    return _forward(x_prompt, x_sample, cache_mla_ckv, cache_mla_krope, cache_sb_k, cache_sb_v, c_prompt, c_sample,
                    w_ada, b_ada, moe_w1, moe_w3, moe_w2, wts, scatter_rows, gather_rows, TOKEN_BLOCK, ATTN_BLOCK,
                    TOKEN_BLOCK, MOE_BLOCK)
```

```python
import functools

import jax
import jax.numpy as jnp
from jax import lax
from jax.experimental import pallas as pl
from jax.experimental.pallas import tpu as pltpu
from jax.experimental.pallas import tpu_sc as plsc

F32 = jnp.float32
BF16 = jnp.bfloat16

D_MODEL = 1024
NORM_EPS = 1e-6
CHUNK = 64
MLA_HEADS = 8
MLA_NOPE = 64
MLA_ROPE = 32
MLA_V = 64
MLA_Q_LORA = 384
MLA_KV_LORA = 256
ROPE_THETA = 10000.0
SB_HEADS = 8
SB_DIM = 64
SB_WIDTH = SB_HEADS * SB_DIM
N_EXPERTS = 256
TOP_K = 8
N_GROUPS = 8
TOPK_GROUPS = 4
GROUP_SIZE = N_EXPERTS // N_GROUPS
EXPERT_DIM = 256
ROUTED_SCALE = 2.5
LOG2_E = 1.4426950408889634

LANES = 128
MXU_TILE = 256
SC_CORES = 2
SC_SUBCORES = 16
SC_WORKERS = SC_CORES * SC_SUBCORES
VMEM_LIMIT = 60 * 1024 * 1024

HEAD_PAD = LANES
QP_WIDTH = MLA_HEADS * HEAD_PAD
HALF = D_MODEL // 2

C_QLAT = 0
C_KV = C_QLAT + MLA_Q_LORA
C_SBQ = C_KV + MLA_KV_LORA
C_SBK = C_SBQ + SB_WIDTH
C_SBV = C_SBK + SB_WIDTH
C_GATE = C_SBV + SB_WIDTH
C_KR = C_GATE + 2 * D_MODEL
C_END = C_KR + LANES

MOE_BLOCK = 512
WIDE_BLOCK = 512
TOKEN_BLOCK = 256
ATTN_BLOCK = 512
PAIRS_PER_LOOP = 1
SC_CHUNK = 64


def _rms(x):
    return x * lax.rsqrt(jnp.mean(x * x, axis=-1, keepdims=True) + NORM_EPS)


def _silu(x):
    return x * jax.nn.sigmoid(x)


def _pack_halves(lo, hi):
    lo_bits = lax.bitcast_convert_type(lo.astype(BF16).astype(F32), jnp.uint32) >> 16
    hi_bits = lax.bitcast_convert_type(hi.astype(BF16).astype(F32), jnp.uint32) & jnp.uint32(0xFFFF0000)
    return lax.bitcast_convert_type(lo_bits | hi_bits, jnp.int32)


def _unpack_halves(p):
    u = lax.bitcast_convert_type(p, jnp.uint32)
    lo = lax.bitcast_convert_type(u << 16, F32)
    hi = lax.bitcast_convert_type(u & jnp.uint32(0xFFFF0000), F32)
    return lo, hi


def _dot(a, b):
    return jnp.dot(a, b, preferred_element_type=F32)


def _dot_nt(a, b):
    return lax.dot_general(a, b, (((1,), (1,)), ((), ())), preferred_element_type=F32)


def _ada_kernel(c_ref, w_ref, b_ref, o_ref):
    c = c_ref[...]
    o_ref[...] = _dot(_silu(c).astype(BF16), w_ref[...].astype(BF16)) + b_ref[...]


def _ada(c, w_ada, b_ada):
    n = c.shape[0]
    width = w_ada.shape[1]
    return pl.pallas_call(
        _ada_kernel,
        grid=(width // D_MODEL,),
        in_specs=[pl.BlockSpec((n, D_MODEL), lambda j: (0, 0)),
                  pl.BlockSpec((D_MODEL, D_MODEL), lambda j: (0, j)),
                  pl.BlockSpec((1, D_MODEL), lambda j: (0, j))],
        out_specs=pl.BlockSpec((n, D_MODEL), lambda j: (0, j)),
        out_shape=jax.ShapeDtypeStruct((n, width), F32),
        name="ada",
    )(c, w_ada, b_ada.reshape(1, width))


def _in_kernel(x_ref, mod_ref, g1_ref, win_ref, gq_ref, wuq_ref, gkv_ref, wukv_ref, cos_ref, sin_ref,
               qp_ref, kmla_ref, vmla_ref, sbq_ref, sbk16_ref, sbv16_ref, gates_ref,
               ckv_ref, krope_ref, sbk_ref, sbv_ref):
    x = x_ref[0]
    mod = mod_ref[0]
    h = _rms(x) * g1_ref[...] * (1.0 + mod[1:2]) + mod[0:1]
    hb = h.astype(BF16)

    def seg(a, b):
        return _dot(hb, win_ref[:, a:b])

    cos = cos_ref[...]
    sin = sin_ref[...]
    lane = lax.broadcasted_iota(jnp.int32, (1, LANES), 1)
    half = MLA_ROPE // 2

    def rotate(blk):
        other = jnp.where(lane < MLA_NOPE + half, pltpu.roll(blk, LANES - half, 1), pltpu.roll(blk, half, 1))
        return blk * cos + other * sin

    qn = (_rms(seg(C_QLAT, C_KV)) * gq_ref[...]).astype(BF16)
    q = _dot(qn, wuq_ref[...])
    qp_ref[0] = jnp.concatenate([rotate(q[:, h * HEAD_PAD:(h + 1) * HEAD_PAD]) for h in range(MLA_HEADS)],
                                axis=1).astype(BF16)

    ckv = _rms(seg(C_KV, C_SBQ)) * gkv_ref[...]
    ckv_ref[0] = ckv
    kv = _dot(ckv.astype(BF16), wukv_ref[...])
    krp = rotate(seg(C_KR, C_END))
    krope_ref[0] = krp[:, MLA_NOPE:MLA_NOPE + MLA_ROPE]
    kmla_ref[0] = (kv[:, :QP_WIDTH] + jnp.tile(krp, (1, MLA_HEADS))).astype(BF16)
    vmla_ref[0] = kv[:, QP_WIDTH:].astype(BF16)

    sbq_ref[0] = seg(C_SBQ, C_SBK).astype(BF16)
    sbk = seg(C_SBK, C_SBV)
    sbk_ref[0] = sbk
    sbk16_ref[0] = sbk.astype(BF16)
    sbv = seg(C_SBV, C_GATE)
    sbv_ref[0] = sbv
    sbv16_ref[0] = sbv.astype(BF16)
    gates_ref[0] = jax.nn.sigmoid(seg(C_GATE, C_KR)).astype(BF16)


def _in_proj(x, mod, wts, cos_t, sin_t, tm):
    b, s, _ = x.shape
    ns = s // tm
    tok = lambda w: pl.BlockSpec((1, tm, w), lambda i, j: (i, j, 0))
    full = lambda a: pl.BlockSpec(a.shape, lambda i, j: (0,) * a.ndim)
    out_widths = [(QP_WIDTH, BF16), (QP_WIDTH, BF16), (SB_WIDTH, BF16), (SB_WIDTH, BF16), (SB_WIDTH, BF16),
                  (SB_WIDTH, BF16), (2 * D_MODEL, BF16), (MLA_KV_LORA, F32), (MLA_ROPE, F32),
                  (SB_WIDTH, F32), (SB_WIDTH, F32)]
    return pl.pallas_call(
        _in_kernel,
        grid=(b, ns),
        in_specs=[tok(D_MODEL),
                  pl.BlockSpec((1, 6, D_MODEL), lambda i, j: (i, 0, 0)),
                  full(wts["g1"]), full(wts["w_in"]), full(wts["gq"]), full(wts["w_uq"]),
                  full(wts["gkv"]), full(wts["w_ukv"]),
                  pl.BlockSpec((tm, LANES), lambda i, j: (j, 0)),
                  pl.BlockSpec((tm, LANES), lambda i, j: (j, 0))],
        out_specs=[tok(w) for w, _ in out_widths],
        out_shape=[jax.ShapeDtypeStruct((b, s, w), dt) for w, dt in out_widths],
        compiler_params=pltpu.CompilerParams(dimension_semantics=("parallel", "parallel"),
                                             vmem_limit_bytes=VMEM_LIMIT),
        name="in_proj",
    )(x, mod, wts["g1"], wts["w_in"], wts["gq"], wts["w_uq"], wts["gkv"], wts["w_ukv"], cos_t, sin_t)


def _kvup_kernel(ckv_ref, kr_ref, wukv_ref, place_ref, kmla_ref, vmla_ref):
    kv = _dot(ckv_ref[0].astype(BF16), wukv_ref[...])
    kr = _dot(kr_ref[0].astype(BF16), place_ref[...])
    kmla_ref[0] = (kv[:, :QP_WIDTH] + kr).astype(BF16)
    vmla_ref[0] = kv[:, QP_WIDTH:].astype(BF16)


def _kv_up(ckv, krope, w_ukv_r, place, tm):
    b, p, _ = ckv.shape
    return pl.pallas_call(
        _kvup_kernel,
        grid=(b, p // tm),
        in_specs=[pl.BlockSpec((1, tm, MLA_KV_LORA), lambda i, j: (i, j, 0)),
                  pl.BlockSpec((1, tm, MLA_ROPE), lambda i, j: (i, j, 0)),
                  pl.BlockSpec(w_ukv_r.shape, lambda i, j: (0, 0)),
                  pl.BlockSpec(place.shape, lambda i, j: (0, 0))],
        out_specs=[pl.BlockSpec((1, tm, QP_WIDTH), lambda i, j: (i, j, 0)),
                   pl.BlockSpec((1, tm, SB_WIDTH), lambda i, j: (i, j, 0))],
        out_shape=[jax.ShapeDtypeStruct((b, p, QP_WIDTH), BF16), jax.ShapeDtypeStruct((b, p, SB_WIDTH), BF16)],
        compiler_params=pltpu.CompilerParams(dimension_semantics=("parallel", "parallel")),
        name="kv_up",
    )(ckv, krope, w_ukv_r, place)


def _tri(n):
    r = lax.broadcasted_iota(jnp.int32, (n, n), 0)
    c = lax.broadcasted_iota(jnp.int32, (n, n), 1)
    return jnp.where(r > c, 1.0, 0.0).astype(BF16)


def _stick_terms(z):
    log_sig = jnp.minimum(z, 0.0) - jnp.log2(1.0 + jnp.exp2(-jnp.abs(z)))
    return log_sig, log_sig - z


def _split_bf16(x):
    hi = x.astype(BF16)
    return hi, (x - hi.astype(F32)).astype(BF16)


def _finish_mixer(o_mla, o_sb, gates_ref, x_ref, mod_ref, wmu_ref, wsu_ref, wo_ref, g2_ref, x1_ref, h2_ref):
    u_mla = _dot(o_mla.astype(BF16), wmu_ref[...])
    u_sb = _dot(o_sb.astype(BF16), wsu_ref[...])
    gates = gates_ref[0]
    merged = gates[:, :D_MODEL].astype(F32) * u_mla + gates[:, D_MODEL:].astype(F32) * u_sb
    mix = _dot(merged.astype(BF16), wo_ref[...])
    mod = mod_ref[0]
    x1 = x_ref[0] + mod[2:3] * mix
    x1_ref[0] = x1
    h2 = _rms(x1) * g2_ref[...] * (1.0 + mod[4:5]) + mod[3:4]
    h2_ref[0] = _pack_halves(h2[:, :HALF], h2[:, HALF:])


def _prompt_attn_kernel(qp_ref, kmla_ref, vmla_ref, sbq_ref, sbk_ref, sbv_ref,
                        gates_ref, x_ref, mod_ref, wmu_ref, wsu_ref, wo_ref, g2_ref, x1_ref, h2_ref,
                        m_ref, l_ref, acc_ref, c_ref, sacc_ref, *, tq):
    i = pl.program_id(1)
    lane = lax.broadcasted_iota(jnp.int32, (1, LANES), 1)
    half_masks = (lane < MLA_V, lane >= MLA_V)
    row = lax.broadcasted_iota(jnp.int32, (tq, tq), 0)
    col = lax.broadcasted_iota(jnp.int32, (tq, tq), 1)
    chunk_mask = (col // CHUNK) <= (row // CHUNK)
    causal_mask = col < row
    piece = min(tq, MXU_TILE)
    tri_m = _tri(piece)
    st_diag = pl.multiple_of(i * tq, tq)

    def load(ref, start, c0):
        return ref[0, pl.ds(start, tq), c0:c0 + LANES]

    def mla_block(sub, q_h, k, v, mask):
        s = _dot_nt(q_h, k)
        if mask is not None:
            s = jnp.where(mask, s, -jnp.inf)
            m_new = jnp.max(s, axis=-1, keepdims=True)
            p = jnp.exp2(s - m_new)
            l_ref[sub] = jnp.sum(p, axis=-1, keepdims=True)
            acc_ref[sub] = _dot(p.astype(BF16), v)
        else:
            m_old = m_ref[sub]
            m_new = jnp.maximum(m_old, jnp.max(s, axis=-1, keepdims=True))
            p = jnp.exp2(s - m_new)
            alpha = jnp.exp2(m_old - m_new)
            l_ref[sub] = alpha * l_ref[sub] + jnp.sum(p, axis=-1, keepdims=True)
            acc_ref[sub] = alpha * acc_ref[sub] + _dot(p.astype(BF16), v)
        m_ref[sub] = m_new

    def sb_block(sub, q_h, k, v, mask):
        log_sig, log_keep = _stick_terms(_dot_nt(q_h, k))
        if mask is not None:
            log_keep = jnp.where(mask, log_keep, 0.0)
        summand = log_keep.astype(BF16)
        pieces = []
        total = None
        for b in reversed(range(tq // piece)):
            sl = slice(b * piece, (b + 1) * piece)
            inner = _dot(summand[:, sl], tri_m)
            piece_total = jnp.sum(log_keep[:, sl], axis=-1, keepdims=True)
            pieces.append(inner if total is None else inner + total)
            total = piece_total if total is None else total + piece_total
        after = jnp.concatenate(pieces[::-1], axis=1)
        if mask is not None:
            a = jnp.where(mask, jnp.exp2(log_sig + after), 0.0)
            sacc_ref[sub] = _dot(a.astype(BF16), v)
            c_ref[sub] = total
        else:
            c_old = c_ref[sub]
            a = jnp.exp2(log_sig + after + c_old)
            sacc_ref[sub] = sacc_ref[sub] + _dot(a.astype(BF16), v)
            c_ref[sub] = c_old + total

    o_mla = []
    o_sb = []
    for first_pair in range(0, MLA_HEADS // 2, PAIRS_PER_LOOP):
        pairs = range(first_pair, first_pair + PAIRS_PER_LOOP)
        q_m = {h: qp_ref[0, :, h * HEAD_PAD:(h + 1) * HEAD_PAD] for p in pairs for h in (2 * p, 2 * p + 1)}
        q_s = {2 * p + sub: jnp.where(half_masks[sub], sbq_ref[0, :, p * LANES:(p + 1) * LANES], 0)
               for p in pairs for sub in range(2)}

        def group_blocks(mla_start, sb_start, masks, pairs=pairs, q_m=q_m, q_s=q_s):
            for p in pairs:
                vcol = p * LANES
                v_m = load(vmla_ref, mla_start, vcol)
                k_s = load(sbk_ref, sb_start, vcol)
                v_s = load(sbv_ref, sb_start, vcol)
                for sub in range(2):
                    head = 2 * p + sub
                    slot = head - 2 * pairs[0]
                    keep = half_masks[sub]
                    mla_block(slot, q_m[head], load(kmla_ref, mla_start, head * HEAD_PAD),
                              jnp.where(keep, v_m, 0), masks[0])
                    sb_block(slot, q_s[head], k_s, jnp.where(keep, v_s, 0), masks[1])

        group_blocks(st_diag, st_diag, (chunk_mask, causal_mask))

        def step(t, _, group_blocks=group_blocks):
            group_blocks(pl.multiple_of(t * tq, tq), pl.multiple_of((i - 1 - t) * tq, tq), (None, None))
            return 0
        lax.fori_loop(0, i, step, 0)
        for p in range(PAIRS_PER_LOOP):
            o_mla.append(acc_ref[2 * p] / l_ref[2 * p] + acc_ref[2 * p + 1] / l_ref[2 * p + 1])
            o_sb.append(sacc_ref[2 * p] + sacc_ref[2 * p + 1])

    _finish_mixer(jnp.concatenate(o_mla, axis=1), jnp.concatenate(o_sb, axis=1),
                  gates_ref, x_ref, mod_ref, wmu_ref, wsu_ref, wo_ref, g2_ref, x1_ref, h2_ref)


def _decode_attn_kernel(qp_ref, kmla_ref, vmla_ref, sbq_ref, sbk_ref, sbv_ref, pkmla_ref, pvmla_ref, psbk_ref,
                        psbv_ref, gates_ref, x_ref, mod_ref, wmu_ref, wsu_ref, wo_ref, g2_ref, x1_ref, h2_ref,
                        *, tq, past_len, past_blk):
    n_past = past_len // past_blk
    lane = lax.broadcasted_iota(jnp.int32, (1, LANES), 1)
    half_masks = (lane < MLA_V, lane >= MLA_V)
    row = lax.broadcasted_iota(jnp.int32, (tq, tq), 0)
    col = lax.broadcasted_iota(jnp.int32, (tq, tq), 1)
    chunk_mask = ((past_len + col) // CHUNK) <= ((past_len + row) // CHUNK)
    causal_mask = col < row
    tri_new = _tri(tq)
    tri_past = _tri(past_blk)

    o_mla = []
    o_sb = []
    for pair in range(MLA_HEADS // 2):
        vcol = pair * LANES
        v_new = vmla_ref[0, :, vcol:vcol + LANES]
        v_past = pvmla_ref[0, :, vcol:vcol + LANES]
        sk_new = sbk_ref[0, :, vcol:vcol + LANES]
        sv_new = sbv_ref[0, :, vcol:vcol + LANES]
        sk_past = psbk_ref[0, :, vcol:vcol + LANES].astype(BF16)
        sv_past = psbv_ref[0, :, vcol:vcol + LANES].astype(BF16)
        mla_pair = None
        sb_pair = None
        for sub in range(2):
            keep = half_masks[sub]
            kcol = (2 * pair + sub) * HEAD_PAD

            q_h = qp_ref[0, :, kcol:kcol + HEAD_PAD]
            s_past = _dot_nt(q_h, pkmla_ref[0, :, kcol:kcol + HEAD_PAD])
            s_new = jnp.where(chunk_mask, _dot_nt(q_h, kmla_ref[0, :, kcol:kcol + HEAD_PAD]), -jnp.inf)
            m = jnp.maximum(jnp.max(s_past, axis=-1, keepdims=True), jnp.max(s_new, axis=-1, keepdims=True))
            p_past = jnp.exp2(s_past - m)
            p_new = jnp.exp2(s_new - m)
            denom = jnp.sum(p_past, axis=-1, keepdims=True) + jnp.sum(p_new, axis=-1, keepdims=True)
            o = (_dot(p_past.astype(BF16), jnp.where(keep, v_past, 0))
                 + _dot(p_new.astype(BF16), jnp.where(keep, v_new, 0))) / denom
            mla_pair = o if mla_pair is None else mla_pair + o

            q_s = jnp.where(keep, sbq_ref[0, :, vcol:vcol + LANES], 0)
            ls_new, lk_new = _stick_terms(_dot_nt(q_s, sk_new))
            lk_new = jnp.where(causal_mask, lk_new, 0.0)
            hi, lo = _split_bf16(lk_new)
            a_new = jnp.where(causal_mask, jnp.exp2(ls_new + _dot(hi, tri_new) + _dot(lo, tri_new)), 0.0)
            acc = _dot(a_new.astype(BF16), jnp.where(keep, sv_new, 0))
            later = jnp.sum(lk_new, axis=-1, keepdims=True)

            ls_past, lk_past = _stick_terms(_dot_nt(q_s, sk_past))
            hi, lo = _split_bf16(lk_past)
            blocks = lambda a: [a[:, b * past_blk:(b + 1) * past_blk] for b in range(n_past)]
            stacked = jnp.concatenate(blocks(hi) + blocks(lo), axis=0)
            within = _dot(stacked, tri_past)
            after = []
            for b in reversed(range(n_past)):
                after.append(within[b * tq:(b + 1) * tq] + within[(n_past + b) * tq:(n_past + b + 1) * tq] + later)
                later = later + jnp.sum(lk_past[:, b * past_blk:(b + 1) * past_blk], axis=-1, keepdims=True)
            a_past = jnp.exp2(ls_past + jnp.concatenate(after[::-1], axis=1))
            acc = acc + _dot(a_past.astype(BF16), jnp.where(keep, sv_past, 0))
            sb_pair = acc if sb_pair is None else sb_pair + acc
        o_mla.append(mla_pair)
        o_sb.append(sb_pair)

    _finish_mixer(jnp.concatenate(o_mla, axis=1), jnp.concatenate(o_sb, axis=1),
                  gates_ref, x_ref, mod_ref, wmu_ref, wsu_ref, wo_ref, g2_ref, x1_ref, h2_ref)


def _mixer_call(kernel_fn, name, tq, args_kv, gates, x, mod, wts, scratch, batch0=0, n_batch=None):
    s = x.shape[1]
    b = x.shape[0] if n_batch is None else n_batch
    tok = lambda w: pl.BlockSpec((1, tq, w), lambda i, j: (i + batch0, j, 0))
    seq = lambda a: pl.BlockSpec((1,) + a.shape[1:], lambda i, j: (i + batch0, 0, 0))
    full = lambda a: pl.BlockSpec(a.shape, lambda i, j: (0,) * a.ndim)
    out = lambda w: pl.BlockSpec((1, tq, w), lambda i, j: (i, j, 0))
    args = list(args_kv) + [gates, x, mod, wts["w_mla_up"], wts["w_sb_up"], wts["w_out"], wts["g2"]]
    specs = [seq(a) if whole else tok(a.shape[-1]) for a, whole in
             zip(args_kv, (False, True, True, False) + (True,) * (len(args_kv) - 4))]
    specs += [tok(2 * D_MODEL), tok(D_MODEL), pl.BlockSpec((1, 6, D_MODEL), lambda i, j: (i + batch0, 0, 0)),
              full(wts["w_mla_up"]), full(wts["w_sb_up"]), full(wts["w_out"]), full(wts["g2"])]
    return pl.pallas_call(
        kernel_fn,
        grid=(b, s // tq),
        in_specs=specs,
        out_specs=[out(D_MODEL), out(HALF)],
        out_shape=[jax.ShapeDtypeStruct((b, s, D_MODEL), F32), jax.ShapeDtypeStruct((b, s, HALF), jnp.int32)],
        scratch_shapes=scratch,
        compiler_params=pltpu.CompilerParams(dimension_semantics=("parallel", "arbitrary"),
                                             vmem_limit_bytes=VMEM_LIMIT),
        name=name,
    )(*args)


def _prompt_attention(qp, kmla, vmla, sbq, sbk16, sbv16, gates, x, mod, wts, tq, batch0, n_batch):
    col = lambda: pltpu.VMEM((2 * PAIRS_PER_LOOP, tq, 1), F32)
    wide = lambda: pltpu.VMEM((2 * PAIRS_PER_LOOP, tq, LANES), F32)
    return _mixer_call(functools.partial(_prompt_attn_kernel, tq=tq), "attention", tq,
                       (qp, kmla, vmla, sbq, sbk16, sbv16), gates, x, mod, wts,
                       [col(), col(), wide(), col(), wide()], batch0, n_batch)


def _decode_attention(qp, kmla, vmla, sbq, sbk16, sbv16, past, gates, x, mod, wts, past_blk):
    tq = x.shape[1]
    past_len = past[0].shape[1]
    kern = functools.partial(_decode_attn_kernel, tq=tq, past_len=past_len, past_blk=past_blk)
    return _mixer_call(kern, "decode_attention", tq, (qp, kmla, vmla, sbq, sbk16, sbv16) + tuple(past),
                       gates, x, mod, wts, [])


def _route_kernel(h2_ref, wr_ref, br_ref, idx_ref, wt_ref, rank_ref, cnt_ref, seen_ref):
    lo, hi = _unpack_halves(h2_ref[...])
    tm = lo.shape[0]
    logits = _dot_nt(wr_ref[:, :HALF], lo.astype(BF16)) + _dot_nt(wr_ref[:, HALF:], hi.astype(BF16))
    scores = jax.nn.sigmoid(logits)
    sel = scores + br_ref[...]
    neg = -jnp.inf

    grp = sel.reshape(N_GROUPS, GROUP_SIZE, tm)
    within = lax.broadcasted_iota(jnp.int32, grp.shape, 1)
    top1 = jnp.max(grp, axis=1, keepdims=True)
    first = jnp.min(jnp.where(grp == top1, within, GROUP_SIZE), axis=1, keepdims=True)
    top2 = jnp.max(jnp.where(within == first, neg, grp), axis=1, keepdims=True)
    gscore = (top1 + top2).reshape(N_GROUPS, tm)

    gid = lax.broadcasted_iota(jnp.int32, gscore.shape, 0)
    chosen = jnp.zeros(gscore.shape, jnp.bool_)
    for _ in range(TOPK_GROUPS):
        best = jnp.max(gscore, axis=0, keepdims=True)
        pick = jnp.min(jnp.where(gscore == best, gid, N_GROUPS), axis=0, keepdims=True)
        hit = gid == pick
        chosen = jnp.logical_or(chosen, hit)
        gscore = jnp.where(hit, neg, gscore)
    chosen3 = jnp.broadcast_to(chosen.reshape(N_GROUPS, 1, tm), grp.shape)
    cand = jnp.where(chosen3, grp, neg).reshape(N_EXPERTS, tm)

    eid = lax.broadcasted_iota(jnp.int32, cand.shape, 0)
    picks = []
    weights = []
    for _ in range(TOP_K):
        best = jnp.max(cand, axis=0, keepdims=True)
        pick = jnp.min(jnp.where(cand == best, eid, N_EXPERTS), axis=0, keepdims=True)
        hit = eid == pick
        weights.append(jnp.sum(jnp.where(hit, scores, 0.0), axis=0, keepdims=True))
        picks.append(pick)
        cand = jnp.where(hit, neg, cand)
    w = jnp.concatenate(weights, axis=0)
    idx_ref[...] = jnp.concatenate(picks, axis=0)
    wt_ref[...] = w / (jnp.sum(w, axis=0, keepdims=True) + 1e-20) * ROUTED_SCALE

    @pl.when(pl.program_id(0) == 0)
    def _():
        seen_ref[...] = jnp.zeros_like(seen_ref)

    onehot = jnp.zeros(cand.shape, F32)
    for pick in picks:
        onehot = onehot + jnp.where(eid == pick, 1.0, 0.0)
    src = lax.broadcasted_iota(jnp.int32, (tm, tm), 0)
    dst = lax.broadcasted_iota(jnp.int32, (tm, tm), 1)
    earlier = jnp.where(src < dst, 1.0, 0.0).astype(BF16)
    before = _dot(onehot.astype(BF16), earlier) + seen_ref[...]
    rank_ref[...] = jnp.concatenate(
        [jnp.sum(jnp.where(eid == pick, before, 0.0), axis=0, keepdims=True) for pick in picks],
        axis=0).astype(jnp.int32)
    seen = seen_ref[...] + jnp.sum(onehot, axis=1, keepdims=True)
    seen_ref[...] = seen
    cnt_ref[...] = seen.astype(jnp.int32)


def _route(h2p, wr_t, br, tm):
    t = h2p.shape[0]
    kt = lambda: pl.BlockSpec((TOP_K, tm), lambda i: (0, i))
    return pl.pallas_call(
        _route_kernel,
        grid=(t // tm,),
        in_specs=[pl.BlockSpec((tm, HALF), lambda i: (i, 0)),
                  pl.BlockSpec(wr_t.shape, lambda i: (0, 0)),
                  pl.BlockSpec(br.shape, lambda i: (0, 0))],
        out_specs=[kt(), kt(), kt(), pl.BlockSpec((N_EXPERTS, 1), lambda i: (0, 0))],
        out_shape=[jax.ShapeDtypeStruct((TOP_K, t), jnp.int32), jax.ShapeDtypeStruct((TOP_K, t), F32),
                   jax.ShapeDtypeStruct((TOP_K, t), jnp.int32), jax.ShapeDtypeStruct((N_EXPERTS, 1), jnp.int32)],
        scratch_shapes=[pltpu.VMEM((N_EXPERTS, 1), F32)],
        compiler_params=pltpu.CompilerParams(dimension_semantics=("arbitrary",)),
        name="route",
    )(h2p, wr_t, br)


def _position_kernel(idx_ref, rank_ref, start_ref, pos_ref):
    idx = idx_ref[...]
    eid = lax.broadcasted_iota(jnp.int32, (N_EXPERTS, idx.shape[1]), 0)
    start = start_ref[...]
    base = jnp.concatenate(
        [jnp.sum(jnp.where(eid == idx[k:k + 1, :], start, 0.0), axis=0, keepdims=True) for k in range(TOP_K)],
        axis=0)
    pos_ref[...] = base.astype(jnp.int32) + rank_ref[...]


def _positions(idx_kt, rank_kt, pad_start, tm):
    t = idx_kt.shape[1]
    kt = lambda: pl.BlockSpec((TOP_K, tm), lambda i: (0, i))
    return pl.pallas_call(
        _position_kernel,
        grid=(t // tm,),
        in_specs=[kt(), kt(), pl.BlockSpec((N_EXPERTS, 1), lambda i: (0, 0))],
        out_specs=kt(),
        out_shape=jax.ShapeDtypeStruct((TOP_K, t), jnp.int32),
        compiler_params=pltpu.CompilerParams(dimension_semantics=("parallel",)),
        name="positions",
    )(idx_kt, rank_kt, pad_start.astype(F32).reshape(N_EXPERTS, 1))


def _sc_mesh():
    return plsc.VectorSubcoreMesh(core_axis_name="c", subcore_axis_name="s",
                                  num_cores=SC_CORES, num_subcores=SC_SUBCORES)


def _sc_scatter_rows(rows, pos, n_out, chunk):
    t, width = rows.shape
    copies = pos.shape[0] // t
    per_worker = t // SC_WORKERS
    n_chunks = per_worker // chunk
    tail = per_worker - n_chunks * chunk
    assert per_worker * SC_WORKERS == t and tail % 8 == 0
    tail_rows = max(tail, 8)

    @functools.partial(
        pl.kernel, mesh=_sc_mesh(),
        out_type=jax.ShapeDtypeStruct((n_out, width), rows.dtype),
        scratch_types=[pltpu.VMEM((copies, chunk), jnp.int32), pltpu.VMEM((chunk, width), rows.dtype),
                       pltpu.VMEM((copies, tail_rows), jnp.int32), pltpu.VMEM((tail_rows, width), rows.dtype),
                       pltpu.SemaphoreType.DMA, pltpu.SemaphoreType.DMA],
    )
    def scatter(rows_hbm, pos_hbm, out_hbm, idx_v, rows_v, idx_t, rows_t, load_sem, store_sem):
        wid = lax.axis_index("s") * SC_CORES + lax.axis_index("c")
        base = wid * per_worker

        def move(off, n, idx_buf, row_buf):
            loads = [pltpu.async_copy(rows_hbm.at[pl.ds(off, n)], row_buf, load_sem)]
            for k in range(copies):
                src = pos_hbm.at[pl.ds(pl.multiple_of(k * t + off, 8), n)]
                loads.append(pltpu.async_copy(src, idx_buf.at[k], load_sem))
            for cp in loads:
                cp.wait()
            stores = [pltpu.async_copy(row_buf, out_hbm.at[idx_buf.at[k]], store_sem) for k in range(copies)]
            for cp in stores:
                cp.wait()

        @pl.loop(0, n_chunks)
        def _(c):
            move(pl.multiple_of(base + c * chunk, 8), chunk, idx_v, rows_v)

        if tail:
            move(pl.multiple_of(base + n_chunks * chunk, 8), tail, idx_t, rows_t)

    return scatter(rows, pos)


def _sc_gather_rows(table, idx, chunk):
    n_rows = idx.shape[0]
    width = table.shape[1]
    per_worker = n_rows // SC_WORKERS
    n_chunks = per_worker // chunk
    assert per_worker * SC_WORKERS == n_rows and n_chunks * chunk == per_worker
    mesh = _sc_mesh()

    @functools.partial(
        pl.kernel, mesh=mesh,
        out_type=jax.ShapeDtypeStruct((n_rows, width), table.dtype),
        scratch_types=[pltpu.VMEM((chunk,), jnp.int32), pltpu.VMEM((chunk, width), table.dtype),
                       pltpu.SemaphoreType.DMA],
    )
    def gather(table_hbm, idx_hbm, out_hbm, idx_v, rows_v, sem):
        wid = lax.axis_index("s") * SC_CORES + lax.axis_index("c")
        base = wid * per_worker

        @pl.loop(0, n_chunks)
        def _(c):
            off = pl.multiple_of(base + c * chunk, 8)
            pltpu.sync_copy(idx_hbm.at[pl.ds(off, chunk)], idx_v)
            pltpu.async_copy(table_hbm.at[idx_v], rows_v, sem).wait()
            pltpu.sync_copy(rows_v, out_hbm.at[pl.ds(off, chunk)])

    return gather(table, idx)


def _expert_kernel(be_ref, nv_ref, nu_ref, xs_ref, w1_ref, w3_ref, w2_ref, os_ref, wb1, wb3, wb2):
    i = pl.program_id(0)

    @pl.when(i < nu_ref[0])
    def _():
        @pl.when(jnp.logical_or(i == 0, be_ref[i] != be_ref[jnp.maximum(i - 1, 0)]))
        def _():
            wb1[...] = w1_ref[0].astype(BF16)
            wb3[...] = w3_ref[0].astype(BF16)
            wb2[...] = w2_ref[0].astype(BF16)

        packed = xs_ref[...]
        live = lax.broadcasted_iota(jnp.int32, packed.shape, 0) < nv_ref[i]
        lo, hi = _unpack_halves(jnp.where(live, packed, 0))
        lo = lo.astype(BF16)
        hi = hi.astype(BF16)
        a = _dot(lo, wb1[:HALF, :]) + _dot(hi, wb1[HALF:, :])
        b = _dot(lo, wb3[:HALF, :]) + _dot(hi, wb3[HALF:, :])
        o = _dot((_silu(a) * b).astype(BF16), wb2[...])
        os_ref[...] = _pack_halves(o[:, :HALF], o[:, HALF:])


def _experts(xs, blk_exp, blk_valid, n_used, w1, w3, w2, blk):
    rows = xs.shape[0]
    nb = rows // blk
    last = lambda i, nu: jnp.minimum(i, nu[0] - 1)
    grid_spec = pltpu.PrefetchScalarGridSpec(
        num_scalar_prefetch=3,
        grid=(nb,),
        in_specs=[pl.BlockSpec((blk, HALF), lambda i, be, nv, nu: (last(i, nu), 0)),
                  pl.BlockSpec((1, D_MODEL, EXPERT_DIM), lambda i, be, nv, nu: (be[last(i, nu)], 0, 0)),
                  pl.BlockSpec((1, D_MODEL, EXPERT_DIM), lambda i, be, nv, nu: (be[last(i, nu)], 0, 0)),
                  pl.BlockSpec((1, EXPERT_DIM, D_MODEL), lambda i, be, nv, nu: (be[last(i, nu)], 0, 0))],
        out_specs=pl.BlockSpec((blk, HALF), lambda i, be, nv, nu: (last(i, nu), 0)),
        scratch_shapes=[pltpu.VMEM((D_MODEL, EXPERT_DIM), BF16), pltpu.VMEM((D_MODEL, EXPERT_DIM), BF16),
                        pltpu.VMEM((EXPERT_DIM, D_MODEL), BF16)],
    )
    return pl.pallas_call(
        _expert_kernel,
        grid_spec=grid_spec,
        out_shape=jax.ShapeDtypeStruct((rows, HALF), jnp.int32),
        compiler_params=pltpu.CompilerParams(dimension_semantics=("arbitrary",), vmem_limit_bytes=VMEM_LIMIT),
        name="experts",
    )(blk_exp, blk_valid, n_used, xs, w1, w3, w2)


def _combine_kernel(g_ref, wt_ref, h2_ref, x1_ref, mod_ref, sw1_ref, sw3_ref, sw2_ref, gf_ref, y_ref):
    wt = wt_ref[...]
    lo_acc = None
    for k in range(TOP_K):
        lo, hi = _unpack_halves(g_ref[k])
        wk = wt[:, k:k + 1]
        lo_acc = wk * lo if lo_acc is None else lo_acc + wk * lo
        hi_acc = wk * hi if k == 0 else hi_acc + wk * hi
    routed = jnp.concatenate([lo_acc, hi_acc], axis=1)
    lo, hi = _unpack_halves(h2_ref[...])
    lo = lo.astype(BF16)
    hi = hi.astype(BF16)
    a = _dot(lo, sw1_ref[:HALF, :]) + _dot(hi, sw1_ref[HALF:, :])
    b = _dot(lo, sw3_ref[:HALF, :]) + _dot(hi, sw3_ref[HALF:, :])
    shared = _dot((_silu(a) * b).astype(BF16), sw2_ref[...])
    mod = mod_ref[0]
    x2 = x1_ref[...] + mod[5:6] * (routed + shared)
    y_ref[...] = _rms(x2) * gf_ref[...]


def _combine_into_kernel(g_ref, wt_ref, h2_ref, x1_ref, mod_ref, sw1_ref, sw3_ref, sw2_ref, gf_ref, prev_ref, y_ref):
    del prev_ref
    _combine_kernel(g_ref, wt_ref, h2_ref, x1_ref, mod_ref, sw1_ref, sw3_ref, sw2_ref, gf_ref, y_ref)


def _combine(gathered, wt, h2p, x1, mod, wts, tm, src_tile0, mod_batch0, tiles_per_batch, out_rows, out_tile0,
             y_prev=None):
    t = x1.shape[0]
    full = lambda a: pl.BlockSpec(a.shape, lambda i: (0,) * a.ndim)
    shifted = lambda w: pl.BlockSpec((tm, w), lambda i: (i + src_tile0, 0))
    args = [gathered, wt, h2p, x1, mod, wts["sw1"], wts["sw3"], wts["sw2"], wts["gf"]]
    specs = [pl.BlockSpec((TOP_K, tm, HALF), lambda i: (0, i + src_tile0, 0)), shifted(TOP_K), shifted(HALF),
             pl.BlockSpec((tm, D_MODEL), lambda i: (i, 0)),
             pl.BlockSpec((1, 6, D_MODEL), lambda i: (i // tiles_per_batch + mod_batch0, 0, 0)),
             full(wts["sw1"]), full(wts["sw3"]), full(wts["sw2"]), full(wts["gf"])]
    aliases = {}
    body = _combine_kernel
    if y_prev is not None:
        args.append(y_prev)
        specs.append(pl.BlockSpec(memory_space=pl.ANY))
        aliases = {len(args) - 1: 0}
        body = _combine_into_kernel
    return pl.pallas_call(
        body,
        grid=(t // tm,),
        in_specs=specs,
        out_specs=pl.BlockSpec((tm, D_MODEL), lambda i: (i + out_tile0, 0)),
        out_shape=jax.ShapeDtypeStruct((out_rows, D_MODEL), F32),
        input_output_aliases=aliases,
        compiler_params=pltpu.CompilerParams(dimension_semantics=("parallel",), vmem_limit_bytes=VMEM_LIMIT),
        name="combine",
    )(*args)


def _rope_tables(pos):
    half = MLA_ROPE // 2
    inv_freq = ROPE_THETA ** (-jnp.arange(half, dtype=F32) / half)
    ang = pos.astype(F32)[:, None] * inv_freq
    cos, sin = jnp.cos(ang), jnp.sin(ang)
    n = pos.shape[0]
    ones = jnp.ones((n, MLA_NOPE), F32)
    z_nope = jnp.zeros((n, MLA_NOPE), F32)
    z_pad = jnp.zeros((n, HEAD_PAD - MLA_NOPE - MLA_ROPE), F32)
    return (jnp.concatenate([ones, cos, cos, z_pad], axis=1),
            jnp.concatenate([z_nope, -sin, sin, z_pad], axis=1))


def _prep_weights(g_norm1, w_in, g_q_lat, w_uq, g_kv_lat, w_ukv, w_mla_up, w_sb_up, w_out, g_norm2,
                  w_router, b_router, shared_w1, shared_w3, shared_w2, g_final):
    w = w_in[0]
    o = 0
    parts = {}
    for name, width in (("qlat", MLA_Q_LORA), ("kv", MLA_KV_LORA), ("kr", MLA_ROPE), ("sbq", SB_WIDTH),
                        ("sbk", SB_WIDTH), ("sbv", SB_WIDTH), ("gm", D_MODEL), ("gs", D_MODEL)):
        parts[name] = w[:, o:o + width]
        o += width
    kr = parts["kr"]
    z_nope = jnp.zeros((D_MODEL, MLA_NOPE), F32)
    z_pad = jnp.zeros((D_MODEL, HEAD_PAD - MLA_NOPE - MLA_ROPE), F32)
    kr_seg = jnp.concatenate([z_nope, kr, z_pad], axis=1)
    w_in_r = jnp.concatenate([parts["qlat"], parts["kv"], parts["sbq"] * (SB_DIM ** -0.5 * LOG2_E), parts["sbk"],
                              parts["sbv"], parts["gm"], parts["gs"], kr_seg], axis=1).astype(BF16)

    scale = (MLA_NOPE + MLA_ROPE) ** -0.5 * LOG2_E
    uq = w_uq[0].reshape(MLA_Q_LORA, MLA_HEADS, MLA_NOPE + MLA_ROPE) * scale
    nope, rope = uq[..., :MLA_NOPE], uq[..., MLA_NOPE:]
    zq_pad = jnp.zeros((MLA_Q_LORA, MLA_HEADS, HEAD_PAD - MLA_NOPE - MLA_ROPE), F32)
    w_uq_r = jnp.concatenate([nope, rope, zq_pad], axis=-1).reshape(MLA_Q_LORA, QP_WIDTH).astype(BF16)

    ukv = w_ukv[0].reshape(MLA_KV_LORA, MLA_HEADS, MLA_NOPE + MLA_V)
    k_nope, v = ukv[..., :MLA_NOPE], ukv[..., MLA_NOPE:]
    k_pad = jnp.concatenate([k_nope, jnp.zeros((MLA_KV_LORA, MLA_HEADS, HEAD_PAD - MLA_NOPE), F32)], axis=-1)
    w_ukv_r = jnp.concatenate([k_pad.reshape(MLA_KV_LORA, QP_WIDTH),
                               v.reshape(MLA_KV_LORA, MLA_HEADS * MLA_V)], axis=1).astype(BF16)

    eye = jnp.eye(MLA_ROPE, dtype=F32)
    place_head = jnp.concatenate([jnp.zeros((MLA_ROPE, MLA_NOPE), F32), eye,
                                  jnp.zeros((MLA_ROPE, HEAD_PAD - MLA_NOPE - MLA_ROPE), F32)], axis=1)
    place = jnp.tile(place_head, (1, MLA_HEADS)).astype(BF16)

    return dict(
        g1=g_norm1[0].reshape(1, D_MODEL), w_in=w_in_r, gq=g_q_lat[0].reshape(1, MLA_Q_LORA), w_uq=w_uq_r,
        gkv=g_kv_lat[0].reshape(1, MLA_KV_LORA), w_ukv=w_ukv_r, place=place,
        w_mla_up=w_mla_up[0].astype(BF16), w_sb_up=w_sb_up[0].astype(BF16), w_out=w_out[0].astype(BF16),
        g2=g_norm2[0].reshape(1, D_MODEL), wr_t=w_router[0].T.astype(BF16),
        br=b_router[0].reshape(N_EXPERTS, 1),
        sw1=shared_w1[0].astype(BF16), sw3=shared_w3[0].astype(BF16), sw2=shared_w2[0].astype(BF16),
        gf=g_final.reshape(1, D_MODEL))


def _block_plan(counts, nb, blk):
    padded = (counts + blk - 1) // blk * blk
    pad_end = jnp.cumsum(padded)
    pad_start = pad_end - padded
    first_row = jnp.arange(nb, dtype=jnp.int32) * blk
    blk_exp = jnp.minimum(jnp.sum((pad_end[None, :] <= first_row[:, None]).astype(jnp.int32), axis=1),
                          N_EXPERTS - 1)
    own = blk_exp[:, None] == jnp.arange(N_EXPERTS, dtype=jnp.int32)[None, :]
    seg_end = jnp.sum(jnp.where(own, (pad_start + counts)[None, :], 0), axis=1)
    blk_valid = jnp.clip(seg_end - first_row, 0, blk).astype(jnp.int32)
    n_used = (pad_end[-1:] // blk).astype(jnp.int32)
    return pad_start, blk_exp.astype(jnp.int32), blk_valid, n_used


def _moe_rows(h2p, idx_kt, rank_kt, counts, w1, w3, w2, scatter_rows, gather_rows, blk, tm):
    t = h2p.shape[0]
    nb = -(-t * TOP_K // blk) + N_EXPERTS
    pad_start, blk_exp, blk_valid, n_used = _block_plan(counts.reshape(N_EXPERTS), nb, blk)
    pos = _positions(idx_kt, rank_kt, pad_start, tm).reshape(TOP_K * t)
    xs = scatter_rows(h2p, pos, nb * blk)
    os_ = _experts(xs, blk_exp, blk_valid, n_used, w1, w3, w2, blk)
    return gather_rows(os_, pos).reshape(TOP_K, t, HALF)


def _forward(x_prompt, x_sample, cache_mla_ckv, cache_mla_krope, cache_sb_k, cache_sb_v, c_prompt, c_sample,
             w_ada, b_ada, moe_w1, moe_w3, moe_w2, wts, scatter_rows, gather_rows, token_block, attn_block,
             route_block, moe_block, wide_block):
    bp, sp, _ = x_prompt.shape
    bs, ss, _ = x_sample.shape
    past_len = cache_mla_ckv.shape[2]

    mod = _ada(jnp.concatenate([c_prompt, c_sample], axis=0), w_ada[0], b_ada[0]).reshape(bp + bs, 6, D_MODEL)
    mod_p, mod_s = mod[:bp], mod[bp:]

    cos_p, sin_p = _rope_tables(jnp.arange(sp))
    (qp, kmla, vmla, sbq, sbk16, sbv16, gates, ckv_p, krope_p, sbk_p, sbv_p) = _in_proj(
        x_prompt, mod_p, wts, cos_p, sin_p, wide_block)
    half_b = bp // 2
    mixed = [_prompt_attention(qp, kmla, vmla, sbq, sbk16, sbv16, gates, x_prompt, mod_p, wts, attn_block,
                               b0, half_b) for b0 in (0, half_b)]

    cos_s, sin_s = _rope_tables(past_len + jnp.arange(ss))
    (qs, kmla_s, vmla_s, sbq_s, sbk16_s, sbv16_s, gates_s, ckv_s, krope_s, sbk_s, sbv_s) = _in_proj(
        x_sample, mod_s, wts, cos_s, sin_s, ss)
    pkmla, pvmla = _kv_up(cache_mla_ckv[0], cache_mla_krope[0], wts["w_ukv"], wts["place"], token_block)
    past = (pkmla, pvmla, cache_sb_k[0].reshape(bs, past_len, SB_WIDTH), cache_sb_v[0].reshape(bs, past_len, SB_WIDTH))
    x1_s, h2_s = _decode_attention(qs, kmla_s, vmla_s, sbq_s, sbk16_s, sbv16_s, past, gates_s, x_sample, mod_s,
                                   wts, token_block)

    tp, ts = bp * sp, bs * ss
    th = half_b * sp

    def moe(h2_rows):
        idx_kt, wt_kt, rank_kt, counts = _route(h2_rows, wts["wr_t"], wts["br"], route_block)
        gathered = _moe_rows(h2_rows, idx_kt, rank_kt, counts, moe_w1[0], moe_w3[0], moe_w2[0],
                             scatter_rows, gather_rows, moe_block, route_block)
        return gathered, wt_kt.T

    tiles_per_batch = sp // wide_block
    (x1_a, h2_a), (x1_b, h2_b) = mixed
    h2_a = h2_a.reshape(th, HALF)
    g_a, wt_a = moe(h2_a)
    y_p = _combine(g_a, wt_a, h2_a, x1_a.reshape(th, D_MODEL), mod_p, wts, wide_block,
                   0, 0, tiles_per_batch, tp, 0)
    h2_b = jnp.concatenate([h2_b.reshape(th, HALF), h2_s.reshape(ts, HALF)], axis=0)
    g_b, wt_b = moe(h2_b)
    y_p = _combine(g_b, wt_b, h2_b, x1_b.reshape(th, D_MODEL), mod_p, wts, wide_block,
                   0, half_b, tiles_per_batch, tp, th // wide_block, y_prev=y_p)
    y_s = _combine(g_b, wt_b, h2_b, x1_s.reshape(ts, D_MODEL), mod_s, wts, ss, th // ss, 0, 1, ts, 0)

    heads = lambda a, b_, s_: a.reshape(1, b_, s_, SB_HEADS, SB_DIM)
    return (y_p.reshape(bp, sp, D_MODEL), y_s.reshape(bs, ss, D_MODEL),
            ckv_p[None], krope_p[None], heads(sbk_p, bp, sp), heads(sbv_p, bp, sp),
            ckv_s[None], krope_s[None], heads(sbk_s, bs, ss), heads(sbv_s, bs, ss))


def kernel(x_prompt, x_sample, cache_mla_ckv, cache_mla_krope, cache_sb_k, cache_sb_v, c_prompt, c_sample, w_ada, b_ada, g_norm1, w_in, g_q_lat, w_uq, g_kv_lat, w_ukv, w_mla_up, w_sb_up, w_out, g_norm2, w_router, b_router, moe_w1, moe_w3, moe_w2, shared_w1, shared_w3, shared_w2, g_final):
    wts = _prep_weights(g_norm1, w_in, g_q_lat, w_uq, g_kv_lat, w_ukv, w_mla_up, w_sb_up, w_out, g_norm2,
                        w_router, b_router, shared_w1, shared_w3, shared_w2, g_final)
    scatter_rows = functools.partial(_sc_scatter_rows, chunk=SC_CHUNK)
    gather_rows = functools.partial(_sc_gather_rows, chunk=SC_CHUNK)
    return _forward(x_prompt, x_sample, cache_mla_ckv, cache_mla_krope, cache_sb_k, cache_sb_v, c_prompt, c_sample,
                    w_ada, b_ada, moe_w1, moe_w3, moe_w2, wts, scatter_rows, gather_rows, TOKEN_BLOCK, ATTN_BLOCK,
                    TOKEN_BLOCK, MOE_BLOCK, WIDE_BLOCK)
```

```python
import functools

import jax
import jax.numpy as jnp
from jax import lax
from jax.experimental import pallas as pl
from jax.experimental.pallas import tpu as pltpu
from jax.experimental.pallas import tpu_sc as plsc

F32 = jnp.float32
BF16 = jnp.bfloat16

D_MODEL = 1024
NORM_EPS = 1e-6
CHUNK = 64
MLA_HEADS = 8
MLA_NOPE = 64
MLA_ROPE = 32
MLA_V = 64
MLA_Q_LORA = 384
MLA_KV_LORA = 256
ROPE_THETA = 10000.0
SB_HEADS = 8
SB_DIM = 64
SB_WIDTH = SB_HEADS * SB_DIM
N_EXPERTS = 256
TOP_K = 8
N_GROUPS = 8
TOPK_GROUPS = 4
GROUP_SIZE = N_EXPERTS // N_GROUPS
EXPERT_DIM = 256
ROUTED_SCALE = 2.5
LOG2_E = 1.4426950408889634

LANES = 128
MXU_TILE = 256
SC_CORES = 2
SC_SUBCORES = 16
SC_WORKERS = SC_CORES * SC_SUBCORES
VMEM_LIMIT = 60 * 1024 * 1024

HEAD_PAD = LANES
DENOM_LANE = (MLA_V, 0)
QP_WIDTH = MLA_HEADS * HEAD_PAD
HALF = D_MODEL // 2

C_QLAT = 0
C_KV = C_QLAT + MLA_Q_LORA
C_SBQ = C_KV + MLA_KV_LORA
C_SBK = C_SBQ + SB_WIDTH
C_SBV = C_SBK + SB_WIDTH
C_GATE = C_SBV + SB_WIDTH
C_KR = C_GATE + 2 * D_MODEL
C_END = C_KR + LANES

MOE_BLOCK = 512
WIDE_BLOCK = 512
TOKEN_BLOCK = 256
ATTN_BLOCK = 512
PAIRS_PER_LOOP = 1
SC_CHUNK = 64


def _rms(x):
    return x * lax.rsqrt(jnp.mean(x * x, axis=-1, keepdims=True) + NORM_EPS)


def _silu(x):
    return x * jax.nn.sigmoid(x)


def _pack_halves(lo, hi):
    lo_bits = lax.bitcast_convert_type(lo.astype(BF16).astype(F32), jnp.uint32) >> 16
    hi_bits = lax.bitcast_convert_type(hi.astype(BF16).astype(F32), jnp.uint32) & jnp.uint32(0xFFFF0000)
    return lax.bitcast_convert_type(lo_bits | hi_bits, jnp.int32)


def _unpack_halves(p):
    u = lax.bitcast_convert_type(p, jnp.uint32)
    lo = lax.bitcast_convert_type(u << 16, F32)
    hi = lax.bitcast_convert_type(u & jnp.uint32(0xFFFF0000), F32)
    return lo, hi


def _dot(a, b):
    return jnp.dot(a, b, preferred_element_type=F32)


def _dot_nt(a, b):
    return lax.dot_general(a, b, (((1,), (1,)), ((), ())), preferred_element_type=F32)


def _ada_kernel(c_ref, w_ref, b_ref, o_ref):
    c = c_ref[...]
    o_ref[...] = _dot(_silu(c).astype(BF16), w_ref[...].astype(BF16)) + b_ref[...]


def _ada(c, w_ada, b_ada):
    n = c.shape[0]
    width = w_ada.shape[1]
    return pl.pallas_call(
        _ada_kernel,
        grid=(width // D_MODEL,),
        in_specs=[pl.BlockSpec((n, D_MODEL), lambda j: (0, 0)),
                  pl.BlockSpec((D_MODEL, D_MODEL), lambda j: (0, j)),
                  pl.BlockSpec((1, D_MODEL), lambda j: (0, j))],
        out_specs=pl.BlockSpec((n, D_MODEL), lambda j: (0, j)),
        out_shape=jax.ShapeDtypeStruct((n, width), F32),
        name="ada",
    )(c, w_ada, b_ada.reshape(1, width))


def _in_kernel(x_ref, mod_ref, g1_ref, win_ref, gq_ref, wuq_ref, gkv_ref, wukv_ref, cos_ref, sin_ref,
               qp_ref, kmla_ref, vmla_ref, sbq_ref, sbk16_ref, sbv16_ref, gates_ref,
               ckv_ref, krope_ref, sbk_ref, sbv_ref):
    x = x_ref[0]
    mod = mod_ref[0]
    h = _rms(x) * g1_ref[...] * (1.0 + mod[1:2]) + mod[0:1]
    hb = h.astype(BF16)

    def seg(a, b):
        return _dot(hb, win_ref[:, a:b])

    cos = cos_ref[...]
    sin = sin_ref[...]
    lane = lax.broadcasted_iota(jnp.int32, (1, LANES), 1)
    half = MLA_ROPE // 2

    def rotate(blk):
        other = jnp.where(lane < MLA_NOPE + half, pltpu.roll(blk, LANES - half, 1), pltpu.roll(blk, half, 1))
        return blk * cos + other * sin

    qn = (_rms(seg(C_QLAT, C_KV)) * gq_ref[...]).astype(BF16)
    q = _dot(qn, wuq_ref[...])
    qp_ref[0] = jnp.concatenate([rotate(q[:, h * HEAD_PAD:(h + 1) * HEAD_PAD]) for h in range(MLA_HEADS)],
                                axis=1).astype(BF16)

    ckv = _rms(seg(C_KV, C_SBQ)) * gkv_ref[...]
    ckv_ref[0] = ckv
    kv = _dot(ckv.astype(BF16), wukv_ref[...])
    krp = rotate(seg(C_KR, C_END))
    krope_ref[0] = krp[:, MLA_NOPE:MLA_NOPE + MLA_ROPE]
    kmla_ref[0] = (kv[:, :QP_WIDTH] + jnp.tile(krp, (1, MLA_HEADS))).astype(BF16)
    vmla_ref[0] = kv[:, QP_WIDTH:].astype(BF16)

    sbq_ref[0] = seg(C_SBQ, C_SBK).astype(BF16)
    sbk = seg(C_SBK, C_SBV)
    sbk_ref[0] = sbk
    sbk16_ref[0] = sbk.astype(BF16)
    sbv = seg(C_SBV, C_GATE)
    sbv_ref[0] = sbv
    sbv16_ref[0] = sbv.astype(BF16)
    gates_ref[0] = jax.nn.sigmoid(seg(C_GATE, C_KR)).astype(BF16)


def _in_proj(x, mod, wts, cos_t, sin_t, tm):
    b, s, _ = x.shape
    ns = s // tm
    tok = lambda w: pl.BlockSpec((1, tm, w), lambda i, j: (i, j, 0))
    full = lambda a: pl.BlockSpec(a.shape, lambda i, j: (0,) * a.ndim)
    out_widths = [(QP_WIDTH, BF16), (QP_WIDTH, BF16), (SB_WIDTH, BF16), (SB_WIDTH, BF16), (SB_WIDTH, BF16),
                  (SB_WIDTH, BF16), (2 * D_MODEL, BF16), (MLA_KV_LORA, F32), (MLA_ROPE, F32),
                  (SB_WIDTH, F32), (SB_WIDTH, F32)]
    return pl.pallas_call(
        _in_kernel,
        grid=(b, ns),
        in_specs=[tok(D_MODEL),
                  pl.BlockSpec((1, 6, D_MODEL), lambda i, j: (i, 0, 0)),
                  full(wts["g1"]), full(wts["w_in"]), full(wts["gq"]), full(wts["w_uq"]),
                  full(wts["gkv"]), full(wts["w_ukv"]),
                  pl.BlockSpec((tm, LANES), lambda i, j: (j, 0)),
                  pl.BlockSpec((tm, LANES), lambda i, j: (j, 0))],
        out_specs=[tok(w) for w, _ in out_widths],
        out_shape=[jax.ShapeDtypeStruct((b, s, w), dt) for w, dt in out_widths],
        compiler_params=pltpu.CompilerParams(dimension_semantics=("parallel", "parallel"),
                                             vmem_limit_bytes=VMEM_LIMIT),
        name="in_proj",
    )(x, mod, wts["g1"], wts["w_in"], wts["gq"], wts["w_uq"], wts["gkv"], wts["w_ukv"], cos_t, sin_t)


def _kvup_kernel(ckv_ref, kr_ref, wukv_ref, place_ref, kmla_ref, vmla_ref):
    kv = _dot(ckv_ref[0].astype(BF16), wukv_ref[...])
    kr = _dot(kr_ref[0].astype(BF16), place_ref[...])
    kmla_ref[0] = (kv[:, :QP_WIDTH] + kr).astype(BF16)
    vmla_ref[0] = kv[:, QP_WIDTH:].astype(BF16)


def _kv_up(ckv, krope, w_ukv_r, place, tm):
    b, p, _ = ckv.shape
    return pl.pallas_call(
        _kvup_kernel,
        grid=(b, p // tm),
        in_specs=[pl.BlockSpec((1, tm, MLA_KV_LORA), lambda i, j: (i, j, 0)),
                  pl.BlockSpec((1, tm, MLA_ROPE), lambda i, j: (i, j, 0)),
                  pl.BlockSpec(w_ukv_r.shape, lambda i, j: (0, 0)),
                  pl.BlockSpec(place.shape, lambda i, j: (0, 0))],
        out_specs=[pl.BlockSpec((1, tm, QP_WIDTH), lambda i, j: (i, j, 0)),
                   pl.BlockSpec((1, tm, SB_WIDTH), lambda i, j: (i, j, 0))],
        out_shape=[jax.ShapeDtypeStruct((b, p, QP_WIDTH), BF16), jax.ShapeDtypeStruct((b, p, SB_WIDTH), BF16)],
        compiler_params=pltpu.CompilerParams(dimension_semantics=("parallel", "parallel")),
        name="kv_up",
    )(ckv, krope, w_ukv_r, place)


def _tri(n):
    r = lax.broadcasted_iota(jnp.int32, (n, n), 0)
    c = lax.broadcasted_iota(jnp.int32, (n, n), 1)
    return jnp.where(r > c, 1.0, 0.0).astype(BF16)


def _stick_terms(z):
    log_sig = jnp.minimum(z, 0.0) - jnp.log2(1.0 + jnp.exp2(-jnp.abs(z)))
    return log_sig, log_sig - z


def _split_bf16(x):
    hi = x.astype(BF16)
    return hi, (x - hi.astype(F32)).astype(BF16)


def _finish_mixer(o_mla, o_sb, gates_ref, x_ref, mod_ref, wmu_ref, wsu_ref, wo_ref, g2_ref, x1_ref, h2_ref):
    u_mla = _dot(o_mla.astype(BF16), wmu_ref[...])
    u_sb = _dot(o_sb.astype(BF16), wsu_ref[...])
    gates = gates_ref[0]
    merged = gates[:, :D_MODEL].astype(F32) * u_mla + gates[:, D_MODEL:].astype(F32) * u_sb
    mix = _dot(merged.astype(BF16), wo_ref[...])
    mod = mod_ref[0]
    x1 = x_ref[0] + mod[2:3] * mix
    x1_ref[0] = x1
    h2 = _rms(x1) * g2_ref[...] * (1.0 + mod[4:5]) + mod[3:4]
    h2_ref[0] = _pack_halves(h2[:, :HALF], h2[:, HALF:])


def _prompt_attn_kernel(qp_ref, kmla_ref, vmla_ref, sbq_ref, sbk_ref, sbv_ref,
                        gates_ref, x_ref, mod_ref, wmu_ref, wsu_ref, wo_ref, g2_ref, x1_ref, h2_ref,
                        m_ref, acc_ref, c_ref, sacc_ref, *, tq):
    i = pl.program_id(1)
    lane = lax.broadcasted_iota(jnp.int32, (1, LANES), 1)
    half_masks = (lane < MLA_V, lane >= MLA_V)
    unit_lane = [jnp.where(lane == DENOM_LANE[sub], 1.0, 0.0).astype(BF16) for sub in range(2)]
    row = lax.broadcasted_iota(jnp.int32, (tq, tq), 0)
    col = lax.broadcasted_iota(jnp.int32, (tq, tq), 1)
    chunk_mask = (col // CHUNK) <= (row // CHUNK)
    causal_mask = col < row
    piece = min(tq, MXU_TILE)
    tri_m = _tri(piece)
    st_diag = pl.multiple_of(i * tq, tq)

    def load(ref, start, c0):
        return ref[0, pl.ds(start, tq), c0:c0 + LANES]

    def mla_block(sub, q_h, k, v, mask):
        s = _dot_nt(q_h, k)
        if mask is not None:
            s = jnp.where(mask, s, -jnp.inf)
            m_new = jnp.max(s, axis=-1, keepdims=True)
            acc_ref[sub] = _dot(jnp.exp2(s - m_new).astype(BF16), v)
        else:
            m_old = m_ref[sub]
            m_new = jnp.maximum(m_old, jnp.max(s, axis=-1, keepdims=True))
            acc_ref[sub] = jnp.exp2(m_old - m_new) * acc_ref[sub] + _dot(jnp.exp2(s - m_new).astype(BF16), v)
        m_ref[sub] = m_new

    def sb_block(sub, q_h, k, v, mask):
        log_sig, log_keep = _stick_terms(_dot_nt(q_h, k))
        if mask is not None:
            log_keep = jnp.where(mask, log_keep, 0.0)
        summand = log_keep.astype(BF16)
        pieces = []
        total = None
        for b in reversed(range(tq // piece)):
            sl = slice(b * piece, (b + 1) * piece)
            inner = _dot(summand[:, sl], tri_m)
            piece_total = jnp.sum(log_keep[:, sl], axis=-1, keepdims=True)
            pieces.append(inner if total is None else inner + total)
            total = piece_total if total is None else total + piece_total
        after = jnp.concatenate(pieces[::-1], axis=1)
        if mask is not None:
            a = jnp.where(mask, jnp.exp2(log_sig + after), 0.0)
            sacc_ref[sub] = _dot(a.astype(BF16), v)
            c_ref[sub] = total
        else:
            c_old = c_ref[sub]
            a = jnp.exp2(log_sig + after + c_old)
            sacc_ref[sub] = sacc_ref[sub] + _dot(a.astype(BF16), v)
            c_ref[sub] = c_old + total

    o_mla = []
    o_sb = []
    for first_pair in range(0, MLA_HEADS // 2, PAIRS_PER_LOOP):
        pairs = range(first_pair, first_pair + PAIRS_PER_LOOP)
        q_m = {h: qp_ref[0, :, h * HEAD_PAD:(h + 1) * HEAD_PAD] for p in pairs for h in (2 * p, 2 * p + 1)}
        q_s = {2 * p + sub: jnp.where(half_masks[sub], sbq_ref[0, :, p * LANES:(p + 1) * LANES], 0)
               for p in pairs for sub in range(2)}

        def group_blocks(mla_start, sb_start, masks, pairs=pairs, q_m=q_m, q_s=q_s):
            for p in pairs:
                vcol = p * LANES
                v_m = load(vmla_ref, mla_start, vcol)
                k_s = load(sbk_ref, sb_start, vcol)
                v_s = load(sbv_ref, sb_start, vcol)
                for sub in range(2):
                    head = 2 * p + sub
                    slot = head - 2 * pairs[0]
                    keep = half_masks[sub]
                    mla_block(slot, q_m[head], load(kmla_ref, mla_start, head * HEAD_PAD),
                              jnp.where(keep, v_m, unit_lane[sub]), masks[0])
                    sb_block(slot, q_s[head], k_s, jnp.where(keep, v_s, 0), masks[1])

        group_blocks(st_diag, st_diag, (chunk_mask, causal_mask))

        def step(t, _, group_blocks=group_blocks):
            group_blocks(pl.multiple_of(t * tq, tq), pl.multiple_of((i - 1 - t) * tq, tq), (None, None))
            return 0
        lax.fori_loop(0, i, step, 0)
        for p in range(PAIRS_PER_LOOP):
            heads_out = []
            for sub in range(2):
                acc = acc_ref[2 * p + sub]
                denom = acc[:, DENOM_LANE[sub]:DENOM_LANE[sub] + 1]
                heads_out.append(jnp.where(half_masks[sub], acc, 0.0) / denom)
            o_mla.append(heads_out[0] + heads_out[1])
            o_sb.append(sacc_ref[2 * p] + sacc_ref[2 * p + 1])

    _finish_mixer(jnp.concatenate(o_mla, axis=1), jnp.concatenate(o_sb, axis=1),
                  gates_ref, x_ref, mod_ref, wmu_ref, wsu_ref, wo_ref, g2_ref, x1_ref, h2_ref)


def _decode_attn_kernel(qp_ref, kmla_ref, vmla_ref, sbq_ref, sbk_ref, sbv_ref, pkmla_ref, pvmla_ref, psbk_ref,
                        psbv_ref, gates_ref, x_ref, mod_ref, wmu_ref, wsu_ref, wo_ref, g2_ref, x1_ref, h2_ref,
                        *, tq, past_len, past_blk):
    n_past = past_len // past_blk
    lane = lax.broadcasted_iota(jnp.int32, (1, LANES), 1)
    half_masks = (lane < MLA_V, lane >= MLA_V)
    row = lax.broadcasted_iota(jnp.int32, (tq, tq), 0)
    col = lax.broadcasted_iota(jnp.int32, (tq, tq), 1)
    chunk_mask = ((past_len + col) // CHUNK) <= ((past_len + row) // CHUNK)
    causal_mask = col < row
    tri_new = _tri(tq)
    tri_past = _tri(past_blk)

    o_mla = []
    o_sb = []
    for pair in range(MLA_HEADS // 2):
        vcol = pair * LANES
        v_new = vmla_ref[0, :, vcol:vcol + LANES]
        v_past = pvmla_ref[0, :, vcol:vcol + LANES]
        sk_new = sbk_ref[0, :, vcol:vcol + LANES]
        sv_new = sbv_ref[0, :, vcol:vcol + LANES]
        sk_past = psbk_ref[0, :, vcol:vcol + LANES].astype(BF16)
        sv_past = psbv_ref[0, :, vcol:vcol + LANES].astype(BF16)
        mla_pair = None
        sb_pair = None
        for sub in range(2):
            keep = half_masks[sub]
            kcol = (2 * pair + sub) * HEAD_PAD

            q_h = qp_ref[0, :, kcol:kcol + HEAD_PAD]
            s_past = _dot_nt(q_h, pkmla_ref[0, :, kcol:kcol + HEAD_PAD])
            s_new = jnp.where(chunk_mask, _dot_nt(q_h, kmla_ref[0, :, kcol:kcol + HEAD_PAD]), -jnp.inf)
            m = jnp.maximum(jnp.max(s_past, axis=-1, keepdims=True), jnp.max(s_new, axis=-1, keepdims=True))
            p_past = jnp.exp2(s_past - m)
            p_new = jnp.exp2(s_new - m)
            denom = jnp.sum(p_past, axis=-1, keepdims=True) + jnp.sum(p_new, axis=-1, keepdims=True)
            o = (_dot(p_past.astype(BF16), jnp.where(keep, v_past, 0))
                 + _dot(p_new.astype(BF16), jnp.where(keep, v_new, 0))) / denom
            mla_pair = o if mla_pair is None else mla_pair + o

            q_s = jnp.where(keep, sbq_ref[0, :, vcol:vcol + LANES], 0)
            ls_new, lk_new = _stick_terms(_dot_nt(q_s, sk_new))
            lk_new = jnp.where(causal_mask, lk_new, 0.0)
            hi, lo = _split_bf16(lk_new)
            a_new = jnp.where(causal_mask, jnp.exp2(ls_new + _dot(hi, tri_new) + _dot(lo, tri_new)), 0.0)
            acc = _dot(a_new.astype(BF16), jnp.where(keep, sv_new, 0))
            later = jnp.sum(lk_new, axis=-1, keepdims=True)

            ls_past, lk_past = _stick_terms(_dot_nt(q_s, sk_past))
            hi, lo = _split_bf16(lk_past)
            blocks = lambda a: [a[:, b * past_blk:(b + 1) * past_blk] for b in range(n_past)]
            stacked = jnp.concatenate(blocks(hi) + blocks(lo), axis=0)
            within = _dot(stacked, tri_past)
            after = []
            for b in reversed(range(n_past)):
                after.append(within[b * tq:(b + 1) * tq] + within[(n_past + b) * tq:(n_past + b + 1) * tq] + later)
                later = later + jnp.sum(lk_past[:, b * past_blk:(b + 1) * past_blk], axis=-1, keepdims=True)
            a_past = jnp.exp2(ls_past + jnp.concatenate(after[::-1], axis=1))
            acc = acc + _dot(a_past.astype(BF16), jnp.where(keep, sv_past, 0))
            sb_pair = acc if sb_pair is None else sb_pair + acc
        o_mla.append(mla_pair)
        o_sb.append(sb_pair)

    _finish_mixer(jnp.concatenate(o_mla, axis=1), jnp.concatenate(o_sb, axis=1),
                  gates_ref, x_ref, mod_ref, wmu_ref, wsu_ref, wo_ref, g2_ref, x1_ref, h2_ref)


def _mixer_call(kernel_fn, name, tq, args_kv, gates, x, mod, wts, scratch, batch0=0, n_batch=None):
    s = x.shape[1]
    b = x.shape[0] if n_batch is None else n_batch
    tok = lambda w: pl.BlockSpec((1, tq, w), lambda i, j: (i + batch0, j, 0))
    seq = lambda a: pl.BlockSpec((1,) + a.shape[1:], lambda i, j: (i + batch0, 0, 0))
    full = lambda a: pl.BlockSpec(a.shape, lambda i, j: (0,) * a.ndim)
    out = lambda w: pl.BlockSpec((1, tq, w), lambda i, j: (i, j, 0))
    args = list(args_kv) + [gates, x, mod, wts["w_mla_up"], wts["w_sb_up"], wts["w_out"], wts["g2"]]
    specs = [seq(a) if whole else tok(a.shape[-1]) for a, whole in
             zip(args_kv, (False, True, True, False) + (True,) * (len(args_kv) - 4))]
    specs += [tok(2 * D_MODEL), tok(D_MODEL), pl.BlockSpec((1, 6, D_MODEL), lambda i, j: (i + batch0, 0, 0)),
              full(wts["w_mla_up"]), full(wts["w_sb_up"]), full(wts["w_out"]), full(wts["g2"])]
    return pl.pallas_call(
        kernel_fn,
        grid=(b, s // tq),
        in_specs=specs,
        out_specs=[out(D_MODEL), out(HALF)],
        out_shape=[jax.ShapeDtypeStruct((b, s, D_MODEL), F32), jax.ShapeDtypeStruct((b, s, HALF), jnp.int32)],
        scratch_shapes=scratch,
        compiler_params=pltpu.CompilerParams(dimension_semantics=("parallel", "arbitrary"),
                                             vmem_limit_bytes=VMEM_LIMIT),
        name=name,
    )(*args)


def _prompt_attention(qp, kmla, vmla, sbq, sbk16, sbv16, gates, x, mod, wts, tq, batch0, n_batch):
    col = lambda: pltpu.VMEM((2 * PAIRS_PER_LOOP, tq, 1), F32)
    wide = lambda: pltpu.VMEM((2 * PAIRS_PER_LOOP, tq, LANES), F32)
    return _mixer_call(functools.partial(_prompt_attn_kernel, tq=tq), "attention", tq,
                       (qp, kmla, vmla, sbq, sbk16, sbv16), gates, x, mod, wts,
                       [col(), wide(), col(), wide()], batch0, n_batch)


def _decode_attention(qp, kmla, vmla, sbq, sbk16, sbv16, past, gates, x, mod, wts, past_blk):
    tq = x.shape[1]
    past_len = past[0].shape[1]
    kern = functools.partial(_decode_attn_kernel, tq=tq, past_len=past_len, past_blk=past_blk)
    return _mixer_call(kern, "decode_attention", tq, (qp, kmla, vmla, sbq, sbk16, sbv16) + tuple(past),
                       gates, x, mod, wts, [])


def _route_kernel(h2_ref, wr_ref, br_ref, idx_ref, wt_ref, rank_ref, cnt_ref, seen_ref):
    lo, hi = _unpack_halves(h2_ref[...])
    tm = lo.shape[0]
    logits = _dot_nt(wr_ref[:, :HALF], lo.astype(BF16)) + _dot_nt(wr_ref[:, HALF:], hi.astype(BF16))
    scores = jax.nn.sigmoid(logits)
    sel = scores + br_ref[...]
    neg = -jnp.inf

    grp = sel.reshape(N_GROUPS, GROUP_SIZE, tm)
    within = lax.broadcasted_iota(jnp.int32, grp.shape, 1)
    top1 = jnp.max(grp, axis=1, keepdims=True)
    first = jnp.min(jnp.where(grp == top1, within, GROUP_SIZE), axis=1, keepdims=True)
    top2 = jnp.max(jnp.where(within == first, neg, grp), axis=1, keepdims=True)
    gscore = (top1 + top2).reshape(N_GROUPS, tm)

    gid = lax.broadcasted_iota(jnp.int32, gscore.shape, 0)
    chosen = jnp.zeros(gscore.shape, jnp.bool_)
    for _ in range(TOPK_GROUPS):
        best = jnp.max(gscore, axis=0, keepdims=True)
        pick = jnp.min(jnp.where(gscore == best, gid, N_GROUPS), axis=0, keepdims=True)
        hit = gid == pick
        chosen = jnp.logical_or(chosen, hit)
        gscore = jnp.where(hit, neg, gscore)
    chosen3 = jnp.broadcast_to(chosen.reshape(N_GROUPS, 1, tm), grp.shape)
    cand = jnp.where(chosen3, grp, neg).reshape(N_EXPERTS, tm)
    outside = jnp.where(cand == neg, 1.0, 0.0)

    eid = lax.broadcasted_iota(jnp.int32, cand.shape, 0)
    picks = []
    weights = []
    for _ in range(TOP_K):
        best = jnp.max(cand, axis=0, keepdims=True)
        pick = jnp.min(jnp.where(cand == best, eid, N_EXPERTS), axis=0, keepdims=True)
        hit = eid == pick
        weights.append(jnp.sum(jnp.where(hit, scores, 0.0), axis=0, keepdims=True))
        picks.append(pick)
        cand = jnp.where(hit, neg, cand)
    w = jnp.concatenate(weights, axis=0)
    idx_ref[...] = jnp.concatenate(picks, axis=0)
    wt_ref[...] = w / (jnp.sum(w, axis=0, keepdims=True) + 1e-20) * ROUTED_SCALE

    @pl.when(pl.program_id(0) == 0)
    def _():
        seen_ref[...] = jnp.zeros_like(seen_ref)

    onehot = jnp.where(cand == neg, 1.0, 0.0) - outside
    src = lax.broadcasted_iota(jnp.int32, (tm, tm), 0)
    dst = lax.broadcasted_iota(jnp.int32, (tm, tm), 1)
    earlier = jnp.where(src < dst, 1.0, 0.0).astype(BF16)
    before = _dot(onehot.astype(BF16), earlier) + seen_ref[...]
    rank_ref[...] = jnp.concatenate(
        [jnp.sum(jnp.where(eid == pick, before, 0.0), axis=0, keepdims=True) for pick in picks],
        axis=0).astype(jnp.int32)
    seen = seen_ref[...] + jnp.sum(onehot, axis=1, keepdims=True)
    seen_ref[...] = seen
    cnt_ref[...] = seen.astype(jnp.int32)


def _route(h2p, wr_t, br, tm):
    t = h2p.shape[0]
    kt = lambda: pl.BlockSpec((TOP_K, tm), lambda i: (0, i))
    return pl.pallas_call(
        _route_kernel,
        grid=(t // tm,),
        in_specs=[pl.BlockSpec((tm, HALF), lambda i: (i, 0)),
                  pl.BlockSpec(wr_t.shape, lambda i: (0, 0)),
                  pl.BlockSpec(br.shape, lambda i: (0, 0))],
        out_specs=[kt(), kt(), kt(), pl.BlockSpec((N_EXPERTS, 1), lambda i: (0, 0))],
        out_shape=[jax.ShapeDtypeStruct((TOP_K, t), jnp.int32), jax.ShapeDtypeStruct((TOP_K, t), F32),
                   jax.ShapeDtypeStruct((TOP_K, t), jnp.int32), jax.ShapeDtypeStruct((N_EXPERTS, 1), jnp.int32)],
        scratch_shapes=[pltpu.VMEM((N_EXPERTS, 1), F32)],
        compiler_params=pltpu.CompilerParams(dimension_semantics=("arbitrary",)),
        name="route",
    )(h2p, wr_t, br)


def _position_kernel(idx_ref, rank_ref, start_ref, pos_ref):
    idx = idx_ref[...]
    eid = lax.broadcasted_iota(jnp.int32, (N_EXPERTS, idx.shape[1]), 0)
    start = start_ref[...]
    base = jnp.concatenate(
        [jnp.sum(jnp.where(eid == idx[k:k + 1, :], start, 0.0), axis=0, keepdims=True) for k in range(TOP_K)],
        axis=0)
    pos_ref[...] = base.astype(jnp.int32) + rank_ref[...]


def _positions(idx_kt, rank_kt, pad_start, tm):
    t = idx_kt.shape[1]
    kt = lambda: pl.BlockSpec((TOP_K, tm), lambda i: (0, i))
    return pl.pallas_call(
        _position_kernel,
        grid=(t // tm,),
        in_specs=[kt(), kt(), pl.BlockSpec((N_EXPERTS, 1), lambda i: (0, 0))],
        out_specs=kt(),
        out_shape=jax.ShapeDtypeStruct((TOP_K, t), jnp.int32),
        compiler_params=pltpu.CompilerParams(dimension_semantics=("parallel",)),
        name="positions",
    )(idx_kt, rank_kt, pad_start.astype(F32).reshape(N_EXPERTS, 1))


def _sc_mesh():
    return plsc.VectorSubcoreMesh(core_axis_name="c", subcore_axis_name="s",
                                  num_cores=SC_CORES, num_subcores=SC_SUBCORES)


def _sc_scatter_rows(rows, pos, n_out, chunk):
    t, width = rows.shape
    copies = pos.shape[0] // t
    per_worker = t // SC_WORKERS
    n_chunks = per_worker // chunk
    tail = per_worker - n_chunks * chunk
    assert per_worker * SC_WORKERS == t and tail % 8 == 0
    tail_rows = max(tail, 8)

    @functools.partial(
        pl.kernel, mesh=_sc_mesh(),
        out_type=jax.ShapeDtypeStruct((n_out, width), rows.dtype),
        scratch_types=[pltpu.VMEM((copies, chunk), jnp.int32), pltpu.VMEM((chunk, width), rows.dtype),
                       pltpu.VMEM((copies, tail_rows), jnp.int32), pltpu.VMEM((tail_rows, width), rows.dtype),
                       pltpu.SemaphoreType.DMA, pltpu.SemaphoreType.DMA],
    )
    def scatter(rows_hbm, pos_hbm, out_hbm, idx_v, rows_v, idx_t, rows_t, load_sem, store_sem):
        wid = lax.axis_index("s") * SC_CORES + lax.axis_index("c")
        base = wid * per_worker

        def move(off, n, idx_buf, row_buf):
            loads = [pltpu.async_copy(rows_hbm.at[pl.ds(off, n)], row_buf, load_sem)]
            for k in range(copies):
                src = pos_hbm.at[pl.ds(pl.multiple_of(k * t + off, 8), n)]
                loads.append(pltpu.async_copy(src, idx_buf.at[k], load_sem))
            for cp in loads:
                cp.wait()
            stores = [pltpu.async_copy(row_buf, out_hbm.at[idx_buf.at[k]], store_sem) for k in range(copies)]
            for cp in stores:
                cp.wait()

        @pl.loop(0, n_chunks)
        def _(c):
            move(pl.multiple_of(base + c * chunk, 8), chunk, idx_v, rows_v)

        if tail:
            move(pl.multiple_of(base + n_chunks * chunk, 8), tail, idx_t, rows_t)

    return scatter(rows, pos)


def _sc_gather_rows(table, idx, chunk):
    n_rows = idx.shape[0]
    width = table.shape[1]
    per_worker = n_rows // SC_WORKERS
    n_chunks = per_worker // chunk
    assert per_worker * SC_WORKERS == n_rows and n_chunks * chunk == per_worker
    mesh = _sc_mesh()

    @functools.partial(
        pl.kernel, mesh=mesh,
        out_type=jax.ShapeDtypeStruct((n_rows, width), table.dtype),
        scratch_types=[pltpu.VMEM((chunk,), jnp.int32), pltpu.VMEM((chunk, width), table.dtype),
                       pltpu.SemaphoreType.DMA],
    )
    def gather(table_hbm, idx_hbm, out_hbm, idx_v, rows_v, sem):
        wid = lax.axis_index("s") * SC_CORES + lax.axis_index("c")
        base = wid * per_worker

        @pl.loop(0, n_chunks)
        def _(c):
            off = pl.multiple_of(base + c * chunk, 8)
            pltpu.sync_copy(idx_hbm.at[pl.ds(off, chunk)], idx_v)
            pltpu.async_copy(table_hbm.at[idx_v], rows_v, sem).wait()
            pltpu.sync_copy(rows_v, out_hbm.at[pl.ds(off, chunk)])

    return gather(table, idx)


def _expert_kernel(be_ref, nv_ref, nu_ref, xs_ref, w1_ref, w3_ref, w2_ref, os_ref, wb1, wb3, wb2):
    i = pl.program_id(0)

    @pl.when(i < nu_ref[0])
    def _():
        @pl.when(jnp.logical_or(i == 0, be_ref[i] != be_ref[jnp.maximum(i - 1, 0)]))
        def _():
            wb1[...] = w1_ref[0].astype(BF16)
            wb3[...] = w3_ref[0].astype(BF16)
            wb2[...] = w2_ref[0].astype(BF16)

        packed = xs_ref[...]
        live = lax.broadcasted_iota(jnp.int32, packed.shape, 0) < nv_ref[i]
        lo, hi = _unpack_halves(jnp.where(live, packed, 0))
        lo = lo.astype(BF16)
        hi = hi.astype(BF16)
        a = _dot(lo, wb1[:HALF, :]) + _dot(hi, wb1[HALF:, :])
        b = _dot(lo, wb3[:HALF, :]) + _dot(hi, wb3[HALF:, :])
        o = _dot((_silu(a) * b).astype(BF16), wb2[...])
        os_ref[...] = _pack_halves(o[:, :HALF], o[:, HALF:])


def _experts(xs, blk_exp, blk_valid, n_used, w1, w3, w2, blk):
    rows = xs.shape[0]
    nb = rows // blk
    last = lambda i, nu: jnp.minimum(i, nu[0] - 1)
    grid_spec = pltpu.PrefetchScalarGridSpec(
        num_scalar_prefetch=3,
        grid=(nb,),
        in_specs=[pl.BlockSpec((blk, HALF), lambda i, be, nv, nu: (last(i, nu), 0)),
                  pl.BlockSpec((1, D_MODEL, EXPERT_DIM), lambda i, be, nv, nu: (be[last(i, nu)], 0, 0)),
                  pl.BlockSpec((1, D_MODEL, EXPERT_DIM), lambda i, be, nv, nu: (be[last(i, nu)], 0, 0)),
                  pl.BlockSpec((1, EXPERT_DIM, D_MODEL), lambda i, be, nv, nu: (be[last(i, nu)], 0, 0))],
        out_specs=pl.BlockSpec((blk, HALF), lambda i, be, nv, nu: (last(i, nu), 0)),
        scratch_shapes=[pltpu.VMEM((D_MODEL, EXPERT_DIM), BF16), pltpu.VMEM((D_MODEL, EXPERT_DIM), BF16),
                        pltpu.VMEM((EXPERT_DIM, D_MODEL), BF16)],
    )
    return pl.pallas_call(
        _expert_kernel,
        grid_spec=grid_spec,
        out_shape=jax.ShapeDtypeStruct((rows, HALF), jnp.int32),
        compiler_params=pltpu.CompilerParams(dimension_semantics=("arbitrary",), vmem_limit_bytes=VMEM_LIMIT),
        name="experts",
    )(blk_exp, blk_valid, n_used, xs, w1, w3, w2)


def _combine_kernel(g_ref, wt_ref, h2_ref, x1_ref, mod_ref, sw1_ref, sw3_ref, sw2_ref, gf_ref, y_ref):
    wt = wt_ref[...]
    lo_acc = None
    for k in range(TOP_K):
        lo, hi = _unpack_halves(g_ref[k])
        wk = wt[:, k:k + 1]
        lo_acc = wk * lo if lo_acc is None else lo_acc + wk * lo
        hi_acc = wk * hi if k == 0 else hi_acc + wk * hi
    routed = jnp.concatenate([lo_acc, hi_acc], axis=1)
    lo, hi = _unpack_halves(h2_ref[...])
    lo = lo.astype(BF16)
    hi = hi.astype(BF16)
    a = _dot(lo, sw1_ref[:HALF, :]) + _dot(hi, sw1_ref[HALF:, :])
    b = _dot(lo, sw3_ref[:HALF, :]) + _dot(hi, sw3_ref[HALF:, :])
    shared = _dot((_silu(a) * b).astype(BF16), sw2_ref[...])
    mod = mod_ref[0]
    x2 = x1_ref[...] + mod[5:6] * (routed + shared)
    y_ref[...] = _rms(x2) * gf_ref[...]


def _combine_into_kernel(g_ref, wt_ref, h2_ref, x1_ref, mod_ref, sw1_ref, sw3_ref, sw2_ref, gf_ref, prev_ref, y_ref):
    del prev_ref
    _combine_kernel(g_ref, wt_ref, h2_ref, x1_ref, mod_ref, sw1_ref, sw3_ref, sw2_ref, gf_ref, y_ref)


def _combine(gathered, wt, h2p, x1, mod, wts, tm, src_tile0, mod_batch0, tiles_per_batch, out_rows, out_tile0,
             y_prev=None):
    t = x1.shape[0]
    full = lambda a: pl.BlockSpec(a.shape, lambda i: (0,) * a.ndim)
    shifted = lambda w: pl.BlockSpec((tm, w), lambda i: (i + src_tile0, 0))
    args = [gathered, wt, h2p, x1, mod, wts["sw1"], wts["sw3"], wts["sw2"], wts["gf"]]
    specs = [pl.BlockSpec((TOP_K, tm, HALF), lambda i: (0, i + src_tile0, 0)), shifted(TOP_K), shifted(HALF),
             pl.BlockSpec((tm, D_MODEL), lambda i: (i, 0)),
             pl.BlockSpec((1, 6, D_MODEL), lambda i: (i // tiles_per_batch + mod_batch0, 0, 0)),
             full(wts["sw1"]), full(wts["sw3"]), full(wts["sw2"]), full(wts["gf"])]
    aliases = {}
    body = _combine_kernel
    if y_prev is not None:
        args.append(y_prev)
        specs.append(pl.BlockSpec(memory_space=pl.ANY))
        aliases = {len(args) - 1: 0}
        body = _combine_into_kernel
    return pl.pallas_call(
        body,
        grid=(t // tm,),
        in_specs=specs,
        out_specs=pl.BlockSpec((tm, D_MODEL), lambda i: (i + out_tile0, 0)),
        out_shape=jax.ShapeDtypeStruct((out_rows, D_MODEL), F32),
        input_output_aliases=aliases,
        compiler_params=pltpu.CompilerParams(dimension_semantics=("parallel",), vmem_limit_bytes=VMEM_LIMIT),
        name="combine",
    )(*args)


def _rope_tables(pos):
    half = MLA_ROPE // 2
    inv_freq = ROPE_THETA ** (-jnp.arange(half, dtype=F32) / half)
    ang = pos.astype(F32)[:, None] * inv_freq
    cos, sin = jnp.cos(ang), jnp.sin(ang)
    n = pos.shape[0]
    ones = jnp.ones((n, MLA_NOPE), F32)
    z_nope = jnp.zeros((n, MLA_NOPE), F32)
    z_pad = jnp.zeros((n, HEAD_PAD - MLA_NOPE - MLA_ROPE), F32)
    return (jnp.concatenate([ones, cos, cos, z_pad], axis=1),
            jnp.concatenate([z_nope, -sin, sin, z_pad], axis=1))


def _prep_weights(g_norm1, w_in, g_q_lat, w_uq, g_kv_lat, w_ukv, w_mla_up, w_sb_up, w_out, g_norm2,
                  w_router, b_router, shared_w1, shared_w3, shared_w2, g_final):
    w = w_in[0]
    o = 0
    parts = {}
    for name, width in (("qlat", MLA_Q_LORA), ("kv", MLA_KV_LORA), ("kr", MLA_ROPE), ("sbq", SB_WIDTH),
                        ("sbk", SB_WIDTH), ("sbv", SB_WIDTH), ("gm", D_MODEL), ("gs", D_MODEL)):
        parts[name] = w[:, o:o + width]
        o += width
    kr = parts["kr"]
    z_nope = jnp.zeros((D_MODEL, MLA_NOPE), F32)
    z_pad = jnp.zeros((D_MODEL, HEAD_PAD - MLA_NOPE - MLA_ROPE), F32)
    kr_seg = jnp.concatenate([z_nope, kr, z_pad], axis=1)
    w_in_r = jnp.concatenate([parts["qlat"], parts["kv"], parts["sbq"] * (SB_DIM ** -0.5 * LOG2_E), parts["sbk"],
                              parts["sbv"], parts["gm"], parts["gs"], kr_seg], axis=1).astype(BF16)

    scale = (MLA_NOPE + MLA_ROPE) ** -0.5 * LOG2_E
    uq = w_uq[0].reshape(MLA_Q_LORA, MLA_HEADS, MLA_NOPE + MLA_ROPE) * scale
    nope, rope = uq[..., :MLA_NOPE], uq[..., MLA_NOPE:]
    zq_pad = jnp.zeros((MLA_Q_LORA, MLA_HEADS, HEAD_PAD - MLA_NOPE - MLA_ROPE), F32)
    w_uq_r = jnp.concatenate([nope, rope, zq_pad], axis=-1).reshape(MLA_Q_LORA, QP_WIDTH).astype(BF16)

    ukv = w_ukv[0].reshape(MLA_KV_LORA, MLA_HEADS, MLA_NOPE + MLA_V)
    k_nope, v = ukv[..., :MLA_NOPE], ukv[..., MLA_NOPE:]
    k_pad = jnp.concatenate([k_nope, jnp.zeros((MLA_KV_LORA, MLA_HEADS, HEAD_PAD - MLA_NOPE), F32)], axis=-1)
    w_ukv_r = jnp.concatenate([k_pad.reshape(MLA_KV_LORA, QP_WIDTH),
                               v.reshape(MLA_KV_LORA, MLA_HEADS * MLA_V)], axis=1).astype(BF16)

    eye = jnp.eye(MLA_ROPE, dtype=F32)
    place_head = jnp.concatenate([jnp.zeros((MLA_ROPE, MLA_NOPE), F32), eye,
                                  jnp.zeros((MLA_ROPE, HEAD_PAD - MLA_NOPE - MLA_ROPE), F32)], axis=1)
    place = jnp.tile(place_head, (1, MLA_HEADS)).astype(BF16)

    return dict(
        g1=g_norm1[0].reshape(1, D_MODEL), w_in=w_in_r, gq=g_q_lat[0].reshape(1, MLA_Q_LORA), w_uq=w_uq_r,
        gkv=g_kv_lat[0].reshape(1, MLA_KV_LORA), w_ukv=w_ukv_r, place=place,
        w_mla_up=w_mla_up[0].astype(BF16), w_sb_up=w_sb_up[0].astype(BF16), w_out=w_out[0].astype(BF16),
        g2=g_norm2[0].reshape(1, D_MODEL), wr_t=w_router[0].T.astype(BF16),
        br=b_router[0].reshape(N_EXPERTS, 1),
        sw1=shared_w1[0].astype(BF16), sw3=shared_w3[0].astype(BF16), sw2=shared_w2[0].astype(BF16),
        gf=g_final.reshape(1, D_MODEL))


def _block_plan(counts, nb, blk):
    padded = (counts + blk - 1) // blk * blk
    pad_end = jnp.cumsum(padded)
    pad_start = pad_end - padded
    first_row = jnp.arange(nb, dtype=jnp.int32) * blk
    blk_exp = jnp.minimum(jnp.sum((pad_end[None, :] <= first_row[:, None]).astype(jnp.int32), axis=1),
                          N_EXPERTS - 1)
    own = blk_exp[:, None] == jnp.arange(N_EXPERTS, dtype=jnp.int32)[None, :]
    seg_end = jnp.sum(jnp.where(own, (pad_start + counts)[None, :], 0), axis=1)
    blk_valid = jnp.clip(seg_end - first_row, 0, blk).astype(jnp.int32)
    n_used = (pad_end[-1:] // blk).astype(jnp.int32)
    return pad_start, blk_exp.astype(jnp.int32), blk_valid, n_used


def _moe_rows(h2p, idx_kt, rank_kt, counts, w1, w3, w2, scatter_rows, gather_rows, blk, tm):
    t = h2p.shape[0]
    nb = -(-t * TOP_K // blk) + N_EXPERTS
    pad_start, blk_exp, blk_valid, n_used = _block_plan(counts.reshape(N_EXPERTS), nb, blk)
    pos = _positions(idx_kt, rank_kt, pad_start, tm).reshape(TOP_K * t)
    xs = scatter_rows(h2p, pos, nb * blk)
    os_ = _experts(xs, blk_exp, blk_valid, n_used, w1, w3, w2, blk)
    return gather_rows(os_, pos).reshape(TOP_K, t, HALF)


def _forward(x_prompt, x_sample, cache_mla_ckv, cache_mla_krope, cache_sb_k, cache_sb_v, c_prompt, c_sample,
             w_ada, b_ada, moe_w1, moe_w3, moe_w2, wts, scatter_rows, gather_rows, token_block, attn_block,
             route_block, moe_block, wide_block):
    bp, sp, _ = x_prompt.shape
    bs, ss, _ = x_sample.shape
    past_len = cache_mla_ckv.shape[2]

    mod = _ada(jnp.concatenate([c_prompt, c_sample], axis=0), w_ada[0], b_ada[0]).reshape(bp + bs, 6, D_MODEL)
    mod_p, mod_s = mod[:bp], mod[bp:]

    cos_p, sin_p = _rope_tables(jnp.arange(sp))
    (qp, kmla, vmla, sbq, sbk16, sbv16, gates, ckv_p, krope_p, sbk_p, sbv_p) = _in_proj(
        x_prompt, mod_p, wts, cos_p, sin_p, wide_block)
    half_b = bp // 2
    mixed = [_prompt_attention(qp, kmla, vmla, sbq, sbk16, sbv16, gates, x_prompt, mod_p, wts, attn_block,
                               b0, half_b) for b0 in (0, half_b)]

    cos_s, sin_s = _rope_tables(past_len + jnp.arange(ss))
    (qs, kmla_s, vmla_s, sbq_s, sbk16_s, sbv16_s, gates_s, ckv_s, krope_s, sbk_s, sbv_s) = _in_proj(
        x_sample, mod_s, wts, cos_s, sin_s, ss)
    pkmla, pvmla = _kv_up(cache_mla_ckv[0], cache_mla_krope[0], wts["w_ukv"], wts["place"], wide_block)
    past = (pkmla, pvmla, cache_sb_k[0].reshape(bs, past_len, SB_WIDTH), cache_sb_v[0].reshape(bs, past_len, SB_WIDTH))
    x1_s, h2_s = _decode_attention(qs, kmla_s, vmla_s, sbq_s, sbk16_s, sbv16_s, past, gates_s, x_sample, mod_s,
                                   wts, token_block)

    tp, ts = bp * sp, bs * ss
    th = half_b * sp

    def moe(h2_rows):
        idx_kt, wt_kt, rank_kt, counts = _route(h2_rows, wts["wr_t"], wts["br"], route_block)
        gathered = _moe_rows(h2_rows, idx_kt, rank_kt, counts, moe_w1[0], moe_w3[0], moe_w2[0],
                             scatter_rows, gather_rows, moe_block, route_block)
        return gathered, wt_kt.T

    tiles_per_batch = sp // wide_block
    (x1_a, h2_a), (x1_b, h2_b) = mixed
    h2_a = h2_a.reshape(th, HALF)
    g_a, wt_a = moe(h2_a)
    y_p = _combine(g_a, wt_a, h2_a, x1_a.reshape(th, D_MODEL), mod_p, wts, wide_block,
                   0, 0, tiles_per_batch, tp, 0)
    h2_b = jnp.concatenate([h2_b.reshape(th, HALF), h2_s.reshape(ts, HALF)], axis=0)
    g_b, wt_b = moe(h2_b)
    y_p = _combine(g_b, wt_b, h2_b, x1_b.reshape(th, D_MODEL), mod_p, wts, wide_block,
                   0, half_b, tiles_per_batch, tp, th // wide_block, y_prev=y_p)
    y_s = _combine(g_b, wt_b, h2_b, x1_s.reshape(ts, D_MODEL), mod_s, wts, ss, th // ss, 0, 1, ts, 0)

    heads = lambda a, b_, s_: a.reshape(1, b_, s_, SB_HEADS, SB_DIM)
    return (y_p.reshape(bp, sp, D_MODEL), y_s.reshape(bs, ss, D_MODEL),
            ckv_p[None], krope_p[None], heads(sbk_p, bp, sp), heads(sbv_p, bp, sp),
            ckv_s[None], krope_s[None], heads(sbk_s, bs, ss), heads(sbv_s, bs, ss))


def kernel(x_prompt, x_sample, cache_mla_ckv, cache_mla_krope, cache_sb_k, cache_sb_v, c_prompt, c_sample, w_ada, b_ada, g_norm1, w_in, g_q_lat, w_uq, g_kv_lat, w_ukv, w_mla_up, w_sb_up, w_out, g_norm2, w_router, b_router, moe_w1, moe_w3, moe_w2, shared_w1, shared_w3, shared_w2, g_final):
    wts = _prep_weights(g_norm1, w_in, g_q_lat, w_uq, g_kv_lat, w_ukv, w_mla_up, w_sb_up, w_out, g_norm2,
                        w_router, b_router, shared_w1, shared_w3, shared_w2, g_final)
    scatter_rows = functools.partial(_sc_scatter_rows, chunk=SC_CHUNK)
    gather_rows = functools.partial(_sc_gather_rows, chunk=SC_CHUNK)
    return _forward(x_prompt, x_sample, cache_mla_ckv, cache_mla_krope, cache_sb_k, cache_sb_v, c_prompt, c_sample,
                    w_ada, b_ada, moe_w1, moe_w3, moe_w2, wts, scatter_rows, gather_rows, TOKEN_BLOCK, ATTN_BLOCK,
                    TOKEN_BLOCK, MOE_BLOCK, WIDE_BLOCK)
```

```python
import functools

import jax
import jax.numpy as jnp
from jax import lax
from jax.experimental import pallas as pl
from jax.experimental.pallas import tpu as pltpu
from jax.experimental.pallas import tpu_sc as plsc

F32 = jnp.float32
BF16 = jnp.bfloat16

D_MODEL = 1024
NORM_EPS = 1e-6
CHUNK = 64
MLA_HEADS = 8
MLA_NOPE = 64
MLA_ROPE = 32
MLA_V = 64
MLA_Q_LORA = 384
MLA_KV_LORA = 256
ROPE_THETA = 10000.0
SB_HEADS = 8
SB_DIM = 64
SB_WIDTH = SB_HEADS * SB_DIM
N_EXPERTS = 256
TOP_K = 8
N_GROUPS = 8
TOPK_GROUPS = 4
GROUP_SIZE = N_EXPERTS // N_GROUPS
EXPERT_DIM = 256
ROUTED_SCALE = 2.5
LOG2_E = 1.4426950408889634

LANES = 128
MXU_TILE = 256
SC_CORES = 2
SC_SUBCORES = 16
SC_WORKERS = SC_CORES * SC_SUBCORES
VMEM_LIMIT = 60 * 1024 * 1024

HEAD_PAD = LANES
DENOM_LANE = (MLA_V, 0)
QP_WIDTH = MLA_HEADS * HEAD_PAD
HALF = D_MODEL // 2

C_QLAT = 0
C_KV = C_QLAT + MLA_Q_LORA
C_SBQ = C_KV + MLA_KV_LORA
C_SBK = C_SBQ + SB_WIDTH
C_SBV = C_SBK + SB_WIDTH
C_GATE = C_SBV + SB_WIDTH
C_KR = C_GATE + 2 * D_MODEL
C_END = C_KR + LANES

MOE_BLOCK = 512
WIDE_BLOCK = 512
TOKEN_BLOCK = 256
ATTN_BLOCK = 512
PAIRS_PER_LOOP = 1
SC_CHUNK = 64


def _rms(x):
    return x * lax.rsqrt(jnp.mean(x * x, axis=-1, keepdims=True) + NORM_EPS)


def _silu(x):
    return x * jax.nn.sigmoid(x)


def _pack_halves(lo, hi):
    lo_bits = lax.bitcast_convert_type(lo.astype(BF16).astype(F32), jnp.uint32) >> 16
    hi_bits = lax.bitcast_convert_type(hi.astype(BF16).astype(F32), jnp.uint32) & jnp.uint32(0xFFFF0000)
    return lax.bitcast_convert_type(lo_bits | hi_bits, jnp.int32)


def _unpack_halves(p):
    u = lax.bitcast_convert_type(p, jnp.uint32)
    lo = lax.bitcast_convert_type(u << 16, F32)
    hi = lax.bitcast_convert_type(u & jnp.uint32(0xFFFF0000), F32)
    return lo, hi


def _dot(a, b):
    return jnp.dot(a, b, preferred_element_type=F32)


def _dot_nt(a, b):
    return lax.dot_general(a, b, (((1,), (1,)), ((), ())), preferred_element_type=F32)


def _ada_kernel(c_ref, w_ref, b_ref, o_ref):
    c = c_ref[...]
    o_ref[...] = _dot(_silu(c).astype(BF16), w_ref[...].astype(BF16)) + b_ref[...]


def _ada(c, w_ada, b_ada):
    n = c.shape[0]
    width = w_ada.shape[1]
    return pl.pallas_call(
        _ada_kernel,
        grid=(width // D_MODEL,),
        in_specs=[pl.BlockSpec((n, D_MODEL), lambda j: (0, 0)),
                  pl.BlockSpec((D_MODEL, D_MODEL), lambda j: (0, j)),
                  pl.BlockSpec((1, D_MODEL), lambda j: (0, j))],
        out_specs=pl.BlockSpec((n, D_MODEL), lambda j: (0, j)),
        out_shape=jax.ShapeDtypeStruct((n, width), F32),
        name="ada",
    )(c, w_ada, b_ada.reshape(1, width))


def _in_kernel(x_ref, mod_ref, g1_ref, win_ref, gq_ref, wuq_ref, gkv_ref, wukv_ref, cos_ref, sin_ref,
               qp_ref, kmla_ref, vmla_ref, sbq_ref, sbk16_ref, sbv16_ref, gates_ref,
               ckv_ref, krope_ref, sbk_ref, sbv_ref):
    x = x_ref[0]
    mod = mod_ref[0]
    h = _rms(x) * g1_ref[...] * (1.0 + mod[1:2]) + mod[0:1]
    hb = h.astype(BF16)

    def seg(a, b):
        return _dot(hb, win_ref[:, a:b])

    cos = cos_ref[...]
    sin = sin_ref[...]
    lane = lax.broadcasted_iota(jnp.int32, (1, LANES), 1)
    half = MLA_ROPE // 2

    def rotate(blk):
        other = jnp.where(lane < MLA_NOPE + half, pltpu.roll(blk, LANES - half, 1), pltpu.roll(blk, half, 1))
        return blk * cos + other * sin

    qn = (_rms(seg(C_QLAT, C_KV)) * gq_ref[...]).astype(BF16)
    q = _dot(qn, wuq_ref[...])
    qp_ref[0] = jnp.concatenate([rotate(q[:, h * HEAD_PAD:(h + 1) * HEAD_PAD]) for h in range(MLA_HEADS)],
                                axis=1).astype(BF16)

    ckv = _rms(seg(C_KV, C_SBQ)) * gkv_ref[...]
    ckv_ref[0] = ckv
    kv = _dot(ckv.astype(BF16), wukv_ref[...])
    krp = rotate(seg(C_KR, C_END))
    krope_ref[0] = krp[:, MLA_NOPE:MLA_NOPE + MLA_ROPE]
    kmla_ref[0] = (kv[:, :QP_WIDTH] + jnp.tile(krp, (1, MLA_HEADS))).astype(BF16)
    vmla_ref[0] = kv[:, QP_WIDTH:].astype(BF16)

    sbq_ref[0] = seg(C_SBQ, C_SBK).astype(BF16)
    sbk = seg(C_SBK, C_SBV)
    sbk_ref[0] = sbk
    sbk16_ref[0] = sbk.astype(BF16)
    sbv = seg(C_SBV, C_GATE)
    sbv_ref[0] = sbv
    sbv16_ref[0] = sbv.astype(BF16)
    gates_ref[0] = jax.nn.sigmoid(seg(C_GATE, C_KR)).astype(BF16)


def _in_proj(x, mod, wts, cos_t, sin_t, tm):
    b, s, _ = x.shape
    ns = s // tm
    tok = lambda w: pl.BlockSpec((1, tm, w), lambda i, j: (i, j, 0))
    full = lambda a: pl.BlockSpec(a.shape, lambda i, j: (0,) * a.ndim)
    out_widths = [(QP_WIDTH, BF16), (QP_WIDTH, BF16), (SB_WIDTH, BF16), (SB_WIDTH, BF16), (SB_WIDTH, BF16),
                  (SB_WIDTH, BF16), (2 * D_MODEL, BF16), (MLA_KV_LORA, F32), (MLA_ROPE, F32),
                  (SB_WIDTH, F32), (SB_WIDTH, F32)]
    return pl.pallas_call(
        _in_kernel,
        grid=(b, ns),
        in_specs=[tok(D_MODEL),
                  pl.BlockSpec((1, 6, D_MODEL), lambda i, j: (i, 0, 0)),
                  full(wts["g1"]), full(wts["w_in"]), full(wts["gq"]), full(wts["w_uq"]),
                  full(wts["gkv"]), full(wts["w_ukv"]),
                  pl.BlockSpec((tm, LANES), lambda i, j: (j, 0)),
                  pl.BlockSpec((tm, LANES), lambda i, j: (j, 0))],
        out_specs=[tok(w) for w, _ in out_widths],
        out_shape=[jax.ShapeDtypeStruct((b, s, w), dt) for w, dt in out_widths],
        compiler_params=pltpu.CompilerParams(dimension_semantics=("parallel", "parallel"),
                                             vmem_limit_bytes=VMEM_LIMIT),
        name="in_proj",
    )(x, mod, wts["g1"], wts["w_in"], wts["gq"], wts["w_uq"], wts["gkv"], wts["w_ukv"], cos_t, sin_t)


def _kvup_kernel(ckv_ref, kr_ref, wukv_ref, place_ref, kmla_ref, vmla_ref):
    kv = _dot(ckv_ref[0].astype(BF16), wukv_ref[...])
    kr = _dot(kr_ref[0].astype(BF16), place_ref[...])
    kmla_ref[0] = (kv[:, :QP_WIDTH] + kr).astype(BF16)
    vmla_ref[0] = kv[:, QP_WIDTH:].astype(BF16)


def _kv_up(ckv, krope, w_ukv_r, place, tm):
    b, p, _ = ckv.shape
    return pl.pallas_call(
        _kvup_kernel,
        grid=(b, p // tm),
        in_specs=[pl.BlockSpec((1, tm, MLA_KV_LORA), lambda i, j: (i, j, 0)),
                  pl.BlockSpec((1, tm, MLA_ROPE), lambda i, j: (i, j, 0)),
                  pl.BlockSpec(w_ukv_r.shape, lambda i, j: (0, 0)),
                  pl.BlockSpec(place.shape, lambda i, j: (0, 0))],
        out_specs=[pl.BlockSpec((1, tm, QP_WIDTH), lambda i, j: (i, j, 0)),
                   pl.BlockSpec((1, tm, SB_WIDTH), lambda i, j: (i, j, 0))],
        out_shape=[jax.ShapeDtypeStruct((b, p, QP_WIDTH), BF16), jax.ShapeDtypeStruct((b, p, SB_WIDTH), BF16)],
        compiler_params=pltpu.CompilerParams(dimension_semantics=("parallel", "parallel")),
        name="kv_up",
    )(ckv, krope, w_ukv_r, place)


def _tri(n):
    r = lax.broadcasted_iota(jnp.int32, (n, n), 0)
    c = lax.broadcasted_iota(jnp.int32, (n, n), 1)
    return jnp.where(r > c, 1.0, 0.0).astype(BF16)


def _stick_terms(z):
    log_sig = jnp.minimum(z, 0.0) - jnp.log2(1.0 + jnp.exp2(-jnp.abs(z)))
    return log_sig, log_sig - z


def _split_bf16(x):
    hi = x.astype(BF16)
    return hi, (x - hi.astype(F32)).astype(BF16)


def _finish_mixer(o_mla, o_sb, gates_ref, x_ref, mod_ref, wmu_ref, wsu_ref, wo_ref, g2_ref, x1_ref, h2_ref):
    u_mla = _dot(o_mla.astype(BF16), wmu_ref[...])
    u_sb = _dot(o_sb.astype(BF16), wsu_ref[...])
    gates = gates_ref[0]
    merged = gates[:, :D_MODEL].astype(F32) * u_mla + gates[:, D_MODEL:].astype(F32) * u_sb
    mix = _dot(merged.astype(BF16), wo_ref[...])
    mod = mod_ref[0]
    x1 = x_ref[0] + mod[2:3] * mix
    x1_ref[0] = x1
    h2 = _rms(x1) * g2_ref[...] * (1.0 + mod[4:5]) + mod[3:4]
    h2_ref[0] = _pack_halves(h2[:, :HALF], h2[:, HALF:])


def _prompt_attn_kernel(qp_ref, kmla_ref, vmla_ref, sbq_ref, sbk_ref, sbv_ref,
                        gates_ref, x_ref, mod_ref, wmu_ref, wsu_ref, wo_ref, g2_ref, x1_ref, h2_ref,
                        m_ref, acc_ref, c_ref, sacc_ref, *, tq):
    i = pl.program_id(1)
    lane = lax.broadcasted_iota(jnp.int32, (1, LANES), 1)
    half_masks = (lane < MLA_V, lane >= MLA_V)
    unit_lane = [jnp.where(lane == DENOM_LANE[sub], 1.0, 0.0).astype(BF16) for sub in range(2)]
    row = lax.broadcasted_iota(jnp.int32, (tq, tq), 0)
    col = lax.broadcasted_iota(jnp.int32, (tq, tq), 1)
    chunk_mask = (col // CHUNK) <= (row // CHUNK)
    causal_mask = col < row
    piece = min(tq, MXU_TILE)
    tri_m = _tri(piece)
    st_diag = pl.multiple_of(i * tq, tq)

    def load(ref, start, c0):
        return ref[0, pl.ds(start, tq), c0:c0 + LANES]

    def mla_block(sub, q_h, k, v, mask):
        s = _dot_nt(q_h, k)
        if mask is not None:
            s = jnp.where(mask, s, -jnp.inf)
            m_new = jnp.max(s, axis=-1, keepdims=True)
            acc_ref[sub] = _dot(jnp.exp2(s - m_new).astype(BF16), v)
        else:
            m_old = m_ref[sub]
            m_new = jnp.maximum(m_old, jnp.max(s, axis=-1, keepdims=True))
            acc_ref[sub] = jnp.exp2(m_old - m_new) * acc_ref[sub] + _dot(jnp.exp2(s - m_new).astype(BF16), v)
        m_ref[sub] = m_new

    def sb_block(sub, q_h, k, v, mask):
        log_sig, log_keep = _stick_terms(_dot_nt(q_h, k))
        if mask is not None:
            log_keep = jnp.where(mask, log_keep, 0.0)
        summand = log_keep.astype(BF16)
        pieces = []
        total = None
        for b in reversed(range(tq // piece)):
            sl = slice(b * piece, (b + 1) * piece)
            inner = _dot(summand[:, sl], tri_m)
            piece_total = jnp.sum(log_keep[:, sl], axis=-1, keepdims=True)
            pieces.append(inner if total is None else inner + total)
            total = piece_total if total is None else total + piece_total
        after = jnp.concatenate(pieces[::-1], axis=1)
        if mask is not None:
            a = jnp.where(mask, jnp.exp2(log_sig + after), 0.0)
            sacc_ref[sub] = _dot(a.astype(BF16), v)
            c_ref[sub] = total
        else:
            c_old = c_ref[sub]
            a = jnp.exp2(log_sig + after + c_old)
            sacc_ref[sub] = sacc_ref[sub] + _dot(a.astype(BF16), v)
            c_ref[sub] = c_old + total

    o_mla = []
    o_sb = []
    for first_pair in range(0, MLA_HEADS // 2, PAIRS_PER_LOOP):
        pairs = range(first_pair, first_pair + PAIRS_PER_LOOP)
        q_m = {h: qp_ref[0, :, h * HEAD_PAD:(h + 1) * HEAD_PAD] for p in pairs for h in (2 * p, 2 * p + 1)}
        q_s = {2 * p + sub: jnp.where(half_masks[sub], sbq_ref[0, :, p * LANES:(p + 1) * LANES], 0)
               for p in pairs for sub in range(2)}

        def group_blocks(mla_start, sb_start, masks, pairs=pairs, q_m=q_m, q_s=q_s):
            for p in pairs:
                vcol = p * LANES
                v_m = load(vmla_ref, mla_start, vcol)
                k_s = load(sbk_ref, sb_start, vcol)
                v_s = load(sbv_ref, sb_start, vcol)
                for sub in range(2):
                    head = 2 * p + sub
                    slot = head - 2 * pairs[0]
                    keep = half_masks[sub]
                    mla_block(slot, q_m[head], load(kmla_ref, mla_start, head * HEAD_PAD),
                              jnp.where(keep, v_m, unit_lane[sub]), masks[0])
                    sb_block(slot, q_s[head], k_s, jnp.where(keep, v_s, 0), masks[1])

        group_blocks(st_diag, st_diag, (chunk_mask, causal_mask))

        def step(t, _, group_blocks=group_blocks):
            group_blocks(pl.multiple_of(t * tq, tq), pl.multiple_of((i - 1 - t) * tq, tq), (None, None))
            return 0
        lax.fori_loop(0, i, step, 0)
        for p in range(PAIRS_PER_LOOP):
            heads_out = []
            for sub in range(2):
                acc = acc_ref[2 * p + sub]
                denom = acc[:, DENOM_LANE[sub]:DENOM_LANE[sub] + 1]
                heads_out.append(jnp.where(half_masks[sub], acc, 0.0) / denom)
            o_mla.append(heads_out[0] + heads_out[1])
            o_sb.append(sacc_ref[2 * p] + sacc_ref[2 * p + 1])

    _finish_mixer(jnp.concatenate(o_mla, axis=1), jnp.concatenate(o_sb, axis=1),
                  gates_ref, x_ref, mod_ref, wmu_ref, wsu_ref, wo_ref, g2_ref, x1_ref, h2_ref)


def _decode_attn_kernel(qp_ref, kmla_ref, vmla_ref, sbq_ref, sbk_ref, sbv_ref, pkmla_ref, pvmla_ref, psbk_ref,
                        psbv_ref, gates_ref, x_ref, mod_ref, wmu_ref, wsu_ref, wo_ref, g2_ref, x1_ref, h2_ref,
                        *, tq, past_len, past_blk):
    n_past = past_len // past_blk
    lane = lax.broadcasted_iota(jnp.int32, (1, LANES), 1)
    half_masks = (lane < MLA_V, lane >= MLA_V)
    row = lax.broadcasted_iota(jnp.int32, (tq, tq), 0)
    col = lax.broadcasted_iota(jnp.int32, (tq, tq), 1)
    chunk_mask = ((past_len + col) // CHUNK) <= ((past_len + row) // CHUNK)
    causal_mask = col < row
    tri_new = _tri(tq)
    tri_past = _tri(past_blk)

    o_mla = []
    o_sb = []
    for pair in range(MLA_HEADS // 2):
        vcol = pair * LANES
        v_new = vmla_ref[0, :, vcol:vcol + LANES]
        v_past = pvmla_ref[0, :, vcol:vcol + LANES]
        sk_new = sbk_ref[0, :, vcol:vcol + LANES]
        sv_new = sbv_ref[0, :, vcol:vcol + LANES]
        sk_past = psbk_ref[0, :, vcol:vcol + LANES].astype(BF16)
        sv_past = psbv_ref[0, :, vcol:vcol + LANES].astype(BF16)
        mla_pair = None
        sb_pair = None
        for sub in range(2):
            keep = half_masks[sub]
            kcol = (2 * pair + sub) * HEAD_PAD

            q_h = qp_ref[0, :, kcol:kcol + HEAD_PAD]
            s_past = _dot_nt(q_h, pkmla_ref[0, :, kcol:kcol + HEAD_PAD])
            s_new = jnp.where(chunk_mask, _dot_nt(q_h, kmla_ref[0, :, kcol:kcol + HEAD_PAD]), -jnp.inf)
            m = jnp.maximum(jnp.max(s_past, axis=-1, keepdims=True), jnp.max(s_new, axis=-1, keepdims=True))
            p_past = jnp.exp2(s_past - m)
            p_new = jnp.exp2(s_new - m)
            denom = jnp.sum(p_past, axis=-1, keepdims=True) + jnp.sum(p_new, axis=-1, keepdims=True)
            o = (_dot(p_past.astype(BF16), jnp.where(keep, v_past, 0))
                 + _dot(p_new.astype(BF16), jnp.where(keep, v_new, 0))) / denom
            mla_pair = o if mla_pair is None else mla_pair + o

            q_s = jnp.where(keep, sbq_ref[0, :, vcol:vcol + LANES], 0)
            ls_new, lk_new = _stick_terms(_dot_nt(q_s, sk_new))
            lk_new = jnp.where(causal_mask, lk_new, 0.0)
            hi, lo = _split_bf16(lk_new)
            a_new = jnp.where(causal_mask, jnp.exp2(ls_new + _dot(hi, tri_new) + _dot(lo, tri_new)), 0.0)
            acc = _dot(a_new.astype(BF16), jnp.where(keep, sv_new, 0))
            later = jnp.sum(lk_new, axis=-1, keepdims=True)

            ls_past, lk_past = _stick_terms(_dot_nt(q_s, sk_past))
            hi, lo = _split_bf16(lk_past)
            blocks = lambda a: [a[:, b * past_blk:(b + 1) * past_blk] for b in range(n_past)]
            stacked = jnp.concatenate(blocks(hi) + blocks(lo), axis=0)
            within = _dot(stacked, tri_past)
            after = []
            for b in reversed(range(n_past)):
                after.append(within[b * tq:(b + 1) * tq] + within[(n_past + b) * tq:(n_past + b + 1) * tq] + later)
                later = later + jnp.sum(lk_past[:, b * past_blk:(b + 1) * past_blk], axis=-1, keepdims=True)
            a_past = jnp.exp2(ls_past + jnp.concatenate(after[::-1], axis=1))
            acc = acc + _dot(a_past.astype(BF16), jnp.where(keep, sv_past, 0))
            sb_pair = acc if sb_pair is None else sb_pair + acc
        o_mla.append(mla_pair)
        o_sb.append(sb_pair)

    _finish_mixer(jnp.concatenate(o_mla, axis=1), jnp.concatenate(o_sb, axis=1),
                  gates_ref, x_ref, mod_ref, wmu_ref, wsu_ref, wo_ref, g2_ref, x1_ref, h2_ref)


def _mixer_call(kernel_fn, name, tq, args_kv, gates, x, mod, wts, scratch, batch0=0, n_batch=None):
    s = x.shape[1]
    b = x.shape[0] if n_batch is None else n_batch
    tok = lambda w: pl.BlockSpec((1, tq, w), lambda i, j: (i + batch0, j, 0))
    seq = lambda a: pl.BlockSpec((1,) + a.shape[1:], lambda i, j: (i + batch0, 0, 0))
    full = lambda a: pl.BlockSpec(a.shape, lambda i, j: (0,) * a.ndim)
    out = lambda w: pl.BlockSpec((1, tq, w), lambda i, j: (i, j, 0))
    args = list(args_kv) + [gates, x, mod, wts["w_mla_up"], wts["w_sb_up"], wts["w_out"], wts["g2"]]
    specs = [seq(a) if whole else tok(a.shape[-1]) for a, whole in
             zip(args_kv, (False, True, True, False) + (True,) * (len(args_kv) - 4))]
    specs += [tok(2 * D_MODEL), tok(D_MODEL), pl.BlockSpec((1, 6, D_MODEL), lambda i, j: (i + batch0, 0, 0)),
              full(wts["w_mla_up"]), full(wts["w_sb_up"]), full(wts["w_out"]), full(wts["g2"])]
    return pl.pallas_call(
        kernel_fn,
        grid=(b, s // tq),
        in_specs=specs,
        out_specs=[out(D_MODEL), out(HALF)],
        out_shape=[jax.ShapeDtypeStruct((b, s, D_MODEL), F32), jax.ShapeDtypeStruct((b, s, HALF), jnp.int32)],
        scratch_shapes=scratch,
        compiler_params=pltpu.CompilerParams(dimension_semantics=("parallel", "arbitrary"),
                                             vmem_limit_bytes=VMEM_LIMIT),
        name=name,
    )(*args)


def _prompt_attention(qp, kmla, vmla, sbq, sbk16, sbv16, gates, x, mod, wts, tq, batch0, n_batch):
    col = lambda: pltpu.VMEM((2 * PAIRS_PER_LOOP, tq, 1), F32)
    wide = lambda: pltpu.VMEM((2 * PAIRS_PER_LOOP, tq, LANES), F32)
    return _mixer_call(functools.partial(_prompt_attn_kernel, tq=tq), "attention", tq,
                       (qp, kmla, vmla, sbq, sbk16, sbv16), gates, x, mod, wts,
                       [col(), wide(), col(), wide()], batch0, n_batch)


def _decode_attention(qp, kmla, vmla, sbq, sbk16, sbv16, past, gates, x, mod, wts, past_blk):
    tq = x.shape[1]
    past_len = past[0].shape[1]
    kern = functools.partial(_decode_attn_kernel, tq=tq, past_len=past_len, past_blk=past_blk)
    return _mixer_call(kern, "decode_attention", tq, (qp, kmla, vmla, sbq, sbk16, sbv16) + tuple(past),
                       gates, x, mod, wts, [])


def _route_kernel(h2_ref, wr_ref, br_ref, idx_ref, wt_ref, rank_ref, cnt_ref, seen_ref):
    lo, hi = _unpack_halves(h2_ref[...])
    tm = lo.shape[0]
    logits = _dot_nt(wr_ref[:, :HALF], lo.astype(BF16)) + _dot_nt(wr_ref[:, HALF:], hi.astype(BF16))
    scores = jax.nn.sigmoid(logits)
    sel = scores + br_ref[...]
    neg = -jnp.inf

    grp = sel.reshape(N_GROUPS, GROUP_SIZE, tm)
    within = lax.broadcasted_iota(jnp.int32, grp.shape, 1)
    top1 = jnp.max(grp, axis=1, keepdims=True)
    first = jnp.min(jnp.where(grp == top1, within, GROUP_SIZE), axis=1, keepdims=True)
    top2 = jnp.max(jnp.where(within == first, neg, grp), axis=1, keepdims=True)
    gscore = (top1 + top2).reshape(N_GROUPS, tm)

    gid = lax.broadcasted_iota(jnp.int32, gscore.shape, 0)
    chosen = jnp.zeros(gscore.shape, jnp.bool_)
    for _ in range(TOPK_GROUPS):
        best = jnp.max(gscore, axis=0, keepdims=True)
        pick = jnp.min(jnp.where(gscore == best, gid, N_GROUPS), axis=0, keepdims=True)
        hit = gid == pick
        chosen = jnp.logical_or(chosen, hit)
        gscore = jnp.where(hit, neg, gscore)
    chosen3 = jnp.broadcast_to(chosen.reshape(N_GROUPS, 1, tm), grp.shape)
    cand = jnp.where(chosen3, grp, neg).reshape(N_EXPERTS, tm)
    outside = jnp.where(cand == neg, 1.0, 0.0)

    eid = lax.broadcasted_iota(jnp.int32, cand.shape, 0)
    picks = []
    weights = []
    for _ in range(TOP_K):
        best = jnp.max(cand, axis=0, keepdims=True)
        pick = jnp.min(jnp.where(cand == best, eid, N_EXPERTS), axis=0, keepdims=True)
        hit = eid == pick
        weights.append(jnp.sum(jnp.where(hit, scores, 0.0), axis=0, keepdims=True))
        picks.append(pick)
        cand = jnp.where(hit, neg, cand)
    w = jnp.concatenate(weights, axis=0)
    idx_ref[...] = jnp.concatenate(picks, axis=0)
    wt_ref[...] = w / (jnp.sum(w, axis=0, keepdims=True) + 1e-20) * ROUTED_SCALE

    @pl.when(pl.program_id(0) == 0)
    def _():
        seen_ref[...] = jnp.zeros_like(seen_ref)

    onehot = jnp.where(cand == neg, 1.0, 0.0) - outside
    src = lax.broadcasted_iota(jnp.int32, (tm, tm), 0)
    dst = lax.broadcasted_iota(jnp.int32, (tm, tm), 1)
    earlier = jnp.where(src < dst, 1.0, 0.0).astype(BF16)
    before = _dot(onehot.astype(BF16), earlier) + seen_ref[...]
    rank_ref[...] = jnp.concatenate(
        [jnp.sum(jnp.where(eid == pick, before, 0.0), axis=0, keepdims=True) for pick in picks],
        axis=0).astype(jnp.int32)
    seen = seen_ref[...] + jnp.sum(onehot, axis=1, keepdims=True)
    seen_ref[...] = seen
    cnt_ref[...] = seen.astype(jnp.int32)


def _route(h2p, wr_t, br, tm):
    t = h2p.shape[0]
    kt = lambda: pl.BlockSpec((TOP_K, tm), lambda i: (0, i))
    return pl.pallas_call(
        _route_kernel,
        grid=(t // tm,),
        in_specs=[pl.BlockSpec((tm, HALF), lambda i: (i, 0)),
                  pl.BlockSpec(wr_t.shape, lambda i: (0, 0)),
                  pl.BlockSpec(br.shape, lambda i: (0, 0))],
        out_specs=[kt(), kt(), kt(), pl.BlockSpec((N_EXPERTS, 1), lambda i: (0, 0))],
        out_shape=[jax.ShapeDtypeStruct((TOP_K, t), jnp.int32), jax.ShapeDtypeStruct((TOP_K, t), F32),
                   jax.ShapeDtypeStruct((TOP_K, t), jnp.int32), jax.ShapeDtypeStruct((N_EXPERTS, 1), jnp.int32)],
        scratch_shapes=[pltpu.VMEM((N_EXPERTS, 1), F32)],
        compiler_params=pltpu.CompilerParams(dimension_semantics=("arbitrary",)),
        name="route",
    )(h2p, wr_t, br)


def _position_kernel(idx_ref, rank_ref, start_ref, pos_ref):
    idx = idx_ref[...]
    eid = lax.broadcasted_iota(jnp.int32, (N_EXPERTS, idx.shape[1]), 0)
    start = start_ref[...]
    base = jnp.concatenate(
        [jnp.sum(jnp.where(eid == idx[k:k + 1, :], start, 0.0), axis=0, keepdims=True) for k in range(TOP_K)],
        axis=0)
    pos_ref[...] = base.astype(jnp.int32) + rank_ref[...]


def _positions(idx_kt, rank_kt, pad_start, tm):
    t = idx_kt.shape[1]
    kt = lambda: pl.BlockSpec((TOP_K, tm), lambda i: (0, i))
    return pl.pallas_call(
        _position_kernel,
        grid=(t // tm,),
        in_specs=[kt(), kt(), pl.BlockSpec((N_EXPERTS, 1), lambda i: (0, 0))],
        out_specs=kt(),
        out_shape=jax.ShapeDtypeStruct((TOP_K, t), jnp.int32),
        compiler_params=pltpu.CompilerParams(dimension_semantics=("parallel",)),
        name="positions",
    )(idx_kt, rank_kt, pad_start.astype(F32).reshape(N_EXPERTS, 1))


def _sc_mesh():
    return plsc.VectorSubcoreMesh(core_axis_name="c", subcore_axis_name="s",
                                  num_cores=SC_CORES, num_subcores=SC_SUBCORES)


def _sc_scatter_rows(rows, pos, n_out, chunk):
    t, width = rows.shape
    copies = pos.shape[0] // t
    per_worker = t // SC_WORKERS
    n_chunks = per_worker // chunk
    tail = per_worker - n_chunks * chunk
    assert per_worker * SC_WORKERS == t and tail % 8 == 0
    tail_rows = max(tail, 8)

    @functools.partial(
        pl.kernel, mesh=_sc_mesh(),
        out_type=jax.ShapeDtypeStruct((n_out, width), rows.dtype),
        scratch_types=[pltpu.VMEM((copies, chunk), jnp.int32), pltpu.VMEM((chunk, width), rows.dtype),
                       pltpu.VMEM((copies, tail_rows), jnp.int32), pltpu.VMEM((tail_rows, width), rows.dtype),
                       pltpu.SemaphoreType.DMA, pltpu.SemaphoreType.DMA],
    )
    def scatter(rows_hbm, pos_hbm, out_hbm, idx_v, rows_v, idx_t, rows_t, load_sem, store_sem):
        wid = lax.axis_index("s") * SC_CORES + lax.axis_index("c")
        base = wid * per_worker

        def move(off, n, idx_buf, row_buf):
            loads = [pltpu.async_copy(rows_hbm.at[pl.ds(off, n)], row_buf, load_sem)]
            for k in range(copies):
                src = pos_hbm.at[pl.ds(pl.multiple_of(k * t + off, 8), n)]
                loads.append(pltpu.async_copy(src, idx_buf.at[k], load_sem))
            for cp in loads:
                cp.wait()
            stores = [pltpu.async_copy(row_buf, out_hbm.at[idx_buf.at[k]], store_sem) for k in range(copies)]
            for cp in stores:
                cp.wait()

        @pl.loop(0, n_chunks)
        def _(c):
            move(pl.multiple_of(base + c * chunk, 8), chunk, idx_v, rows_v)

        if tail:
            move(pl.multiple_of(base + n_chunks * chunk, 8), tail, idx_t, rows_t)

    return scatter(rows, pos)


def _sc_gather_rows(table, idx, chunk):
    n_rows = idx.shape[0]
    width = table.shape[1]
    per_worker = n_rows // SC_WORKERS
    n_chunks = per_worker // chunk
    assert per_worker * SC_WORKERS == n_rows and n_chunks * chunk == per_worker
    mesh = _sc_mesh()

    @functools.partial(
        pl.kernel, mesh=mesh,
        out_type=jax.ShapeDtypeStruct((n_rows, width), table.dtype),
        scratch_types=[pltpu.VMEM((chunk,), jnp.int32), pltpu.VMEM((chunk, width), table.dtype),
                       pltpu.SemaphoreType.DMA],
    )
    def gather(table_hbm, idx_hbm, out_hbm, idx_v, rows_v, sem):
        wid = lax.axis_index("s") * SC_CORES + lax.axis_index("c")
        base = wid * per_worker

        @pl.loop(0, n_chunks)
        def _(c):
            off = pl.multiple_of(base + c * chunk, 8)
            pltpu.sync_copy(idx_hbm.at[pl.ds(off, chunk)], idx_v)
            pltpu.async_copy(table_hbm.at[idx_v], rows_v, sem).wait()
            pltpu.sync_copy(rows_v, out_hbm.at[pl.ds(off, chunk)])

    return gather(table, idx)


ROW_SLOTS = 3
WEIGHT_SLOTS = 2


def _expert_kernel(be_ref, nv_ref, nu_ref, nxt_ref, xs_hbm, w1_hbm, w3_hbm, w2_hbm, os_ref,
                   wb1, wb3, wb2, xbuf, wf1, wf3, wf2, slot_ref, xsem, wsem, *, blk):
    i = pl.program_id(0)
    n_used = nu_ref[0]

    def rows_copy(j):
        src = xs_hbm.at[pl.ds(pl.multiple_of(j * blk, blk), blk)]
        return pltpu.make_async_copy(src, xbuf.at[j % ROW_SLOTS], xsem.at[j % ROW_SLOTS])

    def weight_copies(e, slot):
        return [pltpu.make_async_copy(w_hbm.at[e], buf.at[slot], wsem.at[slot])
                for w_hbm, buf in ((w1_hbm, wf1), (w3_hbm, wf3), (w2_hbm, wf2))]

    @pl.when(i == 0)
    def _():
        rows_copy(0).start()

        @pl.when(n_used > 1)
        def _():
            rows_copy(1).start()
        for cp in weight_copies(be_ref[0], 0):
            cp.start()
        slot_ref[0] = 1

    @pl.when(i + 2 < n_used)
    def _():
        rows_copy(i + 2).start()

    @pl.when(i < n_used)
    def _():
        @pl.when(jnp.logical_or(i == 0, be_ref[i] != be_ref[jnp.maximum(i - 1, 0)]))
        def _():
            slot = 1 - slot_ref[0]
            slot_ref[0] = slot
            for cp in weight_copies(be_ref[i], slot):
                cp.wait()

            @pl.when(nxt_ref[i] >= 0)
            def _():
                for cp in weight_copies(nxt_ref[i], 1 - slot):
                    cp.start()
            wb1[...] = wf1[slot].astype(BF16)
            wb3[...] = wf3[slot].astype(BF16)
            wb2[...] = wf2[slot].astype(BF16)

        rows_copy(i).wait()
        packed = xbuf[i % ROW_SLOTS]
        live = lax.broadcasted_iota(jnp.int32, packed.shape, 0) < nv_ref[i]
        lo, hi = _unpack_halves(jnp.where(live, packed, 0))
        lo = lo.astype(BF16)
        hi = hi.astype(BF16)
        a = _dot(lo, wb1[:HALF, :]) + _dot(hi, wb1[HALF:, :])
        b = _dot(lo, wb3[:HALF, :]) + _dot(hi, wb3[HALF:, :])
        o = _dot((_silu(a) * b).astype(BF16), wb2[...])
        os_ref[...] = _pack_halves(o[:, :HALF], o[:, HALF:])


def _experts(xs, blk_exp, blk_valid, n_used, next_exp, w1, w3, w2, blk):
    rows = xs.shape[0]
    nb = rows // blk
    any_space = pl.BlockSpec(memory_space=pl.ANY)
    grid_spec = pltpu.PrefetchScalarGridSpec(
        num_scalar_prefetch=4,
        grid=(nb,),
        in_specs=[any_space, any_space, any_space, any_space],
        out_specs=pl.BlockSpec((blk, HALF), lambda i, be, nv, nu, nx: (jnp.minimum(i, nu[0] - 1), 0)),
        scratch_shapes=[pltpu.VMEM((D_MODEL, EXPERT_DIM), BF16), pltpu.VMEM((D_MODEL, EXPERT_DIM), BF16),
                        pltpu.VMEM((EXPERT_DIM, D_MODEL), BF16),
                        pltpu.VMEM((ROW_SLOTS, blk, HALF), jnp.int32),
                        pltpu.VMEM((WEIGHT_SLOTS, D_MODEL, EXPERT_DIM), F32),
                        pltpu.VMEM((WEIGHT_SLOTS, D_MODEL, EXPERT_DIM), F32),
                        pltpu.VMEM((WEIGHT_SLOTS, EXPERT_DIM, D_MODEL), F32),
                        pltpu.SMEM((1,), jnp.int32),
                        pltpu.SemaphoreType.DMA((ROW_SLOTS,)), pltpu.SemaphoreType.DMA((WEIGHT_SLOTS,))],
    )
    return pl.pallas_call(
        functools.partial(_expert_kernel, blk=blk),
        grid_spec=grid_spec,
        out_shape=jax.ShapeDtypeStruct((rows, HALF), jnp.int32),
        compiler_params=pltpu.CompilerParams(dimension_semantics=("arbitrary",), vmem_limit_bytes=VMEM_LIMIT),
        name="experts",
    )(blk_exp, blk_valid, n_used, next_exp, xs, w1, w3, w2)


def _combine_kernel(g_ref, wt_ref, h2_ref, x1_ref, mod_ref, sw1_ref, sw3_ref, sw2_ref, gf_ref, y_ref):
    wt = wt_ref[...]
    lo_acc = None
    for k in range(TOP_K):
        lo, hi = _unpack_halves(g_ref[k])
        wk = wt[:, k:k + 1]
        lo_acc = wk * lo if lo_acc is None else lo_acc + wk * lo
        hi_acc = wk * hi if k == 0 else hi_acc + wk * hi
    routed = jnp.concatenate([lo_acc, hi_acc], axis=1)
    lo, hi = _unpack_halves(h2_ref[...])
    lo = lo.astype(BF16)
    hi = hi.astype(BF16)
    a = _dot(lo, sw1_ref[:HALF, :]) + _dot(hi, sw1_ref[HALF:, :])
    b = _dot(lo, sw3_ref[:HALF, :]) + _dot(hi, sw3_ref[HALF:, :])
    shared = _dot((_silu(a) * b).astype(BF16), sw2_ref[...])
    mod = mod_ref[0]
    x2 = x1_ref[...] + mod[5:6] * (routed + shared)
    y_ref[...] = _rms(x2) * gf_ref[...]


def _combine_into_kernel(g_ref, wt_ref, h2_ref, x1_ref, mod_ref, sw1_ref, sw3_ref, sw2_ref, gf_ref, prev_ref, y_ref):
    del prev_ref
    _combine_kernel(g_ref, wt_ref, h2_ref, x1_ref, mod_ref, sw1_ref, sw3_ref, sw2_ref, gf_ref, y_ref)


def _combine(gathered, wt, h2p, x1, mod, wts, tm, src_tile0, mod_batch0, tiles_per_batch, out_rows, out_tile0,
             y_prev=None):
    t = x1.shape[0]
    full = lambda a: pl.BlockSpec(a.shape, lambda i: (0,) * a.ndim)
    shifted = lambda w: pl.BlockSpec((tm, w), lambda i: (i + src_tile0, 0))
    args = [gathered, wt, h2p, x1, mod, wts["sw1"], wts["sw3"], wts["sw2"], wts["gf"]]
    specs = [pl.BlockSpec((TOP_K, tm, HALF), lambda i: (0, i + src_tile0, 0)), shifted(TOP_K), shifted(HALF),
             pl.BlockSpec((tm, D_MODEL), lambda i: (i, 0)),
             pl.BlockSpec((1, 6, D_MODEL), lambda i: (i // tiles_per_batch + mod_batch0, 0, 0)),
             full(wts["sw1"]), full(wts["sw3"]), full(wts["sw2"]), full(wts["gf"])]
    aliases = {}
    body = _combine_kernel
    if y_prev is not None:
        args.append(y_prev)
        specs.append(pl.BlockSpec(memory_space=pl.ANY))
        aliases = {len(args) - 1: 0}
        body = _combine_into_kernel
    return pl.pallas_call(
        body,
        grid=(t // tm,),
        in_specs=specs,
        out_specs=pl.BlockSpec((tm, D_MODEL), lambda i: (i + out_tile0, 0)),
        out_shape=jax.ShapeDtypeStruct((out_rows, D_MODEL), F32),
        input_output_aliases=aliases,
        compiler_params=pltpu.CompilerParams(dimension_semantics=("parallel",), vmem_limit_bytes=VMEM_LIMIT),
        name="combine",
    )(*args)


def _rope_tables(pos):
    half = MLA_ROPE // 2
    inv_freq = ROPE_THETA ** (-jnp.arange(half, dtype=F32) / half)
    ang = pos.astype(F32)[:, None] * inv_freq
    cos, sin = jnp.cos(ang), jnp.sin(ang)
    n = pos.shape[0]
    ones = jnp.ones((n, MLA_NOPE), F32)
    z_nope = jnp.zeros((n, MLA_NOPE), F32)
    z_pad = jnp.zeros((n, HEAD_PAD - MLA_NOPE - MLA_ROPE), F32)
    return (jnp.concatenate([ones, cos, cos, z_pad], axis=1),
            jnp.concatenate([z_nope, -sin, sin, z_pad], axis=1))


def _prep_weights(g_norm1, w_in, g_q_lat, w_uq, g_kv_lat, w_ukv, w_mla_up, w_sb_up, w_out, g_norm2,
                  w_router, b_router, shared_w1, shared_w3, shared_w2, g_final):
    w = w_in[0]
    o = 0
    parts = {}
    for name, width in (("qlat", MLA_Q_LORA), ("kv", MLA_KV_LORA), ("kr", MLA_ROPE), ("sbq", SB_WIDTH),
                        ("sbk", SB_WIDTH), ("sbv", SB_WIDTH), ("gm", D_MODEL), ("gs", D_MODEL)):
        parts[name] = w[:, o:o + width]
        o += width
    kr = parts["kr"]
    z_nope = jnp.zeros((D_MODEL, MLA_NOPE), F32)
    z_pad = jnp.zeros((D_MODEL, HEAD_PAD - MLA_NOPE - MLA_ROPE), F32)
    kr_seg = jnp.concatenate([z_nope, kr, z_pad], axis=1)
    w_in_r = jnp.concatenate([parts["qlat"], parts["kv"], parts["sbq"] * (SB_DIM ** -0.5 * LOG2_E), parts["sbk"],
                              parts["sbv"], parts["gm"], parts["gs"], kr_seg], axis=1).astype(BF16)

    scale = (MLA_NOPE + MLA_ROPE) ** -0.5 * LOG2_E
    uq = w_uq[0].reshape(MLA_Q_LORA, MLA_HEADS, MLA_NOPE + MLA_ROPE) * scale
    nope, rope = uq[..., :MLA_NOPE], uq[..., MLA_NOPE:]
    zq_pad = jnp.zeros((MLA_Q_LORA, MLA_HEADS, HEAD_PAD - MLA_NOPE - MLA_ROPE), F32)
    w_uq_r = jnp.concatenate([nope, rope, zq_pad], axis=-1).reshape(MLA_Q_LORA, QP_WIDTH).astype(BF16)

    ukv = w_ukv[0].reshape(MLA_KV_LORA, MLA_HEADS, MLA_NOPE + MLA_V)
    k_nope, v = ukv[..., :MLA_NOPE], ukv[..., MLA_NOPE:]
    k_pad = jnp.concatenate([k_nope, jnp.zeros((MLA_KV_LORA, MLA_HEADS, HEAD_PAD - MLA_NOPE), F32)], axis=-1)
    w_ukv_r = jnp.concatenate([k_pad.reshape(MLA_KV_LORA, QP_WIDTH),
                               v.reshape(MLA_KV_LORA, MLA_HEADS * MLA_V)], axis=1).astype(BF16)

    eye = jnp.eye(MLA_ROPE, dtype=F32)
    place_head = jnp.concatenate([jnp.zeros((MLA_ROPE, MLA_NOPE), F32), eye,
                                  jnp.zeros((MLA_ROPE, HEAD_PAD - MLA_NOPE - MLA_ROPE), F32)], axis=1)
    place = jnp.tile(place_head, (1, MLA_HEADS)).astype(BF16)

    return dict(
        g1=g_norm1[0].reshape(1, D_MODEL), w_in=w_in_r, gq=g_q_lat[0].reshape(1, MLA_Q_LORA), w_uq=w_uq_r,
        gkv=g_kv_lat[0].reshape(1, MLA_KV_LORA), w_ukv=w_ukv_r, place=place,
        w_mla_up=w_mla_up[0].astype(BF16), w_sb_up=w_sb_up[0].astype(BF16), w_out=w_out[0].astype(BF16),
        g2=g_norm2[0].reshape(1, D_MODEL), wr_t=w_router[0].T.astype(BF16),
        br=b_router[0].reshape(N_EXPERTS, 1),
        sw1=shared_w1[0].astype(BF16), sw3=shared_w3[0].astype(BF16), sw2=shared_w2[0].astype(BF16),
        gf=g_final.reshape(1, D_MODEL))


def _block_plan(counts, nb, blk):
    padded = (counts + blk - 1) // blk * blk
    pad_end = jnp.cumsum(padded)
    pad_start = pad_end - padded
    first_row = jnp.arange(nb, dtype=jnp.int32) * blk
    blk_exp = jnp.minimum(jnp.sum((pad_end[None, :] <= first_row[:, None]).astype(jnp.int32), axis=1),
                          N_EXPERTS - 1)
    own = blk_exp[:, None] == jnp.arange(N_EXPERTS, dtype=jnp.int32)[None, :]
    seg_end = jnp.sum(jnp.where(own, (pad_start + counts)[None, :], 0), axis=1)
    blk_valid = jnp.clip(seg_end - first_row, 0, blk).astype(jnp.int32)
    n_used = (pad_end[-1:] // blk).astype(jnp.int32)
    after_seg = jnp.sum(jnp.where(own, (pad_end // blk)[None, :], 0), axis=1)
    follower = jnp.sum(jnp.where(after_seg[:, None] == jnp.arange(nb, dtype=jnp.int32)[None, :],
                                 blk_exp[None, :], 0), axis=1)
    next_exp = jnp.where(after_seg < n_used[0], follower, -1).astype(jnp.int32)
    return pad_start, blk_exp.astype(jnp.int32), blk_valid, n_used, next_exp


def _moe_rows(h2p, idx_kt, rank_kt, counts, w1, w3, w2, scatter_rows, gather_rows, blk, tm):
    t = h2p.shape[0]
    nb = -(-t * TOP_K // blk) + N_EXPERTS
    pad_start, blk_exp, blk_valid, n_used, next_exp = _block_plan(counts.reshape(N_EXPERTS), nb, blk)
    pos = _positions(idx_kt, rank_kt, pad_start, tm).reshape(TOP_K * t)
    xs = scatter_rows(h2p, pos, nb * blk)
    os_ = _experts(xs, blk_exp, blk_valid, n_used, next_exp, w1, w3, w2, blk)
    return gather_rows(os_, pos).reshape(TOP_K, t, HALF)


def _forward(x_prompt, x_sample, cache_mla_ckv, cache_mla_krope, cache_sb_k, cache_sb_v, c_prompt, c_sample,
             w_ada, b_ada, moe_w1, moe_w3, moe_w2, wts, scatter_rows, gather_rows, token_block, attn_block,
             route_block, moe_block, wide_block):
    bp, sp, _ = x_prompt.shape
    bs, ss, _ = x_sample.shape
    past_len = cache_mla_ckv.shape[2]

    mod = _ada(jnp.concatenate([c_prompt, c_sample], axis=0), w_ada[0], b_ada[0]).reshape(bp + bs, 6, D_MODEL)
    mod_p, mod_s = mod[:bp], mod[bp:]

    cos_p, sin_p = _rope_tables(jnp.arange(sp))
    (qp, kmla, vmla, sbq, sbk16, sbv16, gates, ckv_p, krope_p, sbk_p, sbv_p) = _in_proj(
        x_prompt, mod_p, wts, cos_p, sin_p, wide_block)
    half_b = bp // 2
    mixed = [_prompt_attention(qp, kmla, vmla, sbq, sbk16, sbv16, gates, x_prompt, mod_p, wts, attn_block,
                               b0, half_b) for b0 in (0, half_b)]

    cos_s, sin_s = _rope_tables(past_len + jnp.arange(ss))
    (qs, kmla_s, vmla_s, sbq_s, sbk16_s, sbv16_s, gates_s, ckv_s, krope_s, sbk_s, sbv_s) = _in_proj(
        x_sample, mod_s, wts, cos_s, sin_s, ss)
    pkmla, pvmla = _kv_up(cache_mla_ckv[0], cache_mla_krope[0], wts["w_ukv"], wts["place"], wide_block)
    past = (pkmla, pvmla, cache_sb_k[0].reshape(bs, past_len, SB_WIDTH), cache_sb_v[0].reshape(bs, past_len, SB_WIDTH))
    x1_s, h2_s = _decode_attention(qs, kmla_s, vmla_s, sbq_s, sbk16_s, sbv16_s, past, gates_s, x_sample, mod_s,
                                   wts, token_block)

    tp, ts = bp * sp, bs * ss
    th = half_b * sp

    def moe(h2_rows):
        idx_kt, wt_kt, rank_kt, counts = _route(h2_rows, wts["wr_t"], wts["br"], route_block)
        gathered = _moe_rows(h2_rows, idx_kt, rank_kt, counts, moe_w1[0], moe_w3[0], moe_w2[0],
                             scatter_rows, gather_rows, moe_block, route_block)
        return gathered, wt_kt.T

    tiles_per_batch = sp // wide_block
    (x1_a, h2_a), (x1_b, h2_b) = mixed
    h2_a = h2_a.reshape(th, HALF)
    g_a, wt_a = moe(h2_a)
    y_p = _combine(g_a, wt_a, h2_a, x1_a.reshape(th, D_MODEL), mod_p, wts, wide_block,
                   0, 0, tiles_per_batch, tp, 0)
    h2_b = jnp.concatenate([h2_b.reshape(th, HALF), h2_s.reshape(ts, HALF)], axis=0)
    g_b, wt_b = moe(h2_b)
    y_p = _combine(g_b, wt_b, h2_b, x1_b.reshape(th, D_MODEL), mod_p, wts, wide_block,
                   0, half_b, tiles_per_batch, tp, th // wide_block, y_prev=y_p)
    y_s = _combine(g_b, wt_b, h2_b, x1_s.reshape(ts, D_MODEL), mod_s, wts, ss, th // ss, 0, 1, ts, 0)

    heads = lambda a, b_, s_: a.reshape(1, b_, s_, SB_HEADS, SB_DIM)
    return (y_p.reshape(bp, sp, D_MODEL), y_s.reshape(bs, ss, D_MODEL),
            ckv_p[None], krope_p[None], heads(sbk_p, bp, sp), heads(sbv_p, bp, sp),
            ckv_s[None], krope_s[None], heads(sbk_s, bs, ss), heads(sbv_s, bs, ss))


def kernel(x_prompt, x_sample, cache_mla_ckv, cache_mla_krope, cache_sb_k, cache_sb_v, c_prompt, c_sample, w_ada, b_ada, g_norm1, w_in, g_q_lat, w_uq, g_kv_lat, w_ukv, w_mla_up, w_sb_up, w_out, g_norm2, w_router, b_router, moe_w1, moe_w3, moe_w2, shared_w1, shared_w3, shared_w2, g_final):
    wts = _prep_weights(g_norm1, w_in, g_q_lat, w_uq, g_kv_lat, w_ukv, w_mla_up, w_sb_up, w_out, g_norm2,
                        w_router, b_router, shared_w1, shared_w3, shared_w2, g_final)
    scatter_rows = functools.partial(_sc_scatter_rows, chunk=SC_CHUNK)
    gather_rows = functools.partial(_sc_gather_rows, chunk=SC_CHUNK)
    return _forward(x_prompt, x_sample, cache_mla_ckv, cache_mla_krope, cache_sb_k, cache_sb_v, c_prompt, c_sample,
                    w_ada, b_ada, moe_w1, moe_w3, moe_w2, wts, scatter_rows, gather_rows, TOKEN_BLOCK, ATTN_BLOCK,
                    TOKEN_BLOCK, MOE_BLOCK, WIDE_BLOCK)
```

```python
import functools

import jax
import jax.numpy as jnp
from jax import lax
from jax.experimental import pallas as pl
from jax.experimental.pallas import tpu as pltpu
from jax.experimental.pallas import tpu_sc as plsc

F32 = jnp.float32
BF16 = jnp.bfloat16

D_MODEL = 1024
NORM_EPS = 1e-6
CHUNK = 64
MLA_HEADS = 8
MLA_NOPE = 64
MLA_ROPE = 32
MLA_V = 64
MLA_Q_LORA = 384
MLA_KV_LORA = 256
ROPE_THETA = 10000.0
SB_HEADS = 8
SB_DIM = 64
SB_WIDTH = SB_HEADS * SB_DIM
N_EXPERTS = 256
TOP_K = 8
N_GROUPS = 8
TOPK_GROUPS = 4
GROUP_SIZE = N_EXPERTS // N_GROUPS
EXPERT_DIM = 256
ROUTED_SCALE = 2.5
LOG2_E = 1.4426950408889634

LANES = 128
MXU_TILE = 256
SC_CORES = 2
SC_SUBCORES = 16
SC_WORKERS = SC_CORES * SC_SUBCORES
VMEM_LIMIT = 60 * 1024 * 1024

HEAD_PAD = LANES
DENOM_LANE = (MLA_V, 0)
QP_WIDTH = MLA_HEADS * HEAD_PAD
HALF = D_MODEL // 2

C_QLAT = 0
C_KV = C_QLAT + MLA_Q_LORA
C_SBQ = C_KV + MLA_KV_LORA
C_SBK = C_SBQ + SB_WIDTH
C_SBV = C_SBK + SB_WIDTH
C_GATE = C_SBV + SB_WIDTH
C_KR = C_GATE + 2 * D_MODEL
C_END = C_KR + LANES

MOE_BLOCK = 512
WIDE_BLOCK = 512
TOKEN_BLOCK = 256
ATTN_BLOCK = 512
PAIRS_PER_LOOP = 1
SC_CHUNK = 64
GROUP_A_SHARE = (9, 16)


def _rms(x):
    return x * lax.rsqrt(jnp.mean(x * x, axis=-1, keepdims=True) + NORM_EPS)


def _silu(x):
    return x * jax.nn.sigmoid(x)


def _pack_halves(lo, hi):
    lo_bits = lax.bitcast_convert_type(lo.astype(BF16).astype(F32), jnp.uint32) >> 16
    hi_bits = lax.bitcast_convert_type(hi.astype(BF16).astype(F32), jnp.uint32) & jnp.uint32(0xFFFF0000)
    return lax.bitcast_convert_type(lo_bits | hi_bits, jnp.int32)


def _unpack_halves(p):
    u = lax.bitcast_convert_type(p, jnp.uint32)
    lo = lax.bitcast_convert_type(u << 16, F32)
    hi = lax.bitcast_convert_type(u & jnp.uint32(0xFFFF0000), F32)
    return lo, hi


def _dot(a, b):
    return jnp.dot(a, b, preferred_element_type=F32)


def _dot_nt(a, b):
    return lax.dot_general(a, b, (((1,), (1,)), ((), ())), preferred_element_type=F32)


def _ada_kernel(c_ref, w_ref, b_ref, o_ref):
    c = c_ref[...]
    o_ref[...] = _dot(_silu(c).astype(BF16), w_ref[...].astype(BF16)) + b_ref[...]


def _ada(c, w_ada, b_ada):
    n = c.shape[0]
    width = w_ada.shape[1]
    return pl.pallas_call(
        _ada_kernel,
        grid=(width // D_MODEL,),
        in_specs=[pl.BlockSpec((n, D_MODEL), lambda j: (0, 0)),
                  pl.BlockSpec((D_MODEL, D_MODEL), lambda j: (0, j)),
                  pl.BlockSpec((1, D_MODEL), lambda j: (0, j))],
        out_specs=pl.BlockSpec((n, D_MODEL), lambda j: (0, j)),
        out_shape=jax.ShapeDtypeStruct((n, width), F32),
        name="ada",
    )(c, w_ada, b_ada.reshape(1, width))


def _in_kernel(x_ref, mod_ref, g1_ref, win_ref, gq_ref, wuq_ref, gkv_ref, wukv_ref, cos_ref, sin_ref,
               qp_ref, kmla_ref, vmla_ref, sbq_ref, sbk16_ref, sbv16_ref, gates_ref,
               ckv_ref, krope_ref, sbk_ref, sbv_ref):
    x = x_ref[0]
    mod = mod_ref[0]
    h = _rms(x) * g1_ref[...] * (1.0 + mod[1:2]) + mod[0:1]
    hb = h.astype(BF16)

    def seg(a, b):
        return _dot(hb, win_ref[:, a:b])

    cos = cos_ref[...]
    sin = sin_ref[...]
    lane = lax.broadcasted_iota(jnp.int32, (1, LANES), 1)
    half = MLA_ROPE // 2

    def rotate(blk):
        other = jnp.where(lane < MLA_NOPE + half, pltpu.roll(blk, LANES - half, 1), pltpu.roll(blk, half, 1))
        return blk * cos + other * sin

    qn = (_rms(seg(C_QLAT, C_KV)) * gq_ref[...]).astype(BF16)
    q = _dot(qn, wuq_ref[...])
    qp_ref[0] = jnp.concatenate([rotate(q[:, h * HEAD_PAD:(h + 1) * HEAD_PAD]) for h in range(MLA_HEADS)],
                                axis=1).astype(BF16)

    ckv = _rms(seg(C_KV, C_SBQ)) * gkv_ref[...]
    ckv_ref[0] = ckv
    kv = _dot(ckv.astype(BF16), wukv_ref[...])
    krp = rotate(seg(C_KR, C_END))
    krope_ref[0] = krp[:, MLA_NOPE:MLA_NOPE + MLA_ROPE]
    kmla_ref[0] = (kv[:, :QP_WIDTH] + jnp.tile(krp, (1, MLA_HEADS))).astype(BF16)
    vmla_ref[0] = kv[:, QP_WIDTH:].astype(BF16)

    sbq_ref[0] = seg(C_SBQ, C_SBK).astype(BF16)
    sbk = seg(C_SBK, C_SBV)
    sbk_ref[0] = sbk
    sbk16_ref[0] = sbk.astype(BF16)
    sbv = seg(C_SBV, C_GATE)
    sbv_ref[0] = sbv
    sbv16_ref[0] = sbv.astype(BF16)
    gates_ref[0] = jax.nn.sigmoid(seg(C_GATE, C_KR)).astype(BF16)


def _in_proj(x, mod, wts, cos_t, sin_t, tm):
    b, s, _ = x.shape
    ns = s // tm
    tok = lambda w: pl.BlockSpec((1, tm, w), lambda i, j: (i, j, 0))
    full = lambda a: pl.BlockSpec(a.shape, lambda i, j: (0,) * a.ndim)
    out_widths = [(QP_WIDTH, BF16), (QP_WIDTH, BF16), (SB_WIDTH, BF16), (SB_WIDTH, BF16), (SB_WIDTH, BF16),
                  (SB_WIDTH, BF16), (2 * D_MODEL, BF16), (MLA_KV_LORA, F32), (MLA_ROPE, F32),
                  (SB_WIDTH, F32), (SB_WIDTH, F32)]
    return pl.pallas_call(
        _in_kernel,
        grid=(b, ns),
        in_specs=[tok(D_MODEL),
                  pl.BlockSpec((1, 6, D_MODEL), lambda i, j: (i, 0, 0)),
                  full(wts["g1"]), full(wts["w_in"]), full(wts["gq"]), full(wts["w_uq"]),
                  full(wts["gkv"]), full(wts["w_ukv"]),
                  pl.BlockSpec((tm, LANES), lambda i, j: (j, 0)),
                  pl.BlockSpec((tm, LANES), lambda i, j: (j, 0))],
        out_specs=[tok(w) for w, _ in out_widths],
        out_shape=[jax.ShapeDtypeStruct((b, s, w), dt) for w, dt in out_widths],
        compiler_params=pltpu.CompilerParams(dimension_semantics=("parallel", "parallel"),
                                             vmem_limit_bytes=VMEM_LIMIT),
        name="in_proj",
    )(x, mod, wts["g1"], wts["w_in"], wts["gq"], wts["w_uq"], wts["gkv"], wts["w_ukv"], cos_t, sin_t)


def _kvup_kernel(ckv_ref, kr_ref, wukv_ref, place_ref, kmla_ref, vmla_ref):
    kv = _dot(ckv_ref[0].astype(BF16), wukv_ref[...])
    kr = _dot(kr_ref[0].astype(BF16), place_ref[...])
    kmla_ref[0] = (kv[:, :QP_WIDTH] + kr).astype(BF16)
    vmla_ref[0] = kv[:, QP_WIDTH:].astype(BF16)


def _kv_up(ckv, krope, w_ukv_r, place, tm):
    b, p, _ = ckv.shape
    return pl.pallas_call(
        _kvup_kernel,
        grid=(b, p // tm),
        in_specs=[pl.BlockSpec((1, tm, MLA_KV_LORA), lambda i, j: (i, j, 0)),
                  pl.BlockSpec((1, tm, MLA_ROPE), lambda i, j: (i, j, 0)),
                  pl.BlockSpec(w_ukv_r.shape, lambda i, j: (0, 0)),
                  pl.BlockSpec(place.shape, lambda i, j: (0, 0))],
        out_specs=[pl.BlockSpec((1, tm, QP_WIDTH), lambda i, j: (i, j, 0)),
                   pl.BlockSpec((1, tm, SB_WIDTH), lambda i, j: (i, j, 0))],
        out_shape=[jax.ShapeDtypeStruct((b, p, QP_WIDTH), BF16), jax.ShapeDtypeStruct((b, p, SB_WIDTH), BF16)],
        compiler_params=pltpu.CompilerParams(dimension_semantics=("parallel", "parallel")),
        name="kv_up",
    )(ckv, krope, w_ukv_r, place)


def _tri(n):
    r = lax.broadcasted_iota(jnp.int32, (n, n), 0)
    c = lax.broadcasted_iota(jnp.int32, (n, n), 1)
    return jnp.where(r > c, 1.0, 0.0).astype(BF16)


def _stick_terms(z):
    log_sig = jnp.minimum(z, 0.0) - jnp.log2(1.0 + jnp.exp2(-jnp.abs(z)))
    return log_sig, log_sig - z


def _split_bf16(x):
    hi = x.astype(BF16)
    return hi, (x - hi.astype(F32)).astype(BF16)


def _finish_mixer(o_mla, o_sb, gates_ref, x_ref, mod_ref, wmu_ref, wsu_ref, wo_ref, g2_ref, x1_ref, h2_ref):
    u_mla = _dot(o_mla.astype(BF16), wmu_ref[...])
    u_sb = _dot(o_sb.astype(BF16), wsu_ref[...])
    gates = gates_ref[0]
    merged = gates[:, :D_MODEL].astype(F32) * u_mla + gates[:, D_MODEL:].astype(F32) * u_sb
    mix = _dot(merged.astype(BF16), wo_ref[...])
    mod = mod_ref[0]
    x1 = x_ref[0] + mod[2:3] * mix
    x1_ref[0] = x1
    h2 = _rms(x1) * g2_ref[...] * (1.0 + mod[4:5]) + mod[3:4]
    h2_ref[0] = _pack_halves(h2[:, :HALF], h2[:, HALF:])


def _prompt_attn_kernel(qp_ref, kmla_ref, vmla_ref, sbq_ref, sbk_ref, sbv_ref,
                        gates_ref, x_ref, mod_ref, wmu_ref, wsu_ref, wo_ref, g2_ref, x1_ref, h2_ref,
                        m_ref, acc_ref, c_ref, sacc_ref, *, tq):
    i = pl.program_id(1)
    lane = lax.broadcasted_iota(jnp.int32, (1, LANES), 1)
    half_masks = (lane < MLA_V, lane >= MLA_V)
    unit_lane = [jnp.where(lane == DENOM_LANE[sub], 1.0, 0.0).astype(BF16) for sub in range(2)]
    row = lax.broadcasted_iota(jnp.int32, (tq, tq), 0)
    col = lax.broadcasted_iota(jnp.int32, (tq, tq), 1)
    chunk_mask = (col // CHUNK) <= (row // CHUNK)
    causal_mask = col < row
    piece = min(tq, MXU_TILE)
    tri_m = _tri(piece)
    st_diag = pl.multiple_of(i * tq, tq)

    def load(ref, start, c0):
        return ref[0, pl.ds(start, tq), c0:c0 + LANES]

    def mla_block(sub, q_h, k, v, mask):
        s = _dot_nt(q_h, k)
        if mask is not None:
            s = jnp.where(mask, s, -jnp.inf)
            m_new = jnp.max(s, axis=-1, keepdims=True)
            acc_ref[sub] = _dot(jnp.exp2(s - m_new).astype(BF16), v)
        else:
            m_old = m_ref[sub]
            m_new = jnp.maximum(m_old, jnp.max(s, axis=-1, keepdims=True))
            acc_ref[sub] = jnp.exp2(m_old - m_new) * acc_ref[sub] + _dot(jnp.exp2(s - m_new).astype(BF16), v)
        m_ref[sub] = m_new

    def sb_block(sub, q_h, k, v, mask):
        log_sig, log_keep = _stick_terms(_dot_nt(q_h, k))
        if mask is not None:
            log_keep = jnp.where(mask, log_keep, 0.0)
        summand = log_keep.astype(BF16)
        pieces = []
        total = None
        for b in reversed(range(tq // piece)):
            sl = slice(b * piece, (b + 1) * piece)
            inner = _dot(summand[:, sl], tri_m)
            piece_total = jnp.sum(log_keep[:, sl], axis=-1, keepdims=True)
            pieces.append(inner if total is None else inner + total)
            total = piece_total if total is None else total + piece_total
        after = jnp.concatenate(pieces[::-1], axis=1)
        if mask is not None:
            a = jnp.where(mask, jnp.exp2(log_sig + after), 0.0)
            sacc_ref[sub] = _dot(a.astype(BF16), v)
            c_ref[sub] = total
        else:
            c_old = c_ref[sub]
            a = jnp.exp2(log_sig + after + c_old)
            sacc_ref[sub] = sacc_ref[sub] + _dot(a.astype(BF16), v)
            c_ref[sub] = c_old + total

    o_mla = []
    o_sb = []
    for first_pair in range(0, MLA_HEADS // 2, PAIRS_PER_LOOP):
        pairs = range(first_pair, first_pair + PAIRS_PER_LOOP)
        q_m = {h: qp_ref[0, :, h * HEAD_PAD:(h + 1) * HEAD_PAD] for p in pairs for h in (2 * p, 2 * p + 1)}
        q_s = {2 * p + sub: jnp.where(half_masks[sub], sbq_ref[0, :, p * LANES:(p + 1) * LANES], 0)
               for p in pairs for sub in range(2)}

        def group_blocks(mla_start, sb_start, masks, pairs=pairs, q_m=q_m, q_s=q_s):
            for p in pairs:
                vcol = p * LANES
                v_m = load(vmla_ref, mla_start, vcol)
                k_s = load(sbk_ref, sb_start, vcol)
                v_s = load(sbv_ref, sb_start, vcol)
                for sub in range(2):
                    head = 2 * p + sub
                    slot = head - 2 * pairs[0]
                    keep = half_masks[sub]
                    mla_block(slot, q_m[head], load(kmla_ref, mla_start, head * HEAD_PAD),
                              jnp.where(keep, v_m, unit_lane[sub]), masks[0])
                    sb_block(slot, q_s[head], k_s, jnp.where(keep, v_s, 0), masks[1])

        group_blocks(st_diag, st_diag, (chunk_mask, causal_mask))

        def step(t, _, group_blocks=group_blocks):
            group_blocks(pl.multiple_of(t * tq, tq), pl.multiple_of((i - 1 - t) * tq, tq), (None, None))
            return 0
        lax.fori_loop(0, i, step, 0)
        for p in range(PAIRS_PER_LOOP):
            heads_out = []
            for sub in range(2):
                acc = acc_ref[2 * p + sub]
                denom = acc[:, DENOM_LANE[sub]:DENOM_LANE[sub] + 1]
                heads_out.append(jnp.where(half_masks[sub], acc, 0.0) / denom)
            o_mla.append(heads_out[0] + heads_out[1])
            o_sb.append(sacc_ref[2 * p] + sacc_ref[2 * p + 1])

    _finish_mixer(jnp.concatenate(o_mla, axis=1), jnp.concatenate(o_sb, axis=1),
                  gates_ref, x_ref, mod_ref, wmu_ref, wsu_ref, wo_ref, g2_ref, x1_ref, h2_ref)


def _decode_attn_kernel(qp_ref, kmla_ref, vmla_ref, sbq_ref, sbk_ref, sbv_ref, pkmla_ref, pvmla_ref, psbk_ref,
                        psbv_ref, gates_ref, x_ref, mod_ref, wmu_ref, wsu_ref, wo_ref, g2_ref, x1_ref, h2_ref,
                        *, tq, past_len, past_blk):
    n_past = past_len // past_blk
    lane = lax.broadcasted_iota(jnp.int32, (1, LANES), 1)
    half_masks = (lane < MLA_V, lane >= MLA_V)
    row = lax.broadcasted_iota(jnp.int32, (tq, tq), 0)
    col = lax.broadcasted_iota(jnp.int32, (tq, tq), 1)
    chunk_mask = ((past_len + col) // CHUNK) <= ((past_len + row) // CHUNK)
    causal_mask = col < row
    tri_new = _tri(tq)
    tri_past = _tri(past_blk)

    o_mla = []
    o_sb = []
    for pair in range(MLA_HEADS // 2):
        vcol = pair * LANES
        v_new = vmla_ref[0, :, vcol:vcol + LANES]
        v_past = pvmla_ref[0, :, vcol:vcol + LANES]
        sk_new = sbk_ref[0, :, vcol:vcol + LANES]
        sv_new = sbv_ref[0, :, vcol:vcol + LANES]
        sk_past = psbk_ref[0, :, vcol:vcol + LANES].astype(BF16)
        sv_past = psbv_ref[0, :, vcol:vcol + LANES].astype(BF16)
        mla_pair = None
        sb_pair = None
        for sub in range(2):
            keep = half_masks[sub]
            kcol = (2 * pair + sub) * HEAD_PAD

            q_h = qp_ref[0, :, kcol:kcol + HEAD_PAD]
            s_past = _dot_nt(q_h, pkmla_ref[0, :, kcol:kcol + HEAD_PAD])
            s_new = jnp.where(chunk_mask, _dot_nt(q_h, kmla_ref[0, :, kcol:kcol + HEAD_PAD]), -jnp.inf)
            m = jnp.maximum(jnp.max(s_past, axis=-1, keepdims=True), jnp.max(s_new, axis=-1, keepdims=True))
            p_past = jnp.exp2(s_past - m)
            p_new = jnp.exp2(s_new - m)
            denom = jnp.sum(p_past, axis=-1, keepdims=True) + jnp.sum(p_new, axis=-1, keepdims=True)
            o = (_dot(p_past.astype(BF16), jnp.where(keep, v_past, 0))
                 + _dot(p_new.astype(BF16), jnp.where(keep, v_new, 0))) / denom
            mla_pair = o if mla_pair is None else mla_pair + o

            q_s = jnp.where(keep, sbq_ref[0, :, vcol:vcol + LANES], 0)
            ls_new, lk_new = _stick_terms(_dot_nt(q_s, sk_new))
            lk_new = jnp.where(causal_mask, lk_new, 0.0)
            hi, lo = _split_bf16(lk_new)
            a_new = jnp.where(causal_mask, jnp.exp2(ls_new + _dot(hi, tri_new) + _dot(lo, tri_new)), 0.0)
            acc = _dot(a_new.astype(BF16), jnp.where(keep, sv_new, 0))
            later = jnp.sum(lk_new, axis=-1, keepdims=True)

            ls_past, lk_past = _stick_terms(_dot_nt(q_s, sk_past))
            hi, lo = _split_bf16(lk_past)
            blocks = lambda a: [a[:, b * past_blk:(b + 1) * past_blk] for b in range(n_past)]
            stacked = jnp.concatenate(blocks(hi) + blocks(lo), axis=0)
            within = _dot(stacked, tri_past)
            after = []
            for b in reversed(range(n_past)):
                after.append(within[b * tq:(b + 1) * tq] + within[(n_past + b) * tq:(n_past + b + 1) * tq] + later)
                later = later + jnp.sum(lk_past[:, b * past_blk:(b + 1) * past_blk], axis=-1, keepdims=True)
            a_past = jnp.exp2(ls_past + jnp.concatenate(after[::-1], axis=1))
            acc = acc + _dot(a_past.astype(BF16), jnp.where(keep, sv_past, 0))
            sb_pair = acc if sb_pair is None else sb_pair + acc
        o_mla.append(mla_pair)
        o_sb.append(sb_pair)

    _finish_mixer(jnp.concatenate(o_mla, axis=1), jnp.concatenate(o_sb, axis=1),
                  gates_ref, x_ref, mod_ref, wmu_ref, wsu_ref, wo_ref, g2_ref, x1_ref, h2_ref)


def _mixer_call(kernel_fn, name, tq, args_kv, gates, x, mod, wts, scratch, batch0=0, n_batch=None):
    s = x.shape[1]
    b = x.shape[0] if n_batch is None else n_batch
    tok = lambda w: pl.BlockSpec((1, tq, w), lambda i, j: (i + batch0, j, 0))
    seq = lambda a: pl.BlockSpec((1,) + a.shape[1:], lambda i, j: (i + batch0, 0, 0))
    full = lambda a: pl.BlockSpec(a.shape, lambda i, j: (0,) * a.ndim)
    out = lambda w: pl.BlockSpec((1, tq, w), lambda i, j: (i, j, 0))
    args = list(args_kv) + [gates, x, mod, wts["w_mla_up"], wts["w_sb_up"], wts["w_out"], wts["g2"]]
    specs = [seq(a) if whole else tok(a.shape[-1]) for a, whole in
             zip(args_kv, (False, True, True, False) + (True,) * (len(args_kv) - 4))]
    specs += [tok(2 * D_MODEL), tok(D_MODEL), pl.BlockSpec((1, 6, D_MODEL), lambda i, j: (i + batch0, 0, 0)),
              full(wts["w_mla_up"]), full(wts["w_sb_up"]), full(wts["w_out"]), full(wts["g2"])]
    return pl.pallas_call(
        kernel_fn,
        grid=(b, s // tq),
        in_specs=specs,
        out_specs=[out(D_MODEL), out(HALF)],
        out_shape=[jax.ShapeDtypeStruct((b, s, D_MODEL), F32), jax.ShapeDtypeStruct((b, s, HALF), jnp.int32)],
        scratch_shapes=scratch,
        compiler_params=pltpu.CompilerParams(dimension_semantics=("parallel", "arbitrary"),
                                             vmem_limit_bytes=VMEM_LIMIT),
        name=name,
    )(*args)


def _prompt_attention(qp, kmla, vmla, sbq, sbk16, sbv16, gates, x, mod, wts, tq, batch0, n_batch):
    col = lambda: pltpu.VMEM((2 * PAIRS_PER_LOOP, tq, 1), F32)
    wide = lambda: pltpu.VMEM((2 * PAIRS_PER_LOOP, tq, LANES), F32)
    return _mixer_call(functools.partial(_prompt_attn_kernel, tq=tq), "attention", tq,
                       (qp, kmla, vmla, sbq, sbk16, sbv16), gates, x, mod, wts,
                       [col(), wide(), col(), wide()], batch0, n_batch)


def _decode_attention(qp, kmla, vmla, sbq, sbk16, sbv16, past, gates, x, mod, wts, past_blk):
    tq = x.shape[1]
    past_len = past[0].shape[1]
    kern = functools.partial(_decode_attn_kernel, tq=tq, past_len=past_len, past_blk=past_blk)
    return _mixer_call(kern, "decode_attention", tq, (qp, kmla, vmla, sbq, sbk16, sbv16) + tuple(past),
                       gates, x, mod, wts, [])


def _route_kernel(h2_ref, wr_ref, br_ref, idx_ref, wt_ref, rank_ref, cnt_ref, seen_ref):
    lo, hi = _unpack_halves(h2_ref[...])
    tm = lo.shape[0]
    logits = _dot_nt(wr_ref[:, :HALF], lo.astype(BF16)) + _dot_nt(wr_ref[:, HALF:], hi.astype(BF16))
    scores = jax.nn.sigmoid(logits)
    sel = scores + br_ref[...]
    neg = -jnp.inf

    grp = sel.reshape(N_GROUPS, GROUP_SIZE, tm)
    within = lax.broadcasted_iota(jnp.int32, grp.shape, 1)
    top1 = jnp.max(grp, axis=1, keepdims=True)
    first = jnp.min(jnp.where(grp == top1, within, GROUP_SIZE), axis=1, keepdims=True)
    top2 = jnp.max(jnp.where(within == first, neg, grp), axis=1, keepdims=True)
    gscore = (top1 + top2).reshape(N_GROUPS, tm)

    gid = lax.broadcasted_iota(jnp.int32, gscore.shape, 0)
    chosen = jnp.zeros(gscore.shape, jnp.bool_)
    for _ in range(TOPK_GROUPS):
        best = jnp.max(gscore, axis=0, keepdims=True)
        pick = jnp.min(jnp.where(gscore == best, gid, N_GROUPS), axis=0, keepdims=True)
        hit = gid == pick
        chosen = jnp.logical_or(chosen, hit)
        gscore = jnp.where(hit, neg, gscore)
    chosen3 = jnp.broadcast_to(chosen.reshape(N_GROUPS, 1, tm), grp.shape)
    cand = jnp.where(chosen3, grp, neg).reshape(N_EXPERTS, tm)
    outside = jnp.where(cand == neg, 1.0, 0.0)

    eid = lax.broadcasted_iota(jnp.int32, cand.shape, 0)
    picks = []
    weights = []
    for _ in range(TOP_K):
        best = jnp.max(cand, axis=0, keepdims=True)
        pick = jnp.min(jnp.where(cand == best, eid, N_EXPERTS), axis=0, keepdims=True)
        hit = eid == pick
        weights.append(jnp.sum(jnp.where(hit, scores, 0.0), axis=0, keepdims=True))
        picks.append(pick)
        cand = jnp.where(hit, neg, cand)
    w = jnp.concatenate(weights, axis=0)
    idx_ref[...] = jnp.concatenate(picks, axis=0)
    wt_ref[...] = w / (jnp.sum(w, axis=0, keepdims=True) + 1e-20) * ROUTED_SCALE

    @pl.when(pl.program_id(0) == 0)
    def _():
        seen_ref[...] = jnp.zeros_like(seen_ref)

    onehot = jnp.where(cand == neg, 1.0, 0.0) - outside
    src = lax.broadcasted_iota(jnp.int32, (tm, tm), 0)
    dst = lax.broadcasted_iota(jnp.int32, (tm, tm), 1)
    earlier = jnp.where(src < dst, 1.0, 0.0).astype(BF16)
    before = _dot(onehot.astype(BF16), earlier) + seen_ref[...]
    rank_ref[...] = jnp.concatenate(
        [jnp.sum(jnp.where(eid == pick, before, 0.0), axis=0, keepdims=True) for pick in picks],
        axis=0).astype(jnp.int32)
    seen = seen_ref[...] + jnp.sum(onehot, axis=1, keepdims=True)
    seen_ref[...] = seen
    cnt_ref[...] = seen.astype(jnp.int32)


def _route(h2p, wr_t, br, tm):
    t = h2p.shape[0]
    kt = lambda: pl.BlockSpec((TOP_K, tm), lambda i: (0, i))
    return pl.pallas_call(
        _route_kernel,
        grid=(t // tm,),
        in_specs=[pl.BlockSpec((tm, HALF), lambda i: (i, 0)),
                  pl.BlockSpec(wr_t.shape, lambda i: (0, 0)),
                  pl.BlockSpec(br.shape, lambda i: (0, 0))],
        out_specs=[kt(), kt(), kt(), pl.BlockSpec((N_EXPERTS, 1), lambda i: (0, 0))],
        out_shape=[jax.ShapeDtypeStruct((TOP_K, t), jnp.int32), jax.ShapeDtypeStruct((TOP_K, t), F32),
                   jax.ShapeDtypeStruct((TOP_K, t), jnp.int32), jax.ShapeDtypeStruct((N_EXPERTS, 1), jnp.int32)],
        scratch_shapes=[pltpu.VMEM((N_EXPERTS, 1), F32)],
        compiler_params=pltpu.CompilerParams(dimension_semantics=("arbitrary",)),
        name="route",
    )(h2p, wr_t, br)


def _position_kernel(idx_ref, rank_ref, start_ref, pos_ref):
    idx = idx_ref[...]
    eid = lax.broadcasted_iota(jnp.int32, (N_EXPERTS, idx.shape[1]), 0)
    start = start_ref[...]
    base = jnp.concatenate(
        [jnp.sum(jnp.where(eid == idx[k:k + 1, :], start, 0.0), axis=0, keepdims=True) for k in range(TOP_K)],
        axis=0)
    pos_ref[...] = base.astype(jnp.int32) + rank_ref[...]


def _positions(idx_kt, rank_kt, pad_start, tm):
    t = idx_kt.shape[1]
    tm = max(m for m in range(tm, 8 * tm + 1, tm) if t % m == 0)
    kt = lambda: pl.BlockSpec((TOP_K, tm), lambda i: (0, i))
    return pl.pallas_call(
        _position_kernel,
        grid=(t // tm,),
        in_specs=[kt(), kt(), pl.BlockSpec((N_EXPERTS, 1), lambda i: (0, 0))],
        out_specs=kt(),
        out_shape=jax.ShapeDtypeStruct((TOP_K, t), jnp.int32),
        compiler_params=pltpu.CompilerParams(dimension_semantics=("parallel",)),
        name="positions",
    )(idx_kt, rank_kt, pad_start.astype(F32).reshape(N_EXPERTS, 1))


def _sc_mesh():
    return plsc.VectorSubcoreMesh(core_axis_name="c", subcore_axis_name="s",
                                  num_cores=SC_CORES, num_subcores=SC_SUBCORES)


def _sc_scatter_rows(rows, pos, n_out, chunk):
    t, width = rows.shape
    copies = pos.shape[0] // t
    per_worker = t // SC_WORKERS
    n_chunks = per_worker // chunk
    tail = per_worker - n_chunks * chunk
    assert per_worker * SC_WORKERS == t and tail % 8 == 0
    tail_rows = max(tail, 8)

    @functools.partial(
        pl.kernel, mesh=_sc_mesh(),
        out_type=jax.ShapeDtypeStruct((n_out, width), rows.dtype),
        scratch_types=[pltpu.VMEM((copies, chunk), jnp.int32), pltpu.VMEM((chunk, width), rows.dtype),
                       pltpu.VMEM((copies, tail_rows), jnp.int32), pltpu.VMEM((tail_rows, width), rows.dtype),
                       pltpu.SemaphoreType.DMA, pltpu.SemaphoreType.DMA],
    )
    def scatter(rows_hbm, pos_hbm, out_hbm, idx_v, rows_v, idx_t, rows_t, load_sem, store_sem):
        wid = lax.axis_index("s") * SC_CORES + lax.axis_index("c")
        base = wid * per_worker

        def move(off, n, idx_buf, row_buf):
            loads = [pltpu.async_copy(rows_hbm.at[pl.ds(off, n)], row_buf, load_sem)]
            for k in range(copies):
                src = pos_hbm.at[pl.ds(pl.multiple_of(k * t + off, 8), n)]
                loads.append(pltpu.async_copy(src, idx_buf.at[k], load_sem))
            for cp in loads:
                cp.wait()
            stores = [pltpu.async_copy(row_buf, out_hbm.at[idx_buf.at[k]], store_sem) for k in range(copies)]
            for cp in stores:
                cp.wait()

        @pl.loop(0, n_chunks)
        def _(c):
            move(pl.multiple_of(base + c * chunk, 8), chunk, idx_v, rows_v)

        if tail:
            move(pl.multiple_of(base + n_chunks * chunk, 8), tail, idx_t, rows_t)

    return scatter(rows, pos)


def _sc_gather_rows(table, idx, chunk):
    n_rows = idx.shape[0]
    width = table.shape[1]
    per_worker = n_rows // SC_WORKERS
    n_chunks = per_worker // chunk
    assert per_worker * SC_WORKERS == n_rows and n_chunks * chunk == per_worker
    mesh = _sc_mesh()

    @functools.partial(
        pl.kernel, mesh=mesh,
        out_type=jax.ShapeDtypeStruct((n_rows, width), table.dtype),
        scratch_types=[pltpu.VMEM((chunk,), jnp.int32), pltpu.VMEM((chunk, width), table.dtype),
                       pltpu.SemaphoreType.DMA],
    )
    def gather(table_hbm, idx_hbm, out_hbm, idx_v, rows_v, sem):
        wid = lax.axis_index("s") * SC_CORES + lax.axis_index("c")
        base = wid * per_worker

        @pl.loop(0, n_chunks)
        def _(c):
            off = pl.multiple_of(base + c * chunk, 8)
            pltpu.sync_copy(idx_hbm.at[pl.ds(off, chunk)], idx_v)
            pltpu.async_copy(table_hbm.at[idx_v], rows_v, sem).wait()
            pltpu.sync_copy(rows_v, out_hbm.at[pl.ds(off, chunk)])

    return gather(table, idx)


ROW_SLOTS = 3
WEIGHT_SLOTS = 2


def _expert_kernel(be_ref, nv_ref, nu_ref, nxt_ref, xs_hbm, w1_hbm, w3_hbm, w2_hbm, os_ref,
                   wb1, wb3, wb2, xbuf, wf1, wf3, wf2, slot_ref, xsem, wsem, *, blk):
    i = pl.program_id(0)
    n_used = nu_ref[0]

    def rows_copy(j):
        src = xs_hbm.at[pl.ds(pl.multiple_of(j * blk, blk), blk)]
        return pltpu.make_async_copy(src, xbuf.at[j % ROW_SLOTS], xsem.at[j % ROW_SLOTS])

    def weight_copies(e, slot):
        return [pltpu.make_async_copy(w_hbm.at[e], buf.at[slot], wsem.at[slot])
                for w_hbm, buf in ((w1_hbm, wf1), (w3_hbm, wf3), (w2_hbm, wf2))]

    @pl.when(i == 0)
    def _():
        rows_copy(0).start()

        @pl.when(n_used > 1)
        def _():
            rows_copy(1).start()
        for cp in weight_copies(be_ref[0], 0):
            cp.start()
        slot_ref[0] = 1

    @pl.when(i + 2 < n_used)
    def _():
        rows_copy(i + 2).start()

    @pl.when(i < n_used)
    def _():
        @pl.when(jnp.logical_or(i == 0, be_ref[i] != be_ref[jnp.maximum(i - 1, 0)]))
        def _():
            slot = 1 - slot_ref[0]
            slot_ref[0] = slot
            for cp in weight_copies(be_ref[i], slot):
                cp.wait()

            @pl.when(nxt_ref[i] >= 0)
            def _():
                for cp in weight_copies(nxt_ref[i], 1 - slot):
                    cp.start()
            wb1[...] = wf1[slot].astype(BF16)
            wb3[...] = wf3[slot].astype(BF16)
            wb2[...] = wf2[slot].astype(BF16)

        rows_copy(i).wait()
        packed = xbuf[i % ROW_SLOTS]
        live = lax.broadcasted_iota(jnp.int32, packed.shape, 0) < nv_ref[i]
        lo, hi = _unpack_halves(jnp.where(live, packed, 0))
        lo = lo.astype(BF16)
        hi = hi.astype(BF16)
        a = _dot(lo, wb1[:HALF, :]) + _dot(hi, wb1[HALF:, :])
        b = _dot(lo, wb3[:HALF, :]) + _dot(hi, wb3[HALF:, :])
        o = _dot((_silu(a) * b).astype(BF16), wb2[...])
        os_ref[...] = _pack_halves(o[:, :HALF], o[:, HALF:])


def _experts(xs, blk_exp, blk_valid, n_used, next_exp, w1, w3, w2, blk):
    rows = xs.shape[0]
    nb = rows // blk
    any_space = pl.BlockSpec(memory_space=pl.ANY)
    grid_spec = pltpu.PrefetchScalarGridSpec(
        num_scalar_prefetch=4,
        grid=(nb,),
        in_specs=[any_space, any_space, any_space, any_space],
        out_specs=pl.BlockSpec((blk, HALF), lambda i, be, nv, nu, nx: (jnp.minimum(i, nu[0] - 1), 0)),
        scratch_shapes=[pltpu.VMEM((D_MODEL, EXPERT_DIM), BF16), pltpu.VMEM((D_MODEL, EXPERT_DIM), BF16),
                        pltpu.VMEM((EXPERT_DIM, D_MODEL), BF16),
                        pltpu.VMEM((ROW_SLOTS, blk, HALF), jnp.int32),
                        pltpu.VMEM((WEIGHT_SLOTS, D_MODEL, EXPERT_DIM), F32),
                        pltpu.VMEM((WEIGHT_SLOTS, D_MODEL, EXPERT_DIM), F32),
                        pltpu.VMEM((WEIGHT_SLOTS, EXPERT_DIM, D_MODEL), F32),
                        pltpu.SMEM((1,), jnp.int32),
                        pltpu.SemaphoreType.DMA((ROW_SLOTS,)), pltpu.SemaphoreType.DMA((WEIGHT_SLOTS,))],
    )
    return pl.pallas_call(
        functools.partial(_expert_kernel, blk=blk),
        grid_spec=grid_spec,
        out_shape=jax.ShapeDtypeStruct((rows, HALF), jnp.int32),
        compiler_params=pltpu.CompilerParams(dimension_semantics=("arbitrary",), vmem_limit_bytes=VMEM_LIMIT),
        name="experts",
    )(blk_exp, blk_valid, n_used, next_exp, xs, w1, w3, w2)


def _combine_kernel(g_ref, wt_ref, h2_ref, x1_ref, mod_ref, sw1_ref, sw3_ref, sw2_ref, gf_ref, y_ref):
    wt = wt_ref[...]
    lo_acc = None
    for k in range(TOP_K):
        lo, hi = _unpack_halves(g_ref[k])
        wk = wt[:, k:k + 1]
        lo_acc = wk * lo if lo_acc is None else lo_acc + wk * lo
        hi_acc = wk * hi if k == 0 else hi_acc + wk * hi
    routed = jnp.concatenate([lo_acc, hi_acc], axis=1)
    lo, hi = _unpack_halves(h2_ref[...])
    lo = lo.astype(BF16)
    hi = hi.astype(BF16)
    a = _dot(lo, sw1_ref[:HALF, :]) + _dot(hi, sw1_ref[HALF:, :])
    b = _dot(lo, sw3_ref[:HALF, :]) + _dot(hi, sw3_ref[HALF:, :])
    shared = _dot((_silu(a) * b).astype(BF16), sw2_ref[...])
    mod = mod_ref[0]
    x2 = x1_ref[...] + mod[5:6] * (routed + shared)
    y_ref[...] = _rms(x2) * gf_ref[...]


def _combine_into_kernel(g_ref, wt_ref, h2_ref, x1_ref, mod_ref, sw1_ref, sw3_ref, sw2_ref, gf_ref, prev_ref, y_ref):
    del prev_ref
    _combine_kernel(g_ref, wt_ref, h2_ref, x1_ref, mod_ref, sw1_ref, sw3_ref, sw2_ref, gf_ref, y_ref)


def _combine(gathered, wt, h2p, x1, mod, wts, tm, src_tile0, mod_batch0, tiles_per_batch, out_rows, out_tile0,
             y_prev=None):
    t = x1.shape[0]
    full = lambda a: pl.BlockSpec(a.shape, lambda i: (0,) * a.ndim)
    shifted = lambda w: pl.BlockSpec((tm, w), lambda i: (i + src_tile0, 0))
    args = [gathered, wt, h2p, x1, mod, wts["sw1"], wts["sw3"], wts["sw2"], wts["gf"]]
    specs = [pl.BlockSpec((TOP_K, tm, HALF), lambda i: (0, i + src_tile0, 0)), shifted(TOP_K), shifted(HALF),
             pl.BlockSpec((tm, D_MODEL), lambda i: (i, 0)),
             pl.BlockSpec((1, 6, D_MODEL), lambda i: (i // tiles_per_batch + mod_batch0, 0, 0)),
             full(wts["sw1"]), full(wts["sw3"]), full(wts["sw2"]), full(wts["gf"])]
    aliases = {}
    body = _combine_kernel
    if y_prev is not None:
        args.append(y_prev)
        specs.append(pl.BlockSpec(memory_space=pl.ANY))
        aliases = {len(args) - 1: 0}
        body = _combine_into_kernel
    return pl.pallas_call(
        body,
        grid=(t // tm,),
        in_specs=specs,
        out_specs=pl.BlockSpec((tm, D_MODEL), lambda i: (i + out_tile0, 0)),
        out_shape=jax.ShapeDtypeStruct((out_rows, D_MODEL), F32),
        input_output_aliases=aliases,
        compiler_params=pltpu.CompilerParams(dimension_semantics=("parallel",), vmem_limit_bytes=VMEM_LIMIT),
        name="combine",
    )(*args)


def _rope_tables(pos):
    half = MLA_ROPE // 2
    inv_freq = ROPE_THETA ** (-jnp.arange(half, dtype=F32) / half)
    ang = pos.astype(F32)[:, None] * inv_freq
    cos, sin = jnp.cos(ang), jnp.sin(ang)
    n = pos.shape[0]
    ones = jnp.ones((n, MLA_NOPE), F32)
    z_nope = jnp.zeros((n, MLA_NOPE), F32)
    z_pad = jnp.zeros((n, HEAD_PAD - MLA_NOPE - MLA_ROPE), F32)
    return (jnp.concatenate([ones, cos, cos, z_pad], axis=1),
            jnp.concatenate([z_nope, -sin, sin, z_pad], axis=1))


def _prep_weights(g_norm1, w_in, g_q_lat, w_uq, g_kv_lat, w_ukv, w_mla_up, w_sb_up, w_out, g_norm2,
                  w_router, b_router, shared_w1, shared_w3, shared_w2, g_final):
    w = w_in[0]
    o = 0
    parts = {}
    for name, width in (("qlat", MLA_Q_LORA), ("kv", MLA_KV_LORA), ("kr", MLA_ROPE), ("sbq", SB_WIDTH),
                        ("sbk", SB_WIDTH), ("sbv", SB_WIDTH), ("gm", D_MODEL), ("gs", D_MODEL)):
        parts[name] = w[:, o:o + width]
        o += width
    kr = parts["kr"]
    z_nope = jnp.zeros((D_MODEL, MLA_NOPE), F32)
    z_pad = jnp.zeros((D_MODEL, HEAD_PAD - MLA_NOPE - MLA_ROPE), F32)
    kr_seg = jnp.concatenate([z_nope, kr, z_pad], axis=1)
    w_in_r = jnp.concatenate([parts["qlat"], parts["kv"], parts["sbq"] * (SB_DIM ** -0.5 * LOG2_E), parts["sbk"],
                              parts["sbv"], parts["gm"], parts["gs"], kr_seg], axis=1).astype(BF16)

    scale = (MLA_NOPE + MLA_ROPE) ** -0.5 * LOG2_E
    uq = w_uq[0].reshape(MLA_Q_LORA, MLA_HEADS, MLA_NOPE + MLA_ROPE) * scale
    nope, rope = uq[..., :MLA_NOPE], uq[..., MLA_NOPE:]
    zq_pad = jnp.zeros((MLA_Q_LORA, MLA_HEADS, HEAD_PAD - MLA_NOPE - MLA_ROPE), F32)
    w_uq_r = jnp.concatenate([nope, rope, zq_pad], axis=-1).reshape(MLA_Q_LORA, QP_WIDTH).astype(BF16)

    ukv = w_ukv[0].reshape(MLA_KV_LORA, MLA_HEADS, MLA_NOPE + MLA_V)
    k_nope, v = ukv[..., :MLA_NOPE], ukv[..., MLA_NOPE:]
    k_pad = jnp.concatenate([k_nope, jnp.zeros((MLA_KV_LORA, MLA_HEADS, HEAD_PAD - MLA_NOPE), F32)], axis=-1)
    w_ukv_r = jnp.concatenate([k_pad.reshape(MLA_KV_LORA, QP_WIDTH),
                               v.reshape(MLA_KV_LORA, MLA_HEADS * MLA_V)], axis=1).astype(BF16)

    eye = jnp.eye(MLA_ROPE, dtype=F32)
    place_head = jnp.concatenate([jnp.zeros((MLA_ROPE, MLA_NOPE), F32), eye,
                                  jnp.zeros((MLA_ROPE, HEAD_PAD - MLA_NOPE - MLA_ROPE), F32)], axis=1)
    place = jnp.tile(place_head, (1, MLA_HEADS)).astype(BF16)

    return dict(
        g1=g_norm1[0].reshape(1, D_MODEL), w_in=w_in_r, gq=g_q_lat[0].reshape(1, MLA_Q_LORA), w_uq=w_uq_r,
        gkv=g_kv_lat[0].reshape(1, MLA_KV_LORA), w_ukv=w_ukv_r, place=place,
        w_mla_up=w_mla_up[0].astype(BF16), w_sb_up=w_sb_up[0].astype(BF16), w_out=w_out[0].astype(BF16),
        g2=g_norm2[0].reshape(1, D_MODEL), wr_t=w_router[0].T.astype(BF16),
        br=b_router[0].reshape(N_EXPERTS, 1),
        sw1=shared_w1[0].astype(BF16), sw3=shared_w3[0].astype(BF16), sw2=shared_w2[0].astype(BF16),
        gf=g_final.reshape(1, D_MODEL))


def _block_plan(counts, nb, blk):
    padded = (counts + blk - 1) // blk * blk
    pad_end = jnp.cumsum(padded)
    pad_start = pad_end - padded
    first_row = jnp.arange(nb, dtype=jnp.int32) * blk
    blk_exp = jnp.minimum(jnp.sum((pad_end[None, :] <= first_row[:, None]).astype(jnp.int32), axis=1),
                          N_EXPERTS - 1)
    own = blk_exp[:, None] == jnp.arange(N_EXPERTS, dtype=jnp.int32)[None, :]
    seg_end = jnp.sum(jnp.where(own, (pad_start + counts)[None, :], 0), axis=1)
    blk_valid = jnp.clip(seg_end - first_row, 0, blk).astype(jnp.int32)
    n_used = (pad_end[-1:] // blk).astype(jnp.int32)
    after_seg = jnp.sum(jnp.where(own, (pad_end // blk)[None, :], 0), axis=1)
    follower = jnp.sum(jnp.where(after_seg[:, None] == jnp.arange(nb, dtype=jnp.int32)[None, :],
                                 blk_exp[None, :], 0), axis=1)
    next_exp = jnp.where(after_seg < n_used[0], follower, -1).astype(jnp.int32)
    return pad_start, blk_exp.astype(jnp.int32), blk_valid, n_used, next_exp


def _moe_rows(h2p, idx_kt, rank_kt, counts, w1, w3, w2, scatter_rows, gather_rows, blk, tm):
    t = h2p.shape[0]
    nb = -(-t * TOP_K // blk) + N_EXPERTS
    pad_start, blk_exp, blk_valid, n_used, next_exp = _block_plan(counts.reshape(N_EXPERTS), nb, blk)
    pos = _positions(idx_kt, rank_kt, pad_start, tm).reshape(TOP_K * t)
    xs = scatter_rows(h2p, pos, nb * blk)
    os_ = _experts(xs, blk_exp, blk_valid, n_used, next_exp, w1, w3, w2, blk)
    return gather_rows(os_, pos).reshape(TOP_K, t, HALF)


def _forward(x_prompt, x_sample, cache_mla_ckv, cache_mla_krope, cache_sb_k, cache_sb_v, c_prompt, c_sample,
             w_ada, b_ada, moe_w1, moe_w3, moe_w2, wts, scatter_rows, gather_rows, token_block, attn_block,
             route_block, moe_block, wide_block):
    bp, sp, _ = x_prompt.shape
    bs, ss, _ = x_sample.shape
    past_len = cache_mla_ckv.shape[2]

    mod = _ada(jnp.concatenate([c_prompt, c_sample], axis=0), w_ada[0], b_ada[0]).reshape(bp + bs, 6, D_MODEL)
    mod_p, mod_s = mod[:bp], mod[bp:]

    cos_p, sin_p = _rope_tables(jnp.arange(sp))
    (qp, kmla, vmla, sbq, sbk16, sbv16, gates, ckv_p, krope_p, sbk_p, sbv_p) = _in_proj(
        x_prompt, mod_p, wts, cos_p, sin_p, wide_block)
    split_b = bp * GROUP_A_SHARE[0] // GROUP_A_SHARE[1]
    assert 0 < split_b < bp
    mixed = [_prompt_attention(qp, kmla, vmla, sbq, sbk16, sbv16, gates, x_prompt, mod_p, wts, attn_block,
                               b0, nb_) for b0, nb_ in ((0, split_b), (split_b, bp - split_b))]

    cos_s, sin_s = _rope_tables(past_len + jnp.arange(ss))
    (qs, kmla_s, vmla_s, sbq_s, sbk16_s, sbv16_s, gates_s, ckv_s, krope_s, sbk_s, sbv_s) = _in_proj(
        x_sample, mod_s, wts, cos_s, sin_s, ss)
    pkmla, pvmla = _kv_up(cache_mla_ckv[0], cache_mla_krope[0], wts["w_ukv"], wts["place"], wide_block)
    past = (pkmla, pvmla, cache_sb_k[0].reshape(bs, past_len, SB_WIDTH), cache_sb_v[0].reshape(bs, past_len, SB_WIDTH))
    x1_s, h2_s = _decode_attention(qs, kmla_s, vmla_s, sbq_s, sbk16_s, sbv16_s, past, gates_s, x_sample, mod_s,
                                   wts, token_block)

    tp, ts = bp * sp, bs * ss
    ta = split_b * sp
    tb = tp - ta

    def moe(h2_rows):
        idx_kt, wt_kt, rank_kt, counts = _route(h2_rows, wts["wr_t"], wts["br"], route_block)
        gathered = _moe_rows(h2_rows, idx_kt, rank_kt, counts, moe_w1[0], moe_w3[0], moe_w2[0],
                             scatter_rows, gather_rows, moe_block, route_block)
        return gathered, wt_kt.T

    tiles_per_batch = sp // wide_block
    (x1_a, h2_a), (x1_b, h2_b) = mixed
    h2_a = h2_a.reshape(ta, HALF)
    g_a, wt_a = moe(h2_a)
    y_p = _combine(g_a, wt_a, h2_a, x1_a.reshape(ta, D_MODEL), mod_p, wts, wide_block,
                   0, 0, tiles_per_batch, tp, 0)
    h2_b = jnp.concatenate([h2_b.reshape(tb, HALF), h2_s.reshape(ts, HALF)], axis=0)
    g_b, wt_b = moe(h2_b)
    y_p = _combine(g_b, wt_b, h2_b, x1_b.reshape(tb, D_MODEL), mod_p, wts, wide_block,
                   0, split_b, tiles_per_batch, tp, ta // wide_block, y_prev=y_p)
    y_s = _combine(g_b, wt_b, h2_b, x1_s.reshape(ts, D_MODEL), mod_s, wts, ss, tb // ss, 0, 1, ts, 0)

    heads = lambda a, b_, s_: a.reshape(1, b_, s_, SB_HEADS, SB_DIM)
    return (y_p.reshape(bp, sp, D_MODEL), y_s.reshape(bs, ss, D_MODEL),
            ckv_p[None], krope_p[None], heads(sbk_p, bp, sp), heads(sbv_p, bp, sp),
            ckv_s[None], krope_s[None], heads(sbk_s, bs, ss), heads(sbv_s, bs, ss))


def kernel(x_prompt, x_sample, cache_mla_ckv, cache_mla_krope, cache_sb_k, cache_sb_v, c_prompt, c_sample, w_ada, b_ada, g_norm1, w_in, g_q_lat, w_uq, g_kv_lat, w_ukv, w_mla_up, w_sb_up, w_out, g_norm2, w_router, b_router, moe_w1, moe_w3, moe_w2, shared_w1, shared_w3, shared_w2, g_final):
    wts = _prep_weights(g_norm1, w_in, g_q_lat, w_uq, g_kv_lat, w_ukv, w_mla_up, w_sb_up, w_out, g_norm2,
                        w_router, b_router, shared_w1, shared_w3, shared_w2, g_final)
    scatter_rows = functools.partial(_sc_scatter_rows, chunk=SC_CHUNK)
    gather_rows = functools.partial(_sc_gather_rows, chunk=SC_CHUNK)
    return _forward(x_prompt, x_sample, cache_mla_ckv, cache_mla_krope, cache_sb_k, cache_sb_v, c_prompt, c_sample,
                    w_ada, b_ada, moe_w1, moe_w3, moe_w2, wts, scatter_rows, gather_rows, TOKEN_BLOCK, ATTN_BLOCK,
                    TOKEN_BLOCK, MOE_BLOCK, WIDE_BLOCK)
```

```python
import functools

import jax
import jax.numpy as jnp
from jax import lax
from jax.experimental import pallas as pl
from jax.experimental.pallas import tpu as pltpu
from jax.experimental.pallas import tpu_sc as plsc

F32 = jnp.float32
BF16 = jnp.bfloat16

D_MODEL = 1024
NORM_EPS = 1e-6
CHUNK = 64
MLA_HEADS = 8
MLA_NOPE = 64
MLA_ROPE = 32
MLA_V = 64
MLA_Q_LORA = 384
MLA_KV_LORA = 256
ROPE_THETA = 10000.0
SB_HEADS = 8
SB_DIM = 64
SB_WIDTH = SB_HEADS * SB_DIM
N_EXPERTS = 256
TOP_K = 8
N_GROUPS = 8
TOPK_GROUPS = 4
GROUP_SIZE = N_EXPERTS // N_GROUPS
EXPERT_DIM = 256
ROUTED_SCALE = 2.5
LOG2_E = 1.4426950408889634

LANES = 128
MXU_TILE = 256
SC_CORES = 2
SC_SUBCORES = 16
SC_WORKERS = SC_CORES * SC_SUBCORES
VMEM_LIMIT = 60 * 1024 * 1024

HEAD_PAD = LANES
DENOM_LANE = (MLA_V, 0)
QP_WIDTH = MLA_HEADS * HEAD_PAD
HALF = D_MODEL // 2

C_QLAT = 0
C_KV = C_QLAT + MLA_Q_LORA
C_SBQ = C_KV + MLA_KV_LORA
C_SBK = C_SBQ + SB_WIDTH
C_SBV = C_SBK + SB_WIDTH
C_GATE = C_SBV + SB_WIDTH
C_KR = C_GATE + 2 * D_MODEL
C_END = C_KR + LANES

MOE_BLOCK = 512
WIDE_BLOCK = 512
TOKEN_BLOCK = 256
ATTN_BLOCK = 512
PAIRS_PER_LOOP = 1
SC_CHUNK = 64
GROUP_A_SHARE = (1, 2)


def _rms(x):
    return x * lax.rsqrt(jnp.mean(x * x, axis=-1, keepdims=True) + NORM_EPS)


def _silu(x):
    return x * jax.nn.sigmoid(x)


def _pack_halves(lo, hi):
    lo_bits = lax.bitcast_convert_type(lo.astype(BF16).astype(F32), jnp.uint32) >> 16
    hi_bits = lax.bitcast_convert_type(hi.astype(BF16).astype(F32), jnp.uint32) & jnp.uint32(0xFFFF0000)
    return lax.bitcast_convert_type(lo_bits | hi_bits, jnp.int32)


def _unpack_halves(p):
    u = lax.bitcast_convert_type(p, jnp.uint32)
    lo = lax.bitcast_convert_type(u << 16, F32)
    hi = lax.bitcast_convert_type(u & jnp.uint32(0xFFFF0000), F32)
    return lo, hi


def _dot(a, b):
    return jnp.dot(a, b, preferred_element_type=F32)


def _dot_nt(a, b):
    return lax.dot_general(a, b, (((1,), (1,)), ((), ())), preferred_element_type=F32)


def _ada_kernel(c_ref, w_ref, b_ref, o_ref):
    c = c_ref[...]
    o_ref[...] = _dot(_silu(c).astype(BF16), w_ref[...].astype(BF16)) + b_ref[...]


def _ada(c, w_ada, b_ada):
    n = c.shape[0]
    width = w_ada.shape[1]
    return pl.pallas_call(
        _ada_kernel,
        grid=(width // D_MODEL,),
        in_specs=[pl.BlockSpec((n, D_MODEL), lambda j: (0, 0)),
                  pl.BlockSpec((D_MODEL, D_MODEL), lambda j: (0, j)),
                  pl.BlockSpec((1, D_MODEL), lambda j: (0, j))],
        out_specs=pl.BlockSpec((n, D_MODEL), lambda j: (0, j)),
        out_shape=jax.ShapeDtypeStruct((n, width), F32),
        name="ada",
    )(c, w_ada, b_ada.reshape(1, width))


def _in_kernel(x_ref, mod_ref, g1_ref, win_ref, gq_ref, wuq_ref, gkv_ref, wukv_ref, cos_ref, sin_ref,
               qp_ref, kmla_ref, vmla_ref, sbq_ref, sbk16_ref, sbv16_ref, gates_ref,
               ckv_ref, krope_ref, sbk_ref, sbv_ref):
    x = x_ref[0]
    mod = mod_ref[0]
    h = _rms(x) * g1_ref[...] * (1.0 + mod[1:2]) + mod[0:1]
    hb = h.astype(BF16)

    def seg(a, b):
        return _dot(hb, win_ref[:, a:b])

    cos = cos_ref[...]
    sin = sin_ref[...]
    lane = lax.broadcasted_iota(jnp.int32, (1, LANES), 1)
    half = MLA_ROPE // 2

    def rotate(blk):
        other = jnp.where(lane < MLA_NOPE + half, pltpu.roll(blk, LANES - half, 1), pltpu.roll(blk, half, 1))
        return blk * cos + other * sin

    qn = (_rms(seg(C_QLAT, C_KV)) * gq_ref[...]).astype(BF16)
    q = _dot(qn, wuq_ref[...])
    qp_ref[0] = jnp.concatenate([rotate(q[:, h * HEAD_PAD:(h + 1) * HEAD_PAD]) for h in range(MLA_HEADS)],
                                axis=1).astype(BF16)

    ckv = _rms(seg(C_KV, C_SBQ)) * gkv_ref[...]
    ckv_ref[0] = ckv
    kv = _dot(ckv.astype(BF16), wukv_ref[...])
    krp = rotate(seg(C_KR, C_END))
    krope_ref[0] = krp[:, MLA_NOPE:MLA_NOPE + MLA_ROPE]
    kmla_ref[0] = (kv[:, :QP_WIDTH] + jnp.tile(krp, (1, MLA_HEADS))).astype(BF16)
    vmla_ref[0] = kv[:, QP_WIDTH:].astype(BF16)

    sbq_ref[0] = seg(C_SBQ, C_SBK).astype(BF16)
    sbk = seg(C_SBK, C_SBV)
    sbk_ref[0] = sbk
    sbk16_ref[0] = sbk.astype(BF16)
    sbv = seg(C_SBV, C_GATE)
    sbv_ref[0] = sbv
    sbv16_ref[0] = sbv.astype(BF16)
    gates_ref[0] = jax.nn.sigmoid(seg(C_GATE, C_KR)).astype(BF16)


def _in_proj(x, mod, wts, cos_t, sin_t, tm):
    b, s, _ = x.shape
    ns = s // tm
    tok = lambda w: pl.BlockSpec((1, tm, w), lambda i, j: (i, j, 0))
    full = lambda a: pl.BlockSpec(a.shape, lambda i, j: (0,) * a.ndim)
    out_widths = [(QP_WIDTH, BF16), (QP_WIDTH, BF16), (SB_WIDTH, BF16), (SB_WIDTH, BF16), (SB_WIDTH, BF16),
                  (SB_WIDTH, BF16), (2 * D_MODEL, BF16), (MLA_KV_LORA, F32), (MLA_ROPE, F32),
                  (SB_WIDTH, F32), (SB_WIDTH, F32)]
    return pl.pallas_call(
        _in_kernel,
        grid=(b, ns),
        in_specs=[tok(D_MODEL),
                  pl.BlockSpec((1, 6, D_MODEL), lambda i, j: (i, 0, 0)),
                  full(wts["g1"]), full(wts["w_in"]), full(wts["gq"]), full(wts["w_uq"]),
                  full(wts["gkv"]), full(wts["w_ukv"]),
                  pl.BlockSpec((tm, LANES), lambda i, j: (j, 0)),
                  pl.BlockSpec((tm, LANES), lambda i, j: (j, 0))],
        out_specs=[tok(w) for w, _ in out_widths],
        out_shape=[jax.ShapeDtypeStruct((b, s, w), dt) for w, dt in out_widths],
        compiler_params=pltpu.CompilerParams(dimension_semantics=("parallel", "parallel"),
                                             vmem_limit_bytes=VMEM_LIMIT),
        name="in_proj",
    )(x, mod, wts["g1"], wts["w_in"], wts["gq"], wts["w_uq"], wts["gkv"], wts["w_ukv"], cos_t, sin_t)


def _kvup_kernel(ckv_ref, kr_ref, wukv_ref, place_ref, kmla_ref, vmla_ref):
    kv = _dot(ckv_ref[0].astype(BF16), wukv_ref[...])
    kr = _dot(kr_ref[0].astype(BF16), place_ref[...])
    kmla_ref[0] = (kv[:, :QP_WIDTH] + kr).astype(BF16)
    vmla_ref[0] = kv[:, QP_WIDTH:].astype(BF16)


def _kv_up(ckv, krope, w_ukv_r, place, tm):
    b, p, _ = ckv.shape
    return pl.pallas_call(
        _kvup_kernel,
        grid=(b, p // tm),
        in_specs=[pl.BlockSpec((1, tm, MLA_KV_LORA), lambda i, j: (i, j, 0)),
                  pl.BlockSpec((1, tm, MLA_ROPE), lambda i, j: (i, j, 0)),
                  pl.BlockSpec(w_ukv_r.shape, lambda i, j: (0, 0)),
                  pl.BlockSpec(place.shape, lambda i, j: (0, 0))],
        out_specs=[pl.BlockSpec((1, tm, QP_WIDTH), lambda i, j: (i, j, 0)),
                   pl.BlockSpec((1, tm, SB_WIDTH), lambda i, j: (i, j, 0))],
        out_shape=[jax.ShapeDtypeStruct((b, p, QP_WIDTH), BF16), jax.ShapeDtypeStruct((b, p, SB_WIDTH), BF16)],
        compiler_params=pltpu.CompilerParams(dimension_semantics=("parallel", "parallel")),
        name="kv_up",
    )(ckv, krope, w_ukv_r, place)


def _tri(n):
    r = lax.broadcasted_iota(jnp.int32, (n, n), 0)
    c = lax.broadcasted_iota(jnp.int32, (n, n), 1)
    return jnp.where(r > c, 1.0, 0.0).astype(BF16)


def _stick_terms(z):
    log_sig = jnp.minimum(z, 0.0) - jnp.log2(1.0 + jnp.exp2(-jnp.abs(z)))
    return log_sig, log_sig - z


def _split_bf16(x):
    hi = x.astype(BF16)
    return hi, (x - hi.astype(F32)).astype(BF16)


def _finish_mixer(o_mla, o_sb, gates_ref, x_ref, mod_ref, wmu_ref, wsu_ref, wo_ref, g2_ref, x1_ref, h2_ref):
    u_mla = _dot(o_mla.astype(BF16), wmu_ref[...])
    u_sb = _dot(o_sb.astype(BF16), wsu_ref[...])
    gates = gates_ref[0]
    merged = gates[:, :D_MODEL].astype(F32) * u_mla + gates[:, D_MODEL:].astype(F32) * u_sb
    mix = _dot(merged.astype(BF16), wo_ref[...])
    mod = mod_ref[0]
    x1 = x_ref[0] + mod[2:3] * mix
    x1_ref[0] = x1
    h2 = _rms(x1) * g2_ref[...] * (1.0 + mod[4:5]) + mod[3:4]
    h2_ref[0] = _pack_halves(h2[:, :HALF], h2[:, HALF:])


def _prompt_attn_kernel(qp_ref, kmla_ref, vmla_ref, sbq_ref, sbk_ref, sbv_ref,
                        gates_ref, x_ref, mod_ref, wmu_ref, wsu_ref, wo_ref, g2_ref, x1_ref, h2_ref,
                        m_ref, acc_ref, c_ref, sacc_ref, *, tq):
    i = pl.program_id(1)
    lane = lax.broadcasted_iota(jnp.int32, (1, LANES), 1)
    half_masks = (lane < MLA_V, lane >= MLA_V)
    unit_lane = [jnp.where(lane == DENOM_LANE[sub], 1.0, 0.0).astype(BF16) for sub in range(2)]
    row = lax.broadcasted_iota(jnp.int32, (tq, tq), 0)
    col = lax.broadcasted_iota(jnp.int32, (tq, tq), 1)
    chunk_mask = (col // CHUNK) <= (row // CHUNK)
    causal_mask = col < row
    piece = min(tq, MXU_TILE)
    tri_m = _tri(piece)
    st_diag = pl.multiple_of(i * tq, tq)

    def load(ref, start, c0):
        return ref[0, pl.ds(start, tq), c0:c0 + LANES]

    def mla_block(sub, q_h, k, v, mask):
        s = _dot_nt(q_h, k)
        if mask is not None:
            s = jnp.where(mask, s, -jnp.inf)
            m_new = jnp.max(s, axis=-1, keepdims=True)
            acc_ref[sub] = _dot(jnp.exp2(s - m_new).astype(BF16), v)
        else:
            m_old = m_ref[sub]
            m_new = jnp.maximum(m_old, jnp.max(s, axis=-1, keepdims=True))
            acc_ref[sub] = jnp.exp2(m_old - m_new) * acc_ref[sub] + _dot(jnp.exp2(s - m_new).astype(BF16), v)
        m_ref[sub] = m_new

    def sb_block(sub, q_h, k, v, mask):
        log_sig, log_keep = _stick_terms(_dot_nt(q_h, k))
        if mask is not None:
            log_keep = jnp.where(mask, log_keep, 0.0)
        summand = log_keep.astype(BF16)
        pieces = []
        total = None
        for b in reversed(range(tq // piece)):
            sl = slice(b * piece, (b + 1) * piece)
            inner = _dot(summand[:, sl], tri_m)
            piece_total = jnp.sum(log_keep[:, sl], axis=-1, keepdims=True)
            pieces.append(inner if total is None else inner + total)
            total = piece_total if total is None else total + piece_total
        after = jnp.concatenate(pieces[::-1], axis=1)
        if mask is not None:
            a = jnp.where(mask, jnp.exp2(log_sig + after), 0.0)
            sacc_ref[sub] = _dot(a.astype(BF16), v)
            c_ref[sub] = total
        else:
            c_old = c_ref[sub]
            a = jnp.exp2(log_sig + after + c_old)
            sacc_ref[sub] = sacc_ref[sub] + _dot(a.astype(BF16), v)
            c_ref[sub] = c_old + total

    o_mla = []
    o_sb = []
    for first_pair in range(0, MLA_HEADS // 2, PAIRS_PER_LOOP):
        pairs = range(first_pair, first_pair + PAIRS_PER_LOOP)
        q_m = {h: qp_ref[0, :, h * HEAD_PAD:(h + 1) * HEAD_PAD] for p in pairs for h in (2 * p, 2 * p + 1)}
        q_s = {2 * p + sub: jnp.where(half_masks[sub], sbq_ref[0, :, p * LANES:(p + 1) * LANES], 0)
               for p in pairs for sub in range(2)}

        def group_blocks(mla_start, sb_start, masks, pairs=pairs, q_m=q_m, q_s=q_s):
            for p in pairs:
                vcol = p * LANES
                v_m = load(vmla_ref, mla_start, vcol)
                k_s = load(sbk_ref, sb_start, vcol)
                v_s = load(sbv_ref, sb_start, vcol)
                for sub in range(2):
                    head = 2 * p + sub
                    slot = head - 2 * pairs[0]
                    keep = half_masks[sub]
                    mla_block(slot, q_m[head], load(kmla_ref, mla_start, head * HEAD_PAD),
                              jnp.where(keep, v_m, unit_lane[sub]), masks[0])
                    sb_block(slot, q_s[head], k_s, jnp.where(keep, v_s, 0), masks[1])

        group_blocks(st_diag, st_diag, (chunk_mask, causal_mask))

        def step(t, _, group_blocks=group_blocks):
            group_blocks(pl.multiple_of(t * tq, tq), pl.multiple_of((i - 1 - t) * tq, tq), (None, None))
            return 0
        lax.fori_loop(0, i, step, 0)
        for p in range(PAIRS_PER_LOOP):
            heads_out = []
            for sub in range(2):
                acc = acc_ref[2 * p + sub]
                denom = acc[:, DENOM_LANE[sub]:DENOM_LANE[sub] + 1]
                heads_out.append(jnp.where(half_masks[sub], acc, 0.0) / denom)
            o_mla.append(heads_out[0] + heads_out[1])
            o_sb.append(sacc_ref[2 * p] + sacc_ref[2 * p + 1])

    _finish_mixer(jnp.concatenate(o_mla, axis=1), jnp.concatenate(o_sb, axis=1),
                  gates_ref, x_ref, mod_ref, wmu_ref, wsu_ref, wo_ref, g2_ref, x1_ref, h2_ref)


def _decode_attn_kernel(qp_ref, kmla_ref, vmla_ref, sbq_ref, sbk_ref, sbv_ref, pkmla_ref, pvmla_ref, psbk_ref,
                        psbv_ref, gates_ref, x_ref, mod_ref, wmu_ref, wsu_ref, wo_ref, g2_ref, x1_ref, h2_ref,
                        *, tq, past_len, past_blk):
    n_past = past_len // past_blk
    lane = lax.broadcasted_iota(jnp.int32, (1, LANES), 1)
    half_masks = (lane < MLA_V, lane >= MLA_V)
    row = lax.broadcasted_iota(jnp.int32, (tq, tq), 0)
    col = lax.broadcasted_iota(jnp.int32, (tq, tq), 1)
    chunk_mask = ((past_len + col) // CHUNK) <= ((past_len + row) // CHUNK)
    causal_mask = col < row
    tri_new = _tri(tq)
    tri_past = _tri(past_blk)

    o_mla = []
    o_sb = []
    for pair in range(MLA_HEADS // 2):
        vcol = pair * LANES
        v_new = vmla_ref[0, :, vcol:vcol + LANES]
        v_past = pvmla_ref[0, :, vcol:vcol + LANES]
        sk_new = sbk_ref[0, :, vcol:vcol + LANES]
        sv_new = sbv_ref[0, :, vcol:vcol + LANES]
        sk_past = psbk_ref[0, :, vcol:vcol + LANES].astype(BF16)
        sv_past = psbv_ref[0, :, vcol:vcol + LANES].astype(BF16)
        mla_pair = None
        sb_pair = None
        for sub in range(2):
            keep = half_masks[sub]
            kcol = (2 * pair + sub) * HEAD_PAD

            q_h = qp_ref[0, :, kcol:kcol + HEAD_PAD]
            s_past = _dot_nt(q_h, pkmla_ref[0, :, kcol:kcol + HEAD_PAD])
            s_new = jnp.where(chunk_mask, _dot_nt(q_h, kmla_ref[0, :, kcol:kcol + HEAD_PAD]), -jnp.inf)
            m = jnp.maximum(jnp.max(s_past, axis=-1, keepdims=True), jnp.max(s_new, axis=-1, keepdims=True))
            p_past = jnp.exp2(s_past - m)
            p_new = jnp.exp2(s_new - m)
            denom = jnp.sum(p_past, axis=-1, keepdims=True) + jnp.sum(p_new, axis=-1, keepdims=True)
            o = (_dot(p_past.astype(BF16), jnp.where(keep, v_past, 0))
                 + _dot(p_new.astype(BF16), jnp.where(keep, v_new, 0))) / denom
            mla_pair = o if mla_pair is None else mla_pair + o

            q_s = jnp.where(keep, sbq_ref[0, :, vcol:vcol + LANES], 0)
            ls_new, lk_new = _stick_terms(_dot_nt(q_s, sk_new))
            lk_new = jnp.where(causal_mask, lk_new, 0.0)
            hi, lo = _split_bf16(lk_new)
            a_new = jnp.where(causal_mask, jnp.exp2(ls_new + _dot(hi, tri_new) + _dot(lo, tri_new)), 0.0)
            acc = _dot(a_new.astype(BF16), jnp.where(keep, sv_new, 0))
            later = jnp.sum(lk_new, axis=-1, keepdims=True)

            ls_past, lk_past = _stick_terms(_dot_nt(q_s, sk_past))
            hi, lo = _split_bf16(lk_past)
            blocks = lambda a: [a[:, b * past_blk:(b + 1) * past_blk] for b in range(n_past)]
            stacked = jnp.concatenate(blocks(hi) + blocks(lo), axis=0)
            within = _dot(stacked, tri_past)
            after = []
            for b in reversed(range(n_past)):
                after.append(within[b * tq:(b + 1) * tq] + within[(n_past + b) * tq:(n_past + b + 1) * tq] + later)
                later = later + jnp.sum(lk_past[:, b * past_blk:(b + 1) * past_blk], axis=-1, keepdims=True)
            a_past = jnp.exp2(ls_past + jnp.concatenate(after[::-1], axis=1))
            acc = acc + _dot(a_past.astype(BF16), jnp.where(keep, sv_past, 0))
            sb_pair = acc if sb_pair is None else sb_pair + acc
        o_mla.append(mla_pair)
        o_sb.append(sb_pair)

    _finish_mixer(jnp.concatenate(o_mla, axis=1), jnp.concatenate(o_sb, axis=1),
                  gates_ref, x_ref, mod_ref, wmu_ref, wsu_ref, wo_ref, g2_ref, x1_ref, h2_ref)


def _mixer_call(kernel_fn, name, tq, args_kv, gates, x, mod, wts, scratch, batch0=0, n_batch=None):
    s = x.shape[1]
    b = x.shape[0] if n_batch is None else n_batch
    tok = lambda w: pl.BlockSpec((1, tq, w), lambda i, j: (i + batch0, j, 0))
    seq = lambda a: pl.BlockSpec((1,) + a.shape[1:], lambda i, j: (i + batch0, 0, 0))
    full = lambda a: pl.BlockSpec(a.shape, lambda i, j: (0,) * a.ndim)
    out = lambda w: pl.BlockSpec((1, tq, w), lambda i, j: (i, j, 0))
    args = list(args_kv) + [gates, x, mod, wts["w_mla_up"], wts["w_sb_up"], wts["w_out"], wts["g2"]]
    specs = [seq(a) if whole else tok(a.shape[-1]) for a, whole in
             zip(args_kv, (False, True, True, False) + (True,) * (len(args_kv) - 4))]
    specs += [tok(2 * D_MODEL), tok(D_MODEL), pl.BlockSpec((1, 6, D_MODEL), lambda i, j: (i + batch0, 0, 0)),
              full(wts["w_mla_up"]), full(wts["w_sb_up"]), full(wts["w_out"]), full(wts["g2"])]
    return pl.pallas_call(
        kernel_fn,
        grid=(b, s // tq),
        in_specs=specs,
        out_specs=[out(D_MODEL), out(HALF)],
        out_shape=[jax.ShapeDtypeStruct((b, s, D_MODEL), F32), jax.ShapeDtypeStruct((b, s, HALF), jnp.int32)],
        scratch_shapes=scratch,
        compiler_params=pltpu.CompilerParams(dimension_semantics=("parallel", "arbitrary"),
                                             vmem_limit_bytes=VMEM_LIMIT),
        name=name,
    )(*args)


def _prompt_attention(qp, kmla, vmla, sbq, sbk16, sbv16, gates, x, mod, wts, tq, batch0, n_batch):
    col = lambda: pltpu.VMEM((2 * PAIRS_PER_LOOP, tq, 1), F32)
    wide = lambda: pltpu.VMEM((2 * PAIRS_PER_LOOP, tq, LANES), F32)
    return _mixer_call(functools.partial(_prompt_attn_kernel, tq=tq), "attention", tq,
                       (qp, kmla, vmla, sbq, sbk16, sbv16), gates, x, mod, wts,
                       [col(), wide(), col(), wide()], batch0, n_batch)


def _decode_attention(qp, kmla, vmla, sbq, sbk16, sbv16, past, gates, x, mod, wts, past_blk):
    tq = x.shape[1]
    past_len = past[0].shape[1]
    kern = functools.partial(_decode_attn_kernel, tq=tq, past_len=past_len, past_blk=past_blk)
    return _mixer_call(kern, "decode_attention", tq, (qp, kmla, vmla, sbq, sbk16, sbv16) + tuple(past),
                       gates, x, mod, wts, [])


def _route_kernel(h2_ref, wr_ref, br_ref, idx_ref, wt_ref, rank_ref, cnt_ref, seen_ref):
    lo, hi = _unpack_halves(h2_ref[...])
    tm = lo.shape[0]
    logits = _dot_nt(wr_ref[:, :HALF], lo.astype(BF16)) + _dot_nt(wr_ref[:, HALF:], hi.astype(BF16))
    scores = jax.nn.sigmoid(logits)
    sel = scores + br_ref[...]
    neg = -jnp.inf

    grp = sel.reshape(N_GROUPS, GROUP_SIZE, tm)
    within = lax.broadcasted_iota(jnp.int32, grp.shape, 1)
    top1 = jnp.max(grp, axis=1, keepdims=True)
    first = jnp.min(jnp.where(grp == top1, within, GROUP_SIZE), axis=1, keepdims=True)
    top2 = jnp.max(jnp.where(within == first, neg, grp), axis=1, keepdims=True)
    gscore = (top1 + top2).reshape(N_GROUPS, tm)

    gid = lax.broadcasted_iota(jnp.int32, gscore.shape, 0)
    chosen = jnp.zeros(gscore.shape, jnp.bool_)
    for _ in range(TOPK_GROUPS):
        best = jnp.max(gscore, axis=0, keepdims=True)
        pick = jnp.min(jnp.where(gscore == best, gid, N_GROUPS), axis=0, keepdims=True)
        hit = gid == pick
        chosen = jnp.logical_or(chosen, hit)
        gscore = jnp.where(hit, neg, gscore)
    chosen3 = jnp.broadcast_to(chosen.reshape(N_GROUPS, 1, tm), grp.shape)
    cand = jnp.where(chosen3, grp, neg).reshape(N_EXPERTS, tm)
    outside = jnp.where(cand == neg, 1.0, 0.0)

    eid = lax.broadcasted_iota(jnp.int32, cand.shape, 0)
    picks = []
    weights = []
    for _ in range(TOP_K):
        best = jnp.max(cand, axis=0, keepdims=True)
        pick = jnp.min(jnp.where(cand == best, eid, N_EXPERTS), axis=0, keepdims=True)
        hit = eid == pick
        weights.append(jnp.sum(jnp.where(hit, scores, 0.0), axis=0, keepdims=True))
        picks.append(pick)
        cand = jnp.where(hit, neg, cand)
    w = jnp.concatenate(weights, axis=0)
    idx_ref[...] = jnp.concatenate(picks, axis=0)
    wt_ref[...] = w / (jnp.sum(w, axis=0, keepdims=True) + 1e-20) * ROUTED_SCALE

    @pl.when(pl.program_id(0) == 0)
    def _():
        seen_ref[...] = jnp.zeros_like(seen_ref)

    onehot = jnp.where(cand == neg, 1.0, 0.0) - outside
    src = lax.broadcasted_iota(jnp.int32, (tm, tm), 0)
    dst = lax.broadcasted_iota(jnp.int32, (tm, tm), 1)
    earlier = jnp.where(src < dst, 1.0, 0.0).astype(BF16)
    before = _dot(onehot.astype(BF16), earlier) + seen_ref[...]
    rank_ref[...] = jnp.concatenate(
        [jnp.sum(jnp.where(eid == pick, before, 0.0), axis=0, keepdims=True) for pick in picks],
        axis=0).astype(jnp.int32)
    seen = seen_ref[...] + jnp.sum(onehot, axis=1, keepdims=True)
    seen_ref[...] = seen
    cnt_ref[...] = seen.astype(jnp.int32)


def _route(h2p, wr_t, br, tm):
    t = h2p.shape[0]
    kt = lambda: pl.BlockSpec((TOP_K, tm), lambda i: (0, i))
    return pl.pallas_call(
        _route_kernel,
        grid=(t // tm,),
        in_specs=[pl.BlockSpec((tm, HALF), lambda i: (i, 0)),
                  pl.BlockSpec(wr_t.shape, lambda i: (0, 0)),
                  pl.BlockSpec(br.shape, lambda i: (0, 0))],
        out_specs=[kt(), kt(), kt(), pl.BlockSpec((N_EXPERTS, 1), lambda i: (0, 0))],
        out_shape=[jax.ShapeDtypeStruct((TOP_K, t), jnp.int32), jax.ShapeDtypeStruct((TOP_K, t), F32),
                   jax.ShapeDtypeStruct((TOP_K, t), jnp.int32), jax.ShapeDtypeStruct((N_EXPERTS, 1), jnp.int32)],
        scratch_shapes=[pltpu.VMEM((N_EXPERTS, 1), F32)],
        compiler_params=pltpu.CompilerParams(dimension_semantics=("arbitrary",)),
        name="route",
    )(h2p, wr_t, br)


def _position_kernel(idx_ref, rank_ref, start_ref, pos_ref):
    idx = idx_ref[...]
    eid = lax.broadcasted_iota(jnp.int32, (N_EXPERTS, idx.shape[1]), 0)
    start = start_ref[...]
    base = jnp.concatenate(
        [jnp.sum(jnp.where(eid == idx[k:k + 1, :], start, 0.0), axis=0, keepdims=True) for k in range(TOP_K)],
        axis=0)
    pos_ref[...] = base.astype(jnp.int32) + rank_ref[...]


def _positions(idx_kt, rank_kt, pad_start, tm):
    t = idx_kt.shape[1]
    tm = max(m for m in range(tm, 8 * tm + 1, tm) if t % m == 0)
    kt = lambda: pl.BlockSpec((TOP_K, tm), lambda i: (0, i))
    return pl.pallas_call(
        _position_kernel,
        grid=(t // tm,),
        in_specs=[kt(), kt(), pl.BlockSpec((N_EXPERTS, 1), lambda i: (0, 0))],
        out_specs=kt(),
        out_shape=jax.ShapeDtypeStruct((TOP_K, t), jnp.int32),
        compiler_params=pltpu.CompilerParams(dimension_semantics=("parallel",)),
        name="positions",
    )(idx_kt, rank_kt, pad_start.astype(F32).reshape(N_EXPERTS, 1))


def _sc_mesh():
    return plsc.VectorSubcoreMesh(core_axis_name="c", subcore_axis_name="s",
                                  num_cores=SC_CORES, num_subcores=SC_SUBCORES)


def _sc_scatter_rows(rows, pos, n_out, chunk):
    t, width = rows.shape
    copies = pos.shape[0] // t
    per_worker = t // SC_WORKERS
    n_chunks = per_worker // chunk
    tail = per_worker - n_chunks * chunk
    assert per_worker * SC_WORKERS == t and tail % 8 == 0
    tail_rows = max(tail, 8)

    @functools.partial(
        pl.kernel, mesh=_sc_mesh(),
        out_type=jax.ShapeDtypeStruct((n_out, width), rows.dtype),
        scratch_types=[pltpu.VMEM((copies, chunk), jnp.int32), pltpu.VMEM((chunk, width), rows.dtype),
                       pltpu.VMEM((copies, tail_rows), jnp.int32), pltpu.VMEM((tail_rows, width), rows.dtype),
                       pltpu.SemaphoreType.DMA, pltpu.SemaphoreType.DMA],
    )
    def scatter(rows_hbm, pos_hbm, out_hbm, idx_v, rows_v, idx_t, rows_t, load_sem, store_sem):
        wid = lax.axis_index("s") * SC_CORES + lax.axis_index("c")
        base = wid * per_worker

        def move(off, n, idx_buf, row_buf):
            loads = [pltpu.async_copy(rows_hbm.at[pl.ds(off, n)], row_buf, load_sem)]
            for k in range(copies):
                src = pos_hbm.at[pl.ds(pl.multiple_of(k * t + off, 8), n)]
                loads.append(pltpu.async_copy(src, idx_buf.at[k], load_sem))
            for cp in loads:
                cp.wait()
            stores = [pltpu.async_copy(row_buf, out_hbm.at[idx_buf.at[k]], store_sem) for k in range(copies)]
            for cp in stores:
                cp.wait()

        @pl.loop(0, n_chunks)
        def _(c):
            move(pl.multiple_of(base + c * chunk, 8), chunk, idx_v, rows_v)

        if tail:
            move(pl.multiple_of(base + n_chunks * chunk, 8), tail, idx_t, rows_t)

    return scatter(rows, pos)


def _sc_gather_rows(table, idx, chunk):
    n_rows = idx.shape[0]
    width = table.shape[1]
    per_worker = n_rows // SC_WORKERS
    n_chunks = per_worker // chunk
    assert per_worker * SC_WORKERS == n_rows and n_chunks * chunk == per_worker
    mesh = _sc_mesh()

    @functools.partial(
        pl.kernel, mesh=mesh,
        out_type=jax.ShapeDtypeStruct((n_rows, width), table.dtype),
        scratch_types=[pltpu.VMEM((chunk,), jnp.int32), pltpu.VMEM((chunk, width), table.dtype),
                       pltpu.SemaphoreType.DMA],
    )
    def gather(table_hbm, idx_hbm, out_hbm, idx_v, rows_v, sem):
        wid = lax.axis_index("s") * SC_CORES + lax.axis_index("c")
        base = wid * per_worker

        @pl.loop(0, n_chunks)
        def _(c):
            off = pl.multiple_of(base + c * chunk, 8)
            pltpu.sync_copy(idx_hbm.at[pl.ds(off, chunk)], idx_v)
            pltpu.async_copy(table_hbm.at[idx_v], rows_v, sem).wait()
            pltpu.sync_copy(rows_v, out_hbm.at[pl.ds(off, chunk)])

    return gather(table, idx)


ROW_SLOTS = 3
WEIGHT_SLOTS = 2


def _expert_kernel(be_ref, nv_ref, nu_ref, nxt_ref, xs_hbm, w1_hbm, w3_hbm, w2_hbm, os_ref,
                   wb1, wb3, wb2, xbuf, wf1, wf3, wf2, slot_ref, xsem, wsem, *, blk):
    i = pl.program_id(0)
    n_used = nu_ref[0]

    def rows_copy(j):
        src = xs_hbm.at[pl.ds(pl.multiple_of(j * blk, blk), blk)]
        return pltpu.make_async_copy(src, xbuf.at[j % ROW_SLOTS], xsem.at[j % ROW_SLOTS])

    def weight_copies(e, slot):
        return [pltpu.make_async_copy(w_hbm.at[e], buf.at[slot], wsem.at[slot])
                for w_hbm, buf in ((w1_hbm, wf1), (w3_hbm, wf3), (w2_hbm, wf2))]

    @pl.when(i == 0)
    def _():
        rows_copy(0).start()

        @pl.when(n_used > 1)
        def _():
            rows_copy(1).start()
        for cp in weight_copies(be_ref[0], 0):
            cp.start()
        slot_ref[0] = 1

    @pl.when(i + 2 < n_used)
    def _():
        rows_copy(i + 2).start()

    @pl.when(i < n_used)
    def _():
        @pl.when(jnp.logical_or(i == 0, be_ref[i] != be_ref[jnp.maximum(i - 1, 0)]))
        def _():
            slot = 1 - slot_ref[0]
            slot_ref[0] = slot
            for cp in weight_copies(be_ref[i], slot):
                cp.wait()

            @pl.when(nxt_ref[i] >= 0)
            def _():
                for cp in weight_copies(nxt_ref[i], 1 - slot):
                    cp.start()
            wb1[...] = wf1[slot].astype(BF16)
            wb3[...] = wf3[slot].astype(BF16)
            wb2[...] = wf2[slot].astype(BF16)

        rows_copy(i).wait()
        packed = xbuf[i % ROW_SLOTS]
        live = lax.broadcasted_iota(jnp.int32, packed.shape, 0) < nv_ref[i]
        lo, hi = _unpack_halves(jnp.where(live, packed, 0))
        lo = lo.astype(BF16)
        hi = hi.astype(BF16)
        a = _dot(lo, wb1[:HALF, :]) + _dot(hi, wb1[HALF:, :])
        b = _dot(lo, wb3[:HALF, :]) + _dot(hi, wb3[HALF:, :])
        o = _dot((_silu(a) * b).astype(BF16), wb2[...])
        os_ref[...] = _pack_halves(o[:, :HALF], o[:, HALF:])


def _experts(xs, blk_exp, blk_valid, n_used, next_exp, w1, w3, w2, blk):
    rows = xs.shape[0]
    nb = rows // blk
    any_space = pl.BlockSpec(memory_space=pl.ANY)
    grid_spec = pltpu.PrefetchScalarGridSpec(
        num_scalar_prefetch=4,
        grid=(nb,),
        in_specs=[any_space, any_space, any_space, any_space],
        out_specs=pl.BlockSpec((blk, HALF), lambda i, be, nv, nu, nx: (jnp.minimum(i, nu[0] - 1), 0)),
        scratch_shapes=[pltpu.VMEM((D_MODEL, EXPERT_DIM), BF16), pltpu.VMEM((D_MODEL, EXPERT_DIM), BF16),
                        pltpu.VMEM((EXPERT_DIM, D_MODEL), BF16),
                        pltpu.VMEM((ROW_SLOTS, blk, HALF), jnp.int32),
                        pltpu.VMEM((WEIGHT_SLOTS, D_MODEL, EXPERT_DIM), F32),
                        pltpu.VMEM((WEIGHT_SLOTS, D_MODEL, EXPERT_DIM), F32),
                        pltpu.VMEM((WEIGHT_SLOTS, EXPERT_DIM, D_MODEL), F32),
                        pltpu.SMEM((1,), jnp.int32),
                        pltpu.SemaphoreType.DMA((ROW_SLOTS,)), pltpu.SemaphoreType.DMA((WEIGHT_SLOTS,))],
    )
    return pl.pallas_call(
        functools.partial(_expert_kernel, blk=blk),
        grid_spec=grid_spec,
        out_shape=jax.ShapeDtypeStruct((rows, HALF), jnp.int32),
        compiler_params=pltpu.CompilerParams(dimension_semantics=("arbitrary",), vmem_limit_bytes=VMEM_LIMIT),
        name="experts",
    )(blk_exp, blk_valid, n_used, next_exp, xs, w1, w3, w2)


def _combine_kernel(*refs, tm, src_tile0, n_steps, has_prev):
    (g_hbm, wt_ref, h2_ref, x1_ref, mod_ref, sw1_ref, sw3_ref, sw2_ref, gf_ref) = refs[:9]
    y_ref, gbuf, gsem = refs[-3:]
    assert len(refs) == 12 + int(has_prev)
    i = pl.program_id(0)

    def g_copy(j):
        src = g_hbm.at[:, pl.ds(pl.multiple_of((j + src_tile0) * tm, tm), tm), :]
        return pltpu.make_async_copy(src, gbuf.at[j % ROW_SLOTS], gsem.at[j % ROW_SLOTS])

    @pl.when(i == 0)
    def _():
        g_copy(0).start()
        if n_steps > 1:
            g_copy(1).start()

    @pl.when(i + 2 < n_steps)
    def _():
        g_copy(i + 2).start()
    g_copy(i).wait()
    g_ref = gbuf.at[i % ROW_SLOTS]

    wt = wt_ref[...]
    lo_acc = None
    for k in range(TOP_K):
        lo, hi = _unpack_halves(g_ref[k])
        wk = wt[:, k:k + 1]
        lo_acc = wk * lo if lo_acc is None else lo_acc + wk * lo
        hi_acc = wk * hi if k == 0 else hi_acc + wk * hi
    routed = jnp.concatenate([lo_acc, hi_acc], axis=1)
    lo, hi = _unpack_halves(h2_ref[...])
    lo = lo.astype(BF16)
    hi = hi.astype(BF16)
    a = _dot(lo, sw1_ref[:HALF, :]) + _dot(hi, sw1_ref[HALF:, :])
    b = _dot(lo, sw3_ref[:HALF, :]) + _dot(hi, sw3_ref[HALF:, :])
    shared = _dot((_silu(a) * b).astype(BF16), sw2_ref[...])
    mod = mod_ref[0]
    x2 = x1_ref[...] + mod[5:6] * (routed + shared)
    y_ref[...] = _rms(x2) * gf_ref[...]


def _combine(gathered, wt, h2p, x1, mod, wts, tm, src_tile0, mod_batch0, tiles_per_batch, out_rows, out_tile0,
             y_prev=None):
    t = x1.shape[0]
    full = lambda a: pl.BlockSpec(a.shape, lambda i: (0,) * a.ndim)
    shifted = lambda w: pl.BlockSpec((tm, w), lambda i: (i + src_tile0, 0))
    args = [gathered, wt, h2p, x1, mod, wts["sw1"], wts["sw3"], wts["sw2"], wts["gf"]]
    specs = [pl.BlockSpec(memory_space=pl.ANY), shifted(TOP_K), shifted(HALF),
             pl.BlockSpec((tm, D_MODEL), lambda i: (i, 0)),
             pl.BlockSpec((1, 6, D_MODEL), lambda i: (i // tiles_per_batch + mod_batch0, 0, 0)),
             full(wts["sw1"]), full(wts["sw3"]), full(wts["sw2"]), full(wts["gf"])]
    aliases = {}
    if y_prev is not None:
        args.append(y_prev)
        specs.append(pl.BlockSpec(memory_space=pl.ANY))
        aliases = {len(args) - 1: 0}
    n_steps = t // tm
    return pl.pallas_call(
        functools.partial(_combine_kernel, tm=tm, src_tile0=src_tile0, n_steps=n_steps,
                          has_prev=y_prev is not None),
        grid=(n_steps,),
        in_specs=specs,
        out_specs=pl.BlockSpec((tm, D_MODEL), lambda i: (i + out_tile0, 0)),
        out_shape=jax.ShapeDtypeStruct((out_rows, D_MODEL), F32),
        scratch_shapes=[pltpu.VMEM((ROW_SLOTS, TOP_K, tm, HALF), jnp.int32),
                        pltpu.SemaphoreType.DMA((ROW_SLOTS,))],
        input_output_aliases=aliases,
        compiler_params=pltpu.CompilerParams(dimension_semantics=("arbitrary",), vmem_limit_bytes=VMEM_LIMIT),
        name="combine",
    )(*args)


def _rope_tables(pos):
    half = MLA_ROPE // 2
    inv_freq = ROPE_THETA ** (-jnp.arange(half, dtype=F32) / half)
    ang = pos.astype(F32)[:, None] * inv_freq
    cos, sin = jnp.cos(ang), jnp.sin(ang)
    n = pos.shape[0]
    ones = jnp.ones((n, MLA_NOPE), F32)
    z_nope = jnp.zeros((n, MLA_NOPE), F32)
    z_pad = jnp.zeros((n, HEAD_PAD - MLA_NOPE - MLA_ROPE), F32)
    return (jnp.concatenate([ones, cos, cos, z_pad], axis=1),
            jnp.concatenate([z_nope, -sin, sin, z_pad], axis=1))


def _prep_weights(g_norm1, w_in, g_q_lat, w_uq, g_kv_lat, w_ukv, w_mla_up, w_sb_up, w_out, g_norm2,
                  w_router, b_router, shared_w1, shared_w3, shared_w2, g_final):
    w = w_in[0]
    o = 0
    parts = {}
    for name, width in (("qlat", MLA_Q_LORA), ("kv", MLA_KV_LORA), ("kr", MLA_ROPE), ("sbq", SB_WIDTH),
                        ("sbk", SB_WIDTH), ("sbv", SB_WIDTH), ("gm", D_MODEL), ("gs", D_MODEL)):
        parts[name] = w[:, o:o + width]
        o += width
    kr = parts["kr"]
    z_nope = jnp.zeros((D_MODEL, MLA_NOPE), F32)
    z_pad = jnp.zeros((D_MODEL, HEAD_PAD - MLA_NOPE - MLA_ROPE), F32)
    kr_seg = jnp.concatenate([z_nope, kr, z_pad], axis=1)
    w_in_r = jnp.concatenate([parts["qlat"], parts["kv"], parts["sbq"] * (SB_DIM ** -0.5 * LOG2_E), parts["sbk"],
                              parts["sbv"], parts["gm"], parts["gs"], kr_seg], axis=1).astype(BF16)

    scale = (MLA_NOPE + MLA_ROPE) ** -0.5 * LOG2_E
    uq = w_uq[0].reshape(MLA_Q_LORA, MLA_HEADS, MLA_NOPE + MLA_ROPE) * scale
    nope, rope = uq[..., :MLA_NOPE], uq[..., MLA_NOPE:]
    zq_pad = jnp.zeros((MLA_Q_LORA, MLA_HEADS, HEAD_PAD - MLA_NOPE - MLA_ROPE), F32)
    w_uq_r = jnp.concatenate([nope, rope, zq_pad], axis=-1).reshape(MLA_Q_LORA, QP_WIDTH).astype(BF16)

    ukv = w_ukv[0].reshape(MLA_KV_LORA, MLA_HEADS, MLA_NOPE + MLA_V)
    k_nope, v = ukv[..., :MLA_NOPE], ukv[..., MLA_NOPE:]
    k_pad = jnp.concatenate([k_nope, jnp.zeros((MLA_KV_LORA, MLA_HEADS, HEAD_PAD - MLA_NOPE), F32)], axis=-1)
    w_ukv_r = jnp.concatenate([k_pad.reshape(MLA_KV_LORA, QP_WIDTH),
                               v.reshape(MLA_KV_LORA, MLA_HEADS * MLA_V)], axis=1).astype(BF16)

    eye = jnp.eye(MLA_ROPE, dtype=F32)
    place_head = jnp.concatenate([jnp.zeros((MLA_ROPE, MLA_NOPE), F32), eye,
                                  jnp.zeros((MLA_ROPE, HEAD_PAD - MLA_NOPE - MLA_ROPE), F32)], axis=1)
    place = jnp.tile(place_head, (1, MLA_HEADS)).astype(BF16)

    return dict(
        g1=g_norm1[0].reshape(1, D_MODEL), w_in=w_in_r, gq=g_q_lat[0].reshape(1, MLA_Q_LORA), w_uq=w_uq_r,
        gkv=g_kv_lat[0].reshape(1, MLA_KV_LORA), w_ukv=w_ukv_r, place=place,
        w_mla_up=w_mla_up[0].astype(BF16), w_sb_up=w_sb_up[0].astype(BF16), w_out=w_out[0].astype(BF16),
        g2=g_norm2[0].reshape(1, D_MODEL), wr_t=w_router[0].T.astype(BF16),
        br=b_router[0].reshape(N_EXPERTS, 1),
        sw1=shared_w1[0].astype(BF16), sw3=shared_w3[0].astype(BF16), sw2=shared_w2[0].astype(BF16),
        gf=g_final.reshape(1, D_MODEL))


def _block_plan(counts, nb, blk):
    padded = (counts + blk - 1) // blk * blk
    pad_end = jnp.cumsum(padded)
    pad_start = pad_end - padded
    first_row = jnp.arange(nb, dtype=jnp.int32) * blk
    blk_exp = jnp.minimum(jnp.sum((pad_end[None, :] <= first_row[:, None]).astype(jnp.int32), axis=1),
                          N_EXPERTS - 1)
    own = blk_exp[:, None] == jnp.arange(N_EXPERTS, dtype=jnp.int32)[None, :]
    seg_end = jnp.sum(jnp.where(own, (pad_start + counts)[None, :], 0), axis=1)
    blk_valid = jnp.clip(seg_end - first_row, 0, blk).astype(jnp.int32)
    n_used = (pad_end[-1:] // blk).astype(jnp.int32)
    after_seg = jnp.sum(jnp.where(own, (pad_end // blk)[None, :], 0), axis=1)
    follower = jnp.sum(jnp.where(after_seg[:, None] == jnp.arange(nb, dtype=jnp.int32)[None, :],
                                 blk_exp[None, :], 0), axis=1)
    next_exp = jnp.where(after_seg < n_used[0], follower, -1).astype(jnp.int32)
    return pad_start, blk_exp.astype(jnp.int32), blk_valid, n_used, next_exp


def _moe_rows(h2p, idx_kt, rank_kt, counts, w1, w3, w2, scatter_rows, gather_rows, blk, tm):
    t = h2p.shape[0]
    nb = -(-t * TOP_K // blk) + N_EXPERTS
    pad_start, blk_exp, blk_valid, n_used, next_exp = _block_plan(counts.reshape(N_EXPERTS), nb, blk)
    pos = _positions(idx_kt, rank_kt, pad_start, tm).reshape(TOP_K * t)
    xs = scatter_rows(h2p, pos, nb * blk)
    os_ = _experts(xs, blk_exp, blk_valid, n_used, next_exp, w1, w3, w2, blk)
    return gather_rows(os_, pos).reshape(TOP_K, t, HALF)


def _forward(x_prompt, x_sample, cache_mla_ckv, cache_mla_krope, cache_sb_k, cache_sb_v, c_prompt, c_sample,
             w_ada, b_ada, moe_w1, moe_w3, moe_w2, wts, scatter_rows, gather_rows, token_block, attn_block,
             route_block, moe_block, wide_block):
    bp, sp, _ = x_prompt.shape
    bs, ss, _ = x_sample.shape
    past_len = cache_mla_ckv.shape[2]

    mod = _ada(jnp.concatenate([c_prompt, c_sample], axis=0), w_ada[0], b_ada[0]).reshape(bp + bs, 6, D_MODEL)
    mod_p, mod_s = mod[:bp], mod[bp:]

    cos_p, sin_p = _rope_tables(jnp.arange(sp))
    (qp, kmla, vmla, sbq, sbk16, sbv16, gates, ckv_p, krope_p, sbk_p, sbv_p) = _in_proj(
        x_prompt, mod_p, wts, cos_p, sin_p, wide_block)
    split_b = bp * GROUP_A_SHARE[0] // GROUP_A_SHARE[1]
    assert 0 < split_b < bp
    mixed = [_prompt_attention(qp, kmla, vmla, sbq, sbk16, sbv16, gates, x_prompt, mod_p, wts, attn_block,
                               b0, nb_) for b0, nb_ in ((0, split_b), (split_b, bp - split_b))]

    cos_s, sin_s = _rope_tables(past_len + jnp.arange(ss))
    (qs, kmla_s, vmla_s, sbq_s, sbk16_s, sbv16_s, gates_s, ckv_s, krope_s, sbk_s, sbv_s) = _in_proj(
        x_sample, mod_s, wts, cos_s, sin_s, ss)
    pkmla, pvmla = _kv_up(cache_mla_ckv[0], cache_mla_krope[0], wts["w_ukv"], wts["place"], wide_block)
    past = (pkmla, pvmla, cache_sb_k[0].reshape(bs, past_len, SB_WIDTH), cache_sb_v[0].reshape(bs, past_len, SB_WIDTH))
    x1_s, h2_s = _decode_attention(qs, kmla_s, vmla_s, sbq_s, sbk16_s, sbv16_s, past, gates_s, x_sample, mod_s,
                                   wts, token_block)

    tp, ts = bp * sp, bs * ss
    ta = split_b * sp
    tb = tp - ta

    def moe(h2_rows):
        idx_kt, wt_kt, rank_kt, counts = _route(h2_rows, wts["wr_t"], wts["br"], route_block)
        gathered = _moe_rows(h2_rows, idx_kt, rank_kt, counts, moe_w1[0], moe_w3[0], moe_w2[0],
                             scatter_rows, gather_rows, moe_block, route_block)
        return gathered, wt_kt.T

    tiles_per_batch = sp // wide_block
    (x1_a, h2_a), (x1_b, h2_b) = mixed
    h2_a = h2_a.reshape(ta, HALF)
    g_a, wt_a = moe(h2_a)
    y_p = _combine(g_a, wt_a, h2_a, x1_a.reshape(ta, D_MODEL), mod_p, wts, wide_block,
                   0, 0, tiles_per_batch, tp, 0)
    h2_b = jnp.concatenate([h2_b.reshape(tb, HALF), h2_s.reshape(ts, HALF)], axis=0)
    g_b, wt_b = moe(h2_b)
    y_p = _combine(g_b, wt_b, h2_b, x1_b.reshape(tb, D_MODEL), mod_p, wts, wide_block,
                   0, split_b, tiles_per_batch, tp, ta // wide_block, y_prev=y_p)
    y_s = _combine(g_b, wt_b, h2_b, x1_s.reshape(ts, D_MODEL), mod_s, wts, ss, tb // ss, 0, 1, ts, 0)

    heads = lambda a, b_, s_: a.reshape(1, b_, s_, SB_HEADS, SB_DIM)
    return (y_p.reshape(bp, sp, D_MODEL), y_s.reshape(bs, ss, D_MODEL),
            ckv_p[None], krope_p[None], heads(sbk_p, bp, sp), heads(sbv_p, bp, sp),
            ckv_s[None], krope_s[None], heads(sbk_s, bs, ss), heads(sbv_s, bs, ss))


def kernel(x_prompt, x_sample, cache_mla_ckv, cache_mla_krope, cache_sb_k, cache_sb_v, c_prompt, c_sample, w_ada, b_ada, g_norm1, w_in, g_q_lat, w_uq, g_kv_lat, w_ukv, w_mla_up, w_sb_up, w_out, g_norm2, w_router, b_router, moe_w1, moe_w3, moe_w2, shared_w1, shared_w3, shared_w2, g_final):
    wts = _prep_weights(g_norm1, w_in, g_q_lat, w_uq, g_kv_lat, w_ukv, w_mla_up, w_sb_up, w_out, g_norm2,
                        w_router, b_router, shared_w1, shared_w3, shared_w2, g_final)
    scatter_rows = functools.partial(_sc_scatter_rows, chunk=SC_CHUNK)
    gather_rows = functools.partial(_sc_gather_rows, chunk=SC_CHUNK)
    return _forward(x_prompt, x_sample, cache_mla_ckv, cache_mla_krope, cache_sb_k, cache_sb_v, c_prompt, c_sample,
                    w_ada, b_ada, moe_w1, moe_w3, moe_w2, wts, scatter_rows, gather_rows, TOKEN_BLOCK, ATTN_BLOCK,
                    TOKEN_BLOCK, MOE_BLOCK, WIDE_BLOCK)
```

```python
import functools

import jax
import jax.numpy as jnp
from jax import lax
from jax.experimental import pallas as pl
from jax.experimental.pallas import tpu as pltpu
from jax.experimental.pallas import tpu_sc as plsc

F32 = jnp.float32
BF16 = jnp.bfloat16

D_MODEL = 1024
NORM_EPS = 1e-6
CHUNK = 64
MLA_HEADS = 8
MLA_NOPE = 64
MLA_ROPE = 32
MLA_V = 64
MLA_Q_LORA = 384
MLA_KV_LORA = 256
ROPE_THETA = 10000.0
SB_HEADS = 8
SB_DIM = 64
SB_WIDTH = SB_HEADS * SB_DIM
N_EXPERTS = 256
TOP_K = 8
N_GROUPS = 8
TOPK_GROUPS = 4
GROUP_SIZE = N_EXPERTS // N_GROUPS
EXPERT_DIM = 256
ROUTED_SCALE = 2.5
LOG2_E = 1.4426950408889634

LANES = 128
MXU_TILE = 256
SC_CORES = 2
SC_SUBCORES = 16
SC_WORKERS = SC_CORES * SC_SUBCORES
VMEM_LIMIT = 60 * 1024 * 1024

HEAD_PAD = LANES
DENOM_LANE = (MLA_V, 0)
QP_WIDTH = MLA_HEADS * HEAD_PAD
HALF = D_MODEL // 2

C_QLAT = 0
C_KV = C_QLAT + MLA_Q_LORA
C_SBQ = C_KV + MLA_KV_LORA
C_SBK = C_SBQ + SB_WIDTH
C_SBV = C_SBK + SB_WIDTH
C_GATE = C_SBV + SB_WIDTH
C_KR = C_GATE + 2 * D_MODEL
C_END = C_KR + LANES

MOE_BLOCK = 512
WIDE_BLOCK = 512
TOKEN_BLOCK = 256
ATTN_BLOCK = 512
PAIRS_PER_LOOP = 1
SC_CHUNK = 64
GROUP_A_SHARE = (1, 2)


def _rms(x):
    return x * lax.rsqrt(jnp.mean(x * x, axis=-1, keepdims=True) + NORM_EPS)


def _silu(x):
    return x * jax.nn.sigmoid(x)


def _pack_halves(lo, hi):
    lo_bits = lax.bitcast_convert_type(lo.astype(BF16).astype(F32), jnp.uint32) >> 16
    hi_bits = lax.bitcast_convert_type(hi.astype(BF16).astype(F32), jnp.uint32) & jnp.uint32(0xFFFF0000)
    return lax.bitcast_convert_type(lo_bits | hi_bits, jnp.int32)


def _unpack_halves(p):
    u = lax.bitcast_convert_type(p, jnp.uint32)
    lo = lax.bitcast_convert_type(u << 16, F32)
    hi = lax.bitcast_convert_type(u & jnp.uint32(0xFFFF0000), F32)
    return lo, hi


def _dot(a, b):
    return jnp.dot(a, b, preferred_element_type=F32)


def _dot_nt(a, b):
    return lax.dot_general(a, b, (((1,), (1,)), ((), ())), preferred_element_type=F32)


def _ada_kernel(c_ref, w_ref, b_ref, o_ref):
    c = c_ref[...]
    o_ref[...] = _dot(_silu(c).astype(BF16), w_ref[...].astype(BF16)) + b_ref[...]


def _ada(c, w_ada, b_ada):
    n = c.shape[0]
    width = w_ada.shape[1]
    return pl.pallas_call(
        _ada_kernel,
        grid=(width // D_MODEL,),
        in_specs=[pl.BlockSpec((n, D_MODEL), lambda j: (0, 0)),
                  pl.BlockSpec((D_MODEL, D_MODEL), lambda j: (0, j)),
                  pl.BlockSpec((1, D_MODEL), lambda j: (0, j))],
        out_specs=pl.BlockSpec((n, D_MODEL), lambda j: (0, j)),
        out_shape=jax.ShapeDtypeStruct((n, width), F32),
        name="ada",
    )(c, w_ada, b_ada.reshape(1, width))


def _in_kernel(x_ref, mod_ref, g1_ref, win_ref, gq_ref, wuq_ref, gkv_ref, wukv_ref, cos_ref, sin_ref,
               qp_ref, kmla_ref, vmla_ref, sbq_ref, sbk16_ref, sbv16_ref, gates_ref,
               ckv_ref, krope_ref, sbk_ref, sbv_ref):
    x = x_ref[0]
    mod = mod_ref[0]
    h = _rms(x) * g1_ref[...] * (1.0 + mod[1:2]) + mod[0:1]
    hb = h.astype(BF16)

    def seg(a, b):
        return _dot(hb, win_ref[:, a:b])

    cos = cos_ref[...]
    sin = sin_ref[...]
    lane = lax.broadcasted_iota(jnp.int32, (1, LANES), 1)
    half = MLA_ROPE // 2

    def rotate(blk):
        other = jnp.where(lane < MLA_NOPE + half, pltpu.roll(blk, LANES - half, 1), pltpu.roll(blk, half, 1))
        return blk * cos + other * sin

    qn = (_rms(seg(C_QLAT, C_KV)) * gq_ref[...]).astype(BF16)
    q = _dot(qn, wuq_ref[...])
    qp_ref[0] = jnp.concatenate([rotate(q[:, h * HEAD_PAD:(h + 1) * HEAD_PAD]) for h in range(MLA_HEADS)],
                                axis=1).astype(BF16)

    ckv = _rms(seg(C_KV, C_SBQ)) * gkv_ref[...]
    ckv_ref[0] = ckv
    kv = _dot(ckv.astype(BF16), wukv_ref[...])
    krp = rotate(seg(C_KR, C_END))
    krope_ref[0] = krp[:, MLA_NOPE:MLA_NOPE + MLA_ROPE]
    kmla_ref[0] = (kv[:, :QP_WIDTH] + jnp.tile(krp, (1, MLA_HEADS))).astype(BF16)
    vmla_ref[0] = kv[:, QP_WIDTH:].astype(BF16)

    sbq_ref[0] = seg(C_SBQ, C_SBK).astype(BF16)
    sbk = seg(C_SBK, C_SBV)
    sbk_ref[0] = sbk
    sbk16_ref[0] = sbk.astype(BF16)
    sbv = seg(C_SBV, C_GATE)
    sbv_ref[0] = sbv
    sbv16_ref[0] = sbv.astype(BF16)
    gates_ref[0] = jax.nn.sigmoid(seg(C_GATE, C_KR)).astype(BF16)


def _in_proj(x, mod, wts, cos_t, sin_t, tm):
    b, s, _ = x.shape
    ns = s // tm
    tok = lambda w: pl.BlockSpec((1, tm, w), lambda i, j: (i, j, 0))
    full = lambda a: pl.BlockSpec(a.shape, lambda i, j: (0,) * a.ndim)
    out_widths = [(QP_WIDTH, BF16), (QP_WIDTH, BF16), (SB_WIDTH, BF16), (SB_WIDTH, BF16), (SB_WIDTH, BF16),
                  (SB_WIDTH, BF16), (2 * D_MODEL, BF16), (MLA_KV_LORA, F32), (MLA_ROPE, F32),
                  (SB_WIDTH, F32), (SB_WIDTH, F32)]
    return pl.pallas_call(
        _in_kernel,
        grid=(b, ns),
        in_specs=[tok(D_MODEL),
                  pl.BlockSpec((1, 6, D_MODEL), lambda i, j: (i, 0, 0)),
                  full(wts["g1"]), full(wts["w_in"]), full(wts["gq"]), full(wts["w_uq"]),
                  full(wts["gkv"]), full(wts["w_ukv"]),
                  pl.BlockSpec((tm, LANES), lambda i, j: (j, 0)),
                  pl.BlockSpec((tm, LANES), lambda i, j: (j, 0))],
        out_specs=[tok(w) for w, _ in out_widths],
        out_shape=[jax.ShapeDtypeStruct((b, s, w), dt) for w, dt in out_widths],
        compiler_params=pltpu.CompilerParams(dimension_semantics=("parallel", "parallel"),
                                             vmem_limit_bytes=VMEM_LIMIT),
        name="in_proj",
    )(x, mod, wts["g1"], wts["w_in"], wts["gq"], wts["w_uq"], wts["gkv"], wts["w_ukv"], cos_t, sin_t)


def _kvup_kernel(ckv_ref, kr_ref, wukv_ref, place_ref, kmla_ref, vmla_ref):
    kv = _dot(ckv_ref[0].astype(BF16), wukv_ref[...])
    kr = _dot(kr_ref[0].astype(BF16), place_ref[...])
    kmla_ref[0] = (kv[:, :QP_WIDTH] + kr).astype(BF16)
    vmla_ref[0] = kv[:, QP_WIDTH:].astype(BF16)


def _kv_up(ckv, krope, w_ukv_r, place, tm):
    b, p, _ = ckv.shape
    return pl.pallas_call(
        _kvup_kernel,
        grid=(b, p // tm),
        in_specs=[pl.BlockSpec((1, tm, MLA_KV_LORA), lambda i, j: (i, j, 0)),
                  pl.BlockSpec((1, tm, MLA_ROPE), lambda i, j: (i, j, 0)),
                  pl.BlockSpec(w_ukv_r.shape, lambda i, j: (0, 0)),
                  pl.BlockSpec(place.shape, lambda i, j: (0, 0))],
        out_specs=[pl.BlockSpec((1, tm, QP_WIDTH), lambda i, j: (i, j, 0)),
                   pl.BlockSpec((1, tm, SB_WIDTH), lambda i, j: (i, j, 0))],
        out_shape=[jax.ShapeDtypeStruct((b, p, QP_WIDTH), BF16), jax.ShapeDtypeStruct((b, p, SB_WIDTH), BF16)],
        compiler_params=pltpu.CompilerParams(dimension_semantics=("parallel", "parallel")),
        name="kv_up",
    )(ckv, krope, w_ukv_r, place)


def _tri(n):
    r = lax.broadcasted_iota(jnp.int32, (n, n), 0)
    c = lax.broadcasted_iota(jnp.int32, (n, n), 1)
    return jnp.where(r > c, 1.0, 0.0).astype(BF16)


def _stick_terms(z):
    log_sig = jnp.minimum(z, 0.0) - jnp.log2(1.0 + jnp.exp2(-jnp.abs(z)))
    return log_sig, log_sig - z


def _split_bf16(x):
    hi = x.astype(BF16)
    return hi, (x - hi.astype(F32)).astype(BF16)


def _finish_mixer(o_mla, o_sb, gates_ref, x_ref, mod_ref, wmu_ref, wsu_ref, wo_ref, g2_ref, x1_ref, h2_ref):
    u_mla = _dot(o_mla.astype(BF16), wmu_ref[...])
    u_sb = _dot(o_sb.astype(BF16), wsu_ref[...])
    gates = gates_ref[0]
    merged = gates[:, :D_MODEL].astype(F32) * u_mla + gates[:, D_MODEL:].astype(F32) * u_sb
    mix = _dot(merged.astype(BF16), wo_ref[...])
    mod = mod_ref[0]
    x1 = x_ref[0] + mod[2:3] * mix
    x1_ref[0] = x1
    h2 = _rms(x1) * g2_ref[...] * (1.0 + mod[4:5]) + mod[3:4]
    h2_ref[0] = _pack_halves(h2[:, :HALF], h2[:, HALF:])


def _prompt_attn_kernel(qp_ref, kmla_ref, vmla_ref, sbq_ref, sbk_ref, sbv_ref,
                        gates_ref, x_ref, mod_ref, wmu_ref, wsu_ref, wo_ref, g2_ref, x1_ref, h2_ref,
                        m_ref, acc_ref, c_ref, sacc_ref, *, tq):
    i = pl.program_id(1)
    lane = lax.broadcasted_iota(jnp.int32, (1, LANES), 1)
    half_masks = (lane < MLA_V, lane >= MLA_V)
    unit_lane = [jnp.where(lane == DENOM_LANE[sub], 1.0, 0.0).astype(BF16) for sub in range(2)]
    row = lax.broadcasted_iota(jnp.int32, (tq, tq), 0)
    col = lax.broadcasted_iota(jnp.int32, (tq, tq), 1)
    chunk_mask = (col // CHUNK) <= (row // CHUNK)
    causal_mask = col < row
    piece = min(tq, MXU_TILE)
    tri_m = _tri(piece)
    st_diag = pl.multiple_of(i * tq, tq)

    def load(ref, start, c0):
        return ref[0, pl.ds(start, tq), c0:c0 + LANES]

    def mla_block(sub, q_h, k, v, mask):
        s = _dot_nt(q_h, k)
        if mask is not None:
            s = jnp.where(mask, s, -jnp.inf)
            m_new = jnp.max(s, axis=-1, keepdims=True)
            acc_ref[sub] = _dot(jnp.exp2(s - m_new).astype(BF16), v)
        else:
            m_old = m_ref[sub]
            m_new = jnp.maximum(m_old, jnp.max(s, axis=-1, keepdims=True))
            acc_ref[sub] = jnp.exp2(m_old - m_new) * acc_ref[sub] + _dot(jnp.exp2(s - m_new).astype(BF16), v)
        m_ref[sub] = m_new

    def sb_block(sub, q_h, k, v, mask):
        log_sig, log_keep = _stick_terms(_dot_nt(q_h, k))
        if mask is not None:
            log_keep = jnp.where(mask, log_keep, 0.0)
        summand = log_keep.astype(BF16)
        pieces = []
        total = None
        for b in reversed(range(tq // piece)):
            sl = slice(b * piece, (b + 1) * piece)
            inner = _dot(summand[:, sl], tri_m)
            piece_total = jnp.sum(log_keep[:, sl], axis=-1, keepdims=True)
            pieces.append(inner if total is None else inner + total)
            total = piece_total if total is None else total + piece_total
        after = jnp.concatenate(pieces[::-1], axis=1)
        if mask is not None:
            a = jnp.where(mask, jnp.exp2(log_sig + after), 0.0)
            sacc_ref[sub] = _dot(a.astype(BF16), v)
            c_ref[sub] = total
        else:
            c_old = c_ref[sub]
            a = jnp.exp2(log_sig + after + c_old)
            sacc_ref[sub] = sacc_ref[sub] + _dot(a.astype(BF16), v)
            c_ref[sub] = c_old + total

    o_mla = []
    o_sb = []
    for first_pair in range(0, MLA_HEADS // 2, PAIRS_PER_LOOP):
        pairs = range(first_pair, first_pair + PAIRS_PER_LOOP)
        q_m = {h: qp_ref[0, :, h * HEAD_PAD:(h + 1) * HEAD_PAD] for p in pairs for h in (2 * p, 2 * p + 1)}
        q_s = {2 * p + sub: jnp.where(half_masks[sub], sbq_ref[0, :, p * LANES:(p + 1) * LANES], 0)
               for p in pairs for sub in range(2)}

        def group_blocks(mla_start, sb_start, masks, pairs=pairs, q_m=q_m, q_s=q_s):
            for p in pairs:
                vcol = p * LANES
                v_m = load(vmla_ref, mla_start, vcol)
                k_s = load(sbk_ref, sb_start, vcol)
                v_s = load(sbv_ref, sb_start, vcol)
                for sub in range(2):
                    head = 2 * p + sub
                    slot = head - 2 * pairs[0]
                    keep = half_masks[sub]
                    mla_block(slot, q_m[head], load(kmla_ref, mla_start, head * HEAD_PAD),
                              jnp.where(keep, v_m, unit_lane[sub]), masks[0])
                    sb_block(slot, q_s[head], k_s, jnp.where(keep, v_s, 0), masks[1])

        group_blocks(st_diag, st_diag, (chunk_mask, causal_mask))

        def step(t, _, group_blocks=group_blocks):
            group_blocks(pl.multiple_of(t * tq, tq), pl.multiple_of((i - 1 - t) * tq, tq), (None, None))
            return 0
        lax.fori_loop(0, i, step, 0)
        for p in range(PAIRS_PER_LOOP):
            heads_out = []
            for sub in range(2):
                acc = acc_ref[2 * p + sub]
                denom = acc[:, DENOM_LANE[sub]:DENOM_LANE[sub] + 1]
                heads_out.append(jnp.where(half_masks[sub], acc, 0.0) / denom)
            o_mla.append(heads_out[0] + heads_out[1])
            o_sb.append(sacc_ref[2 * p] + sacc_ref[2 * p + 1])

    _finish_mixer(jnp.concatenate(o_mla, axis=1), jnp.concatenate(o_sb, axis=1),
                  gates_ref, x_ref, mod_ref, wmu_ref, wsu_ref, wo_ref, g2_ref, x1_ref, h2_ref)


def _decode_attn_kernel(qp_ref, kmla_ref, vmla_ref, sbq_ref, sbk_ref, sbv_ref, pkmla_ref, pvmla_ref, psbk_ref,
                        psbv_ref, gates_ref, x_ref, mod_ref, wmu_ref, wsu_ref, wo_ref, g2_ref, x1_ref, h2_ref,
                        *, tq, past_len, past_blk):
    n_past = past_len // past_blk
    lane = lax.broadcasted_iota(jnp.int32, (1, LANES), 1)
    half_masks = (lane < MLA_V, lane >= MLA_V)
    row = lax.broadcasted_iota(jnp.int32, (tq, tq), 0)
    col = lax.broadcasted_iota(jnp.int32, (tq, tq), 1)
    chunk_mask = ((past_len + col) // CHUNK) <= ((past_len + row) // CHUNK)
    causal_mask = col < row
    tri_new = _tri(tq)
    tri_past = _tri(past_blk)

    o_mla = []
    o_sb = []
    for pair in range(MLA_HEADS // 2):
        vcol = pair * LANES
        v_new = vmla_ref[0, :, vcol:vcol + LANES]
        v_past = pvmla_ref[0, :, vcol:vcol + LANES]
        sk_new = sbk_ref[0, :, vcol:vcol + LANES]
        sv_new = sbv_ref[0, :, vcol:vcol + LANES]
        sk_past = psbk_ref[0, :, vcol:vcol + LANES].astype(BF16)
        sv_past = psbv_ref[0, :, vcol:vcol + LANES].astype(BF16)
        mla_pair = None
        sb_pair = None
        for sub in range(2):
            keep = half_masks[sub]
            kcol = (2 * pair + sub) * HEAD_PAD

            q_h = qp_ref[0, :, kcol:kcol + HEAD_PAD]
            s_past = _dot_nt(q_h, pkmla_ref[0, :, kcol:kcol + HEAD_PAD])
            s_new = jnp.where(chunk_mask, _dot_nt(q_h, kmla_ref[0, :, kcol:kcol + HEAD_PAD]), -jnp.inf)
            m = jnp.maximum(jnp.max(s_past, axis=-1, keepdims=True), jnp.max(s_new, axis=-1, keepdims=True))
            p_past = jnp.exp2(s_past - m)
            p_new = jnp.exp2(s_new - m)
            denom = jnp.sum(p_past, axis=-1, keepdims=True) + jnp.sum(p_new, axis=-1, keepdims=True)
            o = (_dot(p_past.astype(BF16), jnp.where(keep, v_past, 0))
                 + _dot(p_new.astype(BF16), jnp.where(keep, v_new, 0))) / denom
            mla_pair = o if mla_pair is None else mla_pair + o

            q_s = jnp.where(keep, sbq_ref[0, :, vcol:vcol + LANES], 0)
            ls_new, lk_new = _stick_terms(_dot_nt(q_s, sk_new))
            lk_new = jnp.where(causal_mask, lk_new, 0.0)
            hi, lo = _split_bf16(lk_new)
            a_new = jnp.where(causal_mask, jnp.exp2(ls_new + _dot(hi, tri_new) + _dot(lo, tri_new)), 0.0)
            acc = _dot(a_new.astype(BF16), jnp.where(keep, sv_new, 0))
            later = jnp.sum(lk_new, axis=-1, keepdims=True)

            ls_past, lk_past = _stick_terms(_dot_nt(q_s, sk_past))
            hi, lo = _split_bf16(lk_past)
            blocks = lambda a: [a[:, b * past_blk:(b + 1) * past_blk] for b in range(n_past)]
            stacked = jnp.concatenate(blocks(hi) + blocks(lo), axis=0)
            within = _dot(stacked, tri_past)
            after = []
            for b in reversed(range(n_past)):
                after.append(within[b * tq:(b + 1) * tq] + within[(n_past + b) * tq:(n_past + b + 1) * tq] + later)
                later = later + jnp.sum(lk_past[:, b * past_blk:(b + 1) * past_blk], axis=-1, keepdims=True)
            a_past = jnp.exp2(ls_past + jnp.concatenate(after[::-1], axis=1))
            acc = acc + _dot(a_past.astype(BF16), jnp.where(keep, sv_past, 0))
            sb_pair = acc if sb_pair is None else sb_pair + acc
        o_mla.append(mla_pair)
        o_sb.append(sb_pair)

    _finish_mixer(jnp.concatenate(o_mla, axis=1), jnp.concatenate(o_sb, axis=1),
                  gates_ref, x_ref, mod_ref, wmu_ref, wsu_ref, wo_ref, g2_ref, x1_ref, h2_ref)


def _mixer_call(kernel_fn, name, tq, args_kv, gates, x, mod, wts, scratch, batch0=0, n_batch=None):
    s = x.shape[1]
    b = x.shape[0] if n_batch is None else n_batch
    tok = lambda w: pl.BlockSpec((1, tq, w), lambda i, j: (i + batch0, j, 0))
    seq = lambda a: pl.BlockSpec((1,) + a.shape[1:], lambda i, j: (i + batch0, 0, 0))
    full = lambda a: pl.BlockSpec(a.shape, lambda i, j: (0,) * a.ndim)
    out = lambda w: pl.BlockSpec((1, tq, w), lambda i, j: (i, j, 0))
    args = list(args_kv) + [gates, x, mod, wts["w_mla_up"], wts["w_sb_up"], wts["w_out"], wts["g2"]]
    specs = [seq(a) if whole else tok(a.shape[-1]) for a, whole in
             zip(args_kv, (False, True, True, False) + (True,) * (len(args_kv) - 4))]
    specs += [tok(2 * D_MODEL), tok(D_MODEL), pl.BlockSpec((1, 6, D_MODEL), lambda i, j: (i + batch0, 0, 0)),
              full(wts["w_mla_up"]), full(wts["w_sb_up"]), full(wts["w_out"]), full(wts["g2"])]
    return pl.pallas_call(
        kernel_fn,
        grid=(b, s // tq),
        in_specs=specs,
        out_specs=[out(D_MODEL), out(HALF)],
        out_shape=[jax.ShapeDtypeStruct((b, s, D_MODEL), F32), jax.ShapeDtypeStruct((b, s, HALF), jnp.int32)],
        scratch_shapes=scratch,
        compiler_params=pltpu.CompilerParams(dimension_semantics=("parallel", "arbitrary"),
                                             vmem_limit_bytes=VMEM_LIMIT),
        name=name,
    )(*args)


def _prompt_attention(qp, kmla, vmla, sbq, sbk16, sbv16, gates, x, mod, wts, tq, batch0, n_batch):
    col = lambda: pltpu.VMEM((2 * PAIRS_PER_LOOP, tq, 1), F32)
    wide = lambda: pltpu.VMEM((2 * PAIRS_PER_LOOP, tq, LANES), F32)
    return _mixer_call(functools.partial(_prompt_attn_kernel, tq=tq), "attention", tq,
                       (qp, kmla, vmla, sbq, sbk16, sbv16), gates, x, mod, wts,
                       [col(), wide(), col(), wide()], batch0, n_batch)


def _decode_attention(qp, kmla, vmla, sbq, sbk16, sbv16, past, gates, x, mod, wts, past_blk):
    tq = x.shape[1]
    past_len = past[0].shape[1]
    kern = functools.partial(_decode_attn_kernel, tq=tq, past_len=past_len, past_blk=past_blk)
    return _mixer_call(kern, "decode_attention", tq, (qp, kmla, vmla, sbq, sbk16, sbv16) + tuple(past),
                       gates, x, mod, wts, [])


def _route_kernel(h2_ref, wr_ref, br_ref, idx_ref, wt_ref, rank_ref, cnt_ref, seen_ref):
    lo, hi = _unpack_halves(h2_ref[...])
    tm = lo.shape[0]
    logits = _dot_nt(wr_ref[:, :HALF], lo.astype(BF16)) + _dot_nt(wr_ref[:, HALF:], hi.astype(BF16))
    scores = jax.nn.sigmoid(logits)
    sel = scores + br_ref[...]
    neg = -jnp.inf

    grp = sel.reshape(N_GROUPS, GROUP_SIZE, tm)
    within = lax.broadcasted_iota(jnp.int32, grp.shape, 1)
    top1 = jnp.max(grp, axis=1, keepdims=True)
    first = jnp.min(jnp.where(grp == top1, within, GROUP_SIZE), axis=1, keepdims=True)
    top2 = jnp.max(jnp.where(within == first, neg, grp), axis=1, keepdims=True)
    gscore = (top1 + top2).reshape(N_GROUPS, tm)

    gid = lax.broadcasted_iota(jnp.int32, gscore.shape, 0)
    chosen = jnp.zeros(gscore.shape, jnp.bool_)
    for _ in range(TOPK_GROUPS):
        best = jnp.max(gscore, axis=0, keepdims=True)
        pick = jnp.min(jnp.where(gscore == best, gid, N_GROUPS), axis=0, keepdims=True)
        hit = gid == pick
        chosen = jnp.logical_or(chosen, hit)
        gscore = jnp.where(hit, neg, gscore)
    chosen3 = jnp.broadcast_to(chosen.reshape(N_GROUPS, 1, tm), grp.shape)
    cand = jnp.where(chosen3, grp, neg).reshape(N_EXPERTS, tm)
    outside = jnp.where(cand == neg, 1.0, 0.0)

    eid = lax.broadcasted_iota(jnp.int32, cand.shape, 0)
    picks = []
    weights = []
    for _ in range(TOP_K):
        best = jnp.max(cand, axis=0, keepdims=True)
        pick = jnp.min(jnp.where(cand == best, eid, N_EXPERTS), axis=0, keepdims=True)
        hit = eid == pick
        weights.append(jnp.sum(jnp.where(hit, scores, 0.0), axis=0, keepdims=True))
        picks.append(pick)
        cand = jnp.where(hit, neg, cand)
    w = jnp.concatenate(weights, axis=0)
    idx_ref[...] = jnp.concatenate(picks, axis=0)
    wt_ref[...] = w / (jnp.sum(w, axis=0, keepdims=True) + 1e-20) * ROUTED_SCALE

    @pl.when(pl.program_id(0) == 0)
    def _():
        seen_ref[...] = jnp.zeros_like(seen_ref)

    onehot = jnp.where(cand == neg, 1.0, 0.0) - outside
    src = lax.broadcasted_iota(jnp.int32, (tm, tm), 0)
    dst = lax.broadcasted_iota(jnp.int32, (tm, tm), 1)
    earlier = jnp.where(src < dst, 1.0, 0.0).astype(BF16)
    before = _dot(onehot.astype(BF16), earlier) + seen_ref[...]
    rank_ref[...] = jnp.concatenate(
        [jnp.sum(jnp.where(eid == pick, before, 0.0), axis=0, keepdims=True) for pick in picks],
        axis=0).astype(jnp.int32)
    seen = seen_ref[...] + jnp.sum(onehot, axis=1, keepdims=True)
    seen_ref[...] = seen
    cnt_ref[...] = seen.astype(jnp.int32)


def _route(h2p, wr_t, br, tm):
    t = h2p.shape[0]
    kt = lambda: pl.BlockSpec((TOP_K, tm), lambda i: (0, i))
    return pl.pallas_call(
        _route_kernel,
        grid=(t // tm,),
        in_specs=[pl.BlockSpec((tm, HALF), lambda i: (i, 0)),
                  pl.BlockSpec(wr_t.shape, lambda i: (0, 0)),
                  pl.BlockSpec(br.shape, lambda i: (0, 0))],
        out_specs=[kt(), kt(), kt(), pl.BlockSpec((N_EXPERTS, 1), lambda i: (0, 0))],
        out_shape=[jax.ShapeDtypeStruct((TOP_K, t), jnp.int32), jax.ShapeDtypeStruct((TOP_K, t), F32),
                   jax.ShapeDtypeStruct((TOP_K, t), jnp.int32), jax.ShapeDtypeStruct((N_EXPERTS, 1), jnp.int32)],
        scratch_shapes=[pltpu.VMEM((N_EXPERTS, 1), F32)],
        compiler_params=pltpu.CompilerParams(dimension_semantics=("arbitrary",)),
        name="route",
    )(h2p, wr_t, br)


def _position_kernel(idx_ref, rank_ref, start_ref, pos_ref):
    idx = idx_ref[...]
    eid = lax.broadcasted_iota(jnp.int32, (N_EXPERTS, idx.shape[1]), 0)
    start = start_ref[...]
    base = jnp.concatenate(
        [jnp.sum(jnp.where(eid == idx[k:k + 1, :], start, 0.0), axis=0, keepdims=True) for k in range(TOP_K)],
        axis=0)
    pos_ref[...] = base.astype(jnp.int32) + rank_ref[...]


def _positions(idx_kt, rank_kt, pad_start, tm):
    t = idx_kt.shape[1]
    tm = max(m for m in range(tm, 8 * tm + 1, tm) if t % m == 0)
    kt = lambda: pl.BlockSpec((TOP_K, tm), lambda i: (0, i))
    return pl.pallas_call(
        _position_kernel,
        grid=(t // tm,),
        in_specs=[kt(), kt(), pl.BlockSpec((N_EXPERTS, 1), lambda i: (0, 0))],
        out_specs=kt(),
        out_shape=jax.ShapeDtypeStruct((TOP_K, t), jnp.int32),
        compiler_params=pltpu.CompilerParams(dimension_semantics=("parallel",)),
        name="positions",
    )(idx_kt, rank_kt, pad_start.astype(F32).reshape(N_EXPERTS, 1))


def _sc_mesh():
    return plsc.VectorSubcoreMesh(core_axis_name="c", subcore_axis_name="s",
                                  num_cores=SC_CORES, num_subcores=SC_SUBCORES)


def _sc_scatter_rows(rows, pos, n_out, chunk):
    t, width = rows.shape
    copies = pos.shape[0] // t
    per_worker = t // SC_WORKERS
    n_chunks = per_worker // chunk
    tail = per_worker - n_chunks * chunk
    assert per_worker * SC_WORKERS == t and tail % 8 == 0
    tail_rows = max(tail, 8)

    @functools.partial(
        pl.kernel, mesh=_sc_mesh(),
        out_type=jax.ShapeDtypeStruct((n_out, width), rows.dtype),
        scratch_types=[pltpu.VMEM((copies, chunk), jnp.int32), pltpu.VMEM((chunk, width), rows.dtype),
                       pltpu.VMEM((copies, tail_rows), jnp.int32), pltpu.VMEM((tail_rows, width), rows.dtype),
                       pltpu.SemaphoreType.DMA, pltpu.SemaphoreType.DMA],
    )
    def scatter(rows_hbm, pos_hbm, out_hbm, idx_v, rows_v, idx_t, rows_t, load_sem, store_sem):
        wid = lax.axis_index("s") * SC_CORES + lax.axis_index("c")
        base = wid * per_worker

        def move(off, n, idx_buf, row_buf):
            loads = [pltpu.async_copy(rows_hbm.at[pl.ds(off, n)], row_buf, load_sem)]
            for k in range(copies):
                src = pos_hbm.at[pl.ds(pl.multiple_of(k * t + off, 8), n)]
                loads.append(pltpu.async_copy(src, idx_buf.at[k], load_sem))
            for cp in loads:
                cp.wait()
            stores = [pltpu.async_copy(row_buf, out_hbm.at[idx_buf.at[k]], store_sem) for k in range(copies)]
            for cp in stores:
                cp.wait()

        @pl.loop(0, n_chunks)
        def _(c):
            move(pl.multiple_of(base + c * chunk, 8), chunk, idx_v, rows_v)

        if tail:
            move(pl.multiple_of(base + n_chunks * chunk, 8), tail, idx_t, rows_t)

    return scatter(rows, pos)


def _sc_gather_rows(table, idx, chunk):
    n_rows = idx.shape[0]
    width = table.shape[1]
    per_worker = n_rows // SC_WORKERS
    n_chunks = per_worker // chunk
    assert per_worker * SC_WORKERS == n_rows and n_chunks * chunk == per_worker
    mesh = _sc_mesh()

    @functools.partial(
        pl.kernel, mesh=mesh,
        out_type=jax.ShapeDtypeStruct((n_rows, width), table.dtype),
        scratch_types=[pltpu.VMEM((chunk,), jnp.int32), pltpu.VMEM((chunk, width), table.dtype),
                       pltpu.SemaphoreType.DMA],
    )
    def gather(table_hbm, idx_hbm, out_hbm, idx_v, rows_v, sem):
        wid = lax.axis_index("s") * SC_CORES + lax.axis_index("c")
        base = wid * per_worker

        @pl.loop(0, n_chunks)
        def _(c):
            off = pl.multiple_of(base + c * chunk, 8)
            pltpu.sync_copy(idx_hbm.at[pl.ds(off, chunk)], idx_v)
            pltpu.async_copy(table_hbm.at[idx_v], rows_v, sem).wait()
            pltpu.sync_copy(rows_v, out_hbm.at[pl.ds(off, chunk)])

    return gather(table, idx)


ROW_SLOTS = 3
WEIGHT_SLOTS = 2


def _expert_kernel(be_ref, nv_ref, nu_ref, nxt_ref, xs_hbm, w1_hbm, w3_hbm, w2_hbm, os_ref,
                   wb1, wb3, wb2, xbuf, wf1, wf3, wf2, slot_ref, xsem, wsem, *, blk):
    i = pl.program_id(0)
    n_used = nu_ref[0]

    def rows_copy(j):
        src = xs_hbm.at[pl.ds(pl.multiple_of(j * blk, blk), blk)]
        return pltpu.make_async_copy(src, xbuf.at[j % ROW_SLOTS], xsem.at[j % ROW_SLOTS])

    def weight_copies(e, slot):
        return [pltpu.make_async_copy(w_hbm.at[e], buf.at[slot], wsem.at[slot])
                for w_hbm, buf in ((w1_hbm, wf1), (w3_hbm, wf3), (w2_hbm, wf2))]

    @pl.when(i == 0)
    def _():
        rows_copy(0).start()

        @pl.when(n_used > 1)
        def _():
            rows_copy(1).start()
        for cp in weight_copies(be_ref[0], 0):
            cp.start()
        slot_ref[0] = 1

    @pl.when(i + 2 < n_used)
    def _():
        rows_copy(i + 2).start()

    @pl.when(i < n_used)
    def _():
        @pl.when(jnp.logical_or(i == 0, be_ref[i] != be_ref[jnp.maximum(i - 1, 0)]))
        def _():
            slot = 1 - slot_ref[0]
            slot_ref[0] = slot
            for cp in weight_copies(be_ref[i], slot):
                cp.wait()

            @pl.when(nxt_ref[i] >= 0)
            def _():
                for cp in weight_copies(nxt_ref[i], 1 - slot):
                    cp.start()
            wb1[...] = wf1[slot].astype(BF16)
            wb3[...] = wf3[slot].astype(BF16)
            wb2[...] = wf2[slot].astype(BF16)

        rows_copy(i).wait()
        packed = xbuf[i % ROW_SLOTS]
        live = lax.broadcasted_iota(jnp.int32, packed.shape, 0) < nv_ref[i]
        lo, hi = _unpack_halves(jnp.where(live, packed, 0))
        lo = lo.astype(BF16)
        hi = hi.astype(BF16)
        a = _dot(lo, wb1[:HALF, :]) + _dot(hi, wb1[HALF:, :])
        b = _dot(lo, wb3[:HALF, :]) + _dot(hi, wb3[HALF:, :])
        o = _dot((_silu(a) * b).astype(BF16), wb2[...])
        os_ref[...] = _pack_halves(o[:, :HALF], o[:, HALF:])


def _experts(xs, blk_exp, blk_valid, n_used, next_exp, w1, w3, w2, blk):
    rows = xs.shape[0]
    nb = rows // blk
    any_space = pl.BlockSpec(memory_space=pl.ANY)
    grid_spec = pltpu.PrefetchScalarGridSpec(
        num_scalar_prefetch=4,
        grid=(nb,),
        in_specs=[any_space, any_space, any_space, any_space],
        out_specs=pl.BlockSpec((blk, HALF), lambda i, be, nv, nu, nx: (jnp.minimum(i, nu[0] - 1), 0)),
        scratch_shapes=[pltpu.VMEM((D_MODEL, EXPERT_DIM), BF16), pltpu.VMEM((D_MODEL, EXPERT_DIM), BF16),
                        pltpu.VMEM((EXPERT_DIM, D_MODEL), BF16),
                        pltpu.VMEM((ROW_SLOTS, blk, HALF), jnp.int32),
                        pltpu.VMEM((WEIGHT_SLOTS, D_MODEL, EXPERT_DIM), F32),
                        pltpu.VMEM((WEIGHT_SLOTS, D_MODEL, EXPERT_DIM), F32),
                        pltpu.VMEM((WEIGHT_SLOTS, EXPERT_DIM, D_MODEL), F32),
                        pltpu.SMEM((1,), jnp.int32),
                        pltpu.SemaphoreType.DMA((ROW_SLOTS,)), pltpu.SemaphoreType.DMA((WEIGHT_SLOTS,))],
    )
    return pl.pallas_call(
        functools.partial(_expert_kernel, blk=blk),
        grid_spec=grid_spec,
        out_shape=jax.ShapeDtypeStruct((rows, HALF), jnp.int32),
        compiler_params=pltpu.CompilerParams(dimension_semantics=("arbitrary",), vmem_limit_bytes=VMEM_LIMIT),
        name="experts",
    )(blk_exp, blk_valid, n_used, next_exp, xs, w1, w3, w2)


def _combine_kernel(g_ref, wt_ref, h2_ref, x1_ref, mod_ref, sw1_ref, sw3_ref, sw2_ref, gf_ref, *rest):
    y_ref = rest[-1]
    wt = wt_ref[...]
    lo_acc = None
    for k in range(TOP_K):
        lo, hi = _unpack_halves(g_ref[k])
        wk = wt[:, k:k + 1]
        lo_acc = wk * lo if lo_acc is None else lo_acc + wk * lo
        hi_acc = wk * hi if k == 0 else hi_acc + wk * hi
    routed = jnp.concatenate([lo_acc, hi_acc], axis=1)
    lo, hi = _unpack_halves(h2_ref[...])
    lo = lo.astype(BF16)
    hi = hi.astype(BF16)
    a = _dot(lo, sw1_ref[:HALF, :]) + _dot(hi, sw1_ref[HALF:, :])
    b = _dot(lo, sw3_ref[:HALF, :]) + _dot(hi, sw3_ref[HALF:, :])
    shared = _dot((_silu(a) * b).astype(BF16), sw2_ref[...])
    mod = mod_ref[0]
    x2 = x1_ref[...] + mod[5:6] * (routed + shared)
    y_ref[...] = _rms(x2) * gf_ref[...]


def _combine(gathered, wt, h2p, x1, mod, wts, tm, src_tile0, mod_batch0, tiles_per_batch, out_rows, out_tile0,
             y_prev=None):
    t = x1.shape[0]
    full = lambda a: pl.BlockSpec(a.shape, lambda i: (0,) * a.ndim)
    shifted = lambda w: pl.BlockSpec((tm, w), lambda i: (i + src_tile0, 0))
    args = [gathered, wt, h2p, x1, mod, wts["sw1"], wts["sw3"], wts["sw2"], wts["gf"]]
    specs = [pl.BlockSpec((TOP_K, tm, HALF), lambda i: (0, i + src_tile0, 0)), shifted(TOP_K), shifted(HALF),
             pl.BlockSpec((tm, D_MODEL), lambda i: (i, 0)),
             pl.BlockSpec((1, 6, D_MODEL), lambda i: (i // tiles_per_batch + mod_batch0, 0, 0)),
             full(wts["sw1"]), full(wts["sw3"]), full(wts["sw2"]), full(wts["gf"])]
    aliases = {}
    if y_prev is not None:
        args.append(y_prev)
        specs.append(pl.BlockSpec(memory_space=pl.ANY))
        aliases = {len(args) - 1: 0}
    return pl.pallas_call(
        _combine_kernel,
        grid=(t // tm,),
        in_specs=specs,
        out_specs=pl.BlockSpec((tm, D_MODEL), lambda i: (i + out_tile0, 0)),
        out_shape=jax.ShapeDtypeStruct((out_rows, D_MODEL), F32),
        input_output_aliases=aliases,
        compiler_params=pltpu.CompilerParams(dimension_semantics=("parallel",), vmem_limit_bytes=VMEM_LIMIT),
        name="combine",
    )(*args)


def _rope_tables(pos):
    half = MLA_ROPE // 2
    inv_freq = ROPE_THETA ** (-jnp.arange(half, dtype=F32) / half)
    ang = pos.astype(F32)[:, None] * inv_freq
    cos, sin = jnp.cos(ang), jnp.sin(ang)
    n = pos.shape[0]
    ones = jnp.ones((n, MLA_NOPE), F32)
    z_nope = jnp.zeros((n, MLA_NOPE), F32)
    z_pad = jnp.zeros((n, HEAD_PAD - MLA_NOPE - MLA_ROPE), F32)
    return (jnp.concatenate([ones, cos, cos, z_pad], axis=1),
            jnp.concatenate([z_nope, -sin, sin, z_pad], axis=1))


def _prep_weights(g_norm1, w_in, g_q_lat, w_uq, g_kv_lat, w_ukv, w_mla_up, w_sb_up, w_out, g_norm2,
                  w_router, b_router, shared_w1, shared_w3, shared_w2, g_final):
    w = w_in[0]
    o = 0
    parts = {}
    for name, width in (("qlat", MLA_Q_LORA), ("kv", MLA_KV_LORA), ("kr", MLA_ROPE), ("sbq", SB_WIDTH),
                        ("sbk", SB_WIDTH), ("sbv", SB_WIDTH), ("gm", D_MODEL), ("gs", D_MODEL)):
        parts[name] = w[:, o:o + width]
        o += width
    kr = parts["kr"]
    z_nope = jnp.zeros((D_MODEL, MLA_NOPE), F32)
    z_pad = jnp.zeros((D_MODEL, HEAD_PAD - MLA_NOPE - MLA_ROPE), F32)
    kr_seg = jnp.concatenate([z_nope, kr, z_pad], axis=1)
    w_in_r = jnp.concatenate([parts["qlat"], parts["kv"], parts["sbq"] * (SB_DIM ** -0.5 * LOG2_E), parts["sbk"],
                              parts["sbv"], parts["gm"], parts["gs"], kr_seg], axis=1).astype(BF16)

    scale = (MLA_NOPE + MLA_ROPE) ** -0.5 * LOG2_E
    uq = w_uq[0].reshape(MLA_Q_LORA, MLA_HEADS, MLA_NOPE + MLA_ROPE) * scale
    nope, rope = uq[..., :MLA_NOPE], uq[..., MLA_NOPE:]
    zq_pad = jnp.zeros((MLA_Q_LORA, MLA_HEADS, HEAD_PAD - MLA_NOPE - MLA_ROPE), F32)
    w_uq_r = jnp.concatenate([nope, rope, zq_pad], axis=-1).reshape(MLA_Q_LORA, QP_WIDTH).astype(BF16)

    ukv = w_ukv[0].reshape(MLA_KV_LORA, MLA_HEADS, MLA_NOPE + MLA_V)
    k_nope, v = ukv[..., :MLA_NOPE], ukv[..., MLA_NOPE:]
    k_pad = jnp.concatenate([k_nope, jnp.zeros((MLA_KV_LORA, MLA_HEADS, HEAD_PAD - MLA_NOPE), F32)], axis=-1)
    w_ukv_r = jnp.concatenate([k_pad.reshape(MLA_KV_LORA, QP_WIDTH),
                               v.reshape(MLA_KV_LORA, MLA_HEADS * MLA_V)], axis=1).astype(BF16)

    eye = jnp.eye(MLA_ROPE, dtype=F32)
    place_head = jnp.concatenate([jnp.zeros((MLA_ROPE, MLA_NOPE), F32), eye,
                                  jnp.zeros((MLA_ROPE, HEAD_PAD - MLA_NOPE - MLA_ROPE), F32)], axis=1)
    place = jnp.tile(place_head, (1, MLA_HEADS)).astype(BF16)

    return dict(
        g1=g_norm1[0].reshape(1, D_MODEL), w_in=w_in_r, gq=g_q_lat[0].reshape(1, MLA_Q_LORA), w_uq=w_uq_r,
        gkv=g_kv_lat[0].reshape(1, MLA_KV_LORA), w_ukv=w_ukv_r, place=place,
        w_mla_up=w_mla_up[0].astype(BF16), w_sb_up=w_sb_up[0].astype(BF16), w_out=w_out[0].astype(BF16),
        g2=g_norm2[0].reshape(1, D_MODEL), wr_t=w_router[0].T.astype(BF16),
        br=b_router[0].reshape(N_EXPERTS, 1),
        sw1=shared_w1[0].astype(BF16), sw3=shared_w3[0].astype(BF16), sw2=shared_w2[0].astype(BF16),
        gf=g_final.reshape(1, D_MODEL))


def _block_plan(counts, nb, blk):
    padded = (counts + blk - 1) // blk * blk
    pad_end = jnp.cumsum(padded)
    pad_start = pad_end - padded
    first_row = jnp.arange(nb, dtype=jnp.int32) * blk
    blk_exp = jnp.minimum(jnp.sum((pad_end[None, :] <= first_row[:, None]).astype(jnp.int32), axis=1),
                          N_EXPERTS - 1)
    own = blk_exp[:, None] == jnp.arange(N_EXPERTS, dtype=jnp.int32)[None, :]
    seg_end = jnp.sum(jnp.where(own, (pad_start + counts)[None, :], 0), axis=1)
    blk_valid = jnp.clip(seg_end - first_row, 0, blk).astype(jnp.int32)
    n_used = (pad_end[-1:] // blk).astype(jnp.int32)
    after_seg = jnp.sum(jnp.where(own, (pad_end // blk)[None, :], 0), axis=1)
    follower = jnp.sum(jnp.where(after_seg[:, None] == jnp.arange(nb, dtype=jnp.int32)[None, :],
                                 blk_exp[None, :], 0), axis=1)
    next_exp = jnp.where(after_seg < n_used[0], follower, -1).astype(jnp.int32)
    return pad_start, blk_exp.astype(jnp.int32), blk_valid, n_used, next_exp


def _moe_rows(h2p, idx_kt, rank_kt, counts, w1, w3, w2, scatter_rows, gather_rows, blk, tm):
    t = h2p.shape[0]
    nb = -(-t * TOP_K // blk) + N_EXPERTS
    pad_start, blk_exp, blk_valid, n_used, next_exp = _block_plan(counts.reshape(N_EXPERTS), nb, blk)
    pos = _positions(idx_kt, rank_kt, pad_start, tm).reshape(TOP_K * t)
    xs = scatter_rows(h2p, pos, nb * blk)
    os_ = _experts(xs, blk_exp, blk_valid, n_used, next_exp, w1, w3, w2, blk)
    return gather_rows(os_, pos).reshape(TOP_K, t, HALF)


def _forward(x_prompt, x_sample, cache_mla_ckv, cache_mla_krope, cache_sb_k, cache_sb_v, c_prompt, c_sample,
             w_ada, b_ada, moe_w1, moe_w3, moe_w2, wts, scatter_rows, gather_rows, token_block, attn_block,
             route_block, moe_block, wide_block):
    bp, sp, _ = x_prompt.shape
    bs, ss, _ = x_sample.shape
    past_len = cache_mla_ckv.shape[2]

    mod = _ada(jnp.concatenate([c_prompt, c_sample], axis=0), w_ada[0], b_ada[0]).reshape(bp + bs, 6, D_MODEL)
    mod_p, mod_s = mod[:bp], mod[bp:]

    cos_p, sin_p = _rope_tables(jnp.arange(sp))
    (qp, kmla, vmla, sbq, sbk16, sbv16, gates, ckv_p, krope_p, sbk_p, sbv_p) = _in_proj(
        x_prompt, mod_p, wts, cos_p, sin_p, wide_block)
    split_b = bp * GROUP_A_SHARE[0] // GROUP_A_SHARE[1]
    assert 0 < split_b < bp
    mixed = [_prompt_attention(qp, kmla, vmla, sbq, sbk16, sbv16, gates, x_prompt, mod_p, wts, attn_block,
                               b0, nb_) for b0, nb_ in ((0, split_b), (split_b, bp - split_b))]

    cos_s, sin_s = _rope_tables(past_len + jnp.arange(ss))
    (qs, kmla_s, vmla_s, sbq_s, sbk16_s, sbv16_s, gates_s, ckv_s, krope_s, sbk_s, sbv_s) = _in_proj(
        x_sample, mod_s, wts, cos_s, sin_s, ss)
    pkmla, pvmla = _kv_up(cache_mla_ckv[0], cache_mla_krope[0], wts["w_ukv"], wts["place"], wide_block)
    past = (pkmla, pvmla, cache_sb_k[0].reshape(bs, past_len, SB_WIDTH), cache_sb_v[0].reshape(bs, past_len, SB_WIDTH))
    x1_s, h2_s = _decode_attention(qs, kmla_s, vmla_s, sbq_s, sbk16_s, sbv16_s, past, gates_s, x_sample, mod_s,
                                   wts, token_block)

    tp, ts = bp * sp, bs * ss
    ta = split_b * sp
    tb = tp - ta

    def moe(h2_rows):
        idx_kt, wt_kt, rank_kt, counts = _route(h2_rows, wts["wr_t"], wts["br"], route_block)
        gathered = _moe_rows(h2_rows, idx_kt, rank_kt, counts, moe_w1[0], moe_w3[0], moe_w2[0],
                             scatter_rows, gather_rows, moe_block, route_block)
        return gathered, wt_kt.T

    tiles_per_batch = sp // wide_block
    (x1_a, h2_a), (x1_b, h2_b) = mixed
    h2_a = h2_a.reshape(ta, HALF)
    g_a, wt_a = moe(h2_a)
    y_p = _combine(g_a, wt_a, h2_a, x1_a.reshape(ta, D_MODEL), mod_p, wts, wide_block,
                   0, 0, tiles_per_batch, tp, 0)
    h2_b = jnp.concatenate([h2_b.reshape(tb, HALF), h2_s.reshape(ts, HALF)], axis=0)
    g_b, wt_b = moe(h2_b)
    y_p = _combine(g_b, wt_b, h2_b, x1_b.reshape(tb, D_MODEL), mod_p, wts, wide_block,
                   0, split_b, tiles_per_batch, tp, ta // wide_block, y_prev=y_p)
    y_s = _combine(g_b, wt_b, h2_b, x1_s.reshape(ts, D_MODEL), mod_s, wts, ss, tb // ss, 0, 1, ts, 0)

    heads = lambda a, b_, s_: a.reshape(1, b_, s_, SB_HEADS, SB_DIM)
    return (y_p.reshape(bp, sp, D_MODEL), y_s.reshape(bs, ss, D_MODEL),
            ckv_p[None], krope_p[None], heads(sbk_p, bp, sp), heads(sbv_p, bp, sp),
            ckv_s[None], krope_s[None], heads(sbk_s, bs, ss), heads(sbv_s, bs, ss))


def kernel(x_prompt, x_sample, cache_mla_ckv, cache_mla_krope, cache_sb_k, cache_sb_v, c_prompt, c_sample, w_ada, b_ada, g_norm1, w_in, g_q_lat, w_uq, g_kv_lat, w_ukv, w_mla_up, w_sb_up, w_out, g_norm2, w_router, b_router, moe_w1, moe_w3, moe_w2, shared_w1, shared_w3, shared_w2, g_final):
    wts = _prep_weights(g_norm1, w_in, g_q_lat, w_uq, g_kv_lat, w_ukv, w_mla_up, w_sb_up, w_out, g_norm2,
                        w_router, b_router, shared_w1, shared_w3, shared_w2, g_final)
    scatter_rows = functools.partial(_sc_scatter_rows, chunk=SC_CHUNK)
    gather_rows = functools.partial(_sc_gather_rows, chunk=SC_CHUNK)
    return _forward(x_prompt, x_sample, cache_mla_ckv, cache_mla_krope, cache_sb_k, cache_sb_v, c_prompt, c_sample,
                    w_ada, b_ada, moe_w1, moe_w3, moe_w2, wts, scatter_rows, gather_rows, TOKEN_BLOCK, ATTN_BLOCK,
                    TOKEN_BLOCK, MOE_BLOCK, WIDE_BLOCK)
```

```python
import functools

import jax
import jax.numpy as jnp
from jax import lax
from jax.experimental import pallas as pl
from jax.experimental.pallas import tpu as pltpu
from jax.experimental.pallas import tpu_sc as plsc

F32 = jnp.float32
BF16 = jnp.bfloat16

D_MODEL = 1024
NORM_EPS = 1e-6
CHUNK = 64
MLA_HEADS = 8
MLA_NOPE = 64
MLA_ROPE = 32
MLA_V = 64
MLA_Q_LORA = 384
MLA_KV_LORA = 256
ROPE_THETA = 10000.0
SB_HEADS = 8
SB_DIM = 64
SB_WIDTH = SB_HEADS * SB_DIM
N_EXPERTS = 256
TOP_K = 8
N_GROUPS = 8
TOPK_GROUPS = 4
GROUP_SIZE = N_EXPERTS // N_GROUPS
EXPERT_DIM = 256
ROUTED_SCALE = 2.5
LOG2_E = 1.4426950408889634

LANES = 128
MXU_TILE = 256
SC_CORES = 2
SC_SUBCORES = 16
SC_WORKERS = SC_CORES * SC_SUBCORES
VMEM_LIMIT = 60 * 1024 * 1024

HEAD_PAD = LANES
DENOM_LANE = (MLA_V, 0)
QP_WIDTH = MLA_HEADS * HEAD_PAD
HALF = D_MODEL // 2

C_QLAT = 0
C_KV = C_QLAT + MLA_Q_LORA
C_SBQ = C_KV + MLA_KV_LORA
C_SBK = C_SBQ + SB_WIDTH
C_SBV = C_SBK + SB_WIDTH
C_GATE = C_SBV + SB_WIDTH
C_KR = C_GATE + 2 * D_MODEL
C_END = C_KR + LANES

MOE_BLOCK = 384
WIDE_BLOCK = 512
TOKEN_BLOCK = 256
ATTN_BLOCK = 512
PAIRS_PER_LOOP = 1
SC_CHUNK = 64
GROUP_A_SHARE = (1, 2)


def _rms(x):
    return x * lax.rsqrt(jnp.mean(x * x, axis=-1, keepdims=True) + NORM_EPS)


def _silu(x):
    return x * jax.nn.sigmoid(x)


def _pack_halves(lo, hi):
    lo_bits = lax.bitcast_convert_type(lo.astype(BF16).astype(F32), jnp.uint32) >> 16
    hi_bits = lax.bitcast_convert_type(hi.astype(BF16).astype(F32), jnp.uint32) & jnp.uint32(0xFFFF0000)
    return lax.bitcast_convert_type(lo_bits | hi_bits, jnp.int32)


def _unpack_halves(p):
    u = lax.bitcast_convert_type(p, jnp.uint32)
    lo = lax.bitcast_convert_type(u << 16, F32)
    hi = lax.bitcast_convert_type(u & jnp.uint32(0xFFFF0000), F32)
    return lo, hi


def _dot(a, b):
    return jnp.dot(a, b, preferred_element_type=F32)


def _dot_nt(a, b):
    return lax.dot_general(a, b, (((1,), (1,)), ((), ())), preferred_element_type=F32)


def _ada_kernel(c_ref, w_ref, b_ref, o_ref):
    c = c_ref[...]
    o_ref[...] = _dot(_silu(c).astype(BF16), w_ref[...].astype(BF16)) + b_ref[...]


def _ada(c, w_ada, b_ada):
    n = c.shape[0]
    width = w_ada.shape[1]
    return pl.pallas_call(
        _ada_kernel,
        grid=(width // D_MODEL,),
        in_specs=[pl.BlockSpec((n, D_MODEL), lambda j: (0, 0)),
                  pl.BlockSpec((D_MODEL, D_MODEL), lambda j: (0, j)),
                  pl.BlockSpec((1, D_MODEL), lambda j: (0, j))],
        out_specs=pl.BlockSpec((n, D_MODEL), lambda j: (0, j)),
        out_shape=jax.ShapeDtypeStruct((n, width), F32),
        name="ada",
    )(c, w_ada, b_ada.reshape(1, width))


def _in_kernel(x_ref, mod_ref, g1_ref, win_ref, gq_ref, wuq_ref, gkv_ref, wukv_ref, cos_ref, sin_ref,
               qp_ref, kmla_ref, vmla_ref, sbq_ref, sbk16_ref, sbv16_ref, gates_ref,
               ckv_ref, krope_ref, sbk_ref, sbv_ref):
    x = x_ref[0]
    mod = mod_ref[0]
    h = _rms(x) * g1_ref[...] * (1.0 + mod[1:2]) + mod[0:1]
    hb = h.astype(BF16)

    def seg(a, b):
        return _dot(hb, win_ref[:, a:b])

    cos = cos_ref[...]
    sin = sin_ref[...]
    lane = lax.broadcasted_iota(jnp.int32, (1, LANES), 1)
    half = MLA_ROPE // 2

    def rotate(blk):
        other = jnp.where(lane < MLA_NOPE + half, pltpu.roll(blk, LANES - half, 1), pltpu.roll(blk, half, 1))
        return blk * cos + other * sin

    qn = (_rms(seg(C_QLAT, C_KV)) * gq_ref[...]).astype(BF16)
    q = _dot(qn, wuq_ref[...])
    qp_ref[0] = jnp.concatenate([rotate(q[:, h * HEAD_PAD:(h + 1) * HEAD_PAD]) for h in range(MLA_HEADS)],
                                axis=1).astype(BF16)

    ckv = _rms(seg(C_KV, C_SBQ)) * gkv_ref[...]
    ckv_ref[0] = ckv
    kv = _dot(ckv.astype(BF16), wukv_ref[...])
    krp = rotate(seg(C_KR, C_END))
    krope_ref[0] = krp[:, MLA_NOPE:MLA_NOPE + MLA_ROPE]
    kmla_ref[0] = (kv[:, :QP_WIDTH] + jnp.tile(krp, (1, MLA_HEADS))).astype(BF16)
    vmla_ref[0] = kv[:, QP_WIDTH:].astype(BF16)

    sbq_ref[0] = seg(C_SBQ, C_SBK).astype(BF16)
    sbk = seg(C_SBK, C_SBV)
    sbk_ref[0] = sbk
    sbk16_ref[0] = sbk.astype(BF16)
    sbv = seg(C_SBV, C_GATE)
    sbv_ref[0] = sbv
    sbv16_ref[0] = sbv.astype(BF16)
    gates_ref[0] = jax.nn.sigmoid(seg(C_GATE, C_KR)).astype(BF16)


def _in_proj(x, mod, wts, cos_t, sin_t, tm):
    b, s, _ = x.shape
    ns = s // tm
    tok = lambda w: pl.BlockSpec((1, tm, w), lambda i, j: (i, j, 0))
    full = lambda a: pl.BlockSpec(a.shape, lambda i, j: (0,) * a.ndim)
    out_widths = [(QP_WIDTH, BF16), (QP_WIDTH, BF16), (SB_WIDTH, BF16), (SB_WIDTH, BF16), (SB_WIDTH, BF16),
                  (SB_WIDTH, BF16), (2 * D_MODEL, BF16), (MLA_KV_LORA, F32), (MLA_ROPE, F32),
                  (SB_WIDTH, F32), (SB_WIDTH, F32)]
    return pl.pallas_call(
        _in_kernel,
        grid=(b, ns),
        in_specs=[tok(D_MODEL),
                  pl.BlockSpec((1, 6, D_MODEL), lambda i, j: (i, 0, 0)),
                  full(wts["g1"]), full(wts["w_in"]), full(wts["gq"]), full(wts["w_uq"]),
                  full(wts["gkv"]), full(wts["w_ukv"]),
                  pl.BlockSpec((tm, LANES), lambda i, j: (j, 0)),
                  pl.BlockSpec((tm, LANES), lambda i, j: (j, 0))],
        out_specs=[tok(w) for w, _ in out_widths],
        out_shape=[jax.ShapeDtypeStruct((b, s, w), dt) for w, dt in out_widths],
        compiler_params=pltpu.CompilerParams(dimension_semantics=("parallel", "parallel"),
                                             vmem_limit_bytes=VMEM_LIMIT),
        name="in_proj",
    )(x, mod, wts["g1"], wts["w_in"], wts["gq"], wts["w_uq"], wts["gkv"], wts["w_ukv"], cos_t, sin_t)


def _kvup_kernel(ckv_ref, kr_ref, wukv_ref, place_ref, kmla_ref, vmla_ref):
    kv = _dot(ckv_ref[0].astype(BF16), wukv_ref[...])
    kr = _dot(kr_ref[0].astype(BF16), place_ref[...])
    kmla_ref[0] = (kv[:, :QP_WIDTH] + kr).astype(BF16)
    vmla_ref[0] = kv[:, QP_WIDTH:].astype(BF16)


def _kv_up(ckv, krope, w_ukv_r, place, tm):
    b, p, _ = ckv.shape
    return pl.pallas_call(
        _kvup_kernel,
        grid=(b, p // tm),
        in_specs=[pl.BlockSpec((1, tm, MLA_KV_LORA), lambda i, j: (i, j, 0)),
                  pl.BlockSpec((1, tm, MLA_ROPE), lambda i, j: (i, j, 0)),
                  pl.BlockSpec(w_ukv_r.shape, lambda i, j: (0, 0)),
                  pl.BlockSpec(place.shape, lambda i, j: (0, 0))],
        out_specs=[pl.BlockSpec((1, tm, QP_WIDTH), lambda i, j: (i, j, 0)),
                   pl.BlockSpec((1, tm, SB_WIDTH), lambda i, j: (i, j, 0))],
        out_shape=[jax.ShapeDtypeStruct((b, p, QP_WIDTH), BF16), jax.ShapeDtypeStruct((b, p, SB_WIDTH), BF16)],
        compiler_params=pltpu.CompilerParams(dimension_semantics=("parallel", "parallel")),
        name="kv_up",
    )(ckv, krope, w_ukv_r, place)


def _tri(n):
    r = lax.broadcasted_iota(jnp.int32, (n, n), 0)
    c = lax.broadcasted_iota(jnp.int32, (n, n), 1)
    return jnp.where(r > c, 1.0, 0.0).astype(BF16)


def _stick_terms(z):
    log_sig = jnp.minimum(z, 0.0) - jnp.log2(1.0 + jnp.exp2(-jnp.abs(z)))
    return log_sig, log_sig - z


def _split_bf16(x):
    hi = x.astype(BF16)
    return hi, (x - hi.astype(F32)).astype(BF16)


def _finish_mixer(o_mla, o_sb, gates_ref, x_ref, mod_ref, wmu_ref, wsu_ref, wo_ref, g2_ref, x1_ref, h2_ref):
    u_mla = _dot(o_mla.astype(BF16), wmu_ref[...])
    u_sb = _dot(o_sb.astype(BF16), wsu_ref[...])
    gates = gates_ref[0]
    merged = gates[:, :D_MODEL].astype(F32) * u_mla + gates[:, D_MODEL:].astype(F32) * u_sb
    mix = _dot(merged.astype(BF16), wo_ref[...])
    mod = mod_ref[0]
    x1 = x_ref[0] + mod[2:3] * mix
    x1_ref[0] = x1
    h2 = _rms(x1) * g2_ref[...] * (1.0 + mod[4:5]) + mod[3:4]
    h2_ref[0] = _pack_halves(h2[:, :HALF], h2[:, HALF:])


def _prompt_attn_kernel(qp_ref, kmla_ref, vmla_ref, sbq_ref, sbk_ref, sbv_ref,
                        gates_ref, x_ref, mod_ref, wmu_ref, wsu_ref, wo_ref, g2_ref, x1_ref, h2_ref,
                        m_ref, acc_ref, c_ref, sacc_ref, *, tq):
    i = pl.program_id(1)
    lane = lax.broadcasted_iota(jnp.int32, (1, LANES), 1)
    half_masks = (lane < MLA_V, lane >= MLA_V)
    unit_lane = [jnp.where(lane == DENOM_LANE[sub], 1.0, 0.0).astype(BF16) for sub in range(2)]
    row = lax.broadcasted_iota(jnp.int32, (tq, tq), 0)
    col = lax.broadcasted_iota(jnp.int32, (tq, tq), 1)
    chunk_mask = (col // CHUNK) <= (row // CHUNK)
    causal_mask = col < row
    piece = min(tq, MXU_TILE)
    tri_m = _tri(piece)
    st_diag = pl.multiple_of(i * tq, tq)

    def load(ref, start, c0):
        return ref[0, pl.ds(start, tq), c0:c0 + LANES]

    def mla_block(sub, q_h, k, v, mask):
        s = _dot_nt(q_h, k)
        if mask is not None:
            s = jnp.where(mask, s, -jnp.inf)
            m_new = jnp.max(s, axis=-1, keepdims=True)
            acc_ref[sub] = _dot(jnp.exp2(s - m_new).astype(BF16), v)
        else:
            m_old = m_ref[sub]
            m_new = jnp.maximum(m_old, jnp.max(s, axis=-1, keepdims=True))
            acc_ref[sub] = jnp.exp2(m_old - m_new) * acc_ref[sub] + _dot(jnp.exp2(s - m_new).astype(BF16), v)
        m_ref[sub] = m_new

    def sb_block(sub, q_h, k, v, mask):
        log_sig, log_keep = _stick_terms(_dot_nt(q_h, k))
        if mask is not None:
            log_keep = jnp.where(mask, log_keep, 0.0)
        summand = log_keep.astype(BF16)
        pieces = []
        total = None
        for b in reversed(range(tq // piece)):
            sl = slice(b * piece, (b + 1) * piece)
            inner = _dot(summand[:, sl], tri_m)
            piece_total = jnp.sum(log_keep[:, sl], axis=-1, keepdims=True)
            pieces.append(inner if total is None else inner + total)
            total = piece_total if total is None else total + piece_total
        after = jnp.concatenate(pieces[::-1], axis=1)
        if mask is not None:
            a = jnp.where(mask, jnp.exp2(log_sig + after), 0.0)
            sacc_ref[sub] = _dot(a.astype(BF16), v)
            c_ref[sub] = total
        else:
            c_old = c_ref[sub]
            a = jnp.exp2(log_sig + after + c_old)
            sacc_ref[sub] = sacc_ref[sub] + _dot(a.astype(BF16), v)
            c_ref[sub] = c_old + total

    o_mla = []
    o_sb = []
    for first_pair in range(0, MLA_HEADS // 2, PAIRS_PER_LOOP):
        pairs = range(first_pair, first_pair + PAIRS_PER_LOOP)
        q_m = {h: qp_ref[0, :, h * HEAD_PAD:(h + 1) * HEAD_PAD] for p in pairs for h in (2 * p, 2 * p + 1)}
        q_s = {2 * p + sub: jnp.where(half_masks[sub], sbq_ref[0, :, p * LANES:(p + 1) * LANES], 0)
               for p in pairs for sub in range(2)}

        def group_blocks(mla_start, sb_start, masks, pairs=pairs, q_m=q_m, q_s=q_s):
            for p in pairs:
                vcol = p * LANES
                v_m = load(vmla_ref, mla_start, vcol)
                k_s = load(sbk_ref, sb_start, vcol)
                v_s = load(sbv_ref, sb_start, vcol)
                for sub in range(2):
                    head = 2 * p + sub
                    slot = head - 2 * pairs[0]
                    keep = half_masks[sub]
                    mla_block(slot, q_m[head], load(kmla_ref, mla_start, head * HEAD_PAD),
                              jnp.where(keep, v_m, unit_lane[sub]), masks[0])
                    sb_block(slot, q_s[head], k_s, jnp.where(keep, v_s, 0), masks[1])

        group_blocks(st_diag, st_diag, (chunk_mask, causal_mask))

        def step(t, _, group_blocks=group_blocks):
            group_blocks(pl.multiple_of(t * tq, tq), pl.multiple_of((i - 1 - t) * tq, tq), (None, None))
            return 0
        lax.fori_loop(0, i, step, 0)
        for p in range(PAIRS_PER_LOOP):
            heads_out = []
            for sub in range(2):
                acc = acc_ref[2 * p + sub]
                denom = acc[:, DENOM_LANE[sub]:DENOM_LANE[sub] + 1]
                heads_out.append(jnp.where(half_masks[sub], acc, 0.0) / denom)
            o_mla.append(heads_out[0] + heads_out[1])
            o_sb.append(sacc_ref[2 * p] + sacc_ref[2 * p + 1])

    _finish_mixer(jnp.concatenate(o_mla, axis=1), jnp.concatenate(o_sb, axis=1),
                  gates_ref, x_ref, mod_ref, wmu_ref, wsu_ref, wo_ref, g2_ref, x1_ref, h2_ref)


def _decode_attn_kernel(qp_ref, kmla_ref, vmla_ref, sbq_ref, sbk_ref, sbv_ref, pkmla_ref, pvmla_ref, psbk_ref,
                        psbv_ref, gates_ref, x_ref, mod_ref, wmu_ref, wsu_ref, wo_ref, g2_ref, x1_ref, h2_ref,
                        *, tq, past_len, past_blk):
    n_past = past_len // past_blk
    lane = lax.broadcasted_iota(jnp.int32, (1, LANES), 1)
    half_masks = (lane < MLA_V, lane >= MLA_V)
    row = lax.broadcasted_iota(jnp.int32, (tq, tq), 0)
    col = lax.broadcasted_iota(jnp.int32, (tq, tq), 1)
    chunk_mask = ((past_len + col) // CHUNK) <= ((past_len + row) // CHUNK)
    causal_mask = col < row
    tri_new = _tri(tq)
    tri_past = _tri(past_blk)

    o_mla = []
    o_sb = []
    for pair in range(MLA_HEADS // 2):
        vcol = pair * LANES
        v_new = vmla_ref[0, :, vcol:vcol + LANES]
        v_past = pvmla_ref[0, :, vcol:vcol + LANES]
        sk_new = sbk_ref[0, :, vcol:vcol + LANES]
        sv_new = sbv_ref[0, :, vcol:vcol + LANES]
        sk_past = psbk_ref[0, :, vcol:vcol + LANES].astype(BF16)
        sv_past = psbv_ref[0, :, vcol:vcol + LANES].astype(BF16)
        mla_pair = None
        sb_pair = None
        for sub in range(2):
            keep = half_masks[sub]
            kcol = (2 * pair + sub) * HEAD_PAD

            q_h = qp_ref[0, :, kcol:kcol + HEAD_PAD]
            s_past = _dot_nt(q_h, pkmla_ref[0, :, kcol:kcol + HEAD_PAD])
            s_new = jnp.where(chunk_mask, _dot_nt(q_h, kmla_ref[0, :, kcol:kcol + HEAD_PAD]), -jnp.inf)
            m = jnp.maximum(jnp.max(s_past, axis=-1, keepdims=True), jnp.max(s_new, axis=-1, keepdims=True))
            p_past = jnp.exp2(s_past - m)
            p_new = jnp.exp2(s_new - m)
            denom = jnp.sum(p_past, axis=-1, keepdims=True) + jnp.sum(p_new, axis=-1, keepdims=True)
            o = (_dot(p_past.astype(BF16), jnp.where(keep, v_past, 0))
                 + _dot(p_new.astype(BF16), jnp.where(keep, v_new, 0))) / denom
            mla_pair = o if mla_pair is None else mla_pair + o

            q_s = jnp.where(keep, sbq_ref[0, :, vcol:vcol + LANES], 0)
            ls_new, lk_new = _stick_terms(_dot_nt(q_s, sk_new))
            lk_new = jnp.where(causal_mask, lk_new, 0.0)
            hi, lo = _split_bf16(lk_new)
            a_new = jnp.where(causal_mask, jnp.exp2(ls_new + _dot(hi, tri_new) + _dot(lo, tri_new)), 0.0)
            acc = _dot(a_new.astype(BF16), jnp.where(keep, sv_new, 0))
            later = jnp.sum(lk_new, axis=-1, keepdims=True)

            ls_past, lk_past = _stick_terms(_dot_nt(q_s, sk_past))
            hi, lo = _split_bf16(lk_past)
            blocks = lambda a: [a[:, b * past_blk:(b + 1) * past_blk] for b in range(n_past)]
            stacked = jnp.concatenate(blocks(hi) + blocks(lo), axis=0)
            within = _dot(stacked, tri_past)
            after = []
            for b in reversed(range(n_past)):
                after.append(within[b * tq:(b + 1) * tq] + within[(n_past + b) * tq:(n_past + b + 1) * tq] + later)
                later = later + jnp.sum(lk_past[:, b * past_blk:(b + 1) * past_blk], axis=-1, keepdims=True)
            a_past = jnp.exp2(ls_past + jnp.concatenate(after[::-1], axis=1))
            acc = acc + _dot(a_past.astype(BF16), jnp.where(keep, sv_past, 0))
            sb_pair = acc if sb_pair is None else sb_pair + acc
        o_mla.append(mla_pair)
        o_sb.append(sb_pair)

    _finish_mixer(jnp.concatenate(o_mla, axis=1), jnp.concatenate(o_sb, axis=1),
                  gates_ref, x_ref, mod_ref, wmu_ref, wsu_ref, wo_ref, g2_ref, x1_ref, h2_ref)


def _mixer_call(kernel_fn, name, tq, args_kv, gates, x, mod, wts, scratch, batch0=0, n_batch=None):
    s = x.shape[1]
    b = x.shape[0] if n_batch is None else n_batch
    tok = lambda w: pl.BlockSpec((1, tq, w), lambda i, j: (i + batch0, j, 0))
    seq = lambda a: pl.BlockSpec((1,) + a.shape[1:], lambda i, j: (i + batch0, 0, 0))
    full = lambda a: pl.BlockSpec(a.shape, lambda i, j: (0,) * a.ndim)
    out = lambda w: pl.BlockSpec((1, tq, w), lambda i, j: (i, j, 0))
    args = list(args_kv) + [gates, x, mod, wts["w_mla_up"], wts["w_sb_up"], wts["w_out"], wts["g2"]]
    specs = [seq(a) if whole else tok(a.shape[-1]) for a, whole in
             zip(args_kv, (False, True, True, False) + (True,) * (len(args_kv) - 4))]
    specs += [tok(2 * D_MODEL), tok(D_MODEL), pl.BlockSpec((1, 6, D_MODEL), lambda i, j: (i + batch0, 0, 0)),
              full(wts["w_mla_up"]), full(wts["w_sb_up"]), full(wts["w_out"]), full(wts["g2"])]
    return pl.pallas_call(
        kernel_fn,
        grid=(b, s // tq),
        in_specs=specs,
        out_specs=[out(D_MODEL), out(HALF)],
        out_shape=[jax.ShapeDtypeStruct((b, s, D_MODEL), F32), jax.ShapeDtypeStruct((b, s, HALF), jnp.int32)],
        scratch_shapes=scratch,
        compiler_params=pltpu.CompilerParams(dimension_semantics=("parallel", "arbitrary"),
                                             vmem_limit_bytes=VMEM_LIMIT),
        name=name,
    )(*args)


def _prompt_attention(qp, kmla, vmla, sbq, sbk16, sbv16, gates, x, mod, wts, tq, batch0, n_batch):
    col = lambda: pltpu.VMEM((2 * PAIRS_PER_LOOP, tq, 1), F32)
    wide = lambda: pltpu.VMEM((2 * PAIRS_PER_LOOP, tq, LANES), F32)
    return _mixer_call(functools.partial(_prompt_attn_kernel, tq=tq), "attention", tq,
                       (qp, kmla, vmla, sbq, sbk16, sbv16), gates, x, mod, wts,
                       [col(), wide(), col(), wide()], batch0, n_batch)


def _decode_attention(qp, kmla, vmla, sbq, sbk16, sbv16, past, gates, x, mod, wts, past_blk):
    tq = x.shape[1]
    past_len = past[0].shape[1]
    kern = functools.partial(_decode_attn_kernel, tq=tq, past_len=past_len, past_blk=past_blk)
    return _mixer_call(kern, "decode_attention", tq, (qp, kmla, vmla, sbq, sbk16, sbv16) + tuple(past),
                       gates, x, mod, wts, [])


def _route_kernel(h2_ref, wr_ref, br_ref, idx_ref, wt_ref, rank_ref, cnt_ref, seen_ref):
    lo, hi = _unpack_halves(h2_ref[...])
    tm = lo.shape[0]
    logits = _dot_nt(wr_ref[:, :HALF], lo.astype(BF16)) + _dot_nt(wr_ref[:, HALF:], hi.astype(BF16))
    scores = jax.nn.sigmoid(logits)
    sel = scores + br_ref[...]
    neg = -jnp.inf

    grp = sel.reshape(N_GROUPS, GROUP_SIZE, tm)
    within = lax.broadcasted_iota(jnp.int32, grp.shape, 1)
    top1 = jnp.max(grp, axis=1, keepdims=True)
    first = jnp.min(jnp.where(grp == top1, within, GROUP_SIZE), axis=1, keepdims=True)
    top2 = jnp.max(jnp.where(within == first, neg, grp), axis=1, keepdims=True)
    gscore = (top1 + top2).reshape(N_GROUPS, tm)

    gid = lax.broadcasted_iota(jnp.int32, gscore.shape, 0)
    chosen = jnp.zeros(gscore.shape, jnp.bool_)
    for _ in range(TOPK_GROUPS):
        best = jnp.max(gscore, axis=0, keepdims=True)
        pick = jnp.min(jnp.where(gscore == best, gid, N_GROUPS), axis=0, keepdims=True)
        hit = gid == pick
        chosen = jnp.logical_or(chosen, hit)
        gscore = jnp.where(hit, neg, gscore)
    chosen3 = jnp.broadcast_to(chosen.reshape(N_GROUPS, 1, tm), grp.shape)
    cand = jnp.where(chosen3, grp, neg).reshape(N_EXPERTS, tm)
    outside = jnp.where(cand == neg, 1.0, 0.0)

    eid = lax.broadcasted_iota(jnp.int32, cand.shape, 0)
    picks = []
    weights = []
    for _ in range(TOP_K):
        best = jnp.max(cand, axis=0, keepdims=True)
        pick = jnp.min(jnp.where(cand == best, eid, N_EXPERTS), axis=0, keepdims=True)
        hit = eid == pick
        weights.append(jnp.sum(jnp.where(hit, scores, 0.0), axis=0, keepdims=True))
        picks.append(pick)
        cand = jnp.where(hit, neg, cand)
    w = jnp.concatenate(weights, axis=0)
    idx_ref[...] = jnp.concatenate(picks, axis=0)
    wt_ref[...] = w / (jnp.sum(w, axis=0, keepdims=True) + 1e-20) * ROUTED_SCALE

    @pl.when(pl.program_id(0) == 0)
    def _():
        seen_ref[...] = jnp.zeros_like(seen_ref)

    onehot = jnp.where(cand == neg, 1.0, 0.0) - outside
    src = lax.broadcasted_iota(jnp.int32, (tm, tm), 0)
    dst = lax.broadcasted_iota(jnp.int32, (tm, tm), 1)
    earlier = jnp.where(src < dst, 1.0, 0.0).astype(BF16)
    before = _dot(onehot.astype(BF16), earlier) + seen_ref[...]
    rank_ref[...] = jnp.concatenate(
        [jnp.sum(jnp.where(eid == pick, before, 0.0), axis=0, keepdims=True) for pick in picks],
        axis=0).astype(jnp.int32)
    seen = seen_ref[...] + jnp.sum(onehot, axis=1, keepdims=True)
    seen_ref[...] = seen
    cnt_ref[...] = seen.astype(jnp.int32)


def _route(h2p, wr_t, br, tm):
    t = h2p.shape[0]
    kt = lambda: pl.BlockSpec((TOP_K, tm), lambda i: (0, i))
    return pl.pallas_call(
        _route_kernel,
        grid=(t // tm,),
        in_specs=[pl.BlockSpec((tm, HALF), lambda i: (i, 0)),
                  pl.BlockSpec(wr_t.shape, lambda i: (0, 0)),
                  pl.BlockSpec(br.shape, lambda i: (0, 0))],
        out_specs=[kt(), kt(), kt(), pl.BlockSpec((N_EXPERTS, 1), lambda i: (0, 0))],
        out_shape=[jax.ShapeDtypeStruct((TOP_K, t), jnp.int32), jax.ShapeDtypeStruct((TOP_K, t), F32),
                   jax.ShapeDtypeStruct((TOP_K, t), jnp.int32), jax.ShapeDtypeStruct((N_EXPERTS, 1), jnp.int32)],
        scratch_shapes=[pltpu.VMEM((N_EXPERTS, 1), F32)],
        compiler_params=pltpu.CompilerParams(dimension_semantics=("arbitrary",)),
        name="route",
    )(h2p, wr_t, br)


def _position_kernel(idx_ref, rank_ref, start_ref, pos_ref):
    idx = idx_ref[...]
    eid = lax.broadcasted_iota(jnp.int32, (N_EXPERTS, idx.shape[1]), 0)
    start = start_ref[...]
    base = jnp.concatenate(
        [jnp.sum(jnp.where(eid == idx[k:k + 1, :], start, 0.0), axis=0, keepdims=True) for k in range(TOP_K)],
        axis=0)
    pos_ref[...] = base.astype(jnp.int32) + rank_ref[...]


def _positions(idx_kt, rank_kt, pad_start, tm):
    t = idx_kt.shape[1]
    tm = max(m for m in range(tm, 8 * tm + 1, tm) if t % m == 0)
    kt = lambda: pl.BlockSpec((TOP_K, tm), lambda i: (0, i))
    return pl.pallas_call(
        _position_kernel,
        grid=(t // tm,),
        in_specs=[kt(), kt(), pl.BlockSpec((N_EXPERTS, 1), lambda i: (0, 0))],
        out_specs=kt(),
        out_shape=jax.ShapeDtypeStruct((TOP_K, t), jnp.int32),
        compiler_params=pltpu.CompilerParams(dimension_semantics=("parallel",)),
        name="positions",
    )(idx_kt, rank_kt, pad_start.astype(F32).reshape(N_EXPERTS, 1))


def _sc_mesh():
    return plsc.VectorSubcoreMesh(core_axis_name="c", subcore_axis_name="s",
                                  num_cores=SC_CORES, num_subcores=SC_SUBCORES)


def _sc_scatter_rows(rows, pos, n_out, chunk):
    t, width = rows.shape
    copies = pos.shape[0] // t
    per_worker = t // SC_WORKERS
    n_chunks = per_worker // chunk
    tail = per_worker - n_chunks * chunk
    assert per_worker * SC_WORKERS == t and tail % 8 == 0
    tail_rows = max(tail, 8)

    @functools.partial(
        pl.kernel, mesh=_sc_mesh(),
        out_type=jax.ShapeDtypeStruct((n_out, width), rows.dtype),
        scratch_types=[pltpu.VMEM((copies, chunk), jnp.int32), pltpu.VMEM((chunk, width), rows.dtype),
                       pltpu.VMEM((copies, tail_rows), jnp.int32), pltpu.VMEM((tail_rows, width), rows.dtype),
                       pltpu.SemaphoreType.DMA, pltpu.SemaphoreType.DMA],
    )
    def scatter(rows_hbm, pos_hbm, out_hbm, idx_v, rows_v, idx_t, rows_t, load_sem, store_sem):
        wid = lax.axis_index("s") * SC_CORES + lax.axis_index("c")
        base = wid * per_worker

        def move(off, n, idx_buf, row_buf):
            loads = [pltpu.async_copy(rows_hbm.at[pl.ds(off, n)], row_buf, load_sem)]
            for k in range(copies):
                src = pos_hbm.at[pl.ds(pl.multiple_of(k * t + off, 8), n)]
                loads.append(pltpu.async_copy(src, idx_buf.at[k], load_sem))
            for cp in loads:
                cp.wait()
            stores = [pltpu.async_copy(row_buf, out_hbm.at[idx_buf.at[k]], store_sem) for k in range(copies)]
            for cp in stores:
                cp.wait()

        @pl.loop(0, n_chunks)
        def _(c):
            move(pl.multiple_of(base + c * chunk, 8), chunk, idx_v, rows_v)

        if tail:
            move(pl.multiple_of(base + n_chunks * chunk, 8), tail, idx_t, rows_t)

    return scatter(rows, pos)


def _sc_gather_rows(table, idx, chunk):
    n_rows = idx.shape[0]
    width = table.shape[1]
    per_worker = n_rows // SC_WORKERS
    n_chunks = per_worker // chunk
    assert per_worker * SC_WORKERS == n_rows and n_chunks * chunk == per_worker
    mesh = _sc_mesh()

    @functools.partial(
        pl.kernel, mesh=mesh,
        out_type=jax.ShapeDtypeStruct((n_rows, width), table.dtype),
        scratch_types=[pltpu.VMEM((chunk,), jnp.int32), pltpu.VMEM((chunk, width), table.dtype),
                       pltpu.SemaphoreType.DMA],
    )
    def gather(table_hbm, idx_hbm, out_hbm, idx_v, rows_v, sem):
        wid = lax.axis_index("s") * SC_CORES + lax.axis_index("c")
        base = wid * per_worker

        @pl.loop(0, n_chunks)
        def _(c):
            off = pl.multiple_of(base + c * chunk, 8)
            pltpu.sync_copy(idx_hbm.at[pl.ds(off, chunk)], idx_v)
            pltpu.async_copy(table_hbm.at[idx_v], rows_v, sem).wait()
            pltpu.sync_copy(rows_v, out_hbm.at[pl.ds(off, chunk)])

    return gather(table, idx)


ROW_SLOTS = 3
WEIGHT_SLOTS = 2


def _expert_kernel(be_ref, nv_ref, nu_ref, nxt_ref, xs_hbm, w1_hbm, w3_hbm, w2_hbm, os_ref,
                   wb1, wb3, wb2, xbuf, wf1, wf3, wf2, slot_ref, xsem, wsem, *, blk):
    i = pl.program_id(0)
    n_used = nu_ref[0]

    def rows_copy(j):
        src = xs_hbm.at[pl.ds(pl.multiple_of(j * blk, blk), blk)]
        return pltpu.make_async_copy(src, xbuf.at[j % ROW_SLOTS], xsem.at[j % ROW_SLOTS])

    def weight_copies(e, slot):
        return [pltpu.make_async_copy(w_hbm.at[e], buf.at[slot], wsem.at[slot])
                for w_hbm, buf in ((w1_hbm, wf1), (w3_hbm, wf3), (w2_hbm, wf2))]

    @pl.when(i == 0)
    def _():
        rows_copy(0).start()

        @pl.when(n_used > 1)
        def _():
            rows_copy(1).start()
        for cp in weight_copies(be_ref[0], 0):
            cp.start()
        slot_ref[0] = 1

    @pl.when(i + 2 < n_used)
    def _():
        rows_copy(i + 2).start()

    @pl.when(i < n_used)
    def _():
        @pl.when(jnp.logical_or(i == 0, be_ref[i] != be_ref[jnp.maximum(i - 1, 0)]))
        def _():
            slot = 1 - slot_ref[0]
            slot_ref[0] = slot
            for cp in weight_copies(be_ref[i], slot):
                cp.wait()

            @pl.when(nxt_ref[i] >= 0)
            def _():
                for cp in weight_copies(nxt_ref[i], 1 - slot):
                    cp.start()
            wb1[...] = wf1[slot].astype(BF16)
            wb3[...] = wf3[slot].astype(BF16)
            wb2[...] = wf2[slot].astype(BF16)

        rows_copy(i).wait()
        packed = xbuf[i % ROW_SLOTS]
        live = lax.broadcasted_iota(jnp.int32, packed.shape, 0) < nv_ref[i]
        lo, hi = _unpack_halves(jnp.where(live, packed, 0))
        lo = lo.astype(BF16)
        hi = hi.astype(BF16)
        a = _dot(lo, wb1[:HALF, :]) + _dot(hi, wb1[HALF:, :])
        b = _dot(lo, wb3[:HALF, :]) + _dot(hi, wb3[HALF:, :])
        o = _dot((_silu(a) * b).astype(BF16), wb2[...])
        os_ref[...] = _pack_halves(o[:, :HALF], o[:, HALF:])


def _experts(xs, blk_exp, blk_valid, n_used, next_exp, w1, w3, w2, blk):
    rows = xs.shape[0]
    nb = rows // blk
    any_space = pl.BlockSpec(memory_space=pl.ANY)
    grid_spec = pltpu.PrefetchScalarGridSpec(
        num_scalar_prefetch=4,
        grid=(nb,),
        in_specs=[any_space, any_space, any_space, any_space],
        out_specs=pl.BlockSpec((blk, HALF), lambda i, be, nv, nu, nx: (jnp.minimum(i, nu[0] - 1), 0)),
        scratch_shapes=[pltpu.VMEM((D_MODEL, EXPERT_DIM), BF16), pltpu.VMEM((D_MODEL, EXPERT_DIM), BF16),
                        pltpu.VMEM((EXPERT_DIM, D_MODEL), BF16),
                        pltpu.VMEM((ROW_SLOTS, blk, HALF), jnp.int32),
                        pltpu.VMEM((WEIGHT_SLOTS, D_MODEL, EXPERT_DIM), F32),
                        pltpu.VMEM((WEIGHT_SLOTS, D_MODEL, EXPERT_DIM), F32),
                        pltpu.VMEM((WEIGHT_SLOTS, EXPERT_DIM, D_MODEL), F32),
                        pltpu.SMEM((1,), jnp.int32),
                        pltpu.SemaphoreType.DMA((ROW_SLOTS,)), pltpu.SemaphoreType.DMA((WEIGHT_SLOTS,))],
    )
    return pl.pallas_call(
        functools.partial(_expert_kernel, blk=blk),
        grid_spec=grid_spec,
        out_shape=jax.ShapeDtypeStruct((rows, HALF), jnp.int32),
        compiler_params=pltpu.CompilerParams(dimension_semantics=("arbitrary",), vmem_limit_bytes=VMEM_LIMIT),
        name="experts",
    )(blk_exp, blk_valid, n_used, next_exp, xs, w1, w3, w2)


def _combine_kernel(g_ref, wt_ref, h2_ref, x1_ref, mod_ref, sw1_ref, sw3_ref, sw2_ref, gf_ref, *rest):
    y_ref = rest[-1]
    wt = wt_ref[...]
    lo_acc = None
    for k in range(TOP_K):
        lo, hi = _unpack_halves(g_ref[k])
        wk = wt[:, k:k + 1]
        lo_acc = wk * lo if lo_acc is None else lo_acc + wk * lo
        hi_acc = wk * hi if k == 0 else hi_acc + wk * hi
    routed = jnp.concatenate([lo_acc, hi_acc], axis=1)
    lo, hi = _unpack_halves(h2_ref[...])
    lo = lo.astype(BF16)
    hi = hi.astype(BF16)
    a = _dot(lo, sw1_ref[:HALF, :]) + _dot(hi, sw1_ref[HALF:, :])
    b = _dot(lo, sw3_ref[:HALF, :]) + _dot(hi, sw3_ref[HALF:, :])
    shared = _dot((_silu(a) * b).astype(BF16), sw2_ref[...])
    mod = mod_ref[0]
    x2 = x1_ref[...] + mod[5:6] * (routed + shared)
    y_ref[...] = _rms(x2) * gf_ref[...]


def _combine(gathered, wt, h2p, x1, mod, wts, tm, src_tile0, mod_batch0, tiles_per_batch, out_rows, out_tile0,
             y_prev=None):
    t = x1.shape[0]
    full = lambda a: pl.BlockSpec(a.shape, lambda i: (0,) * a.ndim)
    shifted = lambda w: pl.BlockSpec((tm, w), lambda i: (i + src_tile0, 0))
    args = [gathered, wt, h2p, x1, mod, wts["sw1"], wts["sw3"], wts["sw2"], wts["gf"]]
    specs = [pl.BlockSpec((TOP_K, tm, HALF), lambda i: (0, i + src_tile0, 0)), shifted(TOP_K), shifted(HALF),
             pl.BlockSpec((tm, D_MODEL), lambda i: (i, 0)),
             pl.BlockSpec((1, 6, D_MODEL), lambda i: (i // tiles_per_batch + mod_batch0, 0, 0)),
             full(wts["sw1"]), full(wts["sw3"]), full(wts["sw2"]), full(wts["gf"])]
    aliases = {}
    if y_prev is not None:
        args.append(y_prev)
        specs.append(pl.BlockSpec(memory_space=pl.ANY))
        aliases = {len(args) - 1: 0}
    return pl.pallas_call(
        _combine_kernel,
        grid=(t // tm,),
        in_specs=specs,
        out_specs=pl.BlockSpec((tm, D_MODEL), lambda i: (i + out_tile0, 0)),
        out_shape=jax.ShapeDtypeStruct((out_rows, D_MODEL), F32),
        input_output_aliases=aliases,
        compiler_params=pltpu.CompilerParams(dimension_semantics=("parallel",), vmem_limit_bytes=VMEM_LIMIT),
        name="combine",
    )(*args)


def _rope_tables(pos):
    half = MLA_ROPE // 2
    inv_freq = ROPE_THETA ** (-jnp.arange(half, dtype=F32) / half)
    ang = pos.astype(F32)[:, None] * inv_freq
    cos, sin = jnp.cos(ang), jnp.sin(ang)
    n = pos.shape[0]
    ones = jnp.ones((n, MLA_NOPE), F32)
    z_nope = jnp.zeros((n, MLA_NOPE), F32)
    z_pad = jnp.zeros((n, HEAD_PAD - MLA_NOPE - MLA_ROPE), F32)
    return (jnp.concatenate([ones, cos, cos, z_pad], axis=1),
            jnp.concatenate([z_nope, -sin, sin, z_pad], axis=1))


def _prep_weights(g_norm1, w_in, g_q_lat, w_uq, g_kv_lat, w_ukv, w_mla_up, w_sb_up, w_out, g_norm2,
                  w_router, b_router, shared_w1, shared_w3, shared_w2, g_final):
    w = w_in[0]
    o = 0
    parts = {}
    for name, width in (("qlat", MLA_Q_LORA), ("kv", MLA_KV_LORA), ("kr", MLA_ROPE), ("sbq", SB_WIDTH),
                        ("sbk", SB_WIDTH), ("sbv", SB_WIDTH), ("gm", D_MODEL), ("gs", D_MODEL)):
        parts[name] = w[:, o:o + width]
        o += width
    kr = parts["kr"]
    z_nope = jnp.zeros((D_MODEL, MLA_NOPE), F32)
    z_pad = jnp.zeros((D_MODEL, HEAD_PAD - MLA_NOPE - MLA_ROPE), F32)
    kr_seg = jnp.concatenate([z_nope, kr, z_pad], axis=1)
    w_in_r = jnp.concatenate([parts["qlat"], parts["kv"], parts["sbq"] * (SB_DIM ** -0.5 * LOG2_E), parts["sbk"],
                              parts["sbv"], parts["gm"], parts["gs"], kr_seg], axis=1).astype(BF16)

    scale = (MLA_NOPE + MLA_ROPE) ** -0.5 * LOG2_E
    uq = w_uq[0].reshape(MLA_Q_LORA, MLA_HEADS, MLA_NOPE + MLA_ROPE) * scale
    nope, rope = uq[..., :MLA_NOPE], uq[..., MLA_NOPE:]
    zq_pad = jnp.zeros((MLA_Q_LORA, MLA_HEADS, HEAD_PAD - MLA_NOPE - MLA_ROPE), F32)
    w_uq_r = jnp.concatenate([nope, rope, zq_pad], axis=-1).reshape(MLA_Q_LORA, QP_WIDTH).astype(BF16)

    ukv = w_ukv[0].reshape(MLA_KV_LORA, MLA_HEADS, MLA_NOPE + MLA_V)
    k_nope, v = ukv[..., :MLA_NOPE], ukv[..., MLA_NOPE:]
    k_pad = jnp.concatenate([k_nope, jnp.zeros((MLA_KV_LORA, MLA_HEADS, HEAD_PAD - MLA_NOPE), F32)], axis=-1)
    w_ukv_r = jnp.concatenate([k_pad.reshape(MLA_KV_LORA, QP_WIDTH),
                               v.reshape(MLA_KV_LORA, MLA_HEADS * MLA_V)], axis=1).astype(BF16)

    eye = jnp.eye(MLA_ROPE, dtype=F32)
    place_head = jnp.concatenate([jnp.zeros((MLA_ROPE, MLA_NOPE), F32), eye,
                                  jnp.zeros((MLA_ROPE, HEAD_PAD - MLA_NOPE - MLA_ROPE), F32)], axis=1)
    place = jnp.tile(place_head, (1, MLA_HEADS)).astype(BF16)

    return dict(
        g1=g_norm1[0].reshape(1, D_MODEL), w_in=w_in_r, gq=g_q_lat[0].reshape(1, MLA_Q_LORA), w_uq=w_uq_r,
        gkv=g_kv_lat[0].reshape(1, MLA_KV_LORA), w_ukv=w_ukv_r, place=place,
        w_mla_up=w_mla_up[0].astype(BF16), w_sb_up=w_sb_up[0].astype(BF16), w_out=w_out[0].astype(BF16),
        g2=g_norm2[0].reshape(1, D_MODEL), wr_t=w_router[0].T.astype(BF16),
        br=b_router[0].reshape(N_EXPERTS, 1),
        sw1=shared_w1[0].astype(BF16), sw3=shared_w3[0].astype(BF16), sw2=shared_w2[0].astype(BF16),
        gf=g_final.reshape(1, D_MODEL))


def _block_plan(counts, nb, blk):
    padded = (counts + blk - 1) // blk * blk
    pad_end = jnp.cumsum(padded)
    pad_start = pad_end - padded
    first_row = jnp.arange(nb, dtype=jnp.int32) * blk
    blk_exp = jnp.minimum(jnp.sum((pad_end[None, :] <= first_row[:, None]).astype(jnp.int32), axis=1),
                          N_EXPERTS - 1)
    own = blk_exp[:, None] == jnp.arange(N_EXPERTS, dtype=jnp.int32)[None, :]
    seg_end = jnp.sum(jnp.where(own, (pad_start + counts)[None, :], 0), axis=1)
    blk_valid = jnp.clip(seg_end - first_row, 0, blk).astype(jnp.int32)
    n_used = (pad_end[-1:] // blk).astype(jnp.int32)
    after_seg = jnp.sum(jnp.where(own, (pad_end // blk)[None, :], 0), axis=1)
    follower = jnp.sum(jnp.where(after_seg[:, None] == jnp.arange(nb, dtype=jnp.int32)[None, :],
                                 blk_exp[None, :], 0), axis=1)
    next_exp = jnp.where(after_seg < n_used[0], follower, -1).astype(jnp.int32)
    return pad_start, blk_exp.astype(jnp.int32), blk_valid, n_used, next_exp


def _moe_rows(h2p, idx_kt, rank_kt, counts, w1, w3, w2, scatter_rows, gather_rows, blk, tm):
    t = h2p.shape[0]
    nb = -(-t * TOP_K // blk) + N_EXPERTS
    pad_start, blk_exp, blk_valid, n_used, next_exp = _block_plan(counts.reshape(N_EXPERTS), nb, blk)
    pos = _positions(idx_kt, rank_kt, pad_start, tm).reshape(TOP_K * t)
    xs = scatter_rows(h2p, pos, nb * blk)
    os_ = _experts(xs, blk_exp, blk_valid, n_used, next_exp, w1, w3, w2, blk)
    return gather_rows(os_, pos).reshape(TOP_K, t, HALF)


def _forward(x_prompt, x_sample, cache_mla_ckv, cache_mla_krope, cache_sb_k, cache_sb_v, c_prompt, c_sample,
             w_ada, b_ada, moe_w1, moe_w3, moe_w2, wts, scatter_rows, gather_rows, token_block, attn_block,
             route_block, moe_block, wide_block):
    bp, sp, _ = x_prompt.shape
    bs, ss, _ = x_sample.shape
    past_len = cache_mla_ckv.shape[2]

    mod = _ada(jnp.concatenate([c_prompt, c_sample], axis=0), w_ada[0], b_ada[0]).reshape(bp + bs, 6, D_MODEL)
    mod_p, mod_s = mod[:bp], mod[bp:]

    cos_p, sin_p = _rope_tables(jnp.arange(sp))
    (qp, kmla, vmla, sbq, sbk16, sbv16, gates, ckv_p, krope_p, sbk_p, sbv_p) = _in_proj(
        x_prompt, mod_p, wts, cos_p, sin_p, wide_block)
    split_b = bp * GROUP_A_SHARE[0] // GROUP_A_SHARE[1]
    assert 0 < split_b < bp
    mixed = [_prompt_attention(qp, kmla, vmla, sbq, sbk16, sbv16, gates, x_prompt, mod_p, wts, attn_block,
                               b0, nb_) for b0, nb_ in ((0, split_b), (split_b, bp - split_b))]

    cos_s, sin_s = _rope_tables(past_len + jnp.arange(ss))
    (qs, kmla_s, vmla_s, sbq_s, sbk16_s, sbv16_s, gates_s, ckv_s, krope_s, sbk_s, sbv_s) = _in_proj(
        x_sample, mod_s, wts, cos_s, sin_s, ss)
    pkmla, pvmla = _kv_up(cache_mla_ckv[0], cache_mla_krope[0], wts["w_ukv"], wts["place"], wide_block)
    past = (pkmla, pvmla, cache_sb_k[0].reshape(bs, past_len, SB_WIDTH), cache_sb_v[0].reshape(bs, past_len, SB_WIDTH))
    x1_s, h2_s = _decode_attention(qs, kmla_s, vmla_s, sbq_s, sbk16_s, sbv16_s, past, gates_s, x_sample, mod_s,
                                   wts, token_block)

    tp, ts = bp * sp, bs * ss
    ta = split_b * sp
    tb = tp - ta

    def moe(h2_rows):
        idx_kt, wt_kt, rank_kt, counts = _route(h2_rows, wts["wr_t"], wts["br"], route_block)
        gathered = _moe_rows(h2_rows, idx_kt, rank_kt, counts, moe_w1[0], moe_w3[0], moe_w2[0],
                             scatter_rows, gather_rows, moe_block, route_block)
        return gathered, wt_kt.T

    tiles_per_batch = sp // wide_block
    (x1_a, h2_a), (x1_b, h2_b) = mixed
    h2_a = h2_a.reshape(ta, HALF)
    g_a, wt_a = moe(h2_a)
    y_p = _combine(g_a, wt_a, h2_a, x1_a.reshape(ta, D_MODEL), mod_p, wts, wide_block,
                   0, 0, tiles_per_batch, tp, 0)
    h2_b = jnp.concatenate([h2_b.reshape(tb, HALF), h2_s.reshape(ts, HALF)], axis=0)
    g_b, wt_b = moe(h2_b)
    y_p = _combine(g_b, wt_b, h2_b, x1_b.reshape(tb, D_MODEL), mod_p, wts, wide_block,
                   0, split_b, tiles_per_batch, tp, ta // wide_block, y_prev=y_p)
    y_s = _combine(g_b, wt_b, h2_b, x1_s.reshape(ts, D_MODEL), mod_s, wts, ss, tb // ss, 0, 1, ts, 0)

    heads = lambda a, b_, s_: a.reshape(1, b_, s_, SB_HEADS, SB_DIM)
    return (y_p.reshape(bp, sp, D_MODEL), y_s.reshape(bs, ss, D_MODEL),
            ckv_p[None], krope_p[None], heads(sbk_p, bp, sp), heads(sbv_p, bp, sp),
            ckv_s[None], krope_s[None], heads(sbk_s, bs, ss), heads(sbv_s, bs, ss))


def kernel(x_prompt, x_sample, cache_mla_ckv, cache_mla_krope, cache_sb_k, cache_sb_v, c_prompt, c_sample, w_ada, b_ada, g_norm1, w_in, g_q_lat, w_uq, g_kv_lat, w_ukv, w_mla_up, w_sb_up, w_out, g_norm2, w_router, b_router, moe_w1, moe_w3, moe_w2, shared_w1, shared_w3, shared_w2, g_final):
    wts = _prep_weights(g_norm1, w_in, g_q_lat, w_uq, g_kv_lat, w_ukv, w_mla_up, w_sb_up, w_out, g_norm2,
                        w_router, b_router, shared_w1, shared_w3, shared_w2, g_final)
    scatter_rows = functools.partial(_sc_scatter_rows, chunk=SC_CHUNK)
    gather_rows = functools.partial(_sc_gather_rows, chunk=SC_CHUNK)
    return _forward(x_prompt, x_sample, cache_mla_ckv, cache_mla_krope, cache_sb_k, cache_sb_v, c_prompt, c_sample,
                    w_ada, b_ada, moe_w1, moe_w3, moe_w2, wts, scatter_rows, gather_rows, TOKEN_BLOCK, ATTN_BLOCK,
                    TOKEN_BLOCK, MOE_BLOCK, WIDE_BLOCK)
```

```python
import functools

import jax
import jax.numpy as jnp
from jax import lax
from jax.experimental import pallas as pl
from jax.experimental.pallas import tpu as pltpu
from jax.experimental.pallas import tpu_sc as plsc

F32 = jnp.float32
BF16 = jnp.bfloat16

D_MODEL = 1024
NORM_EPS = 1e-6
CHUNK = 64
MLA_HEADS = 8
MLA_NOPE = 64
MLA_ROPE = 32
MLA_V = 64
MLA_Q_LORA = 384
MLA_KV_LORA = 256
ROPE_THETA = 10000.0
SB_HEADS = 8
SB_DIM = 64
SB_WIDTH = SB_HEADS * SB_DIM
N_EXPERTS = 256
TOP_K = 8
N_GROUPS = 8
TOPK_GROUPS = 4
GROUP_SIZE = N_EXPERTS // N_GROUPS
EXPERT_DIM = 256
ROUTED_SCALE = 2.5
LOG2_E = 1.4426950408889634

LANES = 128
MXU_TILE = 256
SC_CORES = 2
SC_SUBCORES = 16
SC_WORKERS = SC_CORES * SC_SUBCORES
VMEM_LIMIT = 60 * 1024 * 1024

HEAD_PAD = LANES
DENOM_LANE = (MLA_V, 0)
QP_WIDTH = MLA_HEADS * HEAD_PAD
HALF = D_MODEL // 2

C_QLAT = 0
C_KV = C_QLAT + MLA_Q_LORA
C_SBQ = C_KV + MLA_KV_LORA
C_SBK = C_SBQ + SB_WIDTH
C_SBV = C_SBK + SB_WIDTH
C_GATE = C_SBV + SB_WIDTH
C_KR = C_GATE + 2 * D_MODEL
C_END = C_KR + LANES

MOE_BLOCK = 512
WIDE_BLOCK = 512
TOKEN_BLOCK = 256
ATTN_BLOCK = 512
PAIRS_PER_LOOP = 1
SC_CHUNK = 64
GROUP_A_SHARE = (1, 2)


def _rms(x):
    return x * lax.rsqrt(jnp.mean(x * x, axis=-1, keepdims=True) + NORM_EPS)


def _silu(x):
    return x * jax.nn.sigmoid(x)


def _pack_halves(lo, hi):
    lo_bits = lax.bitcast_convert_type(lo.astype(BF16).astype(F32), jnp.uint32) >> 16
    hi_bits = lax.bitcast_convert_type(hi.astype(BF16).astype(F32), jnp.uint32) & jnp.uint32(0xFFFF0000)
    return lax.bitcast_convert_type(lo_bits | hi_bits, jnp.int32)


def _unpack_halves(p):
    u = lax.bitcast_convert_type(p, jnp.uint32)
    lo = lax.bitcast_convert_type(u << 16, F32)
    hi = lax.bitcast_convert_type(u & jnp.uint32(0xFFFF0000), F32)
    return lo, hi


def _dot(a, b):
    return jnp.dot(a, b, preferred_element_type=F32)


def _dot_nt(a, b):
    return lax.dot_general(a, b, (((1,), (1,)), ((), ())), preferred_element_type=F32)


def _ada_kernel(c_ref, w_ref, b_ref, o_ref):
    c = c_ref[...]
    o_ref[...] = _dot(_silu(c).astype(BF16), w_ref[...].astype(BF16)) + b_ref[...]


def _ada(c, w_ada, b_ada):
    n = c.shape[0]
    width = w_ada.shape[1]
    return pl.pallas_call(
        _ada_kernel,
        grid=(width // D_MODEL,),
        in_specs=[pl.BlockSpec((n, D_MODEL), lambda j: (0, 0)),
                  pl.BlockSpec((D_MODEL, D_MODEL), lambda j: (0, j)),
                  pl.BlockSpec((1, D_MODEL), lambda j: (0, j))],
        out_specs=pl.BlockSpec((n, D_MODEL), lambda j: (0, j)),
        out_shape=jax.ShapeDtypeStruct((n, width), F32),
        name="ada",
    )(c, w_ada, b_ada.reshape(1, width))


def _in_kernel(x_ref, mod_ref, g1_ref, win_ref, gq_ref, wuq_ref, gkv_ref, wukv_ref, cos_ref, sin_ref,
               qp_ref, kmla_ref, vmla_ref, sbq_ref, sbk16_ref, sbv16_ref, gates_ref,
               ckv_ref, krope_ref, sbk_ref, sbv_ref):
    x = x_ref[0]
    mod = mod_ref[0]
    h = _rms(x) * g1_ref[...] * (1.0 + mod[1:2]) + mod[0:1]
    hb = h.astype(BF16)

    def seg(a, b):
        return _dot(hb, win_ref[:, a:b])

    cos = cos_ref[...]
    sin = sin_ref[...]
    lane = lax.broadcasted_iota(jnp.int32, (1, LANES), 1)
    half = MLA_ROPE // 2

    def rotate(blk):
        other = jnp.where(lane < MLA_NOPE + half, pltpu.roll(blk, LANES - half, 1), pltpu.roll(blk, half, 1))
        return blk * cos + other * sin

    qn = (_rms(seg(C_QLAT, C_KV)) * gq_ref[...]).astype(BF16)
    q = _dot(qn, wuq_ref[...])
    qp_ref[0] = jnp.concatenate([rotate(q[:, h * HEAD_PAD:(h + 1) * HEAD_PAD]) for h in range(MLA_HEADS)],
                                axis=1).astype(BF16)

    ckv = _rms(seg(C_KV, C_SBQ)) * gkv_ref[...]
    ckv_ref[0] = ckv
    kv = _dot(ckv.astype(BF16), wukv_ref[...])
    krp = rotate(seg(C_KR, C_END))
    krope_ref[0] = krp[:, MLA_NOPE:MLA_NOPE + MLA_ROPE]
    kmla_ref[0] = (kv[:, :QP_WIDTH] + jnp.tile(krp, (1, MLA_HEADS))).astype(BF16)
    vmla_ref[0] = kv[:, QP_WIDTH:].astype(BF16)

    sbq_ref[0] = seg(C_SBQ, C_SBK).astype(BF16)
    sbk = seg(C_SBK, C_SBV)
    sbk_ref[0] = sbk
    sbk16_ref[0] = sbk.astype(BF16)
    sbv = seg(C_SBV, C_GATE)
    sbv_ref[0] = sbv
    sbv16_ref[0] = sbv.astype(BF16)
    gates_ref[0] = jax.nn.sigmoid(seg(C_GATE, C_KR)).astype(BF16)


def _in_proj(x, mod, wts, cos_t, sin_t, tm):
    b, s, _ = x.shape
    ns = s // tm
    tok = lambda w: pl.BlockSpec((1, tm, w), lambda i, j: (i, j, 0))
    full = lambda a: pl.BlockSpec(a.shape, lambda i, j: (0,) * a.ndim)
    out_widths = [(QP_WIDTH, BF16), (QP_WIDTH, BF16), (SB_WIDTH, BF16), (SB_WIDTH, BF16), (SB_WIDTH, BF16),
                  (SB_WIDTH, BF16), (2 * D_MODEL, BF16), (MLA_KV_LORA, F32), (MLA_ROPE, F32),
                  (SB_WIDTH, F32), (SB_WIDTH, F32)]
    return pl.pallas_call(
        _in_kernel,
        grid=(b, ns),
        in_specs=[tok(D_MODEL),
                  pl.BlockSpec((1, 6, D_MODEL), lambda i, j: (i, 0, 0)),
                  full(wts["g1"]), full(wts["w_in"]), full(wts["gq"]), full(wts["w_uq"]),
                  full(wts["gkv"]), full(wts["w_ukv"]),
                  pl.BlockSpec((tm, LANES), lambda i, j: (j, 0)),
                  pl.BlockSpec((tm, LANES), lambda i, j: (j, 0))],
        out_specs=[tok(w) for w, _ in out_widths],
        out_shape=[jax.ShapeDtypeStruct((b, s, w), dt) for w, dt in out_widths],
        compiler_params=pltpu.CompilerParams(dimension_semantics=("parallel", "parallel"),
                                             vmem_limit_bytes=VMEM_LIMIT),
        name="in_proj",
    )(x, mod, wts["g1"], wts["w_in"], wts["gq"], wts["w_uq"], wts["gkv"], wts["w_ukv"], cos_t, sin_t)


def _kvup_kernel(ckv_ref, kr_ref, wukv_ref, place_ref, kmla_ref, vmla_ref):
    kv = _dot(ckv_ref[0].astype(BF16), wukv_ref[...])
    kr = _dot(kr_ref[0].astype(BF16), place_ref[...])
    kmla_ref[0] = (kv[:, :QP_WIDTH] + kr).astype(BF16)
    vmla_ref[0] = kv[:, QP_WIDTH:].astype(BF16)


def _kv_up(ckv, krope, w_ukv_r, place, tm):
    b, p, _ = ckv.shape
    return pl.pallas_call(
        _kvup_kernel,
        grid=(b, p // tm),
        in_specs=[pl.BlockSpec((1, tm, MLA_KV_LORA), lambda i, j: (i, j, 0)),
                  pl.BlockSpec((1, tm, MLA_ROPE), lambda i, j: (i, j, 0)),
                  pl.BlockSpec(w_ukv_r.shape, lambda i, j: (0, 0)),
                  pl.BlockSpec(place.shape, lambda i, j: (0, 0))],
        out_specs=[pl.BlockSpec((1, tm, QP_WIDTH), lambda i, j: (i, j, 0)),
                   pl.BlockSpec((1, tm, SB_WIDTH), lambda i, j: (i, j, 0))],
        out_shape=[jax.ShapeDtypeStruct((b, p, QP_WIDTH), BF16), jax.ShapeDtypeStruct((b, p, SB_WIDTH), BF16)],
        compiler_params=pltpu.CompilerParams(dimension_semantics=("parallel", "parallel")),
        name="kv_up",
    )(ckv, krope, w_ukv_r, place)


def _tri(n):
    r = lax.broadcasted_iota(jnp.int32, (n, n), 0)
    c = lax.broadcasted_iota(jnp.int32, (n, n), 1)
    return jnp.where(r > c, 1.0, 0.0).astype(BF16)


def _stick_terms(z):
    log_sig = jnp.minimum(z, 0.0) - jnp.log2(1.0 + jnp.exp2(-jnp.abs(z)))
    return log_sig, log_sig - z


def _split_bf16(x):
    hi = x.astype(BF16)
    return hi, (x - hi.astype(F32)).astype(BF16)


def _finish_mixer(o_mla, o_sb, gates_ref, x_ref, mod_ref, wmu_ref, wsu_ref, wo_ref, g2_ref, x1_ref, h2_ref):
    u_mla = _dot(o_mla.astype(BF16), wmu_ref[...])
    u_sb = _dot(o_sb.astype(BF16), wsu_ref[...])
    gates = gates_ref[0]
    merged = gates[:, :D_MODEL].astype(F32) * u_mla + gates[:, D_MODEL:].astype(F32) * u_sb
    mix = _dot(merged.astype(BF16), wo_ref[...])
    mod = mod_ref[0]
    x1 = x_ref[0] + mod[2:3] * mix
    x1_ref[0] = x1
    h2 = _rms(x1) * g2_ref[...] * (1.0 + mod[4:5]) + mod[3:4]
    h2_ref[0] = _pack_halves(h2[:, :HALF], h2[:, HALF:])


def _prompt_attn_kernel(qp_ref, kmla_ref, vmla_ref, sbq_ref, sbk_ref, sbv_ref,
                        gates_ref, x_ref, mod_ref, wmu_ref, wsu_ref, wo_ref, g2_ref, x1_ref, h2_ref,
                        m_ref, acc_ref, c_ref, sacc_ref, *, tq):
    i = pl.program_id(1)
    lane = lax.broadcasted_iota(jnp.int32, (1, LANES), 1)
    half_masks = (lane < MLA_V, lane >= MLA_V)
    unit_lane = [jnp.where(lane == DENOM_LANE[sub], 1.0, 0.0).astype(BF16) for sub in range(2)]
    row = lax.broadcasted_iota(jnp.int32, (tq, tq), 0)
    col = lax.broadcasted_iota(jnp.int32, (tq, tq), 1)
    chunk_mask = (col // CHUNK) <= (row // CHUNK)
    causal_mask = col < row
    piece = min(tq, MXU_TILE)
    tri_m = _tri(piece)
    st_diag = pl.multiple_of(i * tq, tq)

    def load(ref, start, c0):
        return ref[0, pl.ds(start, tq), c0:c0 + LANES]

    def mla_block(sub, q_h, k, v, mask):
        s = _dot_nt(q_h, k)
        if mask is not None:
            s = jnp.where(mask, s, -jnp.inf)
            m_new = jnp.max(s, axis=-1, keepdims=True)
            acc_ref[sub] = _dot(jnp.exp2(s - m_new).astype(BF16), v)
        else:
            m_old = m_ref[sub]
            m_new = jnp.maximum(m_old, jnp.max(s, axis=-1, keepdims=True))
            acc_ref[sub] = jnp.exp2(m_old - m_new) * acc_ref[sub] + _dot(jnp.exp2(s - m_new).astype(BF16), v)
        m_ref[sub] = m_new

    def sb_block(sub, q_h, k, v, mask):
        log_sig, log_keep = _stick_terms(_dot_nt(q_h, k))
        if mask is not None:
            log_keep = jnp.where(mask, log_keep, 0.0)
        summand = log_keep.astype(BF16)
        pieces = []
        total = None
        for b in reversed(range(tq // piece)):
            sl = slice(b * piece, (b + 1) * piece)
            inner = _dot(summand[:, sl], tri_m)
            piece_total = jnp.sum(log_keep[:, sl], axis=-1, keepdims=True)
            pieces.append(inner if total is None else inner + total)
            total = piece_total if total is None else total + piece_total
        after = jnp.concatenate(pieces[::-1], axis=1)
        if mask is not None:
            a = jnp.where(mask, jnp.exp2(log_sig + after), 0.0)
            sacc_ref[sub] = _dot(a.astype(BF16), v)
            c_ref[sub] = total
        else:
            c_old = c_ref[sub]
            a = jnp.exp2(log_sig + after + c_old)
            sacc_ref[sub] = sacc_ref[sub] + _dot(a.astype(BF16), v)
            c_ref[sub] = c_old + total

    o_mla = []
    o_sb = []
    for first_pair in range(0, MLA_HEADS // 2, PAIRS_PER_LOOP):
        pairs = range(first_pair, first_pair + PAIRS_PER_LOOP)
        q_m = {h: qp_ref[0, :, h * HEAD_PAD:(h + 1) * HEAD_PAD] for p in pairs for h in (2 * p, 2 * p + 1)}
        q_s = {2 * p + sub: jnp.where(half_masks[sub], sbq_ref[0, :, p * LANES:(p + 1) * LANES], 0)
               for p in pairs for sub in range(2)}

        def group_blocks(mla_start, sb_start, masks, pairs=pairs, q_m=q_m, q_s=q_s):
            for p in pairs:
                vcol = p * LANES
                v_m = load(vmla_ref, mla_start, vcol)
                k_s = load(sbk_ref, sb_start, vcol)
                v_s = load(sbv_ref, sb_start, vcol)
                for sub in range(2):
                    head = 2 * p + sub
                    slot = head - 2 * pairs[0]
                    keep = half_masks[sub]
                    mla_block(slot, q_m[head], load(kmla_ref, mla_start, head * HEAD_PAD),
                              jnp.where(keep, v_m, unit_lane[sub]), masks[0])
                    sb_block(slot, q_s[head], k_s, jnp.where(keep, v_s, 0), masks[1])

        group_blocks(st_diag, st_diag, (chunk_mask, causal_mask))

        def step(t, _, group_blocks=group_blocks):
            group_blocks(pl.multiple_of(t * tq, tq), pl.multiple_of((i - 1 - t) * tq, tq), (None, None))
            return 0
        lax.fori_loop(0, i, step, 0)
        for p in range(PAIRS_PER_LOOP):
            heads_out = []
            for sub in range(2):
                acc = acc_ref[2 * p + sub]
                denom = acc[:, DENOM_LANE[sub]:DENOM_LANE[sub] + 1]
                heads_out.append(jnp.where(half_masks[sub], acc, 0.0) / denom)
            o_mla.append(heads_out[0] + heads_out[1])
            o_sb.append(sacc_ref[2 * p] + sacc_ref[2 * p + 1])

    _finish_mixer(jnp.concatenate(o_mla, axis=1), jnp.concatenate(o_sb, axis=1),
                  gates_ref, x_ref, mod_ref, wmu_ref, wsu_ref, wo_ref, g2_ref, x1_ref, h2_ref)


def _decode_attn_kernel(qp_ref, kmla_ref, vmla_ref, sbq_ref, sbk_ref, sbv_ref, pkmla_ref, pvmla_ref, psbk_ref,
                        psbv_ref, gates_ref, x_ref, mod_ref, wmu_ref, wsu_ref, wo_ref, g2_ref, x1_ref, h2_ref,
                        *, tq, past_len, past_blk):
    n_past = past_len // past_blk
    lane = lax.broadcasted_iota(jnp.int32, (1, LANES), 1)
    half_masks = (lane < MLA_V, lane >= MLA_V)
    row = lax.broadcasted_iota(jnp.int32, (tq, tq), 0)
    col = lax.broadcasted_iota(jnp.int32, (tq, tq), 1)
    chunk_mask = ((past_len + col) // CHUNK) <= ((past_len + row) // CHUNK)
    causal_mask = col < row
    tri_new = _tri(tq)
    tri_past = _tri(past_blk)

    o_mla = []
    o_sb = []
    for pair in range(MLA_HEADS // 2):
        vcol = pair * LANES
        v_new = vmla_ref[0, :, vcol:vcol + LANES]
        v_past = pvmla_ref[0, :, vcol:vcol + LANES]
        sk_new = sbk_ref[0, :, vcol:vcol + LANES]
        sv_new = sbv_ref[0, :, vcol:vcol + LANES]
        sk_past = psbk_ref[0, :, vcol:vcol + LANES].astype(BF16)
        sv_past = psbv_ref[0, :, vcol:vcol + LANES].astype(BF16)
        mla_pair = None
        sb_pair = None
        for sub in range(2):
            keep = half_masks[sub]
            kcol = (2 * pair + sub) * HEAD_PAD

            q_h = qp_ref[0, :, kcol:kcol + HEAD_PAD]
            s_past = _dot_nt(q_h, pkmla_ref[0, :, kcol:kcol + HEAD_PAD])
            s_new = jnp.where(chunk_mask, _dot_nt(q_h, kmla_ref[0, :, kcol:kcol + HEAD_PAD]), -jnp.inf)
            m = jnp.maximum(jnp.max(s_past, axis=-1, keepdims=True), jnp.max(s_new, axis=-1, keepdims=True))
            p_past = jnp.exp2(s_past - m)
            p_new = jnp.exp2(s_new - m)
            denom = jnp.sum(p_past, axis=-1, keepdims=True) + jnp.sum(p_new, axis=-1, keepdims=True)
            o = (_dot(p_past.astype(BF16), jnp.where(keep, v_past, 0))
                 + _dot(p_new.astype(BF16), jnp.where(keep, v_new, 0))) / denom
            mla_pair = o if mla_pair is None else mla_pair + o

            q_s = jnp.where(keep, sbq_ref[0, :, vcol:vcol + LANES], 0)
            ls_new, lk_new = _stick_terms(_dot_nt(q_s, sk_new))
            lk_new = jnp.where(causal_mask, lk_new, 0.0)
            hi, lo = _split_bf16(lk_new)
            a_new = jnp.where(causal_mask, jnp.exp2(ls_new + _dot(hi, tri_new) + _dot(lo, tri_new)), 0.0)
            acc = _dot(a_new.astype(BF16), jnp.where(keep, sv_new, 0))
            later = jnp.sum(lk_new, axis=-1, keepdims=True)

            ls_past, lk_past = _stick_terms(_dot_nt(q_s, sk_past))
            hi, lo = _split_bf16(lk_past)
            blocks = lambda a: [a[:, b * past_blk:(b + 1) * past_blk] for b in range(n_past)]
            stacked = jnp.concatenate(blocks(hi) + blocks(lo), axis=0)
            within = _dot(stacked, tri_past)
            after = []
            for b in reversed(range(n_past)):
                after.append(within[b * tq:(b + 1) * tq] + within[(n_past + b) * tq:(n_past + b + 1) * tq] + later)
                later = later + jnp.sum(lk_past[:, b * past_blk:(b + 1) * past_blk], axis=-1, keepdims=True)
            a_past = jnp.exp2(ls_past + jnp.concatenate(after[::-1], axis=1))
            acc = acc + _dot(a_past.astype(BF16), jnp.where(keep, sv_past, 0))
            sb_pair = acc if sb_pair is None else sb_pair + acc
        o_mla.append(mla_pair)
        o_sb.append(sb_pair)

    _finish_mixer(jnp.concatenate(o_mla, axis=1), jnp.concatenate(o_sb, axis=1),
                  gates_ref, x_ref, mod_ref, wmu_ref, wsu_ref, wo_ref, g2_ref, x1_ref, h2_ref)


def _mixer_call(kernel_fn, name, tq, args_kv, gates, x, mod, wts, scratch, batch0=0, n_batch=None):
    s = x.shape[1]
    b = x.shape[0] if n_batch is None else n_batch
    tok = lambda w: pl.BlockSpec((1, tq, w), lambda i, j: (i + batch0, j, 0))
    seq = lambda a: pl.BlockSpec((1,) + a.shape[1:], lambda i, j: (i + batch0, 0, 0))
    full = lambda a: pl.BlockSpec(a.shape, lambda i, j: (0,) * a.ndim)
    out = lambda w: pl.BlockSpec((1, tq, w), lambda i, j: (i, j, 0))
    args = list(args_kv) + [gates, x, mod, wts["w_mla_up"], wts["w_sb_up"], wts["w_out"], wts["g2"]]
    specs = [seq(a) if whole else tok(a.shape[-1]) for a, whole in
             zip(args_kv, (False, True, True, False) + (True,) * (len(args_kv) - 4))]
    specs += [tok(2 * D_MODEL), tok(D_MODEL), pl.BlockSpec((1, 6, D_MODEL), lambda i, j: (i + batch0, 0, 0)),
              full(wts["w_mla_up"]), full(wts["w_sb_up"]), full(wts["w_out"]), full(wts["g2"])]
    return pl.pallas_call(
        kernel_fn,
        grid=(b, s // tq),
        in_specs=specs,
        out_specs=[out(D_MODEL), out(HALF)],
        out_shape=[jax.ShapeDtypeStruct((b, s, D_MODEL), F32), jax.ShapeDtypeStruct((b, s, HALF), jnp.int32)],
        scratch_shapes=scratch,
        compiler_params=pltpu.CompilerParams(dimension_semantics=("parallel", "arbitrary"),
                                             vmem_limit_bytes=VMEM_LIMIT),
        name=name,
    )(*args)


def _prompt_attention(qp, kmla, vmla, sbq, sbk16, sbv16, gates, x, mod, wts, tq, batch0, n_batch):
    col = lambda: pltpu.VMEM((2 * PAIRS_PER_LOOP, tq, 1), F32)
    wide = lambda: pltpu.VMEM((2 * PAIRS_PER_LOOP, tq, LANES), F32)
    return _mixer_call(functools.partial(_prompt_attn_kernel, tq=tq), "attention", tq,
                       (qp, kmla, vmla, sbq, sbk16, sbv16), gates, x, mod, wts,
                       [col(), wide(), col(), wide()], batch0, n_batch)


def _decode_attention(qp, kmla, vmla, sbq, sbk16, sbv16, past, gates, x, mod, wts, past_blk):
    tq = x.shape[1]
    past_len = past[0].shape[1]
    kern = functools.partial(_decode_attn_kernel, tq=tq, past_len=past_len, past_blk=past_blk)
    return _mixer_call(kern, "decode_attention", tq, (qp, kmla, vmla, sbq, sbk16, sbv16) + tuple(past),
                       gates, x, mod, wts, [])


def _route_kernel(h2_ref, wr_ref, br_ref, idx_ref, wt_ref, rank_ref, cnt_ref, seen_ref):
    lo, hi = _unpack_halves(h2_ref[...])
    tm = lo.shape[0]
    logits = _dot_nt(wr_ref[:, :HALF], lo.astype(BF16)) + _dot_nt(wr_ref[:, HALF:], hi.astype(BF16))
    scores = jax.nn.sigmoid(logits)
    sel = scores + br_ref[...]
    neg = -jnp.inf

    grp = sel.reshape(N_GROUPS, GROUP_SIZE, tm)
    within = lax.broadcasted_iota(jnp.int32, grp.shape, 1)
    top1 = jnp.max(grp, axis=1, keepdims=True)
    first = jnp.min(jnp.where(grp == top1, within, GROUP_SIZE), axis=1, keepdims=True)
    top2 = jnp.max(jnp.where(within == first, neg, grp), axis=1, keepdims=True)
    gscore = (top1 + top2).reshape(N_GROUPS, tm)

    gid = lax.broadcasted_iota(jnp.int32, gscore.shape, 0)
    chosen = jnp.zeros(gscore.shape, jnp.bool_)
    for _ in range(TOPK_GROUPS):
        best = jnp.max(gscore, axis=0, keepdims=True)
        pick = jnp.min(jnp.where(gscore == best, gid, N_GROUPS), axis=0, keepdims=True)
        hit = gid == pick
        chosen = jnp.logical_or(chosen, hit)
        gscore = jnp.where(hit, neg, gscore)
    chosen3 = jnp.broadcast_to(chosen.reshape(N_GROUPS, 1, tm), grp.shape)
    cand = jnp.where(chosen3, grp, neg).reshape(N_EXPERTS, tm)
    outside = jnp.where(cand == neg, 1.0, 0.0)

    eid = lax.broadcasted_iota(jnp.int32, cand.shape, 0)
    picks = []
    weights = []
    for _ in range(TOP_K):
        best = jnp.max(cand, axis=0, keepdims=True)
        pick = jnp.min(jnp.where(cand == best, eid, N_EXPERTS), axis=0, keepdims=True)
        hit = eid == pick
        weights.append(jnp.sum(jnp.where(hit, scores, 0.0), axis=0, keepdims=True))
        picks.append(pick)
        cand = jnp.where(hit, neg, cand)
    w = jnp.concatenate(weights, axis=0)
    idx_ref[...] = jnp.concatenate(picks, axis=0)
    wt_ref[...] = w / (jnp.sum(w, axis=0, keepdims=True) + 1e-20) * ROUTED_SCALE

    @pl.when(pl.program_id(0) == 0)
    def _():
        seen_ref[...] = jnp.zeros_like(seen_ref)

    onehot = jnp.where(cand == neg, 1.0, 0.0) - outside
    src = lax.broadcasted_iota(jnp.int32, (tm, tm), 0)
    dst = lax.broadcasted_iota(jnp.int32, (tm, tm), 1)
    earlier = jnp.where(src < dst, 1.0, 0.0).astype(BF16)
    before = _dot(onehot.astype(BF16), earlier) + seen_ref[...]
    rank_ref[...] = jnp.concatenate(
        [jnp.sum(jnp.where(eid == pick, before, 0.0), axis=0, keepdims=True) for pick in picks],
        axis=0).astype(jnp.int32)
    seen = seen_ref[...] + jnp.sum(onehot, axis=1, keepdims=True)
    seen_ref[...] = seen
    cnt_ref[...] = seen.astype(jnp.int32)


def _route(h2p, wr_t, br, tm):
    t = h2p.shape[0]
    kt = lambda: pl.BlockSpec((TOP_K, tm), lambda i: (0, i))
    return pl.pallas_call(
        _route_kernel,
        grid=(t // tm,),
        in_specs=[pl.BlockSpec((tm, HALF), lambda i: (i, 0)),
                  pl.BlockSpec(wr_t.shape, lambda i: (0, 0)),
                  pl.BlockSpec(br.shape, lambda i: (0, 0))],
        out_specs=[kt(), kt(), kt(), pl.BlockSpec((N_EXPERTS, 1), lambda i: (0, 0))],
        out_shape=[jax.ShapeDtypeStruct((TOP_K, t), jnp.int32), jax.ShapeDtypeStruct((TOP_K, t), F32),
                   jax.ShapeDtypeStruct((TOP_K, t), jnp.int32), jax.ShapeDtypeStruct((N_EXPERTS, 1), jnp.int32)],
        scratch_shapes=[pltpu.VMEM((N_EXPERTS, 1), F32)],
        compiler_params=pltpu.CompilerParams(dimension_semantics=("arbitrary",)),
        name="route",
    )(h2p, wr_t, br)


def _position_kernel(idx_ref, rank_ref, start_ref, pos_ref):
    idx = idx_ref[...]
    eid = lax.broadcasted_iota(jnp.int32, (N_EXPERTS, idx.shape[1]), 0)
    start = start_ref[...]
    base = jnp.concatenate(
        [jnp.sum(jnp.where(eid == idx[k:k + 1, :], start, 0.0), axis=0, keepdims=True) for k in range(TOP_K)],
        axis=0)
    pos_ref[...] = base.astype(jnp.int32) + rank_ref[...]


def _positions(idx_kt, rank_kt, pad_start, tm):
    t = idx_kt.shape[1]
    tm = max(m for m in range(tm, 8 * tm + 1, tm) if t % m == 0)
    kt = lambda: pl.BlockSpec((TOP_K, tm), lambda i: (0, i))
    return pl.pallas_call(
        _position_kernel,
        grid=(t // tm,),
        in_specs=[kt(), kt(), pl.BlockSpec((N_EXPERTS, 1), lambda i: (0, 0))],
        out_specs=kt(),
        out_shape=jax.ShapeDtypeStruct((TOP_K, t), jnp.int32),
        compiler_params=pltpu.CompilerParams(dimension_semantics=("parallel",)),
        name="positions",
    )(idx_kt, rank_kt, pad_start.astype(F32).reshape(N_EXPERTS, 1))


def _sc_mesh():
    return plsc.VectorSubcoreMesh(core_axis_name="c", subcore_axis_name="s",
                                  num_cores=SC_CORES, num_subcores=SC_SUBCORES)


def _sc_scatter_rows(rows, pos, n_out, chunk):
    t, width = rows.shape
    copies = pos.shape[0] // t
    per_worker = t // SC_WORKERS
    n_chunks = per_worker // chunk
    tail = per_worker - n_chunks * chunk
    assert per_worker * SC_WORKERS == t and tail % 8 == 0
    tail_rows = max(tail, 8)

    @functools.partial(
        pl.kernel, mesh=_sc_mesh(),
        out_type=jax.ShapeDtypeStruct((n_out, width), rows.dtype),
        scratch_types=[pltpu.VMEM((copies, chunk), jnp.int32), pltpu.VMEM((chunk, width), rows.dtype),
                       pltpu.VMEM((copies, tail_rows), jnp.int32), pltpu.VMEM((tail_rows, width), rows.dtype),
                       pltpu.SemaphoreType.DMA, pltpu.SemaphoreType.DMA],
    )
    def scatter(rows_hbm, pos_hbm, out_hbm, idx_v, rows_v, idx_t, rows_t, load_sem, store_sem):
        wid = lax.axis_index("s") * SC_CORES + lax.axis_index("c")
        base = wid * per_worker

        def move(off, n, idx_buf, row_buf):
            loads = [pltpu.async_copy(rows_hbm.at[pl.ds(off, n)], row_buf, load_sem)]
            for k in range(copies):
                src = pos_hbm.at[pl.ds(pl.multiple_of(k * t + off, 8), n)]
                loads.append(pltpu.async_copy(src, idx_buf.at[k], load_sem))
            for cp in loads:
                cp.wait()
            stores = [pltpu.async_copy(row_buf, out_hbm.at[idx_buf.at[k]], store_sem) for k in range(copies)]
            for cp in stores:
                cp.wait()

        @pl.loop(0, n_chunks)
        def _(c):
            move(pl.multiple_of(base + c * chunk, 8), chunk, idx_v, rows_v)

        if tail:
            move(pl.multiple_of(base + n_chunks * chunk, 8), tail, idx_t, rows_t)

    return scatter(rows, pos)


def _sc_gather_rows(table, idx, chunk):
    n_rows = idx.shape[0]
    width = table.shape[1]
    per_worker = n_rows // SC_WORKERS
    n_chunks = per_worker // chunk
    assert per_worker * SC_WORKERS == n_rows and n_chunks * chunk == per_worker
    mesh = _sc_mesh()

    @functools.partial(
        pl.kernel, mesh=mesh,
        out_type=jax.ShapeDtypeStruct((n_rows, width), table.dtype),
        scratch_types=[pltpu.VMEM((chunk,), jnp.int32), pltpu.VMEM((chunk, width), table.dtype),
                       pltpu.SemaphoreType.DMA],
    )
    def gather(table_hbm, idx_hbm, out_hbm, idx_v, rows_v, sem):
        wid = lax.axis_index("s") * SC_CORES + lax.axis_index("c")
        base = wid * per_worker

        @pl.loop(0, n_chunks)
        def _(c):
            off = pl.multiple_of(base + c * chunk, 8)
            pltpu.sync_copy(idx_hbm.at[pl.ds(off, chunk)], idx_v)
            pltpu.async_copy(table_hbm.at[idx_v], rows_v, sem).wait()
            pltpu.sync_copy(rows_v, out_hbm.at[pl.ds(off, chunk)])

    return gather(table, idx)


ROW_SLOTS = 4
WEIGHT_SLOTS = 2


def _expert_kernel(be_ref, nv_ref, nu_ref, nxt_ref, xs_hbm, w1_hbm, w3_hbm, w2_hbm, os_ref,
                   wb1, wb3, wb2, xbuf, wf1, wf3, wf2, slot_ref, xsem, wsem, *, blk):
    i = pl.program_id(0)
    n_used = nu_ref[0]

    def rows_copy(j):
        src = xs_hbm.at[pl.ds(pl.multiple_of(j * blk, blk), blk)]
        return pltpu.make_async_copy(src, xbuf.at[j % ROW_SLOTS], xsem.at[j % ROW_SLOTS])

    def weight_copies(e, slot):
        return [pltpu.make_async_copy(w_hbm.at[e], buf.at[slot], wsem.at[slot])
                for w_hbm, buf in ((w1_hbm, wf1), (w3_hbm, wf3), (w2_hbm, wf2))]

    ahead = ROW_SLOTS - 1

    @pl.when(i == 0)
    def _():
        rows_copy(0).start()
        for j in range(1, ahead):
            @pl.when(n_used > j)
            def _(j=j):
                rows_copy(j).start()
        for cp in weight_copies(be_ref[0], 0):
            cp.start()
        slot_ref[0] = 1

    @pl.when(i + ahead < n_used)
    def _():
        rows_copy(i + ahead).start()

    @pl.when(i < n_used)
    def _():
        @pl.when(jnp.logical_or(i == 0, be_ref[i] != be_ref[jnp.maximum(i - 1, 0)]))
        def _():
            slot = 1 - slot_ref[0]
            slot_ref[0] = slot
            for cp in weight_copies(be_ref[i], slot):
                cp.wait()

            @pl.when(nxt_ref[i] >= 0)
            def _():
                for cp in weight_copies(nxt_ref[i], 1 - slot):
                    cp.start()
            wb1[...] = wf1[slot].astype(BF16)
            wb3[...] = wf3[slot].astype(BF16)
            wb2[...] = wf2[slot].astype(BF16)

        rows_copy(i).wait()
        packed = xbuf[i % ROW_SLOTS]
        live = lax.broadcasted_iota(jnp.int32, packed.shape, 0) < nv_ref[i]
        lo, hi = _unpack_halves(jnp.where(live, packed, 0))
        lo = lo.astype(BF16)
        hi = hi.astype(BF16)
        a = _dot(lo, wb1[:HALF, :]) + _dot(hi, wb1[HALF:, :])
        b = _dot(lo, wb3[:HALF, :]) + _dot(hi, wb3[HALF:, :])
        o = _dot((_silu(a) * b).astype(BF16), wb2[...])
        os_ref[...] = _pack_halves(o[:, :HALF], o[:, HALF:])


def _experts(xs, blk_exp, blk_valid, n_used, next_exp, w1, w3, w2, blk):
    rows = xs.shape[0]
    nb = rows // blk
    any_space = pl.BlockSpec(memory_space=pl.ANY)
    grid_spec = pltpu.PrefetchScalarGridSpec(
        num_scalar_prefetch=4,
        grid=(nb,),
        in_specs=[any_space, any_space, any_space, any_space],
        out_specs=pl.BlockSpec((blk, HALF), lambda i, be, nv, nu, nx: (jnp.minimum(i, nu[0] - 1), 0)),
        scratch_shapes=[pltpu.VMEM((D_MODEL, EXPERT_DIM), BF16), pltpu.VMEM((D_MODEL, EXPERT_DIM), BF16),
                        pltpu.VMEM((EXPERT_DIM, D_MODEL), BF16),
                        pltpu.VMEM((ROW_SLOTS, blk, HALF), jnp.int32),
                        pltpu.VMEM((WEIGHT_SLOTS, D_MODEL, EXPERT_DIM), F32),
                        pltpu.VMEM((WEIGHT_SLOTS, D_MODEL, EXPERT_DIM), F32),
                        pltpu.VMEM((WEIGHT_SLOTS, EXPERT_DIM, D_MODEL), F32),
                        pltpu.SMEM((1,), jnp.int32),
                        pltpu.SemaphoreType.DMA((ROW_SLOTS,)), pltpu.SemaphoreType.DMA((WEIGHT_SLOTS,))],
    )
    return pl.pallas_call(
        functools.partial(_expert_kernel, blk=blk),
        grid_spec=grid_spec,
        out_shape=jax.ShapeDtypeStruct((rows, HALF), jnp.int32),
        compiler_params=pltpu.CompilerParams(dimension_semantics=("arbitrary",), vmem_limit_bytes=VMEM_LIMIT),
        name="experts",
    )(blk_exp, blk_valid, n_used, next_exp, xs, w1, w3, w2)


def _combine_kernel(g_ref, wt_ref, h2_ref, x1_ref, mod_ref, sw1_ref, sw3_ref, sw2_ref, gf_ref, *rest):
    y_ref = rest[-1]
    wt = wt_ref[...]
    lo_acc = None
    for k in range(TOP_K):
        lo, hi = _unpack_halves(g_ref[k])
        wk = wt[:, k:k + 1]
        lo_acc = wk * lo if lo_acc is None else lo_acc + wk * lo
        hi_acc = wk * hi if k == 0 else hi_acc + wk * hi
    routed = jnp.concatenate([lo_acc, hi_acc], axis=1)
    lo, hi = _unpack_halves(h2_ref[...])
    lo = lo.astype(BF16)
    hi = hi.astype(BF16)
    a = _dot(lo, sw1_ref[:HALF, :]) + _dot(hi, sw1_ref[HALF:, :])
    b = _dot(lo, sw3_ref[:HALF, :]) + _dot(hi, sw3_ref[HALF:, :])
    shared = _dot((_silu(a) * b).astype(BF16), sw2_ref[...])
    mod = mod_ref[0]
    x2 = x1_ref[...] + mod[5:6] * (routed + shared)
    y_ref[...] = _rms(x2) * gf_ref[...]


def _combine(gathered, wt, h2p, x1, mod, wts, tm, src_tile0, mod_batch0, tiles_per_batch, out_rows, out_tile0,
             y_prev=None):
    t = x1.shape[0]
    full = lambda a: pl.BlockSpec(a.shape, lambda i: (0,) * a.ndim)
    shifted = lambda w: pl.BlockSpec((tm, w), lambda i: (i + src_tile0, 0))
    args = [gathered, wt, h2p, x1, mod, wts["sw1"], wts["sw3"], wts["sw2"], wts["gf"]]
    specs = [pl.BlockSpec((TOP_K, tm, HALF), lambda i: (0, i + src_tile0, 0)), shifted(TOP_K), shifted(HALF),
             pl.BlockSpec((tm, D_MODEL), lambda i: (i, 0)),
             pl.BlockSpec((1, 6, D_MODEL), lambda i: (i // tiles_per_batch + mod_batch0, 0, 0)),
             full(wts["sw1"]), full(wts["sw3"]), full(wts["sw2"]), full(wts["gf"])]
    aliases = {}
    if y_prev is not None:
        args.append(y_prev)
        specs.append(pl.BlockSpec(memory_space=pl.ANY))
        aliases = {len(args) - 1: 0}
    return pl.pallas_call(
        _combine_kernel,
        grid=(t // tm,),
        in_specs=specs,
        out_specs=pl.BlockSpec((tm, D_MODEL), lambda i: (i + out_tile0, 0)),
        out_shape=jax.ShapeDtypeStruct((out_rows, D_MODEL), F32),
        input_output_aliases=aliases,
        compiler_params=pltpu.CompilerParams(dimension_semantics=("parallel",), vmem_limit_bytes=VMEM_LIMIT),
        name="combine",
    )(*args)


def _rope_tables(pos):
    half = MLA_ROPE // 2
    inv_freq = ROPE_THETA ** (-jnp.arange(half, dtype=F32) / half)
    ang = pos.astype(F32)[:, None] * inv_freq
    cos, sin = jnp.cos(ang), jnp.sin(ang)
    n = pos.shape[0]
    ones = jnp.ones((n, MLA_NOPE), F32)
    z_nope = jnp.zeros((n, MLA_NOPE), F32)
    z_pad = jnp.zeros((n, HEAD_PAD - MLA_NOPE - MLA_ROPE), F32)
    return (jnp.concatenate([ones, cos, cos, z_pad], axis=1),
            jnp.concatenate([z_nope, -sin, sin, z_pad], axis=1))


def _prep_weights(g_norm1, w_in, g_q_lat, w_uq, g_kv_lat, w_ukv, w_mla_up, w_sb_up, w_out, g_norm2,
                  w_router, b_router, shared_w1, shared_w3, shared_w2, g_final):
    w = w_in[0]
    o = 0
    parts = {}
    for name, width in (("qlat", MLA_Q_LORA), ("kv", MLA_KV_LORA), ("kr", MLA_ROPE), ("sbq", SB_WIDTH),
                        ("sbk", SB_WIDTH), ("sbv", SB_WIDTH), ("gm", D_MODEL), ("gs", D_MODEL)):
        parts[name] = w[:, o:o + width]
        o += width
    kr = parts["kr"]
    z_nope = jnp.zeros((D_MODEL, MLA_NOPE), F32)
    z_pad = jnp.zeros((D_MODEL, HEAD_PAD - MLA_NOPE - MLA_ROPE), F32)
    kr_seg = jnp.concatenate([z_nope, kr, z_pad], axis=1)
    w_in_r = jnp.concatenate([parts["qlat"], parts["kv"], parts["sbq"] * (SB_DIM ** -0.5 * LOG2_E), parts["sbk"],
                              parts["sbv"], parts["gm"], parts["gs"], kr_seg], axis=1).astype(BF16)

    scale = (MLA_NOPE + MLA_ROPE) ** -0.5 * LOG2_E
    uq = w_uq[0].reshape(MLA_Q_LORA, MLA_HEADS, MLA_NOPE + MLA_ROPE) * scale
    nope, rope = uq[..., :MLA_NOPE], uq[..., MLA_NOPE:]
    zq_pad = jnp.zeros((MLA_Q_LORA, MLA_HEADS, HEAD_PAD - MLA_NOPE - MLA_ROPE), F32)
    w_uq_r = jnp.concatenate([nope, rope, zq_pad], axis=-1).reshape(MLA_Q_LORA, QP_WIDTH).astype(BF16)

    ukv = w_ukv[0].reshape(MLA_KV_LORA, MLA_HEADS, MLA_NOPE + MLA_V)
    k_nope, v = ukv[..., :MLA_NOPE], ukv[..., MLA_NOPE:]
    k_pad = jnp.concatenate([k_nope, jnp.zeros((MLA_KV_LORA, MLA_HEADS, HEAD_PAD - MLA_NOPE), F32)], axis=-1)
    w_ukv_r = jnp.concatenate([k_pad.reshape(MLA_KV_LORA, QP_WIDTH),
                               v.reshape(MLA_KV_LORA, MLA_HEADS * MLA_V)], axis=1).astype(BF16)

    eye = jnp.eye(MLA_ROPE, dtype=F32)
    place_head = jnp.concatenate([jnp.zeros((MLA_ROPE, MLA_NOPE), F32), eye,
                                  jnp.zeros((MLA_ROPE, HEAD_PAD - MLA_NOPE - MLA_ROPE), F32)], axis=1)
    place = jnp.tile(place_head, (1, MLA_HEADS)).astype(BF16)

    return dict(
        g1=g_norm1[0].reshape(1, D_MODEL), w_in=w_in_r, gq=g_q_lat[0].reshape(1, MLA_Q_LORA), w_uq=w_uq_r,
        gkv=g_kv_lat[0].reshape(1, MLA_KV_LORA), w_ukv=w_ukv_r, place=place,
        w_mla_up=w_mla_up[0].astype(BF16), w_sb_up=w_sb_up[0].astype(BF16), w_out=w_out[0].astype(BF16),
        g2=g_norm2[0].reshape(1, D_MODEL), wr_t=w_router[0].T.astype(BF16),
        br=b_router[0].reshape(N_EXPERTS, 1),
        sw1=shared_w1[0].astype(BF16), sw3=shared_w3[0].astype(BF16), sw2=shared_w2[0].astype(BF16),
        gf=g_final.reshape(1, D_MODEL))


def _block_plan(counts, nb, blk):
    padded = (counts + blk - 1) // blk * blk
    pad_end = jnp.cumsum(padded)
    pad_start = pad_end - padded
    first_row = jnp.arange(nb, dtype=jnp.int32) * blk
    blk_exp = jnp.minimum(jnp.sum((pad_end[None, :] <= first_row[:, None]).astype(jnp.int32), axis=1),
                          N_EXPERTS - 1)
    own = blk_exp[:, None] == jnp.arange(N_EXPERTS, dtype=jnp.int32)[None, :]
    seg_end = jnp.sum(jnp.where(own, (pad_start + counts)[None, :], 0), axis=1)
    blk_valid = jnp.clip(seg_end - first_row, 0, blk).astype(jnp.int32)
    n_used = (pad_end[-1:] // blk).astype(jnp.int32)
    after_seg = jnp.sum(jnp.where(own, (pad_end // blk)[None, :], 0), axis=1)
    follower = jnp.sum(jnp.where(after_seg[:, None] == jnp.arange(nb, dtype=jnp.int32)[None, :],
                                 blk_exp[None, :], 0), axis=1)
    next_exp = jnp.where(after_seg < n_used[0], follower, -1).astype(jnp.int32)
    return pad_start, blk_exp.astype(jnp.int32), blk_valid, n_used, next_exp


def _moe_rows(h2p, idx_kt, rank_kt, counts, w1, w3, w2, scatter_rows, gather_rows, blk, tm):
    t = h2p.shape[0]
    nb = -(-t * TOP_K // blk) + N_EXPERTS
    pad_start, blk_exp, blk_valid, n_used, next_exp = _block_plan(counts.reshape(N_EXPERTS), nb, blk)
    pos = _positions(idx_kt, rank_kt, pad_start, tm).reshape(TOP_K * t)
    xs = scatter_rows(h2p, pos, nb * blk)
    os_ = _experts(xs, blk_exp, blk_valid, n_used, next_exp, w1, w3, w2, blk)
    return gather_rows(os_, pos).reshape(TOP_K, t, HALF)


def _forward(x_prompt, x_sample, cache_mla_ckv, cache_mla_krope, cache_sb_k, cache_sb_v, c_prompt, c_sample,
             w_ada, b_ada, moe_w1, moe_w3, moe_w2, wts, scatter_rows, gather_rows, token_block, attn_block,
             route_block, moe_block, wide_block):
    bp, sp, _ = x_prompt.shape
    bs, ss, _ = x_sample.shape
    past_len = cache_mla_ckv.shape[2]

    mod = _ada(jnp.concatenate([c_prompt, c_sample], axis=0), w_ada[0], b_ada[0]).reshape(bp + bs, 6, D_MODEL)
    mod_p, mod_s = mod[:bp], mod[bp:]

    cos_p, sin_p = _rope_tables(jnp.arange(sp))
    (qp, kmla, vmla, sbq, sbk16, sbv16, gates, ckv_p, krope_p, sbk_p, sbv_p) = _in_proj(
        x_prompt, mod_p, wts, cos_p, sin_p, wide_block)
    split_b = bp * GROUP_A_SHARE[0] // GROUP_A_SHARE[1]
    assert 0 < split_b < bp
    mixed = [_prompt_attention(qp, kmla, vmla, sbq, sbk16, sbv16, gates, x_prompt, mod_p, wts, attn_block,
                               b0, nb_) for b0, nb_ in ((0, split_b), (split_b, bp - split_b))]

    cos_s, sin_s = _rope_tables(past_len + jnp.arange(ss))
    (qs, kmla_s, vmla_s, sbq_s, sbk16_s, sbv16_s, gates_s, ckv_s, krope_s, sbk_s, sbv_s) = _in_proj(
        x_sample, mod_s, wts, cos_s, sin_s, ss)
    pkmla, pvmla = _kv_up(cache_mla_ckv[0], cache_mla_krope[0], wts["w_ukv"], wts["place"], wide_block)
    past = (pkmla, pvmla, cache_sb_k[0].reshape(bs, past_len, SB_WIDTH), cache_sb_v[0].reshape(bs, past_len, SB_WIDTH))
    x1_s, h2_s = _decode_attention(qs, kmla_s, vmla_s, sbq_s, sbk16_s, sbv16_s, past, gates_s, x_sample, mod_s,
                                   wts, token_block)

    tp, ts = bp * sp, bs * ss
    ta = split_b * sp
    tb = tp - ta

    def moe(h2_rows):
        idx_kt, wt_kt, rank_kt, counts = _route(h2_rows, wts["wr_t"], wts["br"], route_block)
        gathered = _moe_rows(h2_rows, idx_kt, rank_kt, counts, moe_w1[0], moe_w3[0], moe_w2[0],
                             scatter_rows, gather_rows, moe_block, route_block)
        return gathered, wt_kt.T

    tiles_per_batch = sp // wide_block
    (x1_a, h2_a), (x1_b, h2_b) = mixed
    h2_a = h2_a.reshape(ta, HALF)
    g_a, wt_a = moe(h2_a)
    y_p = _combine(g_a, wt_a, h2_a, x1_a.reshape(ta, D_MODEL), mod_p, wts, wide_block,
                   0, 0, tiles_per_batch, tp, 0)
    h2_b = jnp.concatenate([h2_b.reshape(tb, HALF), h2_s.reshape(ts, HALF)], axis=0)
    g_b, wt_b = moe(h2_b)
    y_p = _combine(g_b, wt_b, h2_b, x1_b.reshape(tb, D_MODEL), mod_p, wts, wide_block,
                   0, split_b, tiles_per_batch, tp, ta // wide_block, y_prev=y_p)
    y_s = _combine(g_b, wt_b, h2_b, x1_s.reshape(ts, D_MODEL), mod_s, wts, ss, tb // ss, 0, 1, ts, 0)

    heads = lambda a, b_, s_: a.reshape(1, b_, s_, SB_HEADS, SB_DIM)
    return (y_p.reshape(bp, sp, D_MODEL), y_s.reshape(bs, ss, D_MODEL),
            ckv_p[None], krope_p[None], heads(sbk_p, bp, sp), heads(sbv_p, bp, sp),
            ckv_s[None], krope_s[None], heads(sbk_s, bs, ss), heads(sbv_s, bs, ss))


def kernel(x_prompt, x_sample, cache_mla_ckv, cache_mla_krope, cache_sb_k, cache_sb_v, c_prompt, c_sample, w_ada, b_ada, g_norm1, w_in, g_q_lat, w_uq, g_kv_lat, w_ukv, w_mla_up, w_sb_up, w_out, g_norm2, w_router, b_router, moe_w1, moe_w3, moe_w2, shared_w1, shared_w3, shared_w2, g_final):
    wts = _prep_weights(g_norm1, w_in, g_q_lat, w_uq, g_kv_lat, w_ukv, w_mla_up, w_sb_up, w_out, g_norm2,
                        w_router, b_router, shared_w1, shared_w3, shared_w2, g_final)
    scatter_rows = functools.partial(_sc_scatter_rows, chunk=SC_CHUNK)
    gather_rows = functools.partial(_sc_gather_rows, chunk=SC_CHUNK)
    return _forward(x_prompt, x_sample, cache_mla_ckv, cache_mla_krope, cache_sb_k, cache_sb_v, c_prompt, c_sample,
                    w_ada, b_ada, moe_w1, moe_w3, moe_w2, wts, scatter_rows, gather_rows, TOKEN_BLOCK, ATTN_BLOCK,
                    TOKEN_BLOCK, MOE_BLOCK, WIDE_BLOCK)
```

```python
import functools

import jax
import jax.numpy as jnp
from jax import lax
from jax.experimental import pallas as pl
from jax.experimental.pallas import tpu as pltpu
from jax.experimental.pallas import tpu_sc as plsc

F32 = jnp.float32
BF16 = jnp.bfloat16

D_MODEL = 1024
NORM_EPS = 1e-6
CHUNK = 64
MLA_HEADS = 8
MLA_NOPE = 64
MLA_ROPE = 32
MLA_V = 64
MLA_Q_LORA = 384
MLA_KV_LORA = 256
ROPE_THETA = 10000.0
SB_HEADS = 8
SB_DIM = 64
SB_WIDTH = SB_HEADS * SB_DIM
N_EXPERTS = 256
TOP_K = 8
N_GROUPS = 8
TOPK_GROUPS = 4
GROUP_SIZE = N_EXPERTS // N_GROUPS
EXPERT_DIM = 256
ROUTED_SCALE = 2.5
LOG2_E = 1.4426950408889634

LANES = 128
MXU_TILE = 256
SC_CORES = 2
SC_SUBCORES = 16
SC_WORKERS = SC_CORES * SC_SUBCORES
VMEM_LIMIT = 60 * 1024 * 1024

HEAD_PAD = LANES
DENOM_LANE = (MLA_V, 0)
QP_WIDTH = MLA_HEADS * HEAD_PAD
HALF = D_MODEL // 2

C_QLAT = 0
C_KV = C_QLAT + MLA_Q_LORA
C_SBQ = C_KV + MLA_KV_LORA
C_SBK = C_SBQ + SB_WIDTH
C_SBV = C_SBK + SB_WIDTH
C_GATE = C_SBV + SB_WIDTH
C_KR = C_GATE + 2 * D_MODEL
C_END = C_KR + LANES

MOE_BLOCK = 512
WIDE_BLOCK = 512
TOKEN_BLOCK = 256
ATTN_BLOCK = 512
PAIRS_PER_LOOP = 1
SC_CHUNK = 64
GROUP_A_SHARE = (1, 2)


def _rms(x):
    return x * lax.rsqrt(jnp.mean(x * x, axis=-1, keepdims=True) + NORM_EPS)


def _silu(x):
    return x * jax.nn.sigmoid(x)


def _pack_halves(lo, hi):
    lo_bits = lax.bitcast_convert_type(lo.astype(BF16).astype(F32), jnp.uint32) >> 16
    hi_bits = lax.bitcast_convert_type(hi.astype(BF16).astype(F32), jnp.uint32) & jnp.uint32(0xFFFF0000)
    return lax.bitcast_convert_type(lo_bits | hi_bits, jnp.int32)


def _unpack_halves(p):
    u = lax.bitcast_convert_type(p, jnp.uint32)
    lo = lax.bitcast_convert_type(u << 16, F32)
    hi = lax.bitcast_convert_type(u & jnp.uint32(0xFFFF0000), F32)
    return lo, hi


def _dot(a, b):
    return jnp.dot(a, b, preferred_element_type=F32)


def _dot_nt(a, b):
    return lax.dot_general(a, b, (((1,), (1,)), ((), ())), preferred_element_type=F32)


def _ada_kernel(c_ref, w_ref, b_ref, o_ref):
    c = c_ref[...]
    o_ref[...] = _dot(_silu(c).astype(BF16), w_ref[...].astype(BF16)) + b_ref[...]


def _ada(c, w_ada, b_ada):
    n = c.shape[0]
    width = w_ada.shape[1]
    return pl.pallas_call(
        _ada_kernel,
        grid=(width // D_MODEL,),
        in_specs=[pl.BlockSpec((n, D_MODEL), lambda j: (0, 0)),
                  pl.BlockSpec((D_MODEL, D_MODEL), lambda j: (0, j)),
                  pl.BlockSpec((1, D_MODEL), lambda j: (0, j))],
        out_specs=pl.BlockSpec((n, D_MODEL), lambda j: (0, j)),
        out_shape=jax.ShapeDtypeStruct((n, width), F32),
        name="ada",
    )(c, w_ada, b_ada.reshape(1, width))


def _in_kernel(x_ref, mod_ref, g1_ref, win_ref, gq_ref, wuq_ref, gkv_ref, wukv_ref, cos_ref, sin_ref,
               qp_ref, kmla_ref, vmla_ref, sbq_ref, sbk16_ref, sbv16_ref, gates_ref,
               ckv_ref, krope_ref, sbk_ref, sbv_ref):
    x = x_ref[0]
    mod = mod_ref[0]
    h = _rms(x) * g1_ref[...] * (1.0 + mod[1:2]) + mod[0:1]
    hb = h.astype(BF16)

    def seg(a, b):
        return _dot(hb, win_ref[:, a:b])

    cos = cos_ref[...]
    sin = sin_ref[...]
    lane = lax.broadcasted_iota(jnp.int32, (1, LANES), 1)
    half = MLA_ROPE // 2

    def rotate(blk):
        other = jnp.where(lane < MLA_NOPE + half, pltpu.roll(blk, LANES - half, 1), pltpu.roll(blk, half, 1))
        return blk * cos + other * sin

    qn = (_rms(seg(C_QLAT, C_KV)) * gq_ref[...]).astype(BF16)
    q = _dot(qn, wuq_ref[...])
    qp_ref[0] = jnp.concatenate([rotate(q[:, h * HEAD_PAD:(h + 1) * HEAD_PAD]) for h in range(MLA_HEADS)],
                                axis=1).astype(BF16)

    ckv = _rms(seg(C_KV, C_SBQ)) * gkv_ref[...]
    ckv_ref[0] = ckv
    kv = _dot(ckv.astype(BF16), wukv_ref[...])
    krp = rotate(seg(C_KR, C_END))
    krope_ref[0] = krp[:, MLA_NOPE:MLA_NOPE + MLA_ROPE]
    kmla_ref[0] = (kv[:, :QP_WIDTH] + jnp.tile(krp, (1, MLA_HEADS))).astype(BF16)
    vmla_ref[0] = kv[:, QP_WIDTH:].astype(BF16)

    sbq_ref[0] = seg(C_SBQ, C_SBK).astype(BF16)
    sbk = seg(C_SBK, C_SBV)
    sbk_ref[0] = sbk
    sbk16_ref[0] = sbk.astype(BF16)
    sbv = seg(C_SBV, C_GATE)
    sbv_ref[0] = sbv
    sbv16_ref[0] = sbv.astype(BF16)
    gates_ref[0] = jax.nn.sigmoid(seg(C_GATE, C_KR)).astype(BF16)


def _in_proj(x, mod, wts, cos_t, sin_t, tm):
    b, s, _ = x.shape
    ns = s // tm
    tok = lambda w: pl.BlockSpec((1, tm, w), lambda i, j: (i, j, 0))
    full = lambda a: pl.BlockSpec(a.shape, lambda i, j: (0,) * a.ndim)
    out_widths = [(QP_WIDTH, BF16), (QP_WIDTH, BF16), (SB_WIDTH, BF16), (SB_WIDTH, BF16), (SB_WIDTH, BF16),
                  (SB_WIDTH, BF16), (2 * D_MODEL, BF16), (MLA_KV_LORA, F32), (MLA_ROPE, F32),
                  (SB_WIDTH, F32), (SB_WIDTH, F32)]
    return pl.pallas_call(
        _in_kernel,
        grid=(b, ns),
        in_specs=[tok(D_MODEL),
                  pl.BlockSpec((1, 6, D_MODEL), lambda i, j: (i, 0, 0)),
                  full(wts["g1"]), full(wts["w_in"]), full(wts["gq"]), full(wts["w_uq"]),
                  full(wts["gkv"]), full(wts["w_ukv"]),
                  pl.BlockSpec((tm, LANES), lambda i, j: (j, 0)),
                  pl.BlockSpec((tm, LANES), lambda i, j: (j, 0))],
        out_specs=[tok(w) for w, _ in out_widths],
        out_shape=[jax.ShapeDtypeStruct((b, s, w), dt) for w, dt in out_widths],
        compiler_params=pltpu.CompilerParams(dimension_semantics=("parallel", "parallel"),
                                             vmem_limit_bytes=VMEM_LIMIT),
        name="in_proj",
    )(x, mod, wts["g1"], wts["w_in"], wts["gq"], wts["w_uq"], wts["gkv"], wts["w_ukv"], cos_t, sin_t)


def _kvup_kernel(ckv_ref, kr_ref, wukv_ref, place_ref, kmla_ref, vmla_ref):
    kv = _dot(ckv_ref[0].astype(BF16), wukv_ref[...])
    kr = _dot(kr_ref[0].astype(BF16), place_ref[...])
    kmla_ref[0] = (kv[:, :QP_WIDTH] + kr).astype(BF16)
    vmla_ref[0] = kv[:, QP_WIDTH:].astype(BF16)


def _kv_up(ckv, krope, w_ukv_r, place, tm):
    b, p, _ = ckv.shape
    return pl.pallas_call(
        _kvup_kernel,
        grid=(b, p // tm),
        in_specs=[pl.BlockSpec((1, tm, MLA_KV_LORA), lambda i, j: (i, j, 0)),
                  pl.BlockSpec((1, tm, MLA_ROPE), lambda i, j: (i, j, 0)),
                  pl.BlockSpec(w_ukv_r.shape, lambda i, j: (0, 0)),
                  pl.BlockSpec(place.shape, lambda i, j: (0, 0))],
        out_specs=[pl.BlockSpec((1, tm, QP_WIDTH), lambda i, j: (i, j, 0)),
                   pl.BlockSpec((1, tm, SB_WIDTH), lambda i, j: (i, j, 0))],
        out_shape=[jax.ShapeDtypeStruct((b, p, QP_WIDTH), BF16), jax.ShapeDtypeStruct((b, p, SB_WIDTH), BF16)],
        compiler_params=pltpu.CompilerParams(dimension_semantics=("parallel", "parallel")),
        name="kv_up",
    )(ckv, krope, w_ukv_r, place)


def _tri(n):
    r = lax.broadcasted_iota(jnp.int32, (n, n), 0)
    c = lax.broadcasted_iota(jnp.int32, (n, n), 1)
    return jnp.where(r > c, 1.0, 0.0).astype(BF16)


def _stick_terms(z):
    log_sig = jnp.minimum(z, 0.0) - jnp.log2(1.0 + jnp.exp2(-jnp.abs(z)))
    return log_sig, log_sig - z


def _split_bf16(x):
    hi = x.astype(BF16)
    return hi, (x - hi.astype(F32)).astype(BF16)


def _finish_mixer(o_mla, o_sb, gates_ref, x_ref, mod_ref, wmu_ref, wsu_ref, wo_ref, g2_ref, x1_ref, h2_ref):
    u_mla = _dot(o_mla.astype(BF16), wmu_ref[...])
    u_sb = _dot(o_sb.astype(BF16), wsu_ref[...])
    gates = gates_ref[0]
    merged = gates[:, :D_MODEL].astype(F32) * u_mla + gates[:, D_MODEL:].astype(F32) * u_sb
    mix = _dot(merged.astype(BF16), wo_ref[...])
    mod = mod_ref[0]
    x1 = x_ref[0] + mod[2:3] * mix
    x1_ref[0] = x1
    h2 = _rms(x1) * g2_ref[...] * (1.0 + mod[4:5]) + mod[3:4]
    h2_ref[0] = _pack_halves(h2[:, :HALF], h2[:, HALF:])


def _prompt_attn_kernel(qp_ref, kmla_ref, vmla_ref, sbq_ref, sbk_ref, sbv_ref,
                        gates_ref, x_ref, mod_ref, wmu_ref, wsu_ref, wo_ref, g2_ref, x1_ref, h2_ref,
                        m_ref, acc_ref, c_ref, sacc_ref, *, tq):
    i = pl.program_id(1)
    lane = lax.broadcasted_iota(jnp.int32, (1, LANES), 1)
    half_masks = (lane < MLA_V, lane >= MLA_V)
    unit_lane = [jnp.where(lane == DENOM_LANE[sub], 1.0, 0.0).astype(BF16) for sub in range(2)]
    row = lax.broadcasted_iota(jnp.int32, (tq, tq), 0)
    col = lax.broadcasted_iota(jnp.int32, (tq, tq), 1)
    chunk_mask = (col // CHUNK) <= (row // CHUNK)
    causal_mask = col < row
    piece = min(tq, MXU_TILE)
    tri_m = _tri(piece)
    st_diag = pl.multiple_of(i * tq, tq)

    def load(ref, start, c0):
        return ref[0, pl.ds(start, tq), c0:c0 + LANES]

    def mla_block(sub, q_h, k, v, mask):
        s = _dot_nt(q_h, k)
        if mask is not None:
            s = jnp.where(mask, s, -jnp.inf)
            m_new = jnp.max(s, axis=-1, keepdims=True)
            acc_ref[sub] = _dot(jnp.exp2(s - m_new).astype(BF16), v)
        else:
            m_old = m_ref[sub]
            m_new = jnp.maximum(m_old, jnp.max(s, axis=-1, keepdims=True))
            acc_ref[sub] = jnp.exp2(m_old - m_new) * acc_ref[sub] + _dot(jnp.exp2(s - m_new).astype(BF16), v)
        m_ref[sub] = m_new

    def sb_block(sub, q_h, k, v, mask):
        log_sig, log_keep = _stick_terms(_dot_nt(q_h, k))
        if mask is not None:
            log_keep = jnp.where(mask, log_keep, 0.0)
        summand = log_keep.astype(BF16)
        pieces = []
        total = None
        for b in reversed(range(tq // piece)):
            sl = slice(b * piece, (b + 1) * piece)
            inner = _dot(summand[:, sl], tri_m)
            piece_total = jnp.sum(log_keep[:, sl], axis=-1, keepdims=True)
            pieces.append(inner if total is None else inner + total)
            total = piece_total if total is None else total + piece_total
        after = jnp.concatenate(pieces[::-1], axis=1)
        if mask is not None:
            a = jnp.where(mask, jnp.exp2(log_sig + after), 0.0)
            sacc_ref[sub] = _dot(a.astype(BF16), v)
            c_ref[sub] = total
        else:
            c_old = c_ref[sub]
            a = jnp.exp2(log_sig + after + c_old)
            sacc_ref[sub] = sacc_ref[sub] + _dot(a.astype(BF16), v)
            c_ref[sub] = c_old + total

    o_mla = []
    o_sb = []
    for first_pair in range(0, MLA_HEADS // 2, PAIRS_PER_LOOP):
        pairs = range(first_pair, first_pair + PAIRS_PER_LOOP)
        q_m = {h: qp_ref[0, :, h * HEAD_PAD:(h + 1) * HEAD_PAD] for p in pairs for h in (2 * p, 2 * p + 1)}
        q_s = {2 * p + sub: jnp.where(half_masks[sub], sbq_ref[0, :, p * LANES:(p + 1) * LANES], 0)
               for p in pairs for sub in range(2)}

        def group_blocks(mla_start, sb_start, masks, pairs=pairs, q_m=q_m, q_s=q_s):
            for p in pairs:
                vcol = p * LANES
                v_m = load(vmla_ref, mla_start, vcol)
                k_s = load(sbk_ref, sb_start, vcol)
                v_s = load(sbv_ref, sb_start, vcol)
                for sub in range(2):
                    head = 2 * p + sub
                    slot = head - 2 * pairs[0]
                    keep = half_masks[sub]
                    mla_block(slot, q_m[head], load(kmla_ref, mla_start, head * HEAD_PAD),
                              jnp.where(keep, v_m, unit_lane[sub]), masks[0])
                    sb_block(slot, q_s[head], k_s, jnp.where(keep, v_s, 0), masks[1])

        group_blocks(st_diag, st_diag, (chunk_mask, causal_mask))

        def step(t, _, group_blocks=group_blocks):
            group_blocks(pl.multiple_of(t * tq, tq), pl.multiple_of((i - 1 - t) * tq, tq), (None, None))
            return 0
        lax.fori_loop(0, i, step, 0)
        for p in range(PAIRS_PER_LOOP):
            heads_out = []
            for sub in range(2):
                acc = acc_ref[2 * p + sub]
                denom = acc[:, DENOM_LANE[sub]:DENOM_LANE[sub] + 1]
                heads_out.append(jnp.where(half_masks[sub], acc, 0.0) / denom)
            o_mla.append(heads_out[0] + heads_out[1])
            o_sb.append(sacc_ref[2 * p] + sacc_ref[2 * p + 1])

    _finish_mixer(jnp.concatenate(o_mla, axis=1), jnp.concatenate(o_sb, axis=1),
                  gates_ref, x_ref, mod_ref, wmu_ref, wsu_ref, wo_ref, g2_ref, x1_ref, h2_ref)


def _decode_attn_kernel(qp_ref, kmla_ref, vmla_ref, sbq_ref, sbk_ref, sbv_ref, pkmla_ref, pvmla_ref, psbk_ref,
                        psbv_ref, gates_ref, x_ref, mod_ref, wmu_ref, wsu_ref, wo_ref, g2_ref, x1_ref, h2_ref,
                        *, tq, past_len, past_blk):
    n_past = past_len // past_blk
    lane = lax.broadcasted_iota(jnp.int32, (1, LANES), 1)
    half_masks = (lane < MLA_V, lane >= MLA_V)
    row = lax.broadcasted_iota(jnp.int32, (tq, tq), 0)
    col = lax.broadcasted_iota(jnp.int32, (tq, tq), 1)
    chunk_mask = ((past_len + col) // CHUNK) <= ((past_len + row) // CHUNK)
    causal_mask = col < row
    tri_new = _tri(tq)
    tri_past = _tri(past_blk)

    o_mla = []
    o_sb = []
    for pair in range(MLA_HEADS // 2):
        vcol = pair * LANES
        v_new = vmla_ref[0, :, vcol:vcol + LANES]
        v_past = pvmla_ref[0, :, vcol:vcol + LANES]
        sk_new = sbk_ref[0, :, vcol:vcol + LANES]
        sv_new = sbv_ref[0, :, vcol:vcol + LANES]
        sk_past = psbk_ref[0, :, vcol:vcol + LANES].astype(BF16)
        sv_past = psbv_ref[0, :, vcol:vcol + LANES].astype(BF16)
        mla_pair = None
        sb_pair = None
        for sub in range(2):
            keep = half_masks[sub]
            kcol = (2 * pair + sub) * HEAD_PAD

            q_h = qp_ref[0, :, kcol:kcol + HEAD_PAD]
            s_past = _dot_nt(q_h, pkmla_ref[0, :, kcol:kcol + HEAD_PAD])
            s_new = jnp.where(chunk_mask, _dot_nt(q_h, kmla_ref[0, :, kcol:kcol + HEAD_PAD]), -jnp.inf)
            m = jnp.maximum(jnp.max(s_past, axis=-1, keepdims=True), jnp.max(s_new, axis=-1, keepdims=True))
            p_past = jnp.exp2(s_past - m)
            p_new = jnp.exp2(s_new - m)
            denom = jnp.sum(p_past, axis=-1, keepdims=True) + jnp.sum(p_new, axis=-1, keepdims=True)
            o = (_dot(p_past.astype(BF16), jnp.where(keep, v_past, 0))
                 + _dot(p_new.astype(BF16), jnp.where(keep, v_new, 0))) / denom
            mla_pair = o if mla_pair is None else mla_pair + o

            q_s = jnp.where(keep, sbq_ref[0, :, vcol:vcol + LANES], 0)
            ls_new, lk_new = _stick_terms(_dot_nt(q_s, sk_new))
            lk_new = jnp.where(causal_mask, lk_new, 0.0)
            hi, lo = _split_bf16(lk_new)
            a_new = jnp.where(causal_mask, jnp.exp2(ls_new + _dot(hi, tri_new) + _dot(lo, tri_new)), 0.0)
            acc = _dot(a_new.astype(BF16), jnp.where(keep, sv_new, 0))
            later = jnp.sum(lk_new, axis=-1, keepdims=True)

            ls_past, lk_past = _stick_terms(_dot_nt(q_s, sk_past))
            hi, lo = _split_bf16(lk_past)
            blocks = lambda a: [a[:, b * past_blk:(b + 1) * past_blk] for b in range(n_past)]
            stacked = jnp.concatenate(blocks(hi) + blocks(lo), axis=0)
            within = _dot(stacked, tri_past)
            after = []
            for b in reversed(range(n_past)):
                after.append(within[b * tq:(b + 1) * tq] + within[(n_past + b) * tq:(n_past + b + 1) * tq] + later)
                later = later + jnp.sum(lk_past[:, b * past_blk:(b + 1) * past_blk], axis=-1, keepdims=True)
            a_past = jnp.exp2(ls_past + jnp.concatenate(after[::-1], axis=1))
            acc = acc + _dot(a_past.astype(BF16), jnp.where(keep, sv_past, 0))
            sb_pair = acc if sb_pair is None else sb_pair + acc
        o_mla.append(mla_pair)
        o_sb.append(sb_pair)

    _finish_mixer(jnp.concatenate(o_mla, axis=1), jnp.concatenate(o_sb, axis=1),
                  gates_ref, x_ref, mod_ref, wmu_ref, wsu_ref, wo_ref, g2_ref, x1_ref, h2_ref)


def _mixer_call(kernel_fn, name, tq, args_kv, gates, x, mod, wts, scratch, batch0=0, n_batch=None):
    s = x.shape[1]
    b = x.shape[0] if n_batch is None else n_batch
    tok = lambda w: pl.BlockSpec((1, tq, w), lambda i, j: (i + batch0, j, 0))
    seq = lambda a: pl.BlockSpec((1,) + a.shape[1:], lambda i, j: (i + batch0, 0, 0))
    full = lambda a: pl.BlockSpec(a.shape, lambda i, j: (0,) * a.ndim)
    out = lambda w: pl.BlockSpec((1, tq, w), lambda i, j: (i, j, 0))
    args = list(args_kv) + [gates, x, mod, wts["w_mla_up"], wts["w_sb_up"], wts["w_out"], wts["g2"]]
    specs = [seq(a) if whole else tok(a.shape[-1]) for a, whole in
             zip(args_kv, (False, True, True, False) + (True,) * (len(args_kv) - 4))]
    specs += [tok(2 * D_MODEL), tok(D_MODEL), pl.BlockSpec((1, 6, D_MODEL), lambda i, j: (i + batch0, 0, 0)),
              full(wts["w_mla_up"]), full(wts["w_sb_up"]), full(wts["w_out"]), full(wts["g2"])]
    return pl.pallas_call(
        kernel_fn,
        grid=(b, s // tq),
        in_specs=specs,
        out_specs=[out(D_MODEL), out(HALF)],
        out_shape=[jax.ShapeDtypeStruct((b, s, D_MODEL), F32), jax.ShapeDtypeStruct((b, s, HALF), jnp.int32)],
        scratch_shapes=scratch,
        compiler_params=pltpu.CompilerParams(dimension_semantics=("parallel", "arbitrary"),
                                             vmem_limit_bytes=VMEM_LIMIT),
        name=name,
    )(*args)


def _prompt_attention(qp, kmla, vmla, sbq, sbk16, sbv16, gates, x, mod, wts, tq, batch0, n_batch):
    col = lambda: pltpu.VMEM((2 * PAIRS_PER_LOOP, tq, 1), F32)
    wide = lambda: pltpu.VMEM((2 * PAIRS_PER_LOOP, tq, LANES), F32)
    return _mixer_call(functools.partial(_prompt_attn_kernel, tq=tq), "attention", tq,
                       (qp, kmla, vmla, sbq, sbk16, sbv16), gates, x, mod, wts,
                       [col(), wide(), col(), wide()], batch0, n_batch)


def _decode_attention(qp, kmla, vmla, sbq, sbk16, sbv16, past, gates, x, mod, wts, past_blk):
    tq = x.shape[1]
    past_len = past[0].shape[1]
    kern = functools.partial(_decode_attn_kernel, tq=tq, past_len=past_len, past_blk=past_blk)
    return _mixer_call(kern, "decode_attention", tq, (qp, kmla, vmla, sbq, sbk16, sbv16) + tuple(past),
                       gates, x, mod, wts, [])


def _route_kernel(h2_ref, wr_ref, br_ref, idx_ref, wt_ref, rank_ref, cnt_ref, seen_ref):
    lo, hi = _unpack_halves(h2_ref[...])
    tm = lo.shape[0]
    logits = _dot_nt(wr_ref[:, :HALF], lo.astype(BF16)) + _dot_nt(wr_ref[:, HALF:], hi.astype(BF16))
    scores = jax.nn.sigmoid(logits)
    sel = scores + br_ref[...]
    neg = -jnp.inf

    grp = sel.reshape(N_GROUPS, GROUP_SIZE, tm)
    within = lax.broadcasted_iota(jnp.int32, grp.shape, 1)
    top1 = jnp.max(grp, axis=1, keepdims=True)
    first = jnp.min(jnp.where(grp == top1, within, GROUP_SIZE), axis=1, keepdims=True)
    top2 = jnp.max(jnp.where(within == first, neg, grp), axis=1, keepdims=True)
    gscore = (top1 + top2).reshape(N_GROUPS, tm)

    gid = lax.broadcasted_iota(jnp.int32, gscore.shape, 0)
    chosen = jnp.zeros(gscore.shape, jnp.bool_)
    for _ in range(TOPK_GROUPS):
        best = jnp.max(gscore, axis=0, keepdims=True)
        pick = jnp.min(jnp.where(gscore == best, gid, N_GROUPS), axis=0, keepdims=True)
        hit = gid == pick
        chosen = jnp.logical_or(chosen, hit)
        gscore = jnp.where(hit, neg, gscore)
    chosen3 = jnp.broadcast_to(chosen.reshape(N_GROUPS, 1, tm), grp.shape)
    cand = jnp.where(chosen3, grp, neg).reshape(N_EXPERTS, tm)
    outside = jnp.where(cand == neg, 1.0, 0.0)

    eid = lax.broadcasted_iota(jnp.int32, cand.shape, 0)
    picks = []
    weights = []
    for _ in range(TOP_K):
        best = jnp.max(cand, axis=0, keepdims=True)
        pick = jnp.min(jnp.where(cand == best, eid, N_EXPERTS), axis=0, keepdims=True)
        hit = eid == pick
        weights.append(jnp.sum(jnp.where(hit, scores, 0.0), axis=0, keepdims=True))
        picks.append(pick)
        cand = jnp.where(hit, neg, cand)
    w = jnp.concatenate(weights, axis=0)
    idx_ref[...] = jnp.concatenate(picks, axis=0)
    wt_ref[...] = w / (jnp.sum(w, axis=0, keepdims=True) + 1e-20) * ROUTED_SCALE

    @pl.when(pl.program_id(0) == 0)
    def _():
        seen_ref[...] = jnp.zeros_like(seen_ref)

    onehot = jnp.where(cand == neg, 1.0, 0.0) - outside
    src = lax.broadcasted_iota(jnp.int32, (tm, tm), 0)
    dst = lax.broadcasted_iota(jnp.int32, (tm, tm), 1)
    earlier = jnp.where(src < dst, 1.0, 0.0).astype(BF16)
    before = _dot(onehot.astype(BF16), earlier) + seen_ref[...]
    rank_ref[...] = jnp.concatenate(
        [jnp.sum(jnp.where(eid == pick, before, 0.0), axis=0, keepdims=True) for pick in picks],
        axis=0).astype(jnp.int32)
    seen = seen_ref[...] + jnp.sum(onehot, axis=1, keepdims=True)
    seen_ref[...] = seen
    cnt_ref[...] = seen.astype(jnp.int32)


def _route(h2p, wr_t, br, tm):
    t = h2p.shape[0]
    kt = lambda: pl.BlockSpec((TOP_K, tm), lambda i: (0, i))
    return pl.pallas_call(
        _route_kernel,
        grid=(t // tm,),
        in_specs=[pl.BlockSpec((tm, HALF), lambda i: (i, 0)),
                  pl.BlockSpec(wr_t.shape, lambda i: (0, 0)),
                  pl.BlockSpec(br.shape, lambda i: (0, 0))],
        out_specs=[kt(), kt(), kt(), pl.BlockSpec((N_EXPERTS, 1), lambda i: (0, 0))],
        out_shape=[jax.ShapeDtypeStruct((TOP_K, t), jnp.int32), jax.ShapeDtypeStruct((TOP_K, t), F32),
                   jax.ShapeDtypeStruct((TOP_K, t), jnp.int32), jax.ShapeDtypeStruct((N_EXPERTS, 1), jnp.int32)],
        scratch_shapes=[pltpu.VMEM((N_EXPERTS, 1), F32)],
        compiler_params=pltpu.CompilerParams(dimension_semantics=("arbitrary",)),
        name="route",
    )(h2p, wr_t, br)


def _position_kernel(idx_ref, rank_ref, start_ref, pos_ref):
    idx = idx_ref[...]
    eid = lax.broadcasted_iota(jnp.int32, (N_EXPERTS, idx.shape[1]), 0)
    start = start_ref[...]
    base = jnp.concatenate(
        [jnp.sum(jnp.where(eid == idx[k:k + 1, :], start, 0.0), axis=0, keepdims=True) for k in range(TOP_K)],
        axis=0)
    pos_ref[...] = base.astype(jnp.int32) + rank_ref[...]


def _positions(idx_kt, rank_kt, pad_start, tm):
    t = idx_kt.shape[1]
    tm = max(m for m in range(tm, 8 * tm + 1, tm) if t % m == 0)
    kt = lambda: pl.BlockSpec((TOP_K, tm), lambda i: (0, i))
    return pl.pallas_call(
        _position_kernel,
        grid=(t // tm,),
        in_specs=[kt(), kt(), pl.BlockSpec((N_EXPERTS, 1), lambda i: (0, 0))],
        out_specs=kt(),
        out_shape=jax.ShapeDtypeStruct((TOP_K, t), jnp.int32),
        compiler_params=pltpu.CompilerParams(dimension_semantics=("parallel",)),
        name="positions",
    )(idx_kt, rank_kt, pad_start.astype(F32).reshape(N_EXPERTS, 1))


def _sc_mesh():
    return plsc.VectorSubcoreMesh(core_axis_name="c", subcore_axis_name="s",
                                  num_cores=SC_CORES, num_subcores=SC_SUBCORES)


def _sc_scatter_rows(rows, pos, n_out, chunk):
    t, width = rows.shape
    copies = pos.shape[0] // t
    per_worker = t // SC_WORKERS
    n_chunks = per_worker // chunk
    tail = per_worker - n_chunks * chunk
    assert per_worker * SC_WORKERS == t and tail % 8 == 0
    tail_rows = max(tail, 8)

    @functools.partial(
        pl.kernel, mesh=_sc_mesh(),
        out_type=jax.ShapeDtypeStruct((n_out, width), rows.dtype),
        scratch_types=[pltpu.VMEM((copies, chunk), jnp.int32), pltpu.VMEM((chunk, width), rows.dtype),
                       pltpu.VMEM((copies, tail_rows), jnp.int32), pltpu.VMEM((tail_rows, width), rows.dtype),
                       pltpu.SemaphoreType.DMA, pltpu.SemaphoreType.DMA],
    )
    def scatter(rows_hbm, pos_hbm, out_hbm, idx_v, rows_v, idx_t, rows_t, load_sem, store_sem):
        wid = lax.axis_index("s") * SC_CORES + lax.axis_index("c")
        base = wid * per_worker

        def move(off, n, idx_buf, row_buf):
            loads = [pltpu.async_copy(rows_hbm.at[pl.ds(off, n)], row_buf, load_sem)]
            for k in range(copies):
                src = pos_hbm.at[pl.ds(pl.multiple_of(k * t + off, 8), n)]
                loads.append(pltpu.async_copy(src, idx_buf.at[k], load_sem))
            for cp in loads:
                cp.wait()
            stores = [pltpu.async_copy(row_buf, out_hbm.at[idx_buf.at[k]], store_sem) for k in range(copies)]
            for cp in stores:
                cp.wait()

        @pl.loop(0, n_chunks)
        def _(c):
            move(pl.multiple_of(base + c * chunk, 8), chunk, idx_v, rows_v)

        if tail:
            move(pl.multiple_of(base + n_chunks * chunk, 8), tail, idx_t, rows_t)

    return scatter(rows, pos)


def _sc_gather_rows(table, idx, chunk):
    n_rows = idx.shape[0]
    width = table.shape[1]
    per_worker = n_rows // SC_WORKERS
    n_chunks = per_worker // chunk
    assert per_worker * SC_WORKERS == n_rows and n_chunks * chunk == per_worker
    mesh = _sc_mesh()

    @functools.partial(
        pl.kernel, mesh=mesh,
        out_type=jax.ShapeDtypeStruct((n_rows, width), table.dtype),
        scratch_types=[pltpu.VMEM((chunk,), jnp.int32), pltpu.VMEM((chunk, width), table.dtype),
                       pltpu.SemaphoreType.DMA],
    )
    def gather(table_hbm, idx_hbm, out_hbm, idx_v, rows_v, sem):
        wid = lax.axis_index("s") * SC_CORES + lax.axis_index("c")
        base = wid * per_worker

        @pl.loop(0, n_chunks)
        def _(c):
            off = pl.multiple_of(base + c * chunk, 8)
            pltpu.sync_copy(idx_hbm.at[pl.ds(off, chunk)], idx_v)
            pltpu.async_copy(table_hbm.at[idx_v], rows_v, sem).wait()
            pltpu.sync_copy(rows_v, out_hbm.at[pl.ds(off, chunk)])

    return gather(table, idx)


ROW_SLOTS = 4
WEIGHT_SLOTS = 2
WEIGHT_DMA_PRIORITY = 1


def _expert_kernel(be_ref, nv_ref, nu_ref, nxt_ref, xs_hbm, w1_hbm, w3_hbm, w2_hbm, os_ref,
                   wb1, wb3, wb2, xbuf, wf1, wf3, wf2, slot_ref, xsem, wsem, *, blk):
    i = pl.program_id(0)
    n_used = nu_ref[0]

    def rows_copy(j):
        src = xs_hbm.at[pl.ds(pl.multiple_of(j * blk, blk), blk)]
        return pltpu.make_async_copy(src, xbuf.at[j % ROW_SLOTS], xsem.at[j % ROW_SLOTS])

    def weight_copies(e, slot):
        return [pltpu.make_async_copy(w_hbm.at[e], buf.at[slot], wsem.at[slot])
                for w_hbm, buf in ((w1_hbm, wf1), (w3_hbm, wf3), (w2_hbm, wf2))]

    ahead = ROW_SLOTS - 1

    @pl.when(i == 0)
    def _():
        rows_copy(0).start()
        for j in range(1, ahead):
            @pl.when(n_used > j)
            def _(j=j):
                rows_copy(j).start()
        for cp in weight_copies(be_ref[0], 0):
            cp.start(priority=WEIGHT_DMA_PRIORITY)
        slot_ref[0] = 1

    @pl.when(i + ahead < n_used)
    def _():
        rows_copy(i + ahead).start()

    @pl.when(i < n_used)
    def _():
        @pl.when(jnp.logical_or(i == 0, be_ref[i] != be_ref[jnp.maximum(i - 1, 0)]))
        def _():
            slot = 1 - slot_ref[0]
            slot_ref[0] = slot
            for cp in weight_copies(be_ref[i], slot):
                cp.wait()

            @pl.when(nxt_ref[i] >= 0)
            def _():
                for cp in weight_copies(nxt_ref[i], 1 - slot):
                    cp.start(priority=WEIGHT_DMA_PRIORITY)
            wb1[...] = wf1[slot].astype(BF16)
            wb3[...] = wf3[slot].astype(BF16)
            wb2[...] = wf2[slot].astype(BF16)

        rows_copy(i).wait()
        packed = xbuf[i % ROW_SLOTS]
        live = lax.broadcasted_iota(jnp.int32, packed.shape, 0) < nv_ref[i]
        lo, hi = _unpack_halves(jnp.where(live, packed, 0))
        lo = lo.astype(BF16)
        hi = hi.astype(BF16)
        a = _dot(lo, wb1[:HALF, :]) + _dot(hi, wb1[HALF:, :])
        b = _dot(lo, wb3[:HALF, :]) + _dot(hi, wb3[HALF:, :])
        o = _dot((_silu(a) * b).astype(BF16), wb2[...])
        os_ref[...] = _pack_halves(o[:, :HALF], o[:, HALF:])


def _experts(xs, blk_exp, blk_valid, n_used, next_exp, w1, w3, w2, blk):
    rows = xs.shape[0]
    nb = rows // blk
    any_space = pl.BlockSpec(memory_space=pl.ANY)
    grid_spec = pltpu.PrefetchScalarGridSpec(
        num_scalar_prefetch=4,
        grid=(nb,),
        in_specs=[any_space, any_space, any_space, any_space],
        out_specs=pl.BlockSpec((blk, HALF), lambda i, be, nv, nu, nx: (jnp.minimum(i, nu[0] - 1), 0)),
        scratch_shapes=[pltpu.VMEM((D_MODEL, EXPERT_DIM), BF16), pltpu.VMEM((D_MODEL, EXPERT_DIM), BF16),
                        pltpu.VMEM((EXPERT_DIM, D_MODEL), BF16),
                        pltpu.VMEM((ROW_SLOTS, blk, HALF), jnp.int32),
                        pltpu.VMEM((WEIGHT_SLOTS, D_MODEL, EXPERT_DIM), F32),
                        pltpu.VMEM((WEIGHT_SLOTS, D_MODEL, EXPERT_DIM), F32),
                        pltpu.VMEM((WEIGHT_SLOTS, EXPERT_DIM, D_MODEL), F32),
                        pltpu.SMEM((1,), jnp.int32),
                        pltpu.SemaphoreType.DMA((ROW_SLOTS,)), pltpu.SemaphoreType.DMA((WEIGHT_SLOTS,))],
    )
    return pl.pallas_call(
        functools.partial(_expert_kernel, blk=blk),
        grid_spec=grid_spec,
        out_shape=jax.ShapeDtypeStruct((rows, HALF), jnp.int32),
        compiler_params=pltpu.CompilerParams(dimension_semantics=("arbitrary",), vmem_limit_bytes=VMEM_LIMIT),
        name="experts",
    )(blk_exp, blk_valid, n_used, next_exp, xs, w1, w3, w2)


def _combine_kernel(g_ref, wt_ref, h2_ref, x1_ref, mod_ref, sw1_ref, sw3_ref, sw2_ref, gf_ref, *rest):
    y_ref = rest[-1]
    wt = wt_ref[...]
    lo_acc = None
    for k in range(TOP_K):
        lo, hi = _unpack_halves(g_ref[k])
        wk = wt[:, k:k + 1]
        lo_acc = wk * lo if lo_acc is None else lo_acc + wk * lo
        hi_acc = wk * hi if k == 0 else hi_acc + wk * hi
    routed = jnp.concatenate([lo_acc, hi_acc], axis=1)
    lo, hi = _unpack_halves(h2_ref[...])
    lo = lo.astype(BF16)
    hi = hi.astype(BF16)
    a = _dot(lo, sw1_ref[:HALF, :]) + _dot(hi, sw1_ref[HALF:, :])
    b = _dot(lo, sw3_ref[:HALF, :]) + _dot(hi, sw3_ref[HALF:, :])
    shared = _dot((_silu(a) * b).astype(BF16), sw2_ref[...])
    mod = mod_ref[0]
    x2 = x1_ref[...] + mod[5:6] * (routed + shared)
    y_ref[...] = _rms(x2) * gf_ref[...]


def _combine(gathered, wt, h2p, x1, mod, wts, tm, src_tile0, mod_batch0, tiles_per_batch, out_rows, out_tile0,
             y_prev=None):
    t = x1.shape[0]
    full = lambda a: pl.BlockSpec(a.shape, lambda i: (0,) * a.ndim)
    shifted = lambda w: pl.BlockSpec((tm, w), lambda i: (i + src_tile0, 0))
    args = [gathered, wt, h2p, x1, mod, wts["sw1"], wts["sw3"], wts["sw2"], wts["gf"]]
    specs = [pl.BlockSpec((TOP_K, tm, HALF), lambda i: (0, i + src_tile0, 0)), shifted(TOP_K), shifted(HALF),
             pl.BlockSpec((tm, D_MODEL), lambda i: (i, 0)),
             pl.BlockSpec((1, 6, D_MODEL), lambda i: (i // tiles_per_batch + mod_batch0, 0, 0)),
             full(wts["sw1"]), full(wts["sw3"]), full(wts["sw2"]), full(wts["gf"])]
    aliases = {}
    if y_prev is not None:
        args.append(y_prev)
        specs.append(pl.BlockSpec(memory_space=pl.ANY))
        aliases = {len(args) - 1: 0}
    return pl.pallas_call(
        _combine_kernel,
        grid=(t // tm,),
        in_specs=specs,
        out_specs=pl.BlockSpec((tm, D_MODEL), lambda i: (i + out_tile0, 0)),
        out_shape=jax.ShapeDtypeStruct((out_rows, D_MODEL), F32),
        input_output_aliases=aliases,
        compiler_params=pltpu.CompilerParams(dimension_semantics=("parallel",), vmem_limit_bytes=VMEM_LIMIT),
        name="combine",
    )(*args)


def _rope_tables(pos):
    half = MLA_ROPE // 2
    inv_freq = ROPE_THETA ** (-jnp.arange(half, dtype=F32) / half)
    ang = pos.astype(F32)[:, None] * inv_freq
    cos, sin = jnp.cos(ang), jnp.sin(ang)
    n = pos.shape[0]
    ones = jnp.ones((n, MLA_NOPE), F32)
    z_nope = jnp.zeros((n, MLA_NOPE), F32)
    z_pad = jnp.zeros((n, HEAD_PAD - MLA_NOPE - MLA_ROPE), F32)
    return (jnp.concatenate([ones, cos, cos, z_pad], axis=1),
            jnp.concatenate([z_nope, -sin, sin, z_pad], axis=1))


def _prep_weights(g_norm1, w_in, g_q_lat, w_uq, g_kv_lat, w_ukv, w_mla_up, w_sb_up, w_out, g_norm2,
                  w_router, b_router, shared_w1, shared_w3, shared_w2, g_final):
    w = w_in[0]
    o = 0
    parts = {}
    for name, width in (("qlat", MLA_Q_LORA), ("kv", MLA_KV_LORA), ("kr", MLA_ROPE), ("sbq", SB_WIDTH),
                        ("sbk", SB_WIDTH), ("sbv", SB_WIDTH), ("gm", D_MODEL), ("gs", D_MODEL)):
        parts[name] = w[:, o:o + width]
        o += width
    kr = parts["kr"]
    z_nope = jnp.zeros((D_MODEL, MLA_NOPE), F32)
    z_pad = jnp.zeros((D_MODEL, HEAD_PAD - MLA_NOPE - MLA_ROPE), F32)
    kr_seg = jnp.concatenate([z_nope, kr, z_pad], axis=1)
    w_in_r = jnp.concatenate([parts["qlat"], parts["kv"], parts["sbq"] * (SB_DIM ** -0.5 * LOG2_E), parts["sbk"],
                              parts["sbv"], parts["gm"], parts["gs"], kr_seg], axis=1).astype(BF16)

    scale = (MLA_NOPE + MLA_ROPE) ** -0.5 * LOG2_E
    uq = w_uq[0].reshape(MLA_Q_LORA, MLA_HEADS, MLA_NOPE + MLA_ROPE) * scale
    nope, rope = uq[..., :MLA_NOPE], uq[..., MLA_NOPE:]
    zq_pad = jnp.zeros((MLA_Q_LORA, MLA_HEADS, HEAD_PAD - MLA_NOPE - MLA_ROPE), F32)
    w_uq_r = jnp.concatenate([nope, rope, zq_pad], axis=-1).reshape(MLA_Q_LORA, QP_WIDTH).astype(BF16)

    ukv = w_ukv[0].reshape(MLA_KV_LORA, MLA_HEADS, MLA_NOPE + MLA_V)
    k_nope, v = ukv[..., :MLA_NOPE], ukv[..., MLA_NOPE:]
    k_pad = jnp.concatenate([k_nope, jnp.zeros((MLA_KV_LORA, MLA_HEADS, HEAD_PAD - MLA_NOPE), F32)], axis=-1)
    w_ukv_r = jnp.concatenate([k_pad.reshape(MLA_KV_LORA, QP_WIDTH),
                               v.reshape(MLA_KV_LORA, MLA_HEADS * MLA_V)], axis=1).astype(BF16)

    eye = jnp.eye(MLA_ROPE, dtype=F32)
    place_head = jnp.concatenate([jnp.zeros((MLA_ROPE, MLA_NOPE), F32), eye,
                                  jnp.zeros((MLA_ROPE, HEAD_PAD - MLA_NOPE - MLA_ROPE), F32)], axis=1)
    place = jnp.tile(place_head, (1, MLA_HEADS)).astype(BF16)

    return dict(
        g1=g_norm1[0].reshape(1, D_MODEL), w_in=w_in_r, gq=g_q_lat[0].reshape(1, MLA_Q_LORA), w_uq=w_uq_r,
        gkv=g_kv_lat[0].reshape(1, MLA_KV_LORA), w_ukv=w_ukv_r, place=place,
        w_mla_up=w_mla_up[0].astype(BF16), w_sb_up=w_sb_up[0].astype(BF16), w_out=w_out[0].astype(BF16),
        g2=g_norm2[0].reshape(1, D_MODEL), wr_t=w_router[0].T.astype(BF16),
        br=b_router[0].reshape(N_EXPERTS, 1),
        sw1=shared_w1[0].astype(BF16), sw3=shared_w3[0].astype(BF16), sw2=shared_w2[0].astype(BF16),
        gf=g_final.reshape(1, D_MODEL))


def _block_plan(counts, nb, blk):
    padded = (counts + blk - 1) // blk * blk
    pad_end = jnp.cumsum(padded)
    pad_start = pad_end - padded
    first_row = jnp.arange(nb, dtype=jnp.int32) * blk
    blk_exp = jnp.minimum(jnp.sum((pad_end[None, :] <= first_row[:, None]).astype(jnp.int32), axis=1),
                          N_EXPERTS - 1)
    own = blk_exp[:, None] == jnp.arange(N_EXPERTS, dtype=jnp.int32)[None, :]
    seg_end = jnp.sum(jnp.where(own, (pad_start + counts)[None, :], 0), axis=1)
    blk_valid = jnp.clip(seg_end - first_row, 0, blk).astype(jnp.int32)
    n_used = (pad_end[-1:] // blk).astype(jnp.int32)
    after_seg = jnp.sum(jnp.where(own, (pad_end // blk)[None, :], 0), axis=1)
    follower = jnp.sum(jnp.where(after_seg[:, None] == jnp.arange(nb, dtype=jnp.int32)[None, :],
                                 blk_exp[None, :], 0), axis=1)
    next_exp = jnp.where(after_seg < n_used[0], follower, -1).astype(jnp.int32)
    return pad_start, blk_exp.astype(jnp.int32), blk_valid, n_used, next_exp


def _moe_rows(h2p, idx_kt, rank_kt, counts, w1, w3, w2, scatter_rows, gather_rows, blk, tm):
    t = h2p.shape[0]
    nb = -(-t * TOP_K // blk) + N_EXPERTS
    pad_start, blk_exp, blk_valid, n_used, next_exp = _block_plan(counts.reshape(N_EXPERTS), nb, blk)
    pos = _positions(idx_kt, rank_kt, pad_start, tm).reshape(TOP_K * t)
    xs = scatter_rows(h2p, pos, nb * blk)
    os_ = _experts(xs, blk_exp, blk_valid, n_used, next_exp, w1, w3, w2, blk)
    return gather_rows(os_, pos).reshape(TOP_K, t, HALF)


def _forward(x_prompt, x_sample, cache_mla_ckv, cache_mla_krope, cache_sb_k, cache_sb_v, c_prompt, c_sample,
             w_ada, b_ada, moe_w1, moe_w3, moe_w2, wts, scatter_rows, gather_rows, token_block, attn_block,
             route_block, moe_block, wide_block):
    bp, sp, _ = x_prompt.shape
    bs, ss, _ = x_sample.shape
    past_len = cache_mla_ckv.shape[2]

    mod = _ada(jnp.concatenate([c_prompt, c_sample], axis=0), w_ada[0], b_ada[0]).reshape(bp + bs, 6, D_MODEL)
    mod_p, mod_s = mod[:bp], mod[bp:]

    cos_p, sin_p = _rope_tables(jnp.arange(sp))
    (qp, kmla, vmla, sbq, sbk16, sbv16, gates, ckv_p, krope_p, sbk_p, sbv_p) = _in_proj(
        x_prompt, mod_p, wts, cos_p, sin_p, wide_block)
    split_b = bp * GROUP_A_SHARE[0] // GROUP_A_SHARE[1]
    assert 0 < split_b < bp
    mixed = [_prompt_attention(qp, kmla, vmla, sbq, sbk16, sbv16, gates, x_prompt, mod_p, wts, attn_block,
                               b0, nb_) for b0, nb_ in ((0, split_b), (split_b, bp - split_b))]

    cos_s, sin_s = _rope_tables(past_len + jnp.arange(ss))
    (qs, kmla_s, vmla_s, sbq_s, sbk16_s, sbv16_s, gates_s, ckv_s, krope_s, sbk_s, sbv_s) = _in_proj(
        x_sample, mod_s, wts, cos_s, sin_s, ss)
    pkmla, pvmla = _kv_up(cache_mla_ckv[0], cache_mla_krope[0], wts["w_ukv"], wts["place"], wide_block)
    past = (pkmla, pvmla, cache_sb_k[0].reshape(bs, past_len, SB_WIDTH), cache_sb_v[0].reshape(bs, past_len, SB_WIDTH))
    x1_s, h2_s = _decode_attention(qs, kmla_s, vmla_s, sbq_s, sbk16_s, sbv16_s, past, gates_s, x_sample, mod_s,
                                   wts, token_block)

    tp, ts = bp * sp, bs * ss
    ta = split_b * sp
    tb = tp - ta

    def moe(h2_rows):
        idx_kt, wt_kt, rank_kt, counts = _route(h2_rows, wts["wr_t"], wts["br"], route_block)
        gathered = _moe_rows(h2_rows, idx_kt, rank_kt, counts, moe_w1[0], moe_w3[0], moe_w2[0],
                             scatter_rows, gather_rows, moe_block, route_block)
        return gathered, wt_kt.T

    tiles_per_batch = sp // wide_block
    (x1_a, h2_a), (x1_b, h2_b) = mixed
    h2_a = h2_a.reshape(ta, HALF)
    g_a, wt_a = moe(h2_a)
    y_p = _combine(g_a, wt_a, h2_a, x1_a.reshape(ta, D_MODEL), mod_p, wts, wide_block,
                   0, 0, tiles_per_batch, tp, 0)
    h2_b = jnp.concatenate([h2_b.reshape(tb, HALF), h2_s.reshape(ts, HALF)], axis=0)
    g_b, wt_b = moe(h2_b)
    y_p = _combine(g_b, wt_b, h2_b, x1_b.reshape(tb, D_MODEL), mod_p, wts, wide_block,
                   0, split_b, tiles_per_batch, tp, ta // wide_block, y_prev=y_p)
    y_s = _combine(g_b, wt_b, h2_b, x1_s.reshape(ts, D_MODEL), mod_s, wts, ss, tb // ss, 0, 1, ts, 0)

    heads = lambda a, b_, s_: a.reshape(1, b_, s_, SB_HEADS, SB_DIM)
    return (y_p.reshape(bp, sp, D_MODEL), y_s.reshape(bs, ss, D_MODEL),
            ckv_p[None], krope_p[None], heads(sbk_p, bp, sp), heads(sbv_p, bp, sp),
            ckv_s[None], krope_s[None], heads(sbk_s, bs, ss), heads(sbv_s, bs, ss))


def kernel(x_prompt, x_sample, cache_mla_ckv, cache_mla_krope, cache_sb_k, cache_sb_v, c_prompt, c_sample, w_ada, b_ada, g_norm1, w_in, g_q_lat, w_uq, g_kv_lat, w_ukv, w_mla_up, w_sb_up, w_out, g_norm2, w_router, b_router, moe_w1, moe_w3, moe_w2, shared_w1, shared_w3, shared_w2, g_final):
    wts = _prep_weights(g_norm1, w_in, g_q_lat, w_uq, g_kv_lat, w_ukv, w_mla_up, w_sb_up, w_out, g_norm2,
                        w_router, b_router, shared_w1, shared_w3, shared_w2, g_final)
    scatter_rows = functools.partial(_sc_scatter_rows, chunk=SC_CHUNK)
    gather_rows = functools.partial(_sc_gather_rows, chunk=SC_CHUNK)
    return _forward(x_prompt, x_sample, cache_mla_ckv, cache_mla_krope, cache_sb_k, cache_sb_v, c_prompt, c_sample,
                    w_ada, b_ada, moe_w1, moe_w3, moe_w2, wts, scatter_rows, gather_rows, TOKEN_BLOCK, ATTN_BLOCK,
                    TOKEN_BLOCK, MOE_BLOCK, WIDE_BLOCK)
```
